```python
import math
import jax
import jax.numpy as jnp
from jax import lax
import numpy as np

D_MODEL = 2048
BATCH = 4
SEQ = 4096
DEPTH = 1

MEM_LEN = 256
NORM_EPS = 1e-5

RWKV_WIDTH = D_MODEL // 2
RWKV_HEAD_DIM = 64
RWKV_HEADS = RWKV_WIDTH // RWKV_HEAD_DIM
DECAY_LORA = 64
AAA_LORA = 64
GATE_LORA = 160
RWKV_GN_EPS = 64e-5
RWKV_IN = 3 * RWKV_WIDTH + DECAY_LORA + AAA_LORA + GATE_LORA

DIFF_WIDTH = D_MODEL // 2
DIFF_HEAD_DIM = 64
DIFF_HEADS = DIFF_WIDTH // (2 * DIFF_HEAD_DIM)
Q_BLOCK = 128

CROSS_HEADS = 4
CROSS_HEAD_DIM = 128
CROSS_WIDTH = CROSS_HEADS * CROSS_HEAD_DIM

N_EXPERTS = 32
TOP_K = 4
D_EXPERT = D_MODEL
SWIGLU_ALPHA = 1.702
SWIGLU_LIMIT = 7.0
MOE_BLOCK = 128

IN_WIDTH = RWKV_IN + 3 * DIFF_WIDTH + 2 * D_MODEL

kernel_name = "rwkv7_diffattn_gated_hybrid_moe"


def rms_norm(x, g, eps=NORM_EPS):
    xf = x.astype(jnp.float32)
    y = xf * lax.rsqrt(jnp.mean(xf * xf, axis=-1, keepdims=True) + eps)
    return (y * g.astype(jnp.float32)).astype(x.dtype)


def alibi_slopes(n_heads):
    return 2.0 ** (-8.0 * jnp.arange(1, n_heads + 1, dtype=jnp.float32) / n_heads)


def rwkv7_scan(r, w, k, v, kk, a):
    B, S, H, N = r.shape

    def step(state, inp):
        r_t, w_t, k_t, v_t, kk_t, a_t = inp
        sa = jnp.einsum('bhvk,bhk->bhv', state, -kk_t)
        state = (state * w_t[:, :, None, :]
                 + sa[..., None] * (kk_t * a_t)[:, :, None, :]
                 + v_t[..., None] * k_t[:, :, None, :])
        y_t = jnp.einsum('bhvk,bhk->bhv', state, r_t)
        return state, y_t

    xs = (jnp.moveaxis(r, 1, 0), jnp.moveaxis(w, 1, 0), jnp.moveaxis(k, 1, 0),
          jnp.moveaxis(v, 1, 0), jnp.moveaxis(kk, 1, 0), jnp.moveaxis(a, 1, 0))
    state0 = jnp.zeros((B, H, N, N), jnp.float32)
    _, y = lax.scan(step, state0, xs)
    return jnp.moveaxis(y, 0, 1)


def rwkv7_mixer(p, mu, w0, w2, a0, a2, g2, k_k, k_a, r_k, ln_w, ln_b):
    B, S, _ = p.shape
    H, N, C = RWKV_HEADS, RWKV_HEAD_DIM, RWKV_WIDTH
    shifted = jnp.pad(p, ((0, 0), (1, 0), (0, 0)))[:, :-1]
    p = p + (shifted - p) * mu
    r, k, v = p[..., :C], p[..., C:2 * C], p[..., 2 * C:3 * C]
    o = 3 * C
    wd = p[..., o:o + DECAY_LORA]
    ad = p[..., o + DECAY_LORA:o + DECAY_LORA + AAA_LORA]
    gd = p[..., o + DECAY_LORA + AAA_LORA:]
    w = -jax.nn.softplus(-(w0 + jnp.tanh(wd) @ w2)) - 0.5
    decay = jnp.exp(-jnp.exp(w.astype(jnp.float32)))
    a = jax.nn.sigmoid(a0 + ad @ a2)
    g = jax.nn.sigmoid(gd) @ g2
    heads = lambda t: t.reshape(B, S, H, N).astype(jnp.float32)
    kk = heads(k * k_k)
    kk = kk / jnp.maximum(jnp.linalg.norm(kk, axis=-1, keepdims=True), 1e-12)
    k = k * (1.0 + (a - 1.0) * k_a)
    rh, kh, vh, ah = heads(r), heads(k), heads(v), heads(a)
    y = rwkv7_scan(rh, heads(decay), kh, vh, kk, ah)
    mean = jnp.mean(y, axis=-1, keepdims=True)
    var = jnp.mean(jnp.square(y - mean), axis=-1, keepdims=True)
    y = ((y - mean) * lax.rsqrt(var + RWKV_GN_EPS)).reshape(B, S, C)
    y = y * ln_w.astype(jnp.float32) + ln_b.astype(jnp.float32)
    bonus = jnp.sum(rh * kh * r_k.astype(jnp.float32), axis=-1, keepdims=True) * vh
    y = y + bonus.reshape(B, S, C)
    return (y * g.astype(jnp.float32)).astype(p.dtype)


def diff_attention_mixer(p, lq1, lk1, lq2, lk2, subln_g, lambda_init):
    B, S, _ = p.shape
    H, d, W = DIFF_HEADS, DIFF_HEAD_DIM, DIFF_WIDTH
    q = p[..., :W].reshape(B, S, 2 * H, d)
    k = p[..., W:2 * W].reshape(B, S, 2 * H, d)
    v = p[..., 2 * W:].reshape(B, S, H, 2 * d)
    lam = (jnp.exp(jnp.sum(lq1.astype(jnp.float32) * lk1.astype(jnp.float32)))
           - jnp.exp(jnp.sum(lq2.astype(jnp.float32) * lk2.astype(jnp.float32))) + lambda_init)
    slopes = alibi_slopes(H)[:, None, None, None]
    scale = d ** -0.5
    nb = S // Q_BLOCK
    qb = jnp.moveaxis(q.reshape(B, nb, Q_BLOCK, 2 * H, d), 1, 0)
    kpos = jnp.arange(S)

    def block(args):
        qi, bi = args
        qpos = bi * Q_BLOCK + jnp.arange(Q_BLOCK)
        s = jnp.einsum('bqhd,bkhd->bhqk', qi, k).astype(jnp.float32) * scale
        s = s.reshape(B, H, 2, Q_BLOCK, S)
        dist = (qpos[:, None] - kpos[None, :]).astype(jnp.float32)
        s = jnp.where(dist >= 0, s - slopes * dist, -jnp.inf)
        pr = jax.nn.softmax(s, axis=-1)
        wts = pr[:, :, 0] - lam * pr[:, :, 1]
        return jnp.einsum('bhqk,bkhe->bqhe', wts.astype(v.dtype), v)

    o = lax.map(block, (qb, jnp.arange(nb)))
    o = jnp.moveaxis(o, 0, 1).reshape(B, S, H, 2 * d)
    o = rms_norm(o, subln_g) * (1.0 - lambda_init)
    return o.reshape(B, S, W)


def memory_cross_attention(h, mem_n, wq, wkv, wo):
    B, S, _ = h.shape
    M = mem_n.shape[1]
    q = (h @ wq).reshape(B, S, CROSS_HEADS, CROSS_HEAD_DIM)
    kv = mem_n @ wkv
    k = kv[..., :CROSS_WIDTH].reshape(B, M, CROSS_HEADS, CROSS_HEAD_DIM)
    v = kv[..., CROSS_WIDTH:].reshape(B, M, CROSS_HEADS, CROSS_HEAD_DIM)
    s = jnp.einsum('bshd,bmhd->bhsm', q, k).astype(jnp.float32) * (CROSS_HEAD_DIM ** -0.5)
    pr = jax.nn.softmax(s, axis=-1)
    o = jnp.einsum('bhsm,bmhd->bshd', pr.astype(v.dtype), v).reshape(B, S, CROSS_WIDTH)
    return o @ wo


def moe_ffn(h, router_w, router_b, w1, b1, w2, b2):
    B, S, D = h.shape
    T = B * S
    t = h.reshape(T, D)
    logits = (t @ router_w + router_b).astype(jnp.float32)
    top_val, top_idx = lax.top_k(logits, TOP_K)
    gate = jax.nn.softmax(top_val, axis=-1)
    n_rows = T * TOP_K
    flat_e = top_idx.reshape(-1)
    flat_tok = jnp.arange(n_rows) // TOP_K
    order = jnp.argsort(flat_e)
    se = flat_e[order]
    stok = flat_tok[order]
    sgate = gate.reshape(-1)[order]
    counts = jnp.bincount(flat_e, length=N_EXPERTS)
    padded = (counts + MOE_BLOCK - 1) // MOE_BLOCK * MOE_BLOCK
    start = jnp.cumsum(counts) - counts
    pend = jnp.cumsum(padded)
    pstart = pend - padded
    dest = pstart[se] + jnp.arange(n_rows) - start[se]
    n_pad = n_rows + N_EXPERTS * MOE_BLOCK
    n_blocks = n_pad // MOE_BLOCK
    xs = jnp.zeros((n_pad, D), t.dtype).at[dest].set(t[stok])
    block_e = jnp.minimum(
        jnp.searchsorted(pend, jnp.arange(n_blocks) * MOE_BLOCK, side='right'), N_EXPERTS - 1)

    def expert_block(args):
        xb, e = args
        hb = xb @ w1[e] + b1[e]
        x_glu = jnp.minimum(hb[:, ::2], SWIGLU_LIMIT)
        x_lin = jnp.clip(hb[:, 1::2], -SWIGLU_LIMIT, SWIGLU_LIMIT)
        act = x_glu * jax.nn.sigmoid(SWIGLU_ALPHA * x_glu) * (x_lin + 1.0)
        return act @ w2[e] + b2[e]

    ys = lax.map(expert_block, (xs.reshape(n_blocks, MOE_BLOCK, D), block_e)).reshape(n_pad, D)
    contrib = ys[dest] * sgate[:, None].astype(ys.dtype)
    out = jax.ops.segment_sum(contrib, stok, num_segments=T)
    return out.reshape(B, S, D)


def setup_inputs(seed: int = 0) -> dict:
    key = jax.random.key(seed)
    keys = jax.random.split(key, 64)
    ks = (keys[i] for i in range(64))
    nrm = lambda shape, scale: jax.random.normal(next(ks), shape, jnp.float32) * scale
    gain = lambda shape: 1.0 + nrm(shape, 0.05)
    L, D, C, E, F = DEPTH, D_MODEL, RWKV_WIDTH, N_EXPERTS, D_EXPERT
    return {
        "x": nrm((BATCH, SEQ, D), 1.0),
        "mem": nrm((BATCH, MEM_LEN, D), 1.0),
        "norm_mix_g": gain((L, D)),
        "w_in": nrm((L, D, IN_WIDTH), D ** -0.5),
        "rwkv_mu": jax.random.uniform(next(ks), (L, RWKV_IN), jnp.float32),
        "rwkv_w0": jax.random.uniform(next(ks), (L, C), jnp.float32, -6.0, 1.0),
        "rwkv_w2": nrm((L, DECAY_LORA, C), 0.5 * DECAY_LORA ** -0.5),
        "rwkv_a0": nrm((L, C), 0.1),
        "rwkv_a2": nrm((L, AAA_LORA, C), 0.5 * AAA_LORA ** -0.5),
        "rwkv_g2": nrm((L, GATE_LORA, C), GATE_LORA ** -0.5),
        "rwkv_k_k": 0.85 + nrm((L, C), 0.02),
        "rwkv_k_a": 1.0 + nrm((L, C), 0.02),
        "rwkv_r_k": nrm((L, RWKV_HEADS, RWKV_HEAD_DIM), 0.1),
        "rwkv_ln_w": gain((L, C)),
        "rwkv_ln_b": nrm((L, C), 0.02),
        "rwkv_proj": nrm((L, C, D), C ** -0.5),
        "diff_lq1": nrm((L, DIFF_HEAD_DIM), 0.1),
        "diff_lk1": nrm((L, DIFF_HEAD_DIM), 0.1),
        "diff_lq2": nrm((L, DIFF_HEAD_DIM), 0.1),
        "diff_lk2": nrm((L, DIFF_HEAD_DIM), 0.1),
        "diff_subln_g": gain((L, 2 * DIFF_HEAD_DIM)),
        "diff_proj": nrm((L, DIFF_WIDTH, D), DIFF_WIDTH ** -0.5),
        "w_out": nrm((L, D, D), D ** -0.5),
        "norm_cross_g": gain((L, D)),
        "norm_mem_g": gain((L, D)),
        "cross_wq": nrm((L, D, CROSS_WIDTH), D ** -0.5),
        "cross_wkv": nrm((L, D, 2 * CROSS_WIDTH), D ** -0.5),
        "cross_wo": nrm((L, CROSS_WIDTH, D), CROSS_WIDTH ** -0.5),
        "norm_ffn_g": gain((L, D)),
        "router_w": nrm((L, D, E), D ** -0.5),
        "router_b": nrm((L, E), 0.01),
        "expert_w1": nrm((L, E, D, 2 * F), D ** -0.5),
        "expert_b1": nrm((L, E, 2 * F), 0.01),
        "expert_w2": nrm((L, E, F, D), F ** -0.5),
        "expert_b2": nrm((L, E, D), 0.01),
        "final_norm_g": gain((D,)),
    }


def reference(x, mem, norm_mix_g, w_in, rwkv_mu, rwkv_w0, rwkv_w2, rwkv_a0, rwkv_a2, rwkv_g2,
              rwkv_k_k, rwkv_k_a, rwkv_r_k, rwkv_ln_w, rwkv_ln_b, rwkv_proj,
              diff_lq1, diff_lk1, diff_lq2, diff_lk2, diff_subln_g, diff_proj, w_out,
              norm_cross_g, norm_mem_g, cross_wq, cross_wkv, cross_wo,
              norm_ffn_g, router_w, router_b, expert_w1, expert_b1, expert_w2, expert_b2,
              final_norm_g):
    o1 = RWKV_IN
    o2 = o1 + 3 * DIFF_WIDTH
    for l in range(DEPTH):
        lambda_init = 0.8 - 0.6 * math.exp(-0.3 * l)
        h = rms_norm(x, norm_mix_g[l])
        proj = h @ w_in[l]
        y_a = rwkv7_mixer(proj[..., :o1], rwkv_mu[l], rwkv_w0[l], rwkv_w2[l], rwkv_a0[l],
                          rwkv_a2[l], rwkv_g2[l], rwkv_k_k[l], rwkv_k_a[l], rwkv_r_k[l],
                          rwkv_ln_w[l], rwkv_ln_b[l]) @ rwkv_proj[l]
        y_b = diff_attention_mixer(proj[..., o1:o2], diff_lq1[l], diff_lk1[l], diff_lq2[l],
                                   diff_lk2[l], diff_subln_g[l], lambda_init) @ diff_proj[l]
        gate_a = jax.nn.sigmoid(proj[..., o2:o2 + D_MODEL])
        gate_b = jax.nn.sigmoid(proj[..., o2 + D_MODEL:])
        x = x + (gate_a * y_a + gate_b * y_b) @ w_out[l]
        x = x + memory_cross_attention(rms_norm(x, norm_cross_g[l]), rms_norm(mem, norm_mem_g[l]),
                                       cross_wq[l], cross_wkv[l], cross_wo[l])
        x = x + moe_ffn(rms_norm(x, norm_ffn_g[l]), router_w[l], router_b[l], expert_w1[l],
                        expert_b1[l], expert_w2[l], expert_b2[l])
    return rms_norm(x, final_norm_g)
```

```python
import functools
import math

import jax
import jax.numpy as jnp
from jax import lax
from jax.experimental import pallas as pl
from jax.experimental.pallas import tpu as pltpu

F32 = jnp.float32
BF16 = jnp.bfloat16

D_MODEL = 2048
NORM_EPS = 1e-5
LANES = 128
HEAD_DIM = 64
RWKV_WIDTH = 1024
RWKV_PAIRS = RWKV_WIDTH // LANES
CHUNK = 64
RWKV_GN_EPS = 64e-5
DECAY_LORA = 64
AAA_LORA = 64
GATE_LORA = 160
LORA_PAD = 512
RWKV_COLS = 3 * RWKV_WIDTH + LORA_PAD
DIFF_WIDTH = 1024
DIFF_HEADS = 8
CROSS_HEADS = 4
CROSS_WIDTH = 512
N_EXPERTS = 32
TOP_K = 4
D_EXPERT = 2048
SWIGLU_ALPHA = 1.702
SWIGLU_LIMIT = 7.0
MOE_SUPER = 1024
MOE_SUB = 256
MOE_FT = 512
ROW_TILE = D_MODEL // LANES
VMEM_LIMIT = 56 * 1024 * 1024

NN = (((1,), (0,)), ((), ()))
NT = (((1,), (1,)), ((), ()))


def _dot(a, b, dims=NN):
    return lax.dot_general(a, b, dims, preferred_element_type=F32)


def _split2(a):
    hi = a.astype(BF16)
    lo = (a - hi.astype(F32)).astype(BF16)
    return hi, lo


def _split3(a):
    hi = a.astype(BF16)
    r1 = a - hi.astype(F32)
    mid = r1.astype(BF16)
    lo = (r1 - mid.astype(F32)).astype(BF16)
    return hi, mid, lo


def _dot3(a, b, dims=NN):
    ah, al = _split2(a)
    bh, bl = _split2(b)
    return _dot(ah, bh, dims) + (_dot(ah, bl, dims) + _dot(al, bh, dims))


def _dot_exact_rhs(a, b_bf16, dims=NN):
    h, m, l = _split3(a)
    return _dot(h, b_bf16, dims) + (_dot(m, b_bf16, dims) + _dot(l, b_bf16, dims))


def _rms(x, g, eps):
    ms = jnp.mean(x * x, axis=-1, keepdims=True)
    return x * lax.rsqrt(ms + eps) * g


def _cparams(sem):
    return pltpu.CompilerParams(dimension_semantics=sem, vmem_limit_bytes=VMEM_LIMIT)


def _norm_matmul_kernel(x_ref, g_ref, w_ref, o_ref, h_ref, *, act):
    @pl.when(pl.program_id(1) == 0)
    def _():
        h_ref[...] = _rms(x_ref[...], g_ref[...], NORM_EPS).astype(BF16)

    y = _dot(h_ref[...], w_ref[...])
    if act == "sigmoid":
        y = jax.nn.sigmoid(y)
    o_ref[...] = y.astype(o_ref.dtype)


def norm_matmul(x, g, w, *, tm, tn, out_dtype, act=None, name):
    m, d = x.shape
    n = w.shape[1]
    return pl.pallas_call(
        functools.partial(_norm_matmul_kernel, act=act),
        grid=(m // tm, n // tn),
        in_specs=[
            pl.BlockSpec((tm, d), lambda i, j: (i, 0)),
            pl.BlockSpec((1, d), lambda i, j: (0, 0)),
            pl.BlockSpec((d, tn), lambda i, j: (0, j)),
        ],
        out_specs=pl.BlockSpec((tm, tn), lambda i, j: (i, j)),
        out_shape=jax.ShapeDtypeStruct((m, n), out_dtype),
        scratch_shapes=[pltpu.VMEM((tm, d), BF16)],
        compiler_params=_cparams(("parallel", "arbitrary")),
        name=name,
    )(x, g.reshape(1, d), w)


def _head_ones():
    r = lax.broadcasted_iota(jnp.int32, (LANES, LANES), 0)
    c = lax.broadcasted_iota(jnp.int32, (LANES, LANES), 1)
    return ((r // HEAD_DIM) == (c // HEAD_DIM)).astype(BF16)


def _rwkv_prep_kernel(p_ref, prev_ref, mu_ref, w0_ref, a0_ref, kk_ref, ka_ref, w2_ref, a2_ref, g2_ref,
                      r_out, lw_out, k_out, v_out, kkn_out, b_out, g_out, *, tiles_per_seq):
    c = RWKV_WIDTH
    i = pl.program_id(0)
    p = p_ref[...]
    tm = p.shape[0]
    first = (i % tiles_per_seq) == 0
    prev_row = jnp.where(first, 0.0, prev_ref[7:8, :])
    row = lax.broadcasted_iota(jnp.int32, p.shape, 0)
    shifted = jnp.where(row == 0, prev_row, pltpu.roll(p, 1, axis=0))
    ps = p + (shifted - p) * mu_ref[...]
    r = ps[:, 0:c]
    k = ps[:, c:2 * c]
    v = ps[:, 2 * c:3 * c]
    wd = ps[:, 3 * c:3 * c + 128]
    ad = ps[:, 3 * c + 128:3 * c + 256]
    gd = ps[:, 3 * c + 256:3 * c + 512]
    z = -(w0_ref[...] + _dot3(jnp.tanh(wd), w2_ref[...]))
    softplus = jnp.maximum(z, 0.0) + jnp.log1p(jnp.exp(-jnp.abs(z)))
    w = -softplus - 0.5
    lw = -jnp.exp(w)
    a = jax.nn.sigmoid(a0_ref[...] + _dot3(ad, a2_ref[...]))
    g = _dot3(jax.nn.sigmoid(gd), g2_ref[...])
    kkr = k * kk_ref[...]
    k2 = k * (1.0 + (a - 1.0) * ka_ref[...])
    ones = _head_ones()
    for q in range(RWKV_PAIRS):
        sl = slice(q * LANES, (q + 1) * LANES)
        x = kkr[:, sl]
        ss = _dot_exact_rhs(x * x, ones)
        kkn = x / jnp.maximum(jnp.sqrt(ss), 1e-12)
        r_out[q] = r[:, sl]
        lw_out[q] = lw[:, sl]
        k_out[q] = k2[:, sl]
        v_out[q] = v[:, sl]
        kkn_out[q] = kkn
        b_out[q] = kkn * a[:, sl]
        g_out[q] = g[:, sl]


def rwkv_prep(p, mu, w0, a0, k_k, k_a, w2p, a2p, g2p, *, seq, tm):
    t, cols = p.shape
    c = RWKV_WIDTH
    vec = lambda n: pl.BlockSpec((1, n), lambda i: (0, 0))
    full = lambda a: pl.BlockSpec(a.shape, lambda i: (0, 0))
    out_spec = pl.BlockSpec((RWKV_PAIRS, tm, LANES), lambda i: (0, i, 0))
    out_shape = jax.ShapeDtypeStruct((RWKV_PAIRS, t, LANES), F32)
    return pl.pallas_call(
        functools.partial(_rwkv_prep_kernel, tiles_per_seq=seq // tm),
        grid=(t // tm,),
        in_specs=[
            pl.BlockSpec((tm, cols), lambda i: (i, 0)),
            pl.BlockSpec((8, cols), lambda i: (jnp.maximum(i * (tm // 8) - 1, 0), 0)),
            vec(cols), vec(c), vec(c), vec(c), vec(c), full(w2p), full(a2p), full(g2p),
        ],
        out_specs=[out_spec] * 7,
        out_shape=[out_shape] * 7,
        compiler_params=_cparams(("parallel",)),
        name="rwkv_prep",
    )(p, p, mu.reshape(1, cols), w0.reshape(1, c), a0.reshape(1, c), k_k.reshape(1, c), k_a.reshape(1, c),
      w2p, a2p, g2p)


def _block_diag_rows(x, m0):
    return jnp.concatenate([jnp.where(m0, x, 0.0), jnp.where(m0, 0.0, x)], axis=0)


def _rwkv_scan_kernel(r_ref, lw_ref, k_ref, v_ref, kk_ref, b_ref, g_ref, lnw_ref, lnb_ref, rk_ref,
                      o_ref, st_ref):
    L = CHUNK

    @pl.when(pl.program_id(1) == 0)
    def _():
        st_ref[...] = jnp.zeros_like(st_ref)

    t_i = lax.broadcasted_iota(jnp.int32, (L, LANES), 0)
    lane = lax.broadcasted_iota(jnp.int32, (L, LANES), 1)
    j_i = lane % HEAD_DIM
    m0 = lane < HEAD_DIM
    strict = j_i < t_i
    incl = j_i <= t_i
    eye = (j_i == t_i).astype(F32)
    r2 = lax.broadcasted_iota(jnp.int32, (LANES, LANES), 0)
    c2 = lax.broadcasted_iota(jnp.int32, (LANES, LANES), 1)
    same_head = (r2 // HEAD_DIM) == (c2 // HEAD_DIM)
    diag = r2 == c2
    ones = same_head.astype(BF16)
    tr = lax.broadcasted_iota(jnp.int32, (L, L), 0)
    tc = lax.broadcasted_iota(jnp.int32, (L, L), 1)
    tri = (tc <= tr).astype(BF16)
    zero = jnp.zeros((L, LANES), F32)

    def pair(q, carry):
        r = r_ref[q]
        lw = lw_ref[q]
        k = k_ref[q]
        v = v_ref[q]
        kk = kk_ref[q]
        b = b_ref[q]
        h3, m3, l3 = _split3(lw)
        cum = _dot(tri, h3) + (_dot(tri, m3) + _dot(tri, l3))
        cum_prev = cum - lw
        cum_end = cum[L - 1:L, :]
        e_cum = jnp.exp(cum)
        e_neg = jnp.exp(-cum)
        e_end = jnp.exp(cum_end - cum)
        at = -kk * jnp.exp(cum_prev)
        rt = r * e_cum
        bt = b * e_neg
        kt = k * e_neg
        bh = b * e_end
        kh = k * e_end
        w_end = jnp.exp(cum_end)

        bd = lambda x: _block_diag_rows(x, m0)
        g_all = _dot3(jnp.concatenate([at, rt], axis=0),
                      jnp.concatenate([bd(bt), bd(kt)], axis=0), NT)
        a_ab = jnp.where(strict, g_all[0:L, 0:LANES], 0.0)
        a_ak = jnp.where(strict, g_all[0:L, LANES:], 0.0)
        a_rb = jnp.where(incl, g_all[L:, 0:LANES], 0.0)
        a_rk = jnp.where(incl, g_all[L:, LANES:], 0.0)

        tinv = eye
        pw = a_ab
        for it in range(6):
            if it < 5:
                res = _dot3(pw, jnp.concatenate([bd(tinv), bd(pw)], axis=1))
                tinv = tinv + res[:, 0:LANES]
                pw = res[:, LANES:]
            else:
                tinv = tinv + _dot3(pw, bd(tinv))

        av = _dot3(a_ak, bd(v))
        qp = _dot3(tinv, jnp.concatenate([bd(at), bd(av)], axis=1))
        q1 = qp[:, 0:LANES]
        p1 = qp[:, LANES:]
        q2 = rt + _dot3(a_rb, bd(q1))
        p2 = _dot3(a_rb, bd(p1)) + _dot3(a_rk, bd(v))
        lhs_t = jnp.concatenate([bh, kh], axis=0).T
        rhs = jnp.concatenate([jnp.concatenate([q1, p1], axis=1),
                               jnp.concatenate([zero, v], axis=1)], axis=0)
        mp = _dot3(lhs_t, rhs)
        m_bd = jnp.where(same_head, mp[:, 0:LANES], 0.0) + jnp.where(diag, w_end, 0.0)
        p3 = jnp.where(same_head, mp[:, LANES:], 0.0)
        st = st_ref[q]
        y = _dot3(q2, st) + p2
        st_ref[q] = _dot3(m_bd, st) + p3

        mean = _dot_exact_rhs(y, ones) * (1.0 / HEAD_DIM)
        yc = y - mean
        var = _dot_exact_rhs(yc * yc, ones) * (1.0 / HEAD_DIM)
        yn = yc * lax.rsqrt(var + RWKV_GN_EPS) * lnw_ref[q] + lnb_ref[q]
        bonus = _dot_exact_rhs(r * k * rk_ref[q], ones) * v
        o_ref[q] = ((yn + bonus) * g_ref[q]).astype(o_ref.dtype)
        return carry

    lax.fori_loop(0, RWKV_PAIRS, pair, 0)


def rwkv_scan(r, lw, k, v, kk, b, g, ln_w, ln_b, r_k, *, batch, seq):
    nchunk = seq // CHUNK
    blk = pl.BlockSpec((RWKV_PAIRS, CHUNK, LANES), lambda bi, ci: (0, bi * nchunk + ci, 0))
    par = pl.BlockSpec((RWKV_PAIRS, 1, LANES), lambda bi, ci: (0, 0, 0))
    t = batch * seq
    return pl.pallas_call(
        _rwkv_scan_kernel,
        grid=(batch, nchunk),
        in_specs=[blk] * 7 + [par] * 3,
        out_specs=blk,
        out_shape=jax.ShapeDtypeStruct((RWKV_PAIRS, t, LANES), BF16),
        scratch_shapes=[pltpu.VMEM((RWKV_PAIRS, LANES, LANES), F32)],
        compiler_params=_cparams(("arbitrary", "arbitrary")),
        name="rwkv_scan",
    )(r, lw, k, v, kk, b, g, ln_w.reshape(RWKV_PAIRS, 1, LANES), ln_b.reshape(RWKV_PAIRS, 1, LANES),
      r_k.reshape(RWKV_PAIRS, 1, LANES))


def _diff_attn_kernel(slope_ref, q_ref, k_ref, v_ref, lq1_ref, lk1_ref, lq2_ref, lk2_ref, sg_ref,
                      o_ref, m_ref, l_ref, acc_ref, *, tq, lam_init):
    h = pl.program_id(1)
    qi = pl.program_id(2)
    ki = pl.program_id(3)

    @pl.when(ki == 0)
    def _():
        m_ref[...] = jnp.full_like(m_ref, -jnp.inf)
        l_ref[...] = jnp.zeros_like(l_ref)
        acc_ref[...] = jnp.zeros_like(acc_ref)

    @pl.when(ki <= qi)
    def _():
        q = q_ref[...]
        k = k_ref[...]
        v = v_ref[...]
        lane = lax.broadcasted_iota(jnp.int32, q.shape, 1)
        m0 = lane < HEAD_DIM
        zq = jnp.zeros_like(q)
        qs = (jnp.where(m0, q, zq), jnp.where(m0, zq, q))
        rel = (lax.broadcasted_iota(jnp.int32, (tq, tq), 0) - lax.broadcasted_iota(jnp.int32, (tq, tq), 1))
        dist = (rel + (qi - ki) * tq).astype(F32)
        bias = dist * (-slope_ref[h])
        scale = HEAD_DIM ** -0.5
        for idx in range(2):
            s = _dot(qs[idx], k, NT) * scale + bias
            s = jnp.where(dist >= 0.0, s, -jnp.inf)
            m_prev = m_ref[idx]
            m_new = jnp.maximum(m_prev, jnp.max(s, axis=-1, keepdims=True))
            alpha = jnp.exp(m_prev - m_new)
            p = jnp.exp(s - m_new)
            l_ref[idx] = alpha * l_ref[idx] + jnp.sum(p, axis=-1, keepdims=True)
            acc_ref[idx] = alpha * acc_ref[idx] + _dot(p.astype(BF16), v)
            m_ref[idx] = m_new

    @pl.when(ki == qi)
    def _():
        lam = (jnp.exp(jnp.sum(lq1_ref[...] * lk1_ref[...], axis=-1, keepdims=True))
               - jnp.exp(jnp.sum(lq2_ref[...] * lk2_ref[...], axis=-1, keepdims=True)) + lam_init)
        o = acc_ref[0] / l_ref[0] - lam * (acc_ref[1] / l_ref[1])
        o = _rms(o, sg_ref[...], NORM_EPS) * (1.0 - lam_init)
        o_ref[...] = o.astype(o_ref.dtype)


def diff_attention(qkv, slopes, lq1, lk1, lq2, lk2, subln_g, *, batch, seq, tq, lam_init):
    t = batch * seq
    nq = seq // tq
    hb = DIFF_WIDTH // LANES
    small = pl.BlockSpec((1, HEAD_DIM), lambda b, h, qi, ki: (0, 0))
    return pl.pallas_call(
        functools.partial(_diff_attn_kernel, tq=tq, lam_init=lam_init),
        grid=(batch, DIFF_HEADS, nq, nq),
        in_specs=[
            pl.BlockSpec(memory_space=pltpu.SMEM),
            pl.BlockSpec((tq, LANES), lambda b, h, qi, ki: (b * nq + qi, h)),
            pl.BlockSpec((tq, LANES), lambda b, h, qi, ki: (b * nq + jnp.minimum(ki, qi), hb + h)),
            pl.BlockSpec((tq, LANES), lambda b, h, qi, ki: (b * nq + jnp.minimum(ki, qi), 2 * hb + h)),
            small, small, small, small,
            pl.BlockSpec((1, LANES), lambda b, h, qi, ki: (0, 0)),
        ],
        out_specs=pl.BlockSpec((tq, LANES), lambda b, h, qi, ki: (b * nq + qi, h)),
        out_shape=jax.ShapeDtypeStruct((t, DIFF_WIDTH), BF16),
        scratch_shapes=[pltpu.VMEM((2, tq, 1), F32), pltpu.VMEM((2, tq, 1), F32),
                        pltpu.VMEM((2, tq, LANES), F32)],
        compiler_params=_cparams(("parallel", "parallel", "parallel", "arbitrary")),
        name="diff_attention",
    )(slopes, qkv, qkv, qkv, lq1.reshape(1, -1), lk1.reshape(1, -1), lq2.reshape(1, -1), lk2.reshape(1, -1),
      subln_g.reshape(1, -1))


def _mix_kernel(ra_ref, da_ref, ga_ref, gb_ref, x_ref, wa_ref, wb_ref, wo_ref, o_ref):
    ya = _dot(ra_ref[0], wa_ref[0:LANES, :])
    for q in range(1, RWKV_PAIRS):
        ya = ya + _dot(ra_ref[q], wa_ref[q * LANES:(q + 1) * LANES, :])
    yb = _dot(da_ref[...], wb_ref[...])
    mixed = ga_ref[...].astype(F32) * ya + gb_ref[...].astype(F32) * yb
    o_ref[...] = x_ref[...] + _dot(mixed.astype(BF16), wo_ref[...])


def mix_project(ra, da, gates, x, wa, wb, wo, *, tm):
    t, d = x.shape
    const = lambda a: pl.BlockSpec(a.shape, lambda i: (0, 0), pipeline_mode=pl.Buffered(1))
    return pl.pallas_call(
        _mix_kernel,
        grid=(t // tm,),
        in_specs=[
            pl.BlockSpec((RWKV_PAIRS, tm, LANES), lambda i: (0, i, 0)),
            pl.BlockSpec((tm, DIFF_WIDTH), lambda i: (i, 0)),
            pl.BlockSpec((tm, d), lambda i: (i, 0)),
            pl.BlockSpec((tm, d), lambda i: (i, 1)),
            pl.BlockSpec((tm, d), lambda i: (i, 0)),
            const(wa), const(wb), const(wo),
        ],
        out_specs=pl.BlockSpec((tm, d), lambda i: (i, 0)),
        out_shape=jax.ShapeDtypeStruct((t, d), F32),
        compiler_params=_cparams(("parallel",)),
        name="mix_project",
    )(ra, da, gates, gates, x, wa, wb, wo)


def _cross_kernel(x_ref, gc_ref, wq_ref, kv_ref, wo_ref, gf_ref, rw_ref, rb_ref, x2_ref, hf_ref, lg_ref):
    x = x_ref[...]
    h = _rms(x, gc_ref[...], NORM_EPS).astype(BF16)
    q = _dot(h, wq_ref[...]).astype(BF16)
    scale = LANES ** -0.5
    outs = []
    for hd in range(CROSS_HEADS):
        qh = q[:, hd * LANES:(hd + 1) * LANES]
        kh = kv_ref[0, :, hd * LANES:(hd + 1) * LANES]
        vh = kv_ref[0, :, CROSS_WIDTH + hd * LANES:CROSS_WIDTH + (hd + 1) * LANES]
        s = _dot(qh, kh, NT) * scale
        s = s - jnp.max(s, axis=-1, keepdims=True)
        e = jnp.exp(s)
        p = e / jnp.sum(e, axis=-1, keepdims=True)
        outs.append(_dot(p.astype(BF16), vh))
    o = jnp.concatenate(outs, axis=1).astype(BF16)
    x2 = x + _dot(o, wo_ref[...])
    x2_ref[...] = x2
    hf = _rms(x2, gf_ref[...], NORM_EPS)
    tm = hf.shape[0]
    for sidx in range(ROW_TILE):
        hf_ref[pl.ds(sidx, tm, stride=ROW_TILE), :] = hf[:, sidx * LANES:(sidx + 1) * LANES]
    lg_ref[...] = _dot3(hf, rw_ref[...]) + rb_ref[...]


def cross_attention(x, gc, wq, kv, wo, gf, rw, rb, *, seq, tm):
    t, d = x.shape
    per_seq = seq // tm
    const = lambda a: pl.BlockSpec(a.shape, lambda i: (0,) * a.ndim, pipeline_mode=pl.Buffered(1))
    vec = lambda n: pl.BlockSpec((1, n), lambda i: (0, 0))
    return pl.pallas_call(
        _cross_kernel,
        grid=(t // tm,),
        in_specs=[
            pl.BlockSpec((tm, d), lambda i: (i, 0)),
            vec(d), const(wq),
            pl.BlockSpec((1,) + kv.shape[1:], lambda i: (i // per_seq, 0, 0)),
            const(wo), vec(d), const(rw), vec(LANES),
        ],
        out_specs=[pl.BlockSpec((tm, d), lambda i: (i, 0)), pl.BlockSpec((tm * ROW_TILE, LANES), lambda i: (i, 0)),
                   pl.BlockSpec((tm, LANES), lambda i: (i, 0))],
        out_shape=[jax.ShapeDtypeStruct((t, d), F32), jax.ShapeDtypeStruct((t * ROW_TILE, LANES), F32),
                   jax.ShapeDtypeStruct((t, LANES), F32)],
        compiler_params=_cparams(("parallel",)),
        name="cross_attention",
    )(x, gc.reshape(1, d), wq, kv, wo, gf.reshape(1, d), rw, rb)


def _router_kernel(lg_ref, idx_ref, gate_ref):
    x = lg_ref[...]
    lane = lax.broadcasted_iota(jnp.int32, x.shape, 1)
    x = jnp.where(lane < N_EXPERTS, x, -jnp.inf)
    idx_out = jnp.zeros(x.shape, jnp.int32)
    val_out = jnp.zeros(x.shape, F32)
    vals = []
    for j in range(TOP_K):
        m = jnp.max(x, axis=-1, keepdims=True)
        sel = jnp.min(jnp.where(x == m, lane, LANES), axis=-1, keepdims=True)
        idx_out = jnp.where(lane == j, sel, idx_out)
        vals.append(m)
        x = jnp.where(lane == sel, -jnp.inf, x)
    es = [jnp.exp(vj - vals[0]) for vj in vals]
    tot = es[0] + es[1] + es[2] + es[3]
    for j in range(TOP_K):
        val_out = jnp.where(lane == j, es[j] / tot, val_out)
    idx_ref[...] = idx_out[:, 0:TOP_K]
    gate_ref[...] = val_out[:, 0:TOP_K]


def router_topk(logits, *, tm):
    t = logits.shape[0]
    return pl.pallas_call(
        _router_kernel,
        grid=(t // tm,),
        in_specs=[pl.BlockSpec((tm, LANES), lambda i: (i, 0))],
        out_specs=[pl.BlockSpec((tm, TOP_K), lambda i: (i, 0)), pl.BlockSpec((tm, TOP_K), lambda i: (i, 0))],
        out_shape=[jax.ShapeDtypeStruct((t, TOP_K), jnp.int32), jax.ShapeDtypeStruct((t, TOP_K), F32)],
        compiler_params=_cparams(("parallel",)),
        name="router_topk",
    )(logits)


def _moe_kernel(sbe_ref, sbr_ref, src_ref, gate_ref, hf_hbm, w1g_ref, w1l_ref, b1g_ref, b1l_ref, w2_ref, b2_ref,
                ys_hbm, xbuf, x2d, acc, obuf, gsem, ssem):
    s = pl.program_id(0)
    f = pl.program_id(1)
    nf = pl.num_programs(1)
    rows = sbr_ref[s]
    nsub = (rows + (MOE_SUB - 1)) // MOE_SUB

    def slab(ref, r):
        return ref.at[pl.ds(pl.multiple_of(r * ROW_TILE, ROW_TILE), ROW_TILE)]

    def row_in(r):
        tok = jnp.maximum(src_ref[0, 0, r], 0) // TOP_K
        return pltpu.make_async_copy(slab(hf_hbm, tok), slab(xbuf, r), gsem)

    def row_out(r):
        return pltpu.make_async_copy(slab(obuf, r), slab(ys_hbm, src_ref[0, 0, r]), ssem)

    @pl.when((f == 0) & (rows > 0))
    def _():
        n = nsub * MOE_SUB

        def start(r, c):
            row_in(r).start()
            return c

        def wait(r, c):
            row_in(r).wait()
            return c

        lax.fori_loop(0, n, start, 0)
        lax.fori_loop(0, n, wait, 0)

    for sub in range(MOE_SUPER // MOE_SUB):
        @pl.when(sub < nsub)
        def _(sub=sub):
            sl = slice(sub * MOE_SUB, (sub + 1) * MOE_SUB)
            base = sub * MOE_SUB * ROW_TILE

            @pl.when(f == 0)
            def _():
                for sidx in range(ROW_TILE):
                    piece = xbuf[pl.ds(base + sidx, MOE_SUB, stride=ROW_TILE), :]
                    x2d[sl, sidx * LANES:(sidx + 1) * LANES] = piece.astype(BF16)

            x = x2d[sl, :]
            hg = _dot(x, w1g_ref[0]) + b1g_ref[0]
            hl = _dot(x, w1l_ref[0]) + b1l_ref[0]
            xg = jnp.minimum(hg, SWIGLU_LIMIT)
            xl = jnp.clip(hl, -SWIGLU_LIMIT, SWIGLU_LIMIT)
            act = xg * jax.nn.sigmoid(SWIGLU_ALPHA * xg) * (xl + 1.0)
            contrib = _dot(act.astype(BF16), w2_ref[0])

            @pl.when(f == 0)
            def _():
                acc[sl, :] = contrib

            @pl.when(f > 0)
            def _():
                acc[sl, :] += contrib

            @pl.when(f == nf - 1)
            def _():
                y = (acc[sl, :] + b2_ref[0]) * gate_ref[sl, :]
                for sidx in range(ROW_TILE):
                    obuf[pl.ds(base + sidx, MOE_SUB, stride=ROW_TILE), :] = y[:, sidx * LANES:(sidx + 1) * LANES]

    @pl.when((f == nf - 1) & (rows > 0))
    def _():
        def start(r, c):
            row_out(r).start()
            return c

        def wait(r, c):
            row_out(r).wait()
            return c

        lax.fori_loop(0, rows, start, 0)
        lax.fori_loop(0, rows, wait, 0)


def moe_experts(sb_e, sb_rows, row_src, row_gate, hf, w1g, w1l, b1g, b1l, w2, b2):
    d = D_MODEL
    t = hf.shape[0] // ROW_TILE
    nsb = sb_e.shape[0]
    nf = D_EXPERT // MOE_FT
    grid_spec = pltpu.PrefetchScalarGridSpec(
        num_scalar_prefetch=2,
        grid=(nsb, nf),
        in_specs=[
            pl.BlockSpec((1, 1, MOE_SUPER), lambda s, f, e, r: (s, 0, 0), memory_space=pltpu.SMEM),
            pl.BlockSpec((MOE_SUPER, 1), lambda s, f, e, r: (s, 0)),
            pl.BlockSpec(memory_space=pl.ANY),
            pl.BlockSpec((1, d, MOE_FT), lambda s, f, e, r: (e[s], 0, f)),
            pl.BlockSpec((1, d, MOE_FT), lambda s, f, e, r: (e[s], 0, f)),
            pl.BlockSpec((1, 1, MOE_FT), lambda s, f, e, r: (e[s], 0, f)),
            pl.BlockSpec((1, 1, MOE_FT), lambda s, f, e, r: (e[s], 0, f)),
            pl.BlockSpec((1, MOE_FT, d), lambda s, f, e, r: (e[s], f, 0)),
            pl.BlockSpec((1, 1, d), lambda s, f, e, r: (e[s], 0, 0)),
        ],
        out_specs=pl.BlockSpec(memory_space=pl.ANY),
        scratch_shapes=[pltpu.VMEM((MOE_SUPER * ROW_TILE, LANES), F32), pltpu.VMEM((MOE_SUPER, d), BF16),
                        pltpu.VMEM((MOE_SUPER, d), F32), pltpu.VMEM((MOE_SUPER * ROW_TILE, LANES), F32),
                        pltpu.SemaphoreType.DMA, pltpu.SemaphoreType.DMA],
    )
    return pl.pallas_call(
        _moe_kernel,
        grid_spec=grid_spec,
        out_shape=jax.ShapeDtypeStruct((t * TOP_K * ROW_TILE, LANES), F32),
        compiler_params=_cparams(("arbitrary", "arbitrary")),
        name="moe_experts",
    )(sb_e, sb_rows, row_src, row_gate, hf, w1g, w1l, b1g, b1l, w2, b2)


def _combine_kernel(ys_ref, x_ref, g_ref, o_ref, tmp_ref):
    tm = x_ref.shape[0]
    tot = ys_ref[:, 0:ROW_TILE, :]
    for j in range(1, TOP_K):
        tot = tot + ys_ref[:, j * ROW_TILE:(j + 1) * ROW_TILE, :]
    tmp_ref[...] = tot.reshape(tm * ROW_TILE, LANES)
    pieces = [tmp_ref[pl.ds(sidx, tm, stride=ROW_TILE), :] for sidx in range(ROW_TILE)]
    out = x_ref[...] + jnp.concatenate(pieces, axis=1)
    o_ref[...] = _rms(out, g_ref[...], NORM_EPS)


def combine_final(ys, x, g, *, tm):
    t, d = x.shape
    return pl.pallas_call(
        _combine_kernel,
        grid=(t // tm,),
        in_specs=[pl.BlockSpec((tm, TOP_K * ROW_TILE, LANES), lambda i: (i, 0, 0)),
                  pl.BlockSpec((tm, d), lambda i: (i, 0)), pl.BlockSpec((1, d), lambda i: (0, 0))],
        out_specs=pl.BlockSpec((tm, d), lambda i: (i, 0)),
        out_shape=jax.ShapeDtypeStruct((t, d), F32),
        scratch_shapes=[pltpu.VMEM((tm * ROW_TILE, LANES), F32)],
        compiler_params=_cparams(("parallel",)),
        name="combine_final",
    )(ys, x, g.reshape(1, d))


def _routing_tables(top_idx, gate, n_super):
    flat_e = top_idx.reshape(-1)
    n = flat_e.shape[0]
    onehot = (flat_e[:, None] == jnp.arange(N_EXPERTS, dtype=jnp.int32)[None, :]).astype(jnp.int32)
    csum = jnp.cumsum(onehot, axis=0)
    rank = jnp.sum(onehot * csum, axis=1) - 1
    counts = csum[-1]
    nsb = (counts + MOE_SUPER - 1) // MOE_SUPER
    sb_end = jnp.cumsum(nsb)
    sb_start = sb_end - nsb
    dest = sb_start[flat_e] * MOE_SUPER + rank
    row_src = jnp.full((n_super * MOE_SUPER,), -1, jnp.int32).at[dest].set(jnp.arange(n, dtype=jnp.int32))
    row_gate = jnp.zeros((n_super * MOE_SUPER,), F32).at[dest].set(gate.reshape(-1))
    s_ids = jnp.arange(n_super, dtype=jnp.int32)
    sb_e = jnp.minimum(jnp.searchsorted(sb_end, s_ids, side="right"), N_EXPERTS - 1).astype(jnp.int32)
    local = s_ids - sb_start[sb_e]
    sb_rows = jnp.clip(counts[sb_e] - local * MOE_SUPER, 0, MOE_SUPER)
    sb_rows = jnp.where(s_ids < sb_end[-1], sb_rows, 0).astype(jnp.int32)
    return sb_e, sb_rows, row_src.reshape(n_super, 1, MOE_SUPER), row_gate.reshape(-1, 1)


def _pad_rows(a, n):
    return jnp.pad(a, ((0, n - a.shape[0]), (0, 0)))


def _layer(x, mem, l, p, batch, seq):
    t = batch * seq
    c = RWKV_WIDTH
    lam_init = 0.8 - 0.6 * math.exp(-0.3 * l)
    w_in = p["w_in"]
    o1 = 3 * c + DECAY_LORA + AAA_LORA + GATE_LORA
    o2 = o1 + 3 * DIFF_WIDTH
    padc = lambda a, n: jnp.pad(a, ((0, 0), (0, n - a.shape[1])))
    w_rwkv = jnp.concatenate([
        w_in[:, :3 * c],
        padc(w_in[:, 3 * c:3 * c + DECAY_LORA], 128),
        padc(w_in[:, 3 * c + DECAY_LORA:3 * c + DECAY_LORA + AAA_LORA], 128),
        padc(w_in[:, 3 * c + DECAY_LORA + AAA_LORA:o1], 256)], axis=1).astype(BF16)
    mu = p["rwkv_mu"]
    pad1 = lambda a, n: jnp.pad(a, (0, n - a.shape[0]))
    mu_p = jnp.concatenate([mu[:3 * c], pad1(mu[3 * c:3 * c + DECAY_LORA], 128),
                            pad1(mu[3 * c + DECAY_LORA:3 * c + DECAY_LORA + AAA_LORA], 128),
                            pad1(mu[3 * c + DECAY_LORA + AAA_LORA:], 256)])
    w_diff = w_in[:, o1:o2].astype(BF16)
    w_gate = w_in[:, o2:].astype(BF16)

    g_mix = p["norm_mix_g"]
    p_rwkv = norm_matmul(x, g_mix, w_rwkv, tm=1024, tn=RWKV_COLS // 2, out_dtype=F32, name="in_proj_rwkv")
    qkv = norm_matmul(x, g_mix, w_diff, tm=1024, tn=1024, out_dtype=BF16, name="in_proj_diff")
    gates = norm_matmul(x, g_mix, w_gate, tm=1024, tn=1024, out_dtype=BF16, act="sigmoid", name="in_proj_gate")

    prep = rwkv_prep(p_rwkv, mu_p, p["rwkv_w0"], p["rwkv_a0"], p["rwkv_k_k"], p["rwkv_k_a"],
                     _pad_rows(p["rwkv_w2"], 128), _pad_rows(p["rwkv_a2"], 128), _pad_rows(p["rwkv_g2"], 256),
                     seq=seq, tm=256)
    ra = rwkv_scan(*prep, p["rwkv_ln_w"], p["rwkv_ln_b"], p["rwkv_r_k"].reshape(-1), batch=batch, seq=seq)

    slopes = (2.0 ** (-8.0 * jnp.arange(1, DIFF_HEADS + 1, dtype=F32) / DIFF_HEADS)).astype(F32)
    da = diff_attention(qkv, slopes, p["diff_lq1"], p["diff_lk1"], p["diff_lq2"], p["diff_lk2"],
                        p["diff_subln_g"], batch=batch, seq=seq, tq=min(512, seq), lam_init=lam_init)

    x1 = mix_project(ra, da, gates, x, p["rwkv_proj"].astype(BF16), p["diff_proj"].astype(BF16),
                     p["w_out"].astype(BF16), tm=256)

    m_len = mem.shape[0] // batch
    kv = norm_matmul(mem, p["norm_mem_g"], p["cross_wkv"].astype(BF16), tm=min(512, mem.shape[0]),
                     tn=2 * CROSS_WIDTH, out_dtype=BF16, name="cross_kv")
    rw = jnp.pad(p["router_w"], ((0, 0), (0, LANES - N_EXPERTS)))
    rb = jnp.pad(p["router_b"], (0, LANES - N_EXPERTS)).reshape(1, LANES)
    x2, hf, logits = cross_attention(x1, p["norm_cross_g"], p["cross_wq"].astype(BF16),
                                     kv.reshape(batch, m_len, 2 * CROSS_WIDTH), p["cross_wo"].astype(BF16),
                                     p["norm_ffn_g"], rw, rb, seq=seq, tm=256)

    top_idx, gate = router_topk(logits, tm=min(1024, t))
    n_super = (t * TOP_K) // MOE_SUPER + N_EXPERTS
    sb_e, sb_rows, row_src, row_gate = _routing_tables(top_idx, gate, n_super)
    w1 = p["expert_w1"]
    b1 = p["expert_b1"]
    ys = moe_experts(sb_e, sb_rows, row_src, row_gate, hf,
                     w1[:, :, 0::2].astype(BF16), w1[:, :, 1::2].astype(BF16),
                     b1[:, None, 0::2], b1[:, None, 1::2],
                     p["expert_w2"].astype(BF16), p["expert_b2"][:, None, :])
    return x2, ys.reshape(t, TOP_K * ROW_TILE, LANES)


def kernel(x, mem, norm_mix_g, w_in, rwkv_mu, rwkv_w0, rwkv_w2, rwkv_a0, rwkv_a2, rwkv_g2, rwkv_k_k, rwkv_k_a, rwkv_r_k, rwkv_ln_w, rwkv_ln_b, rwkv_proj, diff_lq1, diff_lk1, diff_lq2, diff_lk2, diff_subln_g, diff_proj, w_out, norm_cross_g, norm_mem_g, cross_wq, cross_wkv, cross_wo, norm_ffn_g, router_w, router_b, expert_w1, expert_b1, expert_w2, expert_b2, final_norm_g):
    batch, seq, d = x.shape
    stacked = dict(norm_mix_g=norm_mix_g, w_in=w_in, rwkv_mu=rwkv_mu, rwkv_w0=rwkv_w0, rwkv_w2=rwkv_w2,
                   rwkv_a0=rwkv_a0, rwkv_a2=rwkv_a2, rwkv_g2=rwkv_g2, rwkv_k_k=rwkv_k_k, rwkv_k_a=rwkv_k_a,
                   rwkv_r_k=rwkv_r_k, rwkv_ln_w=rwkv_ln_w, rwkv_ln_b=rwkv_ln_b, rwkv_proj=rwkv_proj,
                   diff_lq1=diff_lq1, diff_lk1=diff_lk1, diff_lq2=diff_lq2, diff_lk2=diff_lk2,
                   diff_subln_g=diff_subln_g, diff_proj=diff_proj, w_out=w_out, norm_cross_g=norm_cross_g,
                   norm_mem_g=norm_mem_g, cross_wq=cross_wq, cross_wkv=cross_wkv, cross_wo=cross_wo,
                   norm_ffn_g=norm_ffn_g, router_w=router_w, router_b=router_b, expert_w1=expert_w1,
                   expert_b1=expert_b1, expert_w2=expert_w2, expert_b2=expert_b2)
    assert w_in.shape[0] == 1, "the closing RMSNorm is fused into the single layer's combine"
    p = {k: v[0] for k, v in stacked.items()}
    x2, ys = _layer(x.reshape(batch * seq, d), mem.reshape(-1, d), 0, p, batch, seq)
    out = combine_final(ys, x2, final_norm_g, tm=256)
    return out.reshape(batch, seq, d)
```

```python
import functools
import math

import jax
import jax.numpy as jnp
from jax import lax
from jax.experimental import pallas as pl
from jax.experimental.pallas import tpu as pltpu

F32 = jnp.float32
BF16 = jnp.bfloat16

D_MODEL = 2048
NORM_EPS = 1e-5
LOG2E = 1.4426950408889634
LANES = 128
HEAD_DIM = 64
RWKV_WIDTH = 1024
RWKV_PAIRS = RWKV_WIDTH // LANES
CHUNK = 64
RWKV_GN_EPS = 64e-5
DECAY_LORA = 64
AAA_LORA = 64
GATE_LORA = 160
LORA_PAD = 512
RWKV_COLS = 3 * RWKV_WIDTH + LORA_PAD
DIFF_WIDTH = 1024
DIFF_HEADS = 8
CROSS_HEADS = 4
CROSS_WIDTH = 512
N_EXPERTS = 32
TOP_K = 4
D_EXPERT = 2048
SWIGLU_ALPHA = 1.702
SWIGLU_LIMIT = 7.0
MOE_SUPER = 1024
MOE_SUB = 256
MOE_FT = 512
ROW_TILE = D_MODEL // LANES
VMEM_LIMIT = 56 * 1024 * 1024

NN = (((1,), (0,)), ((), ()))
NT = (((1,), (1,)), ((), ()))


def _dot(a, b, dims=NN):
    return lax.dot_general(a, b, dims, preferred_element_type=F32)


def _split2(a):
    hi = a.astype(BF16)
    lo = (a - hi.astype(F32)).astype(BF16)
    return hi, lo


def _split3(a):
    hi = a.astype(BF16)
    r1 = a - hi.astype(F32)
    mid = r1.astype(BF16)
    lo = (r1 - mid.astype(F32)).astype(BF16)
    return hi, mid, lo


def _dot3(a, b, dims=NN):
    ah, al = _split2(a)
    bh, bl = _split2(b)
    return _dot(ah, bh, dims) + (_dot(ah, bl, dims) + _dot(al, bh, dims))


def _dot_exact_rhs(a, b_bf16, dims=NN):
    h, m, l = _split3(a)
    return _dot(h, b_bf16, dims) + (_dot(m, b_bf16, dims) + _dot(l, b_bf16, dims))


def _rms(x, g, eps):
    ms = jnp.mean(x * x, axis=-1, keepdims=True)
    return x * lax.rsqrt(ms + eps) * g


def _cparams(sem):
    return pltpu.CompilerParams(dimension_semantics=sem, vmem_limit_bytes=VMEM_LIMIT)


def _norm_matmul_kernel(x_ref, g_ref, w_ref, o_ref, h_ref, *, act):
    @pl.when(pl.program_id(1) == 0)
    def _():
        h_ref[...] = _rms(x_ref[...], g_ref[...], NORM_EPS).astype(BF16)

    y = _dot(h_ref[...], w_ref[...])
    if act == "sigmoid":
        y = jax.nn.sigmoid(y)
    o_ref[...] = y.astype(o_ref.dtype)


def norm_matmul(x, g, w, *, tm, tn, out_dtype, act=None, name):
    m, d = x.shape
    n = w.shape[1]
    return pl.pallas_call(
        functools.partial(_norm_matmul_kernel, act=act),
        grid=(m // tm, n // tn),
        in_specs=[
            pl.BlockSpec((tm, d), lambda i, j: (i, 0)),
            pl.BlockSpec((1, d), lambda i, j: (0, 0)),
            pl.BlockSpec((d, tn), lambda i, j: (0, j)),
        ],
        out_specs=pl.BlockSpec((tm, tn), lambda i, j: (i, j)),
        out_shape=jax.ShapeDtypeStruct((m, n), out_dtype),
        scratch_shapes=[pltpu.VMEM((tm, d), BF16)],
        compiler_params=_cparams(("parallel", "arbitrary")),
        name=name,
    )(x, g.reshape(1, d), w)


def _head_ones():
    r = lax.broadcasted_iota(jnp.int32, (LANES, LANES), 0)
    c = lax.broadcasted_iota(jnp.int32, (LANES, LANES), 1)
    return ((r // HEAD_DIM) == (c // HEAD_DIM)).astype(BF16)


def _rwkv_prep_kernel(p_ref, prev_ref, mu_ref, w0_ref, a0_ref, kk_ref, ka_ref, w2_ref, a2_ref, g2_ref,
                      r_out, lw_out, k_out, v_out, kkn_out, b_out, g_out, *, tiles_per_seq):
    c = RWKV_WIDTH
    i = pl.program_id(0)
    p = p_ref[...]
    tm = p.shape[0]
    first = (i % tiles_per_seq) == 0
    prev_row = jnp.where(first, 0.0, prev_ref[7:8, :])
    row = lax.broadcasted_iota(jnp.int32, p.shape, 0)
    shifted = jnp.where(row == 0, prev_row, pltpu.roll(p, 1, axis=0))
    ps = p + (shifted - p) * mu_ref[...]
    r = ps[:, 0:c]
    k = ps[:, c:2 * c]
    v = ps[:, 2 * c:3 * c]
    wd = ps[:, 3 * c:3 * c + 128]
    ad = ps[:, 3 * c + 128:3 * c + 256]
    gd = ps[:, 3 * c + 256:3 * c + 512]
    z = -(w0_ref[...] + _dot3(jnp.tanh(wd), w2_ref[...]))
    softplus = jnp.maximum(z, 0.0) + jnp.log1p(jnp.exp(-jnp.abs(z)))
    w = -softplus - 0.5
    lw = -jnp.exp(w)
    a = jax.nn.sigmoid(a0_ref[...] + _dot3(ad, a2_ref[...]))
    g = _dot3(jax.nn.sigmoid(gd), g2_ref[...])
    kkr = k * kk_ref[...]
    k2 = k * (1.0 + (a - 1.0) * ka_ref[...])
    ones = _head_ones()
    for q in range(RWKV_PAIRS):
        sl = slice(q * LANES, (q + 1) * LANES)
        x = kkr[:, sl]
        ss = _dot_exact_rhs(x * x, ones)
        kkn = x / jnp.maximum(jnp.sqrt(ss), 1e-12)
        r_out[q] = r[:, sl]
        lw_out[q] = lw[:, sl]
        k_out[q] = k2[:, sl]
        v_out[q] = v[:, sl]
        kkn_out[q] = kkn
        b_out[q] = kkn * a[:, sl]
        g_out[q] = g[:, sl]


def rwkv_prep(p, mu, w0, a0, k_k, k_a, w2p, a2p, g2p, *, seq, tm):
    t, cols = p.shape
    c = RWKV_WIDTH
    vec = lambda n: pl.BlockSpec((1, n), lambda i: (0, 0))
    full = lambda a: pl.BlockSpec(a.shape, lambda i: (0, 0))
    out_spec = pl.BlockSpec((RWKV_PAIRS, tm, LANES), lambda i: (0, i, 0))
    out_shape = jax.ShapeDtypeStruct((RWKV_PAIRS, t, LANES), F32)
    return pl.pallas_call(
        functools.partial(_rwkv_prep_kernel, tiles_per_seq=seq // tm),
        grid=(t // tm,),
        in_specs=[
            pl.BlockSpec((tm, cols), lambda i: (i, 0)),
            pl.BlockSpec((8, cols), lambda i: (jnp.maximum(i * (tm // 8) - 1, 0), 0)),
            vec(cols), vec(c), vec(c), vec(c), vec(c), full(w2p), full(a2p), full(g2p),
        ],
        out_specs=[out_spec] * 7,
        out_shape=[out_shape] * 7,
        compiler_params=_cparams(("parallel",)),
        name="rwkv_prep",
    )(p, p, mu.reshape(1, cols), w0.reshape(1, c), a0.reshape(1, c), k_k.reshape(1, c), k_a.reshape(1, c),
      w2p, a2p, g2p)


def _block_diag_rows(x, m0):
    return jnp.concatenate([jnp.where(m0, x, 0.0), jnp.where(m0, 0.0, x)], axis=0)


def _rwkv_scan_kernel(r_ref, lw_ref, k_ref, v_ref, kk_ref, b_ref, g_ref, lnw_ref, lnb_ref, rk_ref,
                      o_ref, st_ref):
    L = CHUNK

    @pl.when(pl.program_id(1) == 0)
    def _():
        st_ref[...] = jnp.zeros_like(st_ref)

    t_i = lax.broadcasted_iota(jnp.int32, (L, LANES), 0)
    lane = lax.broadcasted_iota(jnp.int32, (L, LANES), 1)
    j_i = lane % HEAD_DIM
    m0 = lane < HEAD_DIM
    strict = j_i < t_i
    incl = j_i <= t_i
    eye = (j_i == t_i).astype(F32)
    r2 = lax.broadcasted_iota(jnp.int32, (LANES, LANES), 0)
    c2 = lax.broadcasted_iota(jnp.int32, (LANES, LANES), 1)
    same_head = (r2 // HEAD_DIM) == (c2 // HEAD_DIM)
    diag = r2 == c2
    ones = same_head.astype(BF16)
    tr = lax.broadcasted_iota(jnp.int32, (L, L), 0)
    tc = lax.broadcasted_iota(jnp.int32, (L, L), 1)
    tri = (tc <= tr).astype(BF16)
    zero = jnp.zeros((L, LANES), F32)

    def pair(q, carry):
        r = r_ref[q]
        lw = lw_ref[q]
        k = k_ref[q]
        v = v_ref[q]
        kk = kk_ref[q]
        b = b_ref[q]
        h3, m3, l3 = _split3(lw)
        cum = _dot(tri, h3) + (_dot(tri, m3) + _dot(tri, l3))
        cum_prev = cum - lw
        cum_end = cum[L - 1:L, :]
        e_cum = jnp.exp(cum)
        e_neg = jnp.exp(-cum)
        e_end = jnp.exp(cum_end - cum)
        at = -kk * jnp.exp(cum_prev)
        rt = r * e_cum
        bt = b * e_neg
        kt = k * e_neg
        bh = b * e_end
        kh = k * e_end
        w_end = jnp.exp(cum_end)

        bd = lambda x: _block_diag_rows(x, m0)
        g_all = _dot3(jnp.concatenate([at, rt], axis=0),
                      jnp.concatenate([bd(bt), bd(kt)], axis=0), NT)
        a_ab = jnp.where(strict, g_all[0:L, 0:LANES], 0.0)
        a_ak = jnp.where(strict, g_all[0:L, LANES:], 0.0)
        a_rb = jnp.where(incl, g_all[L:, 0:LANES], 0.0)
        a_rk = jnp.where(incl, g_all[L:, LANES:], 0.0)

        tinv = eye
        pw = a_ab
        for it in range(6):
            if it < 5:
                res = _dot3(pw, jnp.concatenate([bd(tinv), bd(pw)], axis=1))
                tinv = tinv + res[:, 0:LANES]
                pw = res[:, LANES:]
            else:
                tinv = tinv + _dot3(pw, bd(tinv))

        av = _dot3(a_ak, bd(v))
        qp = _dot3(tinv, jnp.concatenate([bd(at), bd(av)], axis=1))
        q1 = qp[:, 0:LANES]
        p1 = qp[:, LANES:]
        q2 = rt + _dot3(a_rb, bd(q1))
        p2 = _dot3(a_rb, bd(p1)) + _dot3(a_rk, bd(v))
        lhs_t = jnp.concatenate([bh, kh], axis=0).T
        rhs = jnp.concatenate([jnp.concatenate([q1, p1], axis=1),
                               jnp.concatenate([zero, v], axis=1)], axis=0)
        mp = _dot3(lhs_t, rhs)
        m_bd = jnp.where(same_head, mp[:, 0:LANES], 0.0) + jnp.where(diag, w_end, 0.0)
        p3 = jnp.where(same_head, mp[:, LANES:], 0.0)
        st = st_ref[q]
        y = _dot3(q2, st) + p2
        st_ref[q] = _dot3(m_bd, st) + p3

        mean = _dot_exact_rhs(y, ones) * (1.0 / HEAD_DIM)
        yc = y - mean
        var = _dot_exact_rhs(yc * yc, ones) * (1.0 / HEAD_DIM)
        yn = yc * lax.rsqrt(var + RWKV_GN_EPS) * lnw_ref[q] + lnb_ref[q]
        bonus = _dot_exact_rhs(r * k * rk_ref[q], ones) * v
        o_ref[q] = ((yn + bonus) * g_ref[q]).astype(o_ref.dtype)
        return carry

    lax.fori_loop(0, RWKV_PAIRS, pair, 0, unroll=4)


def rwkv_scan(r, lw, k, v, kk, b, g, ln_w, ln_b, r_k, *, batch, seq):
    nchunk = seq // CHUNK
    blk = pl.BlockSpec((RWKV_PAIRS, CHUNK, LANES), lambda bi, ci: (0, bi * nchunk + ci, 0))
    par = pl.BlockSpec((RWKV_PAIRS, 1, LANES), lambda bi, ci: (0, 0, 0))
    t = batch * seq
    return pl.pallas_call(
        _rwkv_scan_kernel,
        grid=(batch, nchunk),
        in_specs=[blk] * 7 + [par] * 3,
        out_specs=blk,
        out_shape=jax.ShapeDtypeStruct((RWKV_PAIRS, t, LANES), BF16),
        scratch_shapes=[pltpu.VMEM((RWKV_PAIRS, LANES, LANES), F32)],
        compiler_params=_cparams(("arbitrary", "arbitrary")),
        name="rwkv_scan",
    )(r, lw, k, v, kk, b, g, ln_w.reshape(RWKV_PAIRS, 1, LANES), ln_b.reshape(RWKV_PAIRS, 1, LANES),
      r_k.reshape(RWKV_PAIRS, 1, LANES))


def _diff_attn_kernel(slope_ref, q_ref, k_ref, v_ref, lq1_ref, lk1_ref, lq2_ref, lk2_ref, sg_ref,
                      o_ref, qs_ref, relb_ref, m_ref, l_ref, acc_ref, *, tq, lam_init):
    h = pl.program_id(1)
    qi = pl.program_id(2)
    ki = pl.program_id(3)
    c2 = slope_ref[h] * LOG2E

    @pl.when(ki == 0)
    def _():
        m_ref[...] = jnp.full_like(m_ref, -jnp.inf)
        l_ref[...] = jnp.zeros_like(l_ref)
        acc_ref[...] = jnp.zeros_like(acc_ref)
        q = q_ref[...].astype(F32) * (HEAD_DIM ** -0.5 * LOG2E)
        lane = lax.broadcasted_iota(jnp.int32, q.shape, 1)
        m0 = lane < HEAD_DIM
        qs_ref[0] = jnp.where(m0, q, 0.0).astype(BF16)
        qs_ref[1] = jnp.where(m0, 0.0, q).astype(BF16)
        rel = (lax.broadcasted_iota(jnp.int32, (tq, tq), 0) - lax.broadcasted_iota(jnp.int32, (tq, tq), 1))
        relb_ref[...] = rel.astype(F32) * (-c2)

    def step(masked):
        k = k_ref[...]
        v = v_ref[...]
        relb = relb_ref[...]
        tile_bias = c2 * ((qi - ki) * tq).astype(F32)
        for idx in range(2):
            s = _dot(qs_ref[idx], k, NT) + relb
            if masked:
                s = jnp.where(relb > 0.0, -jnp.inf, s)
            m_prev = m_ref[idx]
            m_new = jnp.maximum(m_prev, jnp.max(s, axis=-1, keepdims=True) - tile_bias)
            alpha = jnp.exp2(m_prev - m_new)
            p = jnp.exp2(s - (m_new + tile_bias))
            l_ref[idx] = alpha * l_ref[idx] + jnp.sum(p, axis=-1, keepdims=True)
            acc_ref[idx] = alpha * acc_ref[idx] + _dot(p.astype(BF16), v)
            m_ref[idx] = m_new

    @pl.when(ki < qi)
    def _():
        step(False)

    @pl.when(ki == qi)
    def _():
        step(True)
        lam = (jnp.exp(jnp.sum(lq1_ref[...] * lk1_ref[...], axis=-1, keepdims=True))
               - jnp.exp(jnp.sum(lq2_ref[...] * lk2_ref[...], axis=-1, keepdims=True)) + lam_init)
        o = acc_ref[0] / l_ref[0] - lam * (acc_ref[1] / l_ref[1])
        o = _rms(o, sg_ref[...], NORM_EPS) * (1.0 - lam_init)
        o_ref[...] = o.astype(o_ref.dtype)


def diff_attention(qkv, slopes, lq1, lk1, lq2, lk2, subln_g, *, batch, seq, tq, lam_init):
    t = batch * seq
    nq = seq // tq
    hb = DIFF_WIDTH // LANES
    small = pl.BlockSpec((1, HEAD_DIM), lambda b, h, qi, ki: (0, 0))
    return pl.pallas_call(
        functools.partial(_diff_attn_kernel, tq=tq, lam_init=lam_init),
        grid=(batch, DIFF_HEADS, nq, nq),
        in_specs=[
            pl.BlockSpec(memory_space=pltpu.SMEM),
            pl.BlockSpec((tq, LANES), lambda b, h, qi, ki: (b * nq + qi, h)),
            pl.BlockSpec((tq, LANES), lambda b, h, qi, ki: (b * nq + jnp.minimum(ki, qi), hb + h)),
            pl.BlockSpec((tq, LANES), lambda b, h, qi, ki: (b * nq + jnp.minimum(ki, qi), 2 * hb + h)),
            small, small, small, small,
            pl.BlockSpec((1, LANES), lambda b, h, qi, ki: (0, 0)),
        ],
        out_specs=pl.BlockSpec((tq, LANES), lambda b, h, qi, ki: (b * nq + qi, h)),
        out_shape=jax.ShapeDtypeStruct((t, DIFF_WIDTH), BF16),
        scratch_shapes=[pltpu.VMEM((2, tq, LANES), BF16), pltpu.VMEM((tq, tq), F32),
                        pltpu.VMEM((2, tq, 1), F32), pltpu.VMEM((2, tq, 1), F32),
                        pltpu.VMEM((2, tq, LANES), F32)],
        compiler_params=_cparams(("parallel", "parallel", "parallel", "arbitrary")),
        name="diff_attention",
    )(slopes, qkv, qkv, qkv, lq1.reshape(1, -1), lk1.reshape(1, -1), lq2.reshape(1, -1), lk2.reshape(1, -1),
      subln_g.reshape(1, -1))


def _mix_kernel(ra_ref, da_ref, ga_ref, gb_ref, x_ref, wa_ref, wb_ref, wo_ref, o_ref):
    ya = _dot(ra_ref[0], wa_ref[0:LANES, :])
    for q in range(1, RWKV_PAIRS):
        ya = ya + _dot(ra_ref[q], wa_ref[q * LANES:(q + 1) * LANES, :])
    yb = _dot(da_ref[...], wb_ref[...])
    mixed = ga_ref[...].astype(F32) * ya + gb_ref[...].astype(F32) * yb
    o_ref[...] = x_ref[...] + _dot(mixed.astype(BF16), wo_ref[...])


def mix_project(ra, da, gates, x, wa, wb, wo, *, tm):
    t, d = x.shape
    const = lambda a: pl.BlockSpec(a.shape, lambda i: (0, 0), pipeline_mode=pl.Buffered(1))
    return pl.pallas_call(
        _mix_kernel,
        grid=(t // tm,),
        in_specs=[
            pl.BlockSpec((RWKV_PAIRS, tm, LANES), lambda i: (0, i, 0)),
            pl.BlockSpec((tm, DIFF_WIDTH), lambda i: (i, 0)),
            pl.BlockSpec((tm, d), lambda i: (i, 0)),
            pl.BlockSpec((tm, d), lambda i: (i, 1)),
            pl.BlockSpec((tm, d), lambda i: (i, 0)),
            const(wa), const(wb), const(wo),
        ],
        out_specs=pl.BlockSpec((tm, d), lambda i: (i, 0)),
        out_shape=jax.ShapeDtypeStruct((t, d), F32),
        compiler_params=_cparams(("parallel",)),
        name="mix_project",
    )(ra, da, gates, gates, x, wa, wb, wo)


def _cross_kernel(x_ref, gc_ref, wq_ref, kv_ref, wo_ref, gf_ref, rw_ref, rb_ref, x2_ref, hf_ref, lg_ref):
    x = x_ref[...]
    h = _rms(x, gc_ref[...], NORM_EPS).astype(BF16)
    q = _dot(h, wq_ref[...]).astype(BF16)
    scale = LANES ** -0.5
    outs = []
    for hd in range(CROSS_HEADS):
        qh = q[:, hd * LANES:(hd + 1) * LANES]
        kh = kv_ref[0, :, hd * LANES:(hd + 1) * LANES]
        vh = kv_ref[0, :, CROSS_WIDTH + hd * LANES:CROSS_WIDTH + (hd + 1) * LANES]
        s = _dot(qh, kh, NT) * scale
        s = s - jnp.max(s, axis=-1, keepdims=True)
        e = jnp.exp(s)
        p = e / jnp.sum(e, axis=-1, keepdims=True)
        outs.append(_dot(p.astype(BF16), vh))
    o = jnp.concatenate(outs, axis=1).astype(BF16)
    x2 = x + _dot(o, wo_ref[...])
    x2_ref[...] = x2
    hf = _rms(x2, gf_ref[...], NORM_EPS)
    tm = hf.shape[0]
    for sidx in range(ROW_TILE):
        hf_ref[pl.ds(sidx, tm, stride=ROW_TILE), :] = hf[:, sidx * LANES:(sidx + 1) * LANES]
    lg_ref[...] = _dot3(hf, rw_ref[...]) + rb_ref[...]


def cross_attention(x, gc, wq, kv, wo, gf, rw, rb, *, seq, tm):
    t, d = x.shape
    per_seq = seq // tm
    const = lambda a: pl.BlockSpec(a.shape, lambda i: (0,) * a.ndim, pipeline_mode=pl.Buffered(1))
    vec = lambda n: pl.BlockSpec((1, n), lambda i: (0, 0))
    return pl.pallas_call(
        _cross_kernel,
        grid=(t // tm,),
        in_specs=[
            pl.BlockSpec((tm, d), lambda i: (i, 0)),
            vec(d), const(wq),
            pl.BlockSpec((1,) + kv.shape[1:], lambda i: (i // per_seq, 0, 0)),
            const(wo), vec(d), const(rw), vec(LANES),
        ],
        out_specs=[pl.BlockSpec((tm, d), lambda i: (i, 0)), pl.BlockSpec((tm * ROW_TILE, LANES), lambda i: (i, 0)),
                   pl.BlockSpec((tm, LANES), lambda i: (i, 0))],
        out_shape=[jax.ShapeDtypeStruct((t, d), F32), jax.ShapeDtypeStruct((t * ROW_TILE, LANES), F32),
                   jax.ShapeDtypeStruct((t, LANES), F32)],
        compiler_params=_cparams(("parallel",)),
        name="cross_attention",
    )(x, gc.reshape(1, d), wq, kv, wo, gf.reshape(1, d), rw, rb)


def _router_kernel(lg_ref, idx_ref, gate_ref):
    x = lg_ref[...]
    lane = lax.broadcasted_iota(jnp.int32, x.shape, 1)
    x = jnp.where(lane < N_EXPERTS, x, -jnp.inf)
    idx_out = jnp.zeros(x.shape, jnp.int32)
    val_out = jnp.zeros(x.shape, F32)
    vals = []
    for j in range(TOP_K):
        m = jnp.max(x, axis=-1, keepdims=True)
        sel = jnp.min(jnp.where(x == m, lane, LANES), axis=-1, keepdims=True)
        idx_out = jnp.where(lane == j, sel, idx_out)
        vals.append(m)
        x = jnp.where(lane == sel, -jnp.inf, x)
    es = [jnp.exp(vj - vals[0]) for vj in vals]
    tot = es[0] + es[1] + es[2] + es[3]
    for j in range(TOP_K):
        val_out = jnp.where(lane == j, es[j] / tot, val_out)
    idx_ref[...] = idx_out[:, 0:TOP_K]
    gate_ref[...] = val_out[:, 0:TOP_K]


def router_topk(logits, *, tm):
    t = logits.shape[0]
    return pl.pallas_call(
        _router_kernel,
        grid=(t // tm,),
        in_specs=[pl.BlockSpec((tm, LANES), lambda i: (i, 0))],
        out_specs=[pl.BlockSpec((tm, TOP_K), lambda i: (i, 0)), pl.BlockSpec((tm, TOP_K), lambda i: (i, 0))],
        out_shape=[jax.ShapeDtypeStruct((t, TOP_K), jnp.int32), jax.ShapeDtypeStruct((t, TOP_K), F32)],
        compiler_params=_cparams(("parallel",)),
        name="router_topk",
    )(logits)


def _moe_kernel(sbe_ref, sbr_ref, src_ref, gate_ref, hf_hbm, w1_ref, b1_ref, w2_ref, b2_ref,
                ys_hbm, xbuf, x2d, acc, obuf, gsem, ssem):
    s = pl.program_id(0)
    f = pl.program_id(1)
    nf = pl.num_programs(1)
    rows = sbr_ref[s]
    nsub = (rows + (MOE_SUB - 1)) // MOE_SUB

    def slab(ref, r):
        return ref.at[pl.ds(pl.multiple_of(r * ROW_TILE, ROW_TILE), ROW_TILE)]

    def row_in(r):
        tok = jnp.maximum(src_ref[0, 0, r], 0) // TOP_K
        return pltpu.make_async_copy(slab(hf_hbm, tok), slab(xbuf, r), gsem)

    def row_out(r):
        return pltpu.make_async_copy(slab(obuf, r), slab(ys_hbm, src_ref[0, 0, r]), ssem)

    @pl.when((f == 0) & (rows > 0))
    def _():
        n = nsub * MOE_SUB

        def start(r, c):
            row_in(r).start()
            return c

        def wait(r, c):
            row_in(r).wait()
            return c

        lax.fori_loop(0, n, start, 0)
        lax.fori_loop(0, n, wait, 0)

    for sub in range(MOE_SUPER // MOE_SUB):
        @pl.when(sub < nsub)
        def _(sub=sub):
            sl = slice(sub * MOE_SUB, (sub + 1) * MOE_SUB)
            base = sub * MOE_SUB * ROW_TILE

            @pl.when(f == 0)
            def _():
                for sidx in range(ROW_TILE):
                    piece = xbuf[pl.ds(base + sidx, MOE_SUB, stride=ROW_TILE), :]
                    x2d[sl, sidx * LANES:(sidx + 1) * LANES] = piece.astype(BF16)

            x = x2d[sl, :]
            hb = _dot(x, w1_ref[0]) + b1_ref[0]
            even = (lax.broadcasted_iota(jnp.int32, (MOE_SUB, LANES), 1) % 2) == 0
            acts = []
            for cb in range(MOE_FT // LANES):
                a = hb[:, cb * LANES:(cb + 1) * LANES]
                b = hb[:, MOE_FT + cb * LANES:MOE_FT + (cb + 1) * LANES]
                hg = jnp.where(even, a, pltpu.roll(b, 1, axis=1))
                hl = jnp.where(even, pltpu.roll(a, LANES - 1, axis=1), b)
                xg = jnp.minimum(hg, SWIGLU_LIMIT)
                xl = jnp.clip(hl, -SWIGLU_LIMIT, SWIGLU_LIMIT)
                acts.append((xg * jax.nn.sigmoid(SWIGLU_ALPHA * xg) * (xl + 1.0)).astype(BF16))
            contrib = _dot(jnp.concatenate(acts, axis=1), w2_ref[0])

            @pl.when(f == 0)
            def _():
                acc[sl, :] = contrib

            @pl.when(f > 0)
            def _():
                acc[sl, :] += contrib

            @pl.when(f == nf - 1)
            def _():
                y = (acc[sl, :] + b2_ref[0]) * gate_ref[sl, :]
                for sidx in range(ROW_TILE):
                    obuf[pl.ds(base + sidx, MOE_SUB, stride=ROW_TILE), :] = y[:, sidx * LANES:(sidx + 1) * LANES]

    @pl.when((f == nf - 1) & (rows > 0))
    def _():
        def start(r, c):
            row_out(r).start()
            return c

        def wait(r, c):
            row_out(r).wait()
            return c

        lax.fori_loop(0, rows, start, 0)
        lax.fori_loop(0, rows, wait, 0)


def moe_experts(sb_e, sb_rows, row_src, row_gate, hf, w1, b1, w2, b2):
    d = D_MODEL
    t = hf.shape[0] // ROW_TILE
    nsb = sb_e.shape[0]
    nf = D_EXPERT // MOE_FT
    grid_spec = pltpu.PrefetchScalarGridSpec(
        num_scalar_prefetch=2,
        grid=(nsb, nf),
        in_specs=[
            pl.BlockSpec((1, 1, MOE_SUPER), lambda s, f, e, r: (s, 0, 0), memory_space=pltpu.SMEM),
            pl.BlockSpec((MOE_SUPER, 1), lambda s, f, e, r: (s, 0)),
            pl.BlockSpec(memory_space=pl.ANY),
            pl.BlockSpec((1, d, 2 * MOE_FT), lambda s, f, e, r: (e[s], 0, f)),
            pl.BlockSpec((1, 1, 2 * MOE_FT), lambda s, f, e, r: (e[s], 0, f)),
            pl.BlockSpec((1, MOE_FT, d), lambda s, f, e, r: (e[s], f, 0)),
            pl.BlockSpec((1, 1, d), lambda s, f, e, r: (e[s], 0, 0)),
        ],
        out_specs=pl.BlockSpec(memory_space=pl.ANY),
        scratch_shapes=[pltpu.VMEM((MOE_SUPER * ROW_TILE, LANES), F32), pltpu.VMEM((MOE_SUPER, d), BF16),
                        pltpu.VMEM((MOE_SUPER, d), F32), pltpu.VMEM((MOE_SUPER * ROW_TILE, LANES), F32),
                        pltpu.SemaphoreType.DMA, pltpu.SemaphoreType.DMA],
    )
    return pl.pallas_call(
        _moe_kernel,
        grid_spec=grid_spec,
        out_shape=jax.ShapeDtypeStruct((t * TOP_K * ROW_TILE, LANES), F32),
        compiler_params=_cparams(("arbitrary", "arbitrary")),
        name="moe_experts",
    )(sb_e, sb_rows, row_src, row_gate, hf, w1, b1, w2, b2)


def _combine_kernel(ys_ref, x_ref, g_ref, o_ref, tmp_ref):
    tm = x_ref.shape[0]
    tot = ys_ref[:, 0:ROW_TILE, :]
    for j in range(1, TOP_K):
        tot = tot + ys_ref[:, j * ROW_TILE:(j + 1) * ROW_TILE, :]
    tmp_ref[...] = tot.reshape(tm * ROW_TILE, LANES)
    pieces = [tmp_ref[pl.ds(sidx, tm, stride=ROW_TILE), :] for sidx in range(ROW_TILE)]
    out = x_ref[...] + jnp.concatenate(pieces, axis=1)
    o_ref[...] = _rms(out, g_ref[...], NORM_EPS)


def combine_final(ys, x, g, *, tm):
    t, d = x.shape
    return pl.pallas_call(
        _combine_kernel,
        grid=(t // tm,),
        in_specs=[pl.BlockSpec((tm, TOP_K * ROW_TILE, LANES), lambda i: (i, 0, 0)),
                  pl.BlockSpec((tm, d), lambda i: (i, 0)), pl.BlockSpec((1, d), lambda i: (0, 0))],
        out_specs=pl.BlockSpec((tm, d), lambda i: (i, 0)),
        out_shape=jax.ShapeDtypeStruct((t, d), F32),
        scratch_shapes=[pltpu.VMEM((tm * ROW_TILE, LANES), F32)],
        compiler_params=_cparams(("parallel",)),
        name="combine_final",
    )(ys, x, g.reshape(1, d))


def _routing_tables(top_idx, gate, n_super):
    flat_e = top_idx.reshape(-1)
    n = flat_e.shape[0]
    onehot = (flat_e[:, None] == jnp.arange(N_EXPERTS, dtype=jnp.int32)[None, :]).astype(jnp.int32)
    csum = jnp.cumsum(onehot, axis=0)
    rank = jnp.sum(onehot * csum, axis=1) - 1
    counts = csum[-1]
    nsb = (counts + MOE_SUPER - 1) // MOE_SUPER
    sb_end = jnp.cumsum(nsb)
    sb_start = sb_end - nsb
    dest = sb_start[flat_e] * MOE_SUPER + rank
    row_src = jnp.full((n_super * MOE_SUPER,), -1, jnp.int32).at[dest].set(jnp.arange(n, dtype=jnp.int32))
    row_gate = jnp.zeros((n_super * MOE_SUPER,), F32).at[dest].set(gate.reshape(-1))
    s_ids = jnp.arange(n_super, dtype=jnp.int32)
    sb_e = jnp.minimum(jnp.searchsorted(sb_end, s_ids, side="right"), N_EXPERTS - 1).astype(jnp.int32)
    local = s_ids - sb_start[sb_e]
    sb_rows = jnp.clip(counts[sb_e] - local * MOE_SUPER, 0, MOE_SUPER)
    sb_rows = jnp.where(s_ids < sb_end[-1], sb_rows, 0).astype(jnp.int32)
    return sb_e, sb_rows, row_src.reshape(n_super, 1, MOE_SUPER), row_gate.reshape(-1, 1)


def _pad_rows(a, n):
    return jnp.pad(a, ((0, n - a.shape[0]), (0, 0)))


def _layer(x, mem, l, p, batch, seq):
    t = batch * seq
    c = RWKV_WIDTH
    lam_init = 0.8 - 0.6 * math.exp(-0.3 * l)
    w_in = p["w_in"]
    o1 = 3 * c + DECAY_LORA + AAA_LORA + GATE_LORA
    o2 = o1 + 3 * DIFF_WIDTH
    padc = lambda a, n: jnp.pad(a, ((0, 0), (0, n - a.shape[1])))
    w_rwkv = jnp.concatenate([
        w_in[:, :3 * c],
        padc(w_in[:, 3 * c:3 * c + DECAY_LORA], 128),
        padc(w_in[:, 3 * c + DECAY_LORA:3 * c + DECAY_LORA + AAA_LORA], 128),
        padc(w_in[:, 3 * c + DECAY_LORA + AAA_LORA:o1], 256)], axis=1).astype(BF16)
    mu = p["rwkv_mu"]
    pad1 = lambda a, n: jnp.pad(a, (0, n - a.shape[0]))
    mu_p = jnp.concatenate([mu[:3 * c], pad1(mu[3 * c:3 * c + DECAY_LORA], 128),
                            pad1(mu[3 * c + DECAY_LORA:3 * c + DECAY_LORA + AAA_LORA], 128),
                            pad1(mu[3 * c + DECAY_LORA + AAA_LORA:], 256)])
    w_diff = w_in[:, o1:o2].astype(BF16)
    w_gate = w_in[:, o2:].astype(BF16)

    g_mix = p["norm_mix_g"]
    p_rwkv = norm_matmul(x, g_mix, w_rwkv, tm=1024, tn=RWKV_COLS // 2, out_dtype=F32, name="in_proj_rwkv")
    qkv = norm_matmul(x, g_mix, w_diff, tm=1024, tn=1024, out_dtype=BF16, name="in_proj_diff")
    gates = norm_matmul(x, g_mix, w_gate, tm=1024, tn=1024, out_dtype=BF16, act="sigmoid", name="in_proj_gate")

    prep = rwkv_prep(p_rwkv, mu_p, p["rwkv_w0"], p["rwkv_a0"], p["rwkv_k_k"], p["rwkv_k_a"],
                     _pad_rows(p["rwkv_w2"], 128), _pad_rows(p["rwkv_a2"], 128), _pad_rows(p["rwkv_g2"], 256),
                     seq=seq, tm=256)
    ra = rwkv_scan(*prep, p["rwkv_ln_w"], p["rwkv_ln_b"], p["rwkv_r_k"].reshape(-1), batch=batch, seq=seq)

    slopes = (2.0 ** (-8.0 * jnp.arange(1, DIFF_HEADS + 1, dtype=F32) / DIFF_HEADS)).astype(F32)
    da = diff_attention(qkv, slopes, p["diff_lq1"], p["diff_lk1"], p["diff_lq2"], p["diff_lk2"],
                        p["diff_subln_g"], batch=batch, seq=seq, tq=min(512, seq), lam_init=lam_init)

    x1 = mix_project(ra, da, gates, x, p["rwkv_proj"].astype(BF16), p["diff_proj"].astype(BF16),
                     p["w_out"].astype(BF16), tm=256)

    m_len = mem.shape[0] // batch
    kv = norm_matmul(mem, p["norm_mem_g"], p["cross_wkv"].astype(BF16), tm=min(512, mem.shape[0]),
                     tn=2 * CROSS_WIDTH, out_dtype=BF16, name="cross_kv")
    rw = jnp.pad(p["router_w"], ((0, 0), (0, LANES - N_EXPERTS)))
    rb = jnp.pad(p["router_b"], (0, LANES - N_EXPERTS)).reshape(1, LANES)
    x2, hf, logits = cross_attention(x1, p["norm_cross_g"], p["cross_wq"].astype(BF16),
                                     kv.reshape(batch, m_len, 2 * CROSS_WIDTH), p["cross_wo"].astype(BF16),
                                     p["norm_ffn_g"], rw, rb, seq=seq, tm=256)

    top_idx, gate = router_topk(logits, tm=min(1024, t))
    n_super = (t * TOP_K) // MOE_SUPER + N_EXPERTS
    sb_e, sb_rows, row_src, row_gate = _routing_tables(top_idx, gate, n_super)
    nf = D_EXPERT // MOE_FT
    w2 = p["expert_w2"].reshape(N_EXPERTS, nf, 2, MOE_FT // 2, D_MODEL)
    w2 = jnp.swapaxes(w2, 2, 3).reshape(N_EXPERTS, D_EXPERT, D_MODEL).astype(BF16)
    ys = moe_experts(sb_e, sb_rows, row_src, row_gate, hf, p["expert_w1"].astype(BF16),
                     p["expert_b1"][:, None, :], w2, p["expert_b2"][:, None, :])
    return x2, ys.reshape(t, TOP_K * ROW_TILE, LANES)


def kernel(x, mem, norm_mix_g, w_in, rwkv_mu, rwkv_w0, rwkv_w2, rwkv_a0, rwkv_a2, rwkv_g2, rwkv_k_k, rwkv_k_a, rwkv_r_k, rwkv_ln_w, rwkv_ln_b, rwkv_proj, diff_lq1, diff_lk1, diff_lq2, diff_lk2, diff_subln_g, diff_proj, w_out, norm_cross_g, norm_mem_g, cross_wq, cross_wkv, cross_wo, norm_ffn_g, router_w, router_b, expert_w1, expert_b1, expert_w2, expert_b2, final_norm_g):
    batch, seq, d = x.shape
    stacked = dict(norm_mix_g=norm_mix_g, w_in=w_in, rwkv_mu=rwkv_mu, rwkv_w0=rwkv_w0, rwkv_w2=rwkv_w2,
                   rwkv_a0=rwkv_a0, rwkv_a2=rwkv_a2, rwkv_g2=rwkv_g2, rwkv_k_k=rwkv_k_k, rwkv_k_a=rwkv_k_a,
                   rwkv_r_k=rwkv_r_k, rwkv_ln_w=rwkv_ln_w, rwkv_ln_b=rwkv_ln_b, rwkv_proj=rwkv_proj,
                   diff_lq1=diff_lq1, diff_lk1=diff_lk1, diff_lq2=diff_lq2, diff_lk2=diff_lk2,
                   diff_subln_g=diff_subln_g, diff_proj=diff_proj, w_out=w_out, norm_cross_g=norm_cross_g,
                   norm_mem_g=norm_mem_g, cross_wq=cross_wq, cross_wkv=cross_wkv, cross_wo=cross_wo,
                   norm_ffn_g=norm_ffn_g, router_w=router_w, router_b=router_b, expert_w1=expert_w1,
                   expert_b1=expert_b1, expert_w2=expert_w2, expert_b2=expert_b2)
    assert w_in.shape[0] == 1, "the closing RMSNorm is fused into the single layer's combine"
    p = {k: v[0] for k, v in stacked.items()}
    x2, ys = _layer(x.reshape(batch * seq, d), mem.reshape(-1, d), 0, p, batch, seq)
    out = combine_final(ys, x2, final_norm_g, tm=256)
    return out.reshape(batch, seq, d)
```

```python
import functools
import math

import jax
import jax.numpy as jnp
from jax import lax
from jax.experimental import pallas as pl
from jax.experimental.pallas import tpu as pltpu

F32 = jnp.float32
BF16 = jnp.bfloat16

D_MODEL = 2048
NORM_EPS = 1e-5
LOG2E = 1.4426950408889634
LANES = 128
HEAD_DIM = 64
RWKV_WIDTH = 1024
SLAB = 256
RWKV_SLABS = RWKV_WIDTH // SLAB
HEADS_PER_SLAB = SLAB // HEAD_DIM
CHUNK = 64
RWKV_GN_EPS = 64e-5
DECAY_LORA = 64
AAA_LORA = 64
GATE_LORA = 160
LORA_PAD = 512
RWKV_COLS = 3 * RWKV_WIDTH + LORA_PAD
DIFF_WIDTH = 1024
DIFF_HEADS = 8
CROSS_HEADS = 4
CROSS_WIDTH = 512
N_EXPERTS = 32
TOP_K = 4
D_EXPERT = 2048
SWIGLU_ALPHA = 1.702
SWIGLU_LIMIT = 7.0
MOE_SUPER = 1024
MOE_SUB = 256
MOE_FT = 512
ROW_TILE = D_MODEL // LANES
VMEM_LIMIT = 56 * 1024 * 1024

NN = (((1,), (0,)), ((), ()))
NT = (((1,), (1,)), ((), ()))


def _dot(a, b, dims=NN):
    return lax.dot_general(a, b, dims, preferred_element_type=F32)


def _split2(a):
    hi = a.astype(BF16)
    lo = (a - hi.astype(F32)).astype(BF16)
    return hi, lo


def _split3(a):
    hi = a.astype(BF16)
    r1 = a - hi.astype(F32)
    mid = r1.astype(BF16)
    lo = (r1 - mid.astype(F32)).astype(BF16)
    return hi, mid, lo


def _dot3(a, b, dims=NN):
    ah, al = _split2(a)
    bh, bl = _split2(b)
    return _dot(ah, bh, dims) + (_dot(ah, bl, dims) + _dot(al, bh, dims))


def _dot_exact_rhs(a, b_bf16, dims=NN):
    h, m, l = _split3(a)
    return _dot(h, b_bf16, dims) + (_dot(m, b_bf16, dims) + _dot(l, b_bf16, dims))


def _rms(x, g, eps):
    ms = jnp.mean(x * x, axis=-1, keepdims=True)
    return x * lax.rsqrt(ms + eps) * g


def _cparams(sem):
    return pltpu.CompilerParams(dimension_semantics=sem, vmem_limit_bytes=VMEM_LIMIT)


def _norm_matmul_kernel(x_ref, g_ref, w_ref, o_ref, h_ref, *, act):
    @pl.when(pl.program_id(1) == 0)
    def _():
        h_ref[...] = _rms(x_ref[...], g_ref[...], NORM_EPS).astype(BF16)

    y = _dot(h_ref[...], w_ref[...])
    if act == "sigmoid":
        y = jax.nn.sigmoid(y)
    o_ref[...] = y.astype(o_ref.dtype)


def norm_matmul(x, g, w, *, tm, tn, out_dtype, act=None, name):
    m, d = x.shape
    n = w.shape[1]
    return pl.pallas_call(
        functools.partial(_norm_matmul_kernel, act=act),
        grid=(m // tm, n // tn),
        in_specs=[
            pl.BlockSpec((tm, d), lambda i, j: (i, 0)),
            pl.BlockSpec((1, d), lambda i, j: (0, 0)),
            pl.BlockSpec((d, tn), lambda i, j: (0, j)),
        ],
        out_specs=pl.BlockSpec((tm, tn), lambda i, j: (i, j)),
        out_shape=jax.ShapeDtypeStruct((m, n), out_dtype),
        scratch_shapes=[pltpu.VMEM((tm, d), BF16)],
        compiler_params=_cparams(("parallel", "arbitrary")),
        name=name,
    )(x, g.reshape(1, d), w)


def _head_ones():
    r = lax.broadcasted_iota(jnp.int32, (SLAB, SLAB), 0)
    c = lax.broadcasted_iota(jnp.int32, (SLAB, SLAB), 1)
    return ((r // HEAD_DIM) == (c // HEAD_DIM)).astype(BF16)


def _rwkv_prep_kernel(p_ref, prev_ref, mu_ref, w0_ref, a0_ref, kk_ref, ka_ref, w2_ref, a2_ref, g2_ref,
                      r_out, lw_out, k_out, v_out, kkn_out, b_out, g_out, *, tiles_per_seq):
    c = RWKV_WIDTH
    i = pl.program_id(0)
    p = p_ref[...]
    tm = p.shape[0]
    first = (i % tiles_per_seq) == 0
    prev_row = jnp.where(first, 0.0, prev_ref[7:8, :])
    row = lax.broadcasted_iota(jnp.int32, p.shape, 0)
    shifted = jnp.where(row == 0, prev_row, pltpu.roll(p, 1, axis=0))
    ps = p + (shifted - p) * mu_ref[...]
    r = ps[:, 0:c]
    k = ps[:, c:2 * c]
    v = ps[:, 2 * c:3 * c]
    wd = ps[:, 3 * c:3 * c + 128]
    ad = ps[:, 3 * c + 128:3 * c + 256]
    gd = ps[:, 3 * c + 256:3 * c + 512]
    z = -(w0_ref[...] + _dot3(jnp.tanh(wd), w2_ref[...]))
    softplus = jnp.maximum(z, 0.0) + jnp.log1p(jnp.exp(-jnp.abs(z)))
    w = -softplus - 0.5
    lw = -jnp.exp(w)
    a = jax.nn.sigmoid(a0_ref[...] + _dot3(ad, a2_ref[...]))
    g = _dot3(jax.nn.sigmoid(gd), g2_ref[...])
    kkr = k * kk_ref[...]
    k2 = k * (1.0 + (a - 1.0) * ka_ref[...])
    ones = _head_ones()
    for q in range(RWKV_SLABS):
        sl = slice(q * SLAB, (q + 1) * SLAB)
        x = kkr[:, sl]
        ss = _dot_exact_rhs(x * x, ones)
        kkn = x / jnp.maximum(jnp.sqrt(ss), 1e-12)
        r_out[q] = r[:, sl]
        lw_out[q] = lw[:, sl]
        k_out[q] = k2[:, sl]
        v_out[q] = v[:, sl]
        kkn_out[q] = kkn
        b_out[q] = kkn * a[:, sl]
        g_out[q] = g[:, sl]


def rwkv_prep(p, mu, w0, a0, k_k, k_a, w2p, a2p, g2p, *, seq, tm):
    t, cols = p.shape
    c = RWKV_WIDTH
    vec = lambda n: pl.BlockSpec((1, n), lambda i: (0, 0))
    full = lambda a: pl.BlockSpec(a.shape, lambda i: (0, 0))
    out_spec = pl.BlockSpec((RWKV_SLABS, tm, SLAB), lambda i: (0, i, 0))
    out_shape = jax.ShapeDtypeStruct((RWKV_SLABS, t, SLAB), F32)
    return pl.pallas_call(
        functools.partial(_rwkv_prep_kernel, tiles_per_seq=seq // tm),
        grid=(t // tm,),
        in_specs=[
            pl.BlockSpec((tm, cols), lambda i: (i, 0)),
            pl.BlockSpec((8, cols), lambda i: (jnp.maximum(i * (tm // 8) - 1, 0), 0)),
            vec(cols), vec(c), vec(c), vec(c), vec(c), full(w2p), full(a2p), full(g2p),
        ],
        out_specs=[out_spec] * 7,
        out_shape=[out_shape] * 7,
        compiler_params=_cparams(("parallel",)),
        name="rwkv_prep",
    )(p, p, mu.reshape(1, cols), w0.reshape(1, c), a0.reshape(1, c), k_k.reshape(1, c), k_a.reshape(1, c),
      w2p, a2p, g2p)


def _dot3s(a, b, dims=NN):
    ah, al = _split2(a)
    bh, bl = _split2(b)
    m = a.shape[0]
    lhs = jnp.concatenate([ah, al], axis=0)
    if dims is NN:
        n = b.shape[1]
        rhs = jnp.concatenate([bh, bl], axis=1)
    else:
        n = b.shape[0]
        rhs = jnp.concatenate([bh, bl], axis=0)
    p = _dot(lhs, rhs, dims)
    return p[:m, :n] + (p[:m, n:] + p[m:, :n])


def _sum3_exact_rhs(xs, ones):
    parts = []
    for x in xs:
        parts.extend(_split3(x))
    res = _dot(jnp.concatenate(parts, axis=0), ones)
    L = xs[0].shape[0]
    return [res[(3 * i) * L:(3 * i + 1) * L] + (res[(3 * i + 1) * L:(3 * i + 2) * L] + res[(3 * i + 2) * L:(3 * i + 3) * L])
            for i in range(len(xs))]


def _rwkv_scan_kernel(r_ref, lw_ref, k_ref, v_ref, kk_ref, b_ref, g_ref, lnw_ref, lnb_ref, rk_ref,
                      o_ref, st_ref):
    L = CHUNK
    W = SLAB

    @pl.when(pl.program_id(1) == 0)
    def _():
        st_ref[...] = jnp.zeros_like(st_ref)

    t_i = lax.broadcasted_iota(jnp.int32, (L, W), 0)
    lane = lax.broadcasted_iota(jnp.int32, (L, W), 1)
    j_i = lane % HEAD_DIM
    hid = lane // HEAD_DIM
    strict = j_i < t_i
    incl = j_i <= t_i
    eye = (j_i == t_i).astype(F32)
    r2 = lax.broadcasted_iota(jnp.int32, (W, W), 0)
    c2 = lax.broadcasted_iota(jnp.int32, (W, W), 1)
    same_head = (r2 // HEAD_DIM) == (c2 // HEAD_DIM)
    diag = r2 == c2
    ones = same_head.astype(BF16)
    tr = lax.broadcasted_iota(jnp.int32, (L, L), 0)
    tc = lax.broadcasted_iota(jnp.int32, (L, L), 1)
    tri = (tc <= tr).astype(BF16)

    def bd(x):
        return jnp.concatenate([jnp.where(hid == h, x, 0.0) for h in range(HEADS_PER_SLAB)], axis=0)

    U = range(RWKV_SLABS)
    cat0 = lambda xs: jnp.concatenate(xs, axis=0)
    cat1 = lambda xs: jnp.concatenate(xs, axis=1)
    r = [r_ref[u] for u in U]
    lw = [lw_ref[u] for u in U]
    k = [k_ref[u] for u in U]
    v = [v_ref[u] for u in U]
    c3 = [_dot(tri, cat1(_split3(lw[u]))) for u in U]
    cum = [c3[u][:, 0:W] + (c3[u][:, W:2 * W] + c3[u][:, 2 * W:]) for u in U]
    cum_end = [cum[u][L - 1:L, :] for u in U]
    e_cum = [jnp.exp(cum[u]) for u in U]
    e_neg = [jnp.exp(-cum[u]) for u in U]
    e_end = [jnp.exp(cum_end[u] - cum[u]) for u in U]
    at = [-kk_ref[u] * jnp.exp(cum[u] - lw[u]) for u in U]
    rt = [r[u] * e_cum[u] for u in U]
    bt = [b_ref[u] * e_neg[u] for u in U]
    kt = [k[u] * e_neg[u] for u in U]
    bh = [b_ref[u] * e_end[u] for u in U]
    kh = [k[u] * e_end[u] for u in U]
    w_end = [jnp.exp(cum_end[u]) for u in U]

    g_all = [_dot3s(cat0([at[u], rt[u]]), cat0([bd(bt[u]), bd(kt[u])]), NT) for u in U]
    a_ab = [jnp.where(strict, g_all[u][0:L, 0:W], 0.0) for u in U]
    a_ak = [jnp.where(strict, g_all[u][0:L, W:], 0.0) for u in U]
    a_rb = [jnp.where(incl, g_all[u][L:, 0:W], 0.0) for u in U]
    a_rk = [jnp.where(incl, g_all[u][L:, W:], 0.0) for u in U]

    tinv = [eye for u in U]
    pw = a_ab
    for it in range(6):
        if it < 5:
            res = [_dot3s(pw[u], cat1([bd(tinv[u]), bd(pw[u])])) for u in U]
            tinv = [tinv[u] + res[u][:, 0:W] for u in U]
            pw = [res[u][:, W:] for u in U]
        else:
            tinv = [tinv[u] + _dot3s(pw[u], bd(tinv[u])) for u in U]

    bdv = [bd(v[u]) for u in U]
    av = [_dot3s(a_ak[u], bdv[u]) for u in U]
    qp = [_dot3s(tinv[u], cat1([bd(at[u]), bd(av[u])])) for u in U]
    q1 = [qp[u][:, 0:W] for u in U]
    p1 = [qp[u][:, W:] for u in U]
    qp2 = [_dot3s(cat1([a_rb[u], a_rk[u]]),
                  cat0([cat1([bd(q1[u]), bd(p1[u])]), cat1([jnp.zeros_like(bdv[u]), bdv[u]])])) for u in U]
    q2 = [rt[u] + qp2[u][:, 0:W] for u in U]
    p2 = [qp2[u][:, W:] for u in U]
    mp = [_dot3s(cat0([bh[u], kh[u]]).T,
                 cat0([cat1([q1[u], p1[u]]), cat1([jnp.zeros_like(v[u]), v[u]])])) for u in U]
    m_bd = [jnp.where(same_head, mp[u][:, 0:W], 0.0) + jnp.where(diag, w_end[u], 0.0) for u in U]
    ys = [_dot3s(cat0([q2[u], m_bd[u]]), st_ref[u]) for u in U]
    y = [ys[u][0:L] + p2[u] for u in U]
    for u in U:
        st_ref[u] = ys[u][L:] + jnp.where(same_head, mp[u][:, W:], 0.0)

    sums = [_sum3_exact_rhs([y[u], r[u] * k[u] * rk_ref[u]], ones) for u in U]
    yc = [y[u] - sums[u][0] * (1.0 / HEAD_DIM) for u in U]
    var = [_sum3_exact_rhs([yc[u] * yc[u]], ones)[0] for u in U]
    for u in U:
        yn = yc[u] * lax.rsqrt(var[u] * (1.0 / HEAD_DIM) + RWKV_GN_EPS) * lnw_ref[u] + lnb_ref[u]
        o_ref[u] = ((yn + sums[u][1] * v[u]) * g_ref[u]).astype(o_ref.dtype)


def rwkv_scan(r, lw, k, v, kk, b, g, ln_w, ln_b, r_k, *, batch, seq):
    nchunk = seq // CHUNK
    blk = pl.BlockSpec((RWKV_SLABS, CHUNK, SLAB), lambda bi, ci: (0, bi * nchunk + ci, 0))
    par = pl.BlockSpec((RWKV_SLABS, 1, SLAB), lambda bi, ci: (0, 0, 0))
    t = batch * seq
    return pl.pallas_call(
        _rwkv_scan_kernel,
        grid=(batch, nchunk),
        in_specs=[blk] * 7 + [par] * 3,
        out_specs=blk,
        out_shape=jax.ShapeDtypeStruct((RWKV_SLABS, t, SLAB), BF16),
        scratch_shapes=[pltpu.VMEM((RWKV_SLABS, SLAB, SLAB), F32)],
        compiler_params=_cparams(("arbitrary", "arbitrary")),
        name="rwkv_scan",
    )(r, lw, k, v, kk, b, g, ln_w.reshape(RWKV_SLABS, 1, SLAB), ln_b.reshape(RWKV_SLABS, 1, SLAB),
      r_k.reshape(RWKV_SLABS, 1, SLAB))


def _diff_attn_kernel(qi_ref, ki_ref, slope_ref, q_ref, k_ref, v_ref, lq1_ref, lk1_ref, lq2_ref, lk2_ref, sg_ref,
                      o_ref, qs_ref, relb_ref, m_ref, l_ref, acc_ref, *, tq, lam_init):
    h = pl.program_id(1)
    qi = qi_ref[pl.program_id(2)]
    ki = ki_ref[pl.program_id(2)]
    c2 = slope_ref[h] * LOG2E

    @pl.when(ki == 0)
    def _():
        m_ref[...] = jnp.full_like(m_ref, -jnp.inf)
        l_ref[...] = jnp.zeros_like(l_ref)
        acc_ref[...] = jnp.zeros_like(acc_ref)
        q = q_ref[...].astype(F32) * (HEAD_DIM ** -0.5 * LOG2E)
        lane = lax.broadcasted_iota(jnp.int32, q.shape, 1)
        m0 = lane < HEAD_DIM
        qs_ref[0] = jnp.where(m0, q, 0.0).astype(BF16)
        qs_ref[1] = jnp.where(m0, 0.0, q).astype(BF16)
        rel = (lax.broadcasted_iota(jnp.int32, (tq, tq), 1) - lax.broadcasted_iota(jnp.int32, (tq, tq), 0))
        relb_ref[...] = rel.astype(F32) * (-c2)

    def step(masked):
        k = k_ref[...]
        vt = v_ref[...].T
        relb = relb_ref[...]
        tile_bias = c2 * ((qi - ki) * tq).astype(F32)
        for idx in range(2):
            s = _dot(k, qs_ref[idx], NT) + relb
            if masked:
                s = jnp.where(relb > 0.0, -jnp.inf, s)
            m_prev = m_ref[idx]
            m_new = jnp.maximum(m_prev, jnp.max(s, axis=0, keepdims=True) - tile_bias)
            alpha = jnp.exp2(m_prev - m_new)
            p = jnp.exp2(s - (m_new + tile_bias))
            l_ref[idx] = alpha * l_ref[idx] + jnp.sum(p, axis=0, keepdims=True)
            acc_ref[idx] = alpha * acc_ref[idx] + _dot(vt, p.astype(BF16))
            m_ref[idx] = m_new

    @pl.when(ki < qi)
    def _():
        step(False)

    @pl.when(ki == qi)
    def _():
        step(True)
        lam = (jnp.exp(jnp.sum(lq1_ref[...] * lk1_ref[...], axis=-1, keepdims=True))
               - jnp.exp(jnp.sum(lq2_ref[...] * lk2_ref[...], axis=-1, keepdims=True)) + lam_init)
        ot = acc_ref[0] / l_ref[0] - lam * (acc_ref[1] / l_ref[1])
        o = _rms(ot.T, sg_ref[...], NORM_EPS) * (1.0 - lam_init)
        o_ref[...] = o.astype(o_ref.dtype)


def diff_attention(qkv, slopes, lq1, lk1, lq2, lk2, subln_g, *, batch, seq, tq, lam_init):
    t = batch * seq
    nq = seq // tq
    hb = DIFF_WIDTH // LANES
    pairs = [(qi, ki) for qi in range(nq) for ki in range(qi + 1)]
    qi_tab = jnp.asarray([pq for pq, _ in pairs], jnp.int32)
    ki_tab = jnp.asarray([pk for _, pk in pairs], jnp.int32)
    small = pl.BlockSpec((1, HEAD_DIM), lambda b, h, j, qt, kt: (0, 0))
    grid_spec = pltpu.PrefetchScalarGridSpec(
        num_scalar_prefetch=2,
        grid=(batch, DIFF_HEADS, len(pairs)),
        in_specs=[
            pl.BlockSpec(memory_space=pltpu.SMEM),
            pl.BlockSpec((tq, LANES), lambda b, h, j, qt, kt: (b * nq + qt[j], h)),
            pl.BlockSpec((tq, LANES), lambda b, h, j, qt, kt: (b * nq + kt[j], hb + h)),
            pl.BlockSpec((tq, LANES), lambda b, h, j, qt, kt: (b * nq + kt[j], 2 * hb + h)),
            small, small, small, small,
            pl.BlockSpec((1, LANES), lambda b, h, j, qt, kt: (0, 0)),
        ],
        out_specs=pl.BlockSpec((tq, LANES), lambda b, h, j, qt, kt: (b * nq + qt[j], h)),
        scratch_shapes=[pltpu.VMEM((2, tq, LANES), BF16), pltpu.VMEM((tq, tq), F32),
                        pltpu.VMEM((2, 1, tq), F32), pltpu.VMEM((2, 1, tq), F32),
                        pltpu.VMEM((2, LANES, tq), F32)],
    )
    return pl.pallas_call(
        functools.partial(_diff_attn_kernel, tq=tq, lam_init=lam_init),
        grid_spec=grid_spec,
        out_shape=jax.ShapeDtypeStruct((t, DIFF_WIDTH), BF16),
        compiler_params=_cparams(("parallel", "parallel", "arbitrary")),
        name="diff_attention",
    )(qi_tab, ki_tab, slopes, qkv, qkv, qkv, lq1.reshape(1, -1), lk1.reshape(1, -1), lq2.reshape(1, -1),
      lk2.reshape(1, -1), subln_g.reshape(1, -1))


def _mix_kernel(ra_ref, da_ref, ga_ref, gb_ref, x_ref, wa_ref, wb_ref, wo_ref, o_ref):
    ya = _dot(ra_ref[0], wa_ref[0:SLAB, :])
    for q in range(1, RWKV_SLABS):
        ya = ya + _dot(ra_ref[q], wa_ref[q * SLAB:(q + 1) * SLAB, :])
    yb = _dot(da_ref[...], wb_ref[...])
    mixed = ga_ref[...].astype(F32) * ya + gb_ref[...].astype(F32) * yb
    o_ref[...] = x_ref[...] + _dot(mixed.astype(BF16), wo_ref[...])


def mix_project(ra, da, gates, x, wa, wb, wo, *, tm):
    t, d = x.shape
    const = lambda a: pl.BlockSpec(a.shape, lambda i: (0, 0), pipeline_mode=pl.Buffered(1))
    return pl.pallas_call(
        _mix_kernel,
        grid=(t // tm,),
        in_specs=[
            pl.BlockSpec((RWKV_SLABS, tm, SLAB), lambda i: (0, i, 0)),
            pl.BlockSpec((tm, DIFF_WIDTH), lambda i: (i, 0)),
            pl.BlockSpec((tm, d), lambda i: (i, 0)),
            pl.BlockSpec((tm, d), lambda i: (i, 1)),
            pl.BlockSpec((tm, d), lambda i: (i, 0)),
            const(wa), const(wb), const(wo),
        ],
        out_specs=pl.BlockSpec((tm, d), lambda i: (i, 0)),
        out_shape=jax.ShapeDtypeStruct((t, d), F32),
        compiler_params=_cparams(("parallel",)),
        name="mix_project",
    )(ra, da, gates, gates, x, wa, wb, wo)


def _cross_kernel(x_ref, gc_ref, wq_ref, kv_ref, wo_ref, gf_ref, rw_ref, rb_ref, x2_ref, hf_ref, lg_ref):
    x = x_ref[...]
    h = _rms(x, gc_ref[...], NORM_EPS).astype(BF16)
    q = _dot(h, wq_ref[...]).astype(BF16)
    scale = LANES ** -0.5
    outs = []
    for hd in range(CROSS_HEADS):
        qh = q[:, hd * LANES:(hd + 1) * LANES]
        kh = kv_ref[0, :, hd * LANES:(hd + 1) * LANES]
        vh = kv_ref[0, :, CROSS_WIDTH + hd * LANES:CROSS_WIDTH + (hd + 1) * LANES]
        s = _dot(qh, kh, NT) * scale
        s = s - jnp.max(s, axis=-1, keepdims=True)
        e = jnp.exp(s)
        p = e / jnp.sum(e, axis=-1, keepdims=True)
        outs.append(_dot(p.astype(BF16), vh))
    o = jnp.concatenate(outs, axis=1).astype(BF16)
    x2 = x + _dot(o, wo_ref[...])
    x2_ref[...] = x2
    hf = _rms(x2, gf_ref[...], NORM_EPS)
    tm = hf.shape[0]
    for sidx in range(ROW_TILE):
        hf_ref[pl.ds(sidx, tm, stride=ROW_TILE), :] = hf[:, sidx * LANES:(sidx + 1) * LANES]
    lg_ref[...] = _dot3(hf, rw_ref[...]) + rb_ref[...]


def cross_attention(x, gc, wq, kv, wo, gf, rw, rb, *, seq, tm):
    t, d = x.shape
    per_seq = seq // tm
    const = lambda a: pl.BlockSpec(a.shape, lambda i: (0,) * a.ndim, pipeline_mode=pl.Buffered(1))
    vec = lambda n: pl.BlockSpec((1, n), lambda i: (0, 0))
    return pl.pallas_call(
        _cross_kernel,
        grid=(t // tm,),
        in_specs=[
            pl.BlockSpec((tm, d), lambda i: (i, 0)),
            vec(d), const(wq),
            pl.BlockSpec((1,) + kv.shape[1:], lambda i: (i // per_seq, 0, 0)),
            const(wo), vec(d), const(rw), vec(LANES),
        ],
        out_specs=[pl.BlockSpec((tm, d), lambda i: (i, 0)), pl.BlockSpec((tm * ROW_TILE, LANES), lambda i: (i, 0)),
                   pl.BlockSpec((tm, LANES), lambda i: (i, 0))],
        out_shape=[jax.ShapeDtypeStruct((t, d), F32), jax.ShapeDtypeStruct((t * ROW_TILE, LANES), F32),
                   jax.ShapeDtypeStruct((t, LANES), F32)],
        compiler_params=_cparams(("parallel",)),
        name="cross_attention",
    )(x, gc.reshape(1, d), wq, kv, wo, gf.reshape(1, d), rw, rb)


def _router_kernel(lg_ref, idx_ref, gate_ref):
    x = lg_ref[...]
    lane = lax.broadcasted_iota(jnp.int32, x.shape, 1)
    x = jnp.where(lane < N_EXPERTS, x, -jnp.inf)
    idx_out = jnp.zeros(x.shape, jnp.int32)
    val_out = jnp.zeros(x.shape, F32)
    vals = []
    for j in range(TOP_K):
        m = jnp.max(x, axis=-1, keepdims=True)
        sel = jnp.min(jnp.where(x == m, lane, LANES), axis=-1, keepdims=True)
        idx_out = jnp.where(lane == j, sel, idx_out)
        vals.append(m)
        x = jnp.where(lane == sel, -jnp.inf, x)
    es = [jnp.exp(vj - vals[0]) for vj in vals]
    tot = es[0] + es[1] + es[2] + es[3]
    for j in range(TOP_K):
        val_out = jnp.where(lane == j, es[j] / tot, val_out)
    idx_ref[...] = idx_out[:, 0:TOP_K]
    gate_ref[...] = val_out[:, 0:TOP_K]


def router_topk(logits, *, tm):
    t = logits.shape[0]
    return pl.pallas_call(
        _router_kernel,
        grid=(t // tm,),
        in_specs=[pl.BlockSpec((tm, LANES), lambda i: (i, 0))],
        out_specs=[pl.BlockSpec((tm, TOP_K), lambda i: (i, 0)), pl.BlockSpec((tm, TOP_K), lambda i: (i, 0))],
        out_shape=[jax.ShapeDtypeStruct((t, TOP_K), jnp.int32), jax.ShapeDtypeStruct((t, TOP_K), F32)],
        compiler_params=_cparams(("parallel",)),
        name="router_topk",
    )(logits)


def _moe_kernel(sbe_ref, sbr_ref, src_ref, nsrc_ref, gate_ref, hf_hbm, w1_ref, b1_ref, w2_ref, b2_ref,
                ys_hbm, xbuf, x2d, acc, obuf, gsem, ssem):
    s = pl.program_id(0)
    f = pl.program_id(1)
    nf = pl.num_programs(1)
    nsb = pl.num_programs(0)
    sub_rows = MOE_SUB * ROW_TILE
    nsubs = MOE_SUPER // MOE_SUB
    unroll = 8

    def ceil_sub(n):
        return (n + (MOE_SUB - 1)) // MOE_SUB

    rows = sbr_ref[s]
    nsub = ceil_sub(rows)
    rows_next = jnp.where(s + 1 < nsb, sbr_ref[jnp.minimum(s + 1, nsb - 1)], 0)
    rows_prev = jnp.where(s > 0, sbr_ref[jnp.maximum(s - 1, 0)], 0)

    def slab(ref, r):
        return ref.at[pl.ds(pl.multiple_of(r * ROW_TILE, ROW_TILE), ROW_TILE)]

    def row_in(table, r):
        tok = lax.shift_right_logical(jnp.maximum(table[0, 0, r], 0), 2)
        return pltpu.make_async_copy(slab(hf_hbm, tok), slab(xbuf, r), gsem)

    def row_out(r):
        return pltpu.make_async_copy(slab(obuf, r), slab(ys_hbm, src_ref[0, 0, r]), ssem)

    def start_gather(table, n):
        def group(gi, c):
            for j in range(unroll):
                row_in(table, gi * unroll + j).start()
            return c

        lax.fori_loop(0, n // unroll, group, 0)

    def wait_sub_blocks(buf, other, sem, n):
        for sub in range(nsubs):
            @pl.when(sub < n)
            def _(sub=sub):
                pltpu.make_async_copy(other.at[pl.ds(0, sub_rows)], buf.at[pl.ds(sub * sub_rows, sub_rows)], sem).wait()

    def wait_scatter(n):
        full = n // MOE_SUB
        for sub in range(nsubs):
            @pl.when(sub < full)
            def _(sub=sub):
                pltpu.make_async_copy(obuf.at[pl.ds(sub * sub_rows, sub_rows)], ys_hbm.at[pl.ds(0, sub_rows)], ssem).wait()

        def one(r, c):
            row_out(r).wait()
            return c

        lax.fori_loop(0, n - full * MOE_SUB, one, 0)

    @pl.when((f == 0) & (s == 0) & (rows > 0))
    def _():
        start_gather(src_ref, nsub * MOE_SUB)

    @pl.when((f == 0) & (rows > 0))
    def _():
        wait_sub_blocks(xbuf, hf_hbm, gsem, nsub)

    @pl.when((f == nf - 1) & (rows > 0) & (rows_prev > 0))
    def _():
        wait_scatter(rows_prev)

    for sub in range(nsubs):
        @pl.when(sub < nsub)
        def _(sub=sub):
            sl = slice(sub * MOE_SUB, (sub + 1) * MOE_SUB)
            base = sub * MOE_SUB * ROW_TILE

            @pl.when(f == 0)
            def _():
                for sidx in range(ROW_TILE):
                    piece = xbuf[pl.ds(base + sidx, MOE_SUB, stride=ROW_TILE), :]
                    x2d[sl, sidx * LANES:(sidx + 1) * LANES] = piece.astype(BF16)

            x = x2d[sl, :]
            hb = _dot(x, w1_ref[0]) + b1_ref[0]
            even = (lax.broadcasted_iota(jnp.int32, (MOE_SUB, LANES), 1) % 2) == 0
            acts = []
            for cb in range(MOE_FT // LANES):
                a = hb[:, cb * LANES:(cb + 1) * LANES]
                b = hb[:, MOE_FT + cb * LANES:MOE_FT + (cb + 1) * LANES]
                hg = jnp.where(even, a, pltpu.roll(b, 1, axis=1))
                hl = jnp.where(even, pltpu.roll(a, LANES - 1, axis=1), b)
                xg = jnp.minimum(hg, SWIGLU_LIMIT)
                xl = jnp.clip(hl, -SWIGLU_LIMIT, SWIGLU_LIMIT)
                acts.append((xg * jax.nn.sigmoid(SWIGLU_ALPHA * xg) * (xl + 1.0)).astype(BF16))
            contrib = _dot(jnp.concatenate(acts, axis=1), w2_ref[0])

            @pl.when(f == 0)
            def _():
                acc[sl, :] = contrib

            @pl.when(f > 0)
            def _():
                acc[sl, :] += contrib

            @pl.when(f == nf - 1)
            def _():
                y = (acc[sl, :] + b2_ref[0]) * gate_ref[sl, :]
                for sidx in range(ROW_TILE):
                    obuf[pl.ds(base + sidx, MOE_SUB, stride=ROW_TILE), :] = y[:, sidx * LANES:(sidx + 1) * LANES]

    @pl.when((f == 0) & (rows_next > 0))
    def _():
        start_gather(nsrc_ref, ceil_sub(rows_next) * MOE_SUB)

    @pl.when((f == nf - 1) & (rows > 0))
    def _():
        def group(gi, c):
            for j in range(unroll):
                row_out(gi * unroll + j).start()
            return c

        def one(r, c):
            row_out(r).start()
            return c

        ngroups = rows // unroll
        lax.fori_loop(0, ngroups, group, 0)
        lax.fori_loop(ngroups * unroll, rows, one, 0)

        @pl.when(rows_next == 0)
        def _():
            wait_scatter(rows)


def moe_experts(sb_e, sb_rows, row_src, row_gate, hf, w1, b1, w2, b2):
    d = D_MODEL
    t = hf.shape[0] // ROW_TILE
    nsb = sb_e.shape[0]
    nf = D_EXPERT // MOE_FT
    grid_spec = pltpu.PrefetchScalarGridSpec(
        num_scalar_prefetch=2,
        grid=(nsb, nf),
        in_specs=[
            pl.BlockSpec((1, 1, MOE_SUPER), lambda s, f, e, r: (s, 0, 0), memory_space=pltpu.SMEM),
            pl.BlockSpec((1, 1, MOE_SUPER), lambda s, f, e, r: (jnp.minimum(s + 1, nsb - 1), 0, 0),
                         memory_space=pltpu.SMEM),
            pl.BlockSpec((MOE_SUPER, 1), lambda s, f, e, r: (s, 0)),
            pl.BlockSpec(memory_space=pl.ANY),
            pl.BlockSpec((1, d, 2 * MOE_FT), lambda s, f, e, r: (e[s], 0, f)),
            pl.BlockSpec((1, 1, 2 * MOE_FT), lambda s, f, e, r: (e[s], 0, f)),
            pl.BlockSpec((1, MOE_FT, d), lambda s, f, e, r: (e[s], f, 0)),
            pl.BlockSpec((1, 1, d), lambda s, f, e, r: (e[s], 0, 0)),
        ],
        out_specs=pl.BlockSpec(memory_space=pl.ANY),
        scratch_shapes=[pltpu.VMEM((MOE_SUPER * ROW_TILE, LANES), F32), pltpu.VMEM((MOE_SUPER, d), BF16),
                        pltpu.VMEM((MOE_SUPER, d), F32), pltpu.VMEM((MOE_SUPER * ROW_TILE, LANES), F32),
                        pltpu.SemaphoreType.DMA, pltpu.SemaphoreType.DMA],
    )
    return pl.pallas_call(
        _moe_kernel,
        grid_spec=grid_spec,
        out_shape=jax.ShapeDtypeStruct((t * TOP_K * ROW_TILE, LANES), F32),
        compiler_params=_cparams(("arbitrary", "arbitrary")),
        name="moe_experts",
    )(sb_e, sb_rows, row_src, row_src, row_gate, hf, w1, b1, w2, b2)


def _w2_prep_kernel(w_ref, o_ref, tmp_ref):
    half = MOE_FT // 2
    for c in range(ROW_TILE):
        cols = slice(c * LANES, (c + 1) * LANES)
        tmp_ref[c, pl.ds(0, half, stride=2), :] = w_ref[0, 0:half, cols]
        tmp_ref[c, pl.ds(1, half, stride=2), :] = w_ref[0, half:, cols]
        o_ref[0, :, cols] = tmp_ref[c].astype(BF16)


def expert_w2_prep(w2):
    e, fdim, d = w2.shape
    return pl.pallas_call(
        _w2_prep_kernel,
        grid=(e, fdim // MOE_FT),
        in_specs=[pl.BlockSpec((1, MOE_FT, d), lambda i, j: (i, j, 0))],
        out_specs=pl.BlockSpec((1, MOE_FT, d), lambda i, j: (i, j, 0)),
        out_shape=jax.ShapeDtypeStruct((e, fdim, d), BF16),
        scratch_shapes=[pltpu.VMEM((ROW_TILE, MOE_FT, LANES), F32)],
        compiler_params=_cparams(("parallel", "parallel")),
        name="expert_w2_prep",
    )(w2)


def _combine_kernel(ys_ref, x_ref, g_ref, o_ref, tmp_ref):
    tm = x_ref.shape[0]
    tot = ys_ref[:, 0:ROW_TILE, :]
    for j in range(1, TOP_K):
        tot = tot + ys_ref[:, j * ROW_TILE:(j + 1) * ROW_TILE, :]
    tmp_ref[...] = tot.reshape(tm * ROW_TILE, LANES)
    pieces = [tmp_ref[pl.ds(sidx, tm, stride=ROW_TILE), :] for sidx in range(ROW_TILE)]
    out = x_ref[...] + jnp.concatenate(pieces, axis=1)
    o_ref[...] = _rms(out, g_ref[...], NORM_EPS)


def combine_final(ys, x, g, *, tm):
    t, d = x.shape
    return pl.pallas_call(
        _combine_kernel,
        grid=(t // tm,),
        in_specs=[pl.BlockSpec((tm, TOP_K * ROW_TILE, LANES), lambda i: (i, 0, 0)),
                  pl.BlockSpec((tm, d), lambda i: (i, 0)), pl.BlockSpec((1, d), lambda i: (0, 0))],
        out_specs=pl.BlockSpec((tm, d), lambda i: (i, 0)),
        out_shape=jax.ShapeDtypeStruct((t, d), F32),
        scratch_shapes=[pltpu.VMEM((tm * ROW_TILE, LANES), F32)],
        compiler_params=_cparams(("parallel",)),
        name="combine_final",
    )(ys, x, g.reshape(1, d))


def _routing_tables(top_idx, gate, n_super):
    flat_e = top_idx.reshape(-1)
    n = flat_e.shape[0]
    onehot = (flat_e[:, None] == jnp.arange(N_EXPERTS, dtype=jnp.int32)[None, :]).astype(jnp.int32)
    csum = jnp.cumsum(onehot, axis=0)
    rank = jnp.sum(onehot * csum, axis=1) - 1
    counts = csum[-1]
    nsb = (counts + MOE_SUPER - 1) // MOE_SUPER
    sb_end = jnp.cumsum(nsb)
    sb_start = sb_end - nsb
    dest = sb_start[flat_e] * MOE_SUPER + rank
    row_src = jnp.full((n_super * MOE_SUPER,), -1, jnp.int32).at[dest].set(jnp.arange(n, dtype=jnp.int32))
    row_gate = jnp.zeros((n_super * MOE_SUPER,), F32).at[dest].set(gate.reshape(-1))
    s_ids = jnp.arange(n_super, dtype=jnp.int32)
    sb_e = jnp.minimum(jnp.searchsorted(sb_end, s_ids, side="right"), N_EXPERTS - 1).astype(jnp.int32)
    local = s_ids - sb_start[sb_e]
    sb_rows = jnp.clip(counts[sb_e] - local * MOE_SUPER, 0, MOE_SUPER)
    sb_rows = jnp.where(s_ids < sb_end[-1], sb_rows, 0).astype(jnp.int32)
    return sb_e, sb_rows, row_src.reshape(n_super, 1, MOE_SUPER), row_gate.reshape(-1, 1)


def _pad_rows(a, n):
    return jnp.pad(a, ((0, n - a.shape[0]), (0, 0)))


def _layer(x, mem, l, p, batch, seq):
    t = batch * seq
    c = RWKV_WIDTH
    lam_init = 0.8 - 0.6 * math.exp(-0.3 * l)
    w_in = p["w_in"]
    o1 = 3 * c + DECAY_LORA + AAA_LORA + GATE_LORA
    o2 = o1 + 3 * DIFF_WIDTH
    padc = lambda a, n: jnp.pad(a, ((0, 0), (0, n - a.shape[1])))
    w_rwkv = jnp.concatenate([
        w_in[:, :3 * c],
        padc(w_in[:, 3 * c:3 * c + DECAY_LORA], 128),
        padc(w_in[:, 3 * c + DECAY_LORA:3 * c + DECAY_LORA + AAA_LORA], 128),
        padc(w_in[:, 3 * c + DECAY_LORA + AAA_LORA:o1], 256)], axis=1).astype(BF16)
    mu = p["rwkv_mu"]
    pad1 = lambda a, n: jnp.pad(a, (0, n - a.shape[0]))
    mu_p = jnp.concatenate([mu[:3 * c], pad1(mu[3 * c:3 * c + DECAY_LORA], 128),
                            pad1(mu[3 * c + DECAY_LORA:3 * c + DECAY_LORA + AAA_LORA], 128),
                            pad1(mu[3 * c + DECAY_LORA + AAA_LORA:], 256)])
    w_diff = w_in[:, o1:o2].astype(BF16)
    w_gate = w_in[:, o2:].astype(BF16)

    g_mix = p["norm_mix_g"]
    p_rwkv = norm_matmul(x, g_mix, w_rwkv, tm=1024, tn=RWKV_COLS // 2, out_dtype=F32, name="in_proj_rwkv")
    qkv = norm_matmul(x, g_mix, w_diff, tm=1024, tn=1024, out_dtype=BF16, name="in_proj_diff")
    gates = norm_matmul(x, g_mix, w_gate, tm=1024, tn=1024, out_dtype=BF16, act="sigmoid", name="in_proj_gate")

    prep = rwkv_prep(p_rwkv, mu_p, p["rwkv_w0"], p["rwkv_a0"], p["rwkv_k_k"], p["rwkv_k_a"],
                     _pad_rows(p["rwkv_w2"], 128), _pad_rows(p["rwkv_a2"], 128), _pad_rows(p["rwkv_g2"], 256),
                     seq=seq, tm=256)
    ra = rwkv_scan(*prep, p["rwkv_ln_w"], p["rwkv_ln_b"], p["rwkv_r_k"].reshape(-1), batch=batch, seq=seq)

    slopes = (2.0 ** (-8.0 * jnp.arange(1, DIFF_HEADS + 1, dtype=F32) / DIFF_HEADS)).astype(F32)
    da = diff_attention(qkv, slopes, p["diff_lq1"], p["diff_lk1"], p["diff_lq2"], p["diff_lk2"],
                        p["diff_subln_g"], batch=batch, seq=seq, tq=min(512, seq), lam_init=lam_init)

    x1 = mix_project(ra, da, gates, x, p["rwkv_proj"].astype(BF16), p["diff_proj"].astype(BF16),
                     p["w_out"].astype(BF16), tm=256)

    m_len = mem.shape[0] // batch
    kv = norm_matmul(mem, p["norm_mem_g"], p["cross_wkv"].astype(BF16), tm=min(512, mem.shape[0]),
                     tn=2 * CROSS_WIDTH, out_dtype=BF16, name="cross_kv")
    rw = jnp.pad(p["router_w"], ((0, 0), (0, LANES - N_EXPERTS)))
    rb = jnp.pad(p["router_b"], (0, LANES - N_EXPERTS)).reshape(1, LANES)
    x2, hf, logits = cross_attention(x1, p["norm_cross_g"], p["cross_wq"].astype(BF16),
                                     kv.reshape(batch, m_len, 2 * CROSS_WIDTH), p["cross_wo"].astype(BF16),
                                     p["norm_ffn_g"], rw, rb, seq=seq, tm=256)

    top_idx, gate = router_topk(logits, tm=min(1024, t))
    n_super = (t * TOP_K) // MOE_SUPER + N_EXPERTS
    sb_e, sb_rows, row_src, row_gate = _routing_tables(top_idx, gate, n_super)
    ys = moe_experts(sb_e, sb_rows, row_src, row_gate, hf, p["expert_w1"].astype(BF16),
                     p["expert_b1"][:, None, :], expert_w2_prep(p["expert_w2"]), p["expert_b2"][:, None, :])
    return x2, ys.reshape(t, TOP_K * ROW_TILE, LANES)


def kernel(x, mem, norm_mix_g, w_in, rwkv_mu, rwkv_w0, rwkv_w2, rwkv_a0, rwkv_a2, rwkv_g2, rwkv_k_k, rwkv_k_a, rwkv_r_k, rwkv_ln_w, rwkv_ln_b, rwkv_proj, diff_lq1, diff_lk1, diff_lq2, diff_lk2, diff_subln_g, diff_proj, w_out, norm_cross_g, norm_mem_g, cross_wq, cross_wkv, cross_wo, norm_ffn_g, router_w, router_b, expert_w1, expert_b1, expert_w2, expert_b2, final_norm_g):
    batch, seq, d = x.shape
    stacked = dict(norm_mix_g=norm_mix_g, w_in=w_in, rwkv_mu=rwkv_mu, rwkv_w0=rwkv_w0, rwkv_w2=rwkv_w2,
                   rwkv_a0=rwkv_a0, rwkv_a2=rwkv_a2, rwkv_g2=rwkv_g2, rwkv_k_k=rwkv_k_k, rwkv_k_a=rwkv_k_a,
                   rwkv_r_k=rwkv_r_k, rwkv_ln_w=rwkv_ln_w, rwkv_ln_b=rwkv_ln_b, rwkv_proj=rwkv_proj,
                   diff_lq1=diff_lq1, diff_lk1=diff_lk1, diff_lq2=diff_lq2, diff_lk2=diff_lk2,
                   diff_subln_g=diff_subln_g, diff_proj=diff_proj, w_out=w_out, norm_cross_g=norm_cross_g,
                   norm_mem_g=norm_mem_g, cross_wq=cross_wq, cross_wkv=cross_wkv, cross_wo=cross_wo,
                   norm_ffn_g=norm_ffn_g, router_w=router_w, router_b=router_b, expert_w1=expert_w1,
                   expert_b1=expert_b1, expert_w2=expert_w2, expert_b2=expert_b2)
    assert w_in.shape[0] == 1, "the closing RMSNorm is fused into the single layer's combine"
    p = {k: v[0] for k, v in stacked.items()}
    x2, ys = _layer(x.reshape(batch * seq, d), mem.reshape(-1, d), 0, p, batch, seq)
    out = combine_final(ys, x2, final_norm_g, tm=256)
    return out.reshape(batch, seq, d)
```

```python
import functools
import math

import jax
import jax.numpy as jnp
from jax import lax
from jax.experimental import pallas as pl
from jax.experimental.pallas import tpu as pltpu

F32 = jnp.float32
BF16 = jnp.bfloat16

D_MODEL = 2048
NORM_EPS = 1e-5
LOG2E = 1.4426950408889634
LANES = 128
HEAD_DIM = 64
RWKV_WIDTH = 1024
SLAB = 256
RWKV_SLABS = RWKV_WIDTH // SLAB
HEADS_PER_SLAB = SLAB // HEAD_DIM
CHUNK = 64
RWKV_GN_EPS = 64e-5
DECAY_LORA = 64
AAA_LORA = 64
GATE_LORA = 160
LORA_PAD = 512
RWKV_COLS = 3 * RWKV_WIDTH + LORA_PAD
DIFF_WIDTH = 1024
DIFF_HEADS = 8
CROSS_HEADS = 4
CROSS_WIDTH = 512
N_EXPERTS = 32
TOP_K = 4
D_EXPERT = 2048
SWIGLU_ALPHA = 1.702
SWIGLU_LIMIT = 7.0
MOE_SUPER = 1024
MOE_SUB = 256
MOE_FT = 512
ROW_TILE = D_MODEL // LANES
VMEM_LIMIT = 56 * 1024 * 1024

NN = (((1,), (0,)), ((), ()))
NT = (((1,), (1,)), ((), ()))


def _dot(a, b, dims=NN):
    return lax.dot_general(a, b, dims, preferred_element_type=F32)


def _split2(a):
    hi = a.astype(BF16)
    lo = (a - hi.astype(F32)).astype(BF16)
    return hi, lo


def _split3(a):
    hi = a.astype(BF16)
    r1 = a - hi.astype(F32)
    mid = r1.astype(BF16)
    lo = (r1 - mid.astype(F32)).astype(BF16)
    return hi, mid, lo


def _dot3(a, b, dims=NN):
    ah, al = _split2(a)
    bh, bl = _split2(b)
    return _dot(ah, bh, dims) + (_dot(ah, bl, dims) + _dot(al, bh, dims))


def _dot_exact_rhs(a, b_bf16, dims=NN):
    h, m, l = _split3(a)
    return _dot(h, b_bf16, dims) + (_dot(m, b_bf16, dims) + _dot(l, b_bf16, dims))


def _rms(x, g, eps):
    ms = jnp.mean(x * x, axis=-1, keepdims=True)
    return x * lax.rsqrt(ms + eps) * g


def _cparams(sem):
    return pltpu.CompilerParams(dimension_semantics=sem, vmem_limit_bytes=VMEM_LIMIT)


def _norm_matmul_kernel(x_ref, g_ref, w_ref, o_ref, h_ref, *, act):
    @pl.when(pl.program_id(1) == 0)
    def _():
        h_ref[...] = _rms(x_ref[...], g_ref[...], NORM_EPS).astype(BF16)

    y = _dot(h_ref[...], w_ref[...])
    if act == "sigmoid":
        y = jax.nn.sigmoid(y)
    o_ref[...] = y.astype(o_ref.dtype)


def norm_matmul(x, g, w, *, tm, tn, out_dtype, act=None, name):
    m, d = x.shape
    n = w.shape[1]
    return pl.pallas_call(
        functools.partial(_norm_matmul_kernel, act=act),
        grid=(m // tm, n // tn),
        in_specs=[
            pl.BlockSpec((tm, d), lambda i, j: (i, 0)),
            pl.BlockSpec((1, d), lambda i, j: (0, 0)),
            pl.BlockSpec((d, tn), lambda i, j: (0, j)),
        ],
        out_specs=pl.BlockSpec((tm, tn), lambda i, j: (i, j)),
        out_shape=jax.ShapeDtypeStruct((m, n), out_dtype),
        scratch_shapes=[pltpu.VMEM((tm, d), BF16)],
        compiler_params=_cparams(("parallel", "arbitrary")),
        name=name,
    )(x, g.reshape(1, d), w)


def _head_ones():
    r = lax.broadcasted_iota(jnp.int32, (SLAB, SLAB), 0)
    c = lax.broadcasted_iota(jnp.int32, (SLAB, SLAB), 1)
    return ((r // HEAD_DIM) == (c // HEAD_DIM)).astype(BF16)


def _rwkv_prep_kernel(p_ref, prev_ref, mu_ref, w0_ref, a0_ref, kk_ref, ka_ref, w2_ref, a2_ref, g2_ref,
                      r_out, lw_out, k_out, v_out, kkn_out, b_out, g_out, *, tiles_per_seq):
    c = RWKV_WIDTH
    i = pl.program_id(0)
    p = p_ref[...]
    tm = p.shape[0]
    first = (i % tiles_per_seq) == 0
    prev_row = jnp.where(first, 0.0, prev_ref[7:8, :])
    row = lax.broadcasted_iota(jnp.int32, p.shape, 0)
    shifted = jnp.where(row == 0, prev_row, pltpu.roll(p, 1, axis=0))
    ps = p + (shifted - p) * mu_ref[...]
    r = ps[:, 0:c]
    k = ps[:, c:2 * c]
    v = ps[:, 2 * c:3 * c]
    wd = ps[:, 3 * c:3 * c + 128]
    ad = ps[:, 3 * c + 128:3 * c + 256]
    gd = ps[:, 3 * c + 256:3 * c + 512]
    z = -(w0_ref[...] + _dot3(jnp.tanh(wd), w2_ref[...]))
    softplus = jnp.maximum(z, 0.0) + jnp.log1p(jnp.exp(-jnp.abs(z)))
    w = -softplus - 0.5
    lw = -jnp.exp(w)
    a = jax.nn.sigmoid(a0_ref[...] + _dot3(ad, a2_ref[...]))
    g = _dot3(jax.nn.sigmoid(gd), g2_ref[...])
    kkr = k * kk_ref[...]
    k2 = k * (1.0 + (a - 1.0) * ka_ref[...])
    ones = _head_ones()
    for q in range(RWKV_SLABS):
        sl = slice(q * SLAB, (q + 1) * SLAB)
        x = kkr[:, sl]
        ss = _dot_exact_rhs(x * x, ones)
        kkn = x / jnp.maximum(jnp.sqrt(ss), 1e-12)
        r_out[q] = r[:, sl]
        lw_out[q] = lw[:, sl]
        k_out[q] = k2[:, sl]
        v_out[q] = v[:, sl]
        kkn_out[q] = kkn
        b_out[q] = kkn * a[:, sl]
        g_out[q] = g[:, sl]


def rwkv_prep(p, mu, w0, a0, k_k, k_a, w2p, a2p, g2p, *, seq, tm):
    t, cols = p.shape
    c = RWKV_WIDTH
    vec = lambda n: pl.BlockSpec((1, n), lambda i: (0, 0))
    full = lambda a: pl.BlockSpec(a.shape, lambda i: (0, 0))
    out_spec = pl.BlockSpec((RWKV_SLABS, tm, SLAB), lambda i: (0, i, 0))
    out_shape = jax.ShapeDtypeStruct((RWKV_SLABS, t, SLAB), F32)
    return pl.pallas_call(
        functools.partial(_rwkv_prep_kernel, tiles_per_seq=seq // tm),
        grid=(t // tm,),
        in_specs=[
            pl.BlockSpec((tm, cols), lambda i: (i, 0)),
            pl.BlockSpec((8, cols), lambda i: (jnp.maximum(i * (tm // 8) - 1, 0), 0)),
            vec(cols), vec(c), vec(c), vec(c), vec(c), full(w2p), full(a2p), full(g2p),
        ],
        out_specs=[out_spec] * 7,
        out_shape=[out_shape] * 7,
        compiler_params=_cparams(("parallel",)),
        name="rwkv_prep",
    )(p, p, mu.reshape(1, cols), w0.reshape(1, c), a0.reshape(1, c), k_k.reshape(1, c), k_a.reshape(1, c),
      w2p, a2p, g2p)


def _dot1(a, b, dims=NN):
    return _dot(a.astype(BF16), b.astype(BF16), dims)


def _dot3s(a, b, dims=NN):
    ah, al = _split2(a)
    bh, bl = _split2(b)
    m = a.shape[0]
    lhs = jnp.concatenate([ah, al], axis=0)
    if dims is NN:
        n = b.shape[1]
        rhs = jnp.concatenate([bh, bl], axis=1)
    else:
        n = b.shape[0]
        rhs = jnp.concatenate([bh, bl], axis=0)
    p = _dot(lhs, rhs, dims)
    return p[:m, :n] + (p[:m, n:] + p[m:, :n])


def _sum3_exact_rhs(xs, ones):
    parts = []
    for x in xs:
        parts.extend(_split3(x))
    res = _dot(jnp.concatenate(parts, axis=0), ones)
    L = xs[0].shape[0]
    return [res[(3 * i) * L:(3 * i + 1) * L] + (res[(3 * i + 1) * L:(3 * i + 2) * L] + res[(3 * i + 2) * L:(3 * i + 3) * L])
            for i in range(len(xs))]


def _rwkv_scan_kernel(r_ref, lw_ref, k_ref, v_ref, kk_ref, b_ref, g_ref, lnw_ref, lnb_ref, rk_ref,
                      o_ref, st_ref):
    L = CHUNK
    W = SLAB

    @pl.when(pl.program_id(1) == 0)
    def _():
        st_ref[...] = jnp.zeros_like(st_ref)

    t_i = lax.broadcasted_iota(jnp.int32, (L, W), 0)
    lane = lax.broadcasted_iota(jnp.int32, (L, W), 1)
    j_i = lane % HEAD_DIM
    hid = lane // HEAD_DIM
    strict = j_i < t_i
    incl = j_i <= t_i
    eye = (j_i == t_i).astype(F32)
    r2 = lax.broadcasted_iota(jnp.int32, (W, W), 0)
    c2 = lax.broadcasted_iota(jnp.int32, (W, W), 1)
    same_head = (r2 // HEAD_DIM) == (c2 // HEAD_DIM)
    diag = r2 == c2
    ones = same_head.astype(BF16)
    tr = lax.broadcasted_iota(jnp.int32, (L, L), 0)
    tc = lax.broadcasted_iota(jnp.int32, (L, L), 1)
    tri = (tc <= tr).astype(BF16)

    def bd(x):
        return jnp.concatenate([jnp.where(hid == h, x, 0.0) for h in range(HEADS_PER_SLAB)], axis=0)

    U = range(RWKV_SLABS)
    cat0 = lambda xs: jnp.concatenate(xs, axis=0)
    cat1 = lambda xs: jnp.concatenate(xs, axis=1)
    r = [r_ref[u] for u in U]
    lw = [lw_ref[u] for u in U]
    k = [k_ref[u] for u in U]
    v = [v_ref[u] for u in U]
    c3 = [_dot(tri, cat1(_split3(lw[u]))) for u in U]
    cum = [c3[u][:, 0:W] + (c3[u][:, W:2 * W] + c3[u][:, 2 * W:]) for u in U]
    cum_end = [cum[u][L - 1:L, :] for u in U]
    e_cum = [jnp.exp(cum[u]) for u in U]
    e_neg = [jnp.exp(-cum[u]) for u in U]
    e_end = [jnp.exp(cum_end[u] - cum[u]) for u in U]
    at = [-kk_ref[u] * jnp.exp(cum[u] - lw[u]) for u in U]
    rt = [r[u] * e_cum[u] for u in U]
    bt = [b_ref[u] * e_neg[u] for u in U]
    kt = [k[u] * e_neg[u] for u in U]
    bh = [b_ref[u] * e_end[u] for u in U]
    kh = [k[u] * e_end[u] for u in U]
    w_end = [jnp.exp(cum_end[u]) for u in U]

    g_all = [_dot1(cat0([at[u], rt[u]]), cat0([bd(bt[u]), bd(kt[u])]), NT) for u in U]
    a_ab = [jnp.where(strict, g_all[u][0:L, 0:W], 0.0) for u in U]
    a_ak = [jnp.where(strict, g_all[u][0:L, W:], 0.0) for u in U]
    a_rb = [jnp.where(incl, g_all[u][L:, 0:W], 0.0) for u in U]
    a_rk = [jnp.where(incl, g_all[u][L:, W:], 0.0) for u in U]

    tinv = [eye for u in U]
    pw = a_ab
    for it in range(6):
        if it < 5:
            res = [_dot1(pw[u], cat1([bd(tinv[u]), bd(pw[u])])) for u in U]
            tinv = [tinv[u] + res[u][:, 0:W] for u in U]
            pw = [res[u][:, W:] for u in U]
        else:
            tinv = [tinv[u] + _dot1(pw[u], bd(tinv[u])) for u in U]

    bdv = [bd(v[u]) for u in U]
    av = [_dot1(a_ak[u], bdv[u]) for u in U]
    qp = [_dot1(tinv[u], cat1([bd(at[u]), bd(av[u])])) for u in U]
    q1 = [qp[u][:, 0:W] for u in U]
    p1 = [qp[u][:, W:] for u in U]
    qp2 = [_dot1(cat1([a_rb[u], a_rk[u]]),
                  cat0([cat1([bd(q1[u]), bd(p1[u])]), cat1([jnp.zeros_like(bdv[u]), bdv[u]])])) for u in U]
    q2 = [rt[u] + qp2[u][:, 0:W] for u in U]
    p2 = [qp2[u][:, W:] for u in U]
    mp = [_dot1(cat0([bh[u], kh[u]]).T,
                 cat0([cat1([q1[u], p1[u]]), cat1([jnp.zeros_like(v[u]), v[u]])])) for u in U]
    m_bd = [jnp.where(same_head, mp[u][:, 0:W], 0.0) + jnp.where(diag, w_end[u], 0.0) for u in U]
    ys = [_dot3s(cat0([q2[u], m_bd[u]]), st_ref[u]) for u in U]
    y = [ys[u][0:L] + p2[u] for u in U]
    for u in U:
        st_ref[u] = ys[u][L:] + jnp.where(same_head, mp[u][:, W:], 0.0)

    sums = [_sum3_exact_rhs([y[u], r[u] * k[u] * rk_ref[u]], ones) for u in U]
    yc = [y[u] - sums[u][0] * (1.0 / HEAD_DIM) for u in U]
    var = [_sum3_exact_rhs([yc[u] * yc[u]], ones)[0] for u in U]
    for u in U:
        yn = yc[u] * lax.rsqrt(var[u] * (1.0 / HEAD_DIM) + RWKV_GN_EPS) * lnw_ref[u] + lnb_ref[u]
        o_ref[u] = ((yn + sums[u][1] * v[u]) * g_ref[u]).astype(o_ref.dtype)


def rwkv_scan(r, lw, k, v, kk, b, g, ln_w, ln_b, r_k, *, batch, seq):
    nchunk = seq // CHUNK
    blk = pl.BlockSpec((RWKV_SLABS, CHUNK, SLAB), lambda bi, ci: (0, bi * nchunk + ci, 0))
    par = pl.BlockSpec((RWKV_SLABS, 1, SLAB), lambda bi, ci: (0, 0, 0))
    t = batch * seq
    return pl.pallas_call(
        _rwkv_scan_kernel,
        grid=(batch, nchunk),
        in_specs=[blk] * 7 + [par] * 3,
        out_specs=blk,
        out_shape=jax.ShapeDtypeStruct((RWKV_SLABS, t, SLAB), BF16),
        scratch_shapes=[pltpu.VMEM((RWKV_SLABS, SLAB, SLAB), F32)],
        compiler_params=_cparams(("arbitrary", "arbitrary")),
        name="rwkv_scan",
    )(r, lw, k, v, kk, b, g, ln_w.reshape(RWKV_SLABS, 1, SLAB), ln_b.reshape(RWKV_SLABS, 1, SLAB),
      r_k.reshape(RWKV_SLABS, 1, SLAB))


def _diff_attn_kernel(qi_ref, ki_ref, slope_ref, q_ref, k_ref, v_ref, lq1_ref, lk1_ref, lq2_ref, lk2_ref, sg_ref,
                      o_ref, qs_ref, relb_ref, m_ref, l_ref, acc_ref, *, tq, lam_init):
    h = pl.program_id(1)
    qi = qi_ref[pl.program_id(2)]
    ki = ki_ref[pl.program_id(2)]
    c2 = slope_ref[h] * LOG2E

    @pl.when(ki == 0)
    def _():
        m_ref[...] = jnp.full_like(m_ref, -jnp.inf)
        l_ref[...] = jnp.zeros_like(l_ref)
        acc_ref[...] = jnp.zeros_like(acc_ref)
        q = q_ref[...].astype(F32) * (HEAD_DIM ** -0.5 * LOG2E)
        lane = lax.broadcasted_iota(jnp.int32, q.shape, 1)
        m0 = lane < HEAD_DIM
        qs_ref[0:tq, :] = jnp.where(m0, q, 0.0).astype(BF16)
        qs_ref[tq:, :] = jnp.where(m0, 0.0, q).astype(BF16)
        rel = (lax.broadcasted_iota(jnp.int32, (tq, tq), 1) - lax.broadcasted_iota(jnp.int32, (tq, tq), 0))
        relb = rel.astype(F32) * (-c2)
        relb_ref[:, 0:tq] = relb
        relb_ref[:, tq:] = relb

    def step(masked):
        k = k_ref[...]
        vt = v_ref[...].T
        relb = relb_ref[...]
        tile_bias = c2 * ((qi - ki) * tq).astype(F32)
        s = _dot(k, qs_ref[...], NT) + relb
        if masked:
            s = jnp.where(relb > 0.0, -jnp.inf, s)
        m_prev = m_ref[...]
        m_new = jnp.maximum(m_prev, jnp.max(s, axis=0, keepdims=True) - tile_bias)
        alpha = jnp.exp2(m_prev - m_new)
        p = jnp.exp2(s - (m_new + tile_bias))
        l_ref[...] = alpha * l_ref[...] + jnp.sum(p, axis=0, keepdims=True)
        acc_ref[...] = alpha * acc_ref[...] + _dot(vt, p.astype(BF16))
        m_ref[...] = m_new

    @pl.when(ki < qi)
    def _():
        step(False)

    @pl.when(ki == qi)
    def _():
        step(True)
        lam = (jnp.exp(jnp.sum(lq1_ref[...] * lk1_ref[...], axis=-1, keepdims=True))
               - jnp.exp(jnp.sum(lq2_ref[...] * lk2_ref[...], axis=-1, keepdims=True)) + lam_init)
        ot = acc_ref[:, 0:tq] / l_ref[:, 0:tq] - lam * (acc_ref[:, tq:] / l_ref[:, tq:])
        o = _rms(ot.T, sg_ref[...], NORM_EPS) * (1.0 - lam_init)
        o_ref[...] = o.astype(o_ref.dtype)


def diff_attention(qkv, slopes, lq1, lk1, lq2, lk2, subln_g, *, batch, seq, tq, lam_init):
    t = batch * seq
    nq = seq // tq
    hb = DIFF_WIDTH // LANES
    pairs = [(qi, ki) for qi in range(nq) for ki in range(qi + 1)]
    qi_tab = jnp.asarray([pq for pq, _ in pairs], jnp.int32)
    ki_tab = jnp.asarray([pk for _, pk in pairs], jnp.int32)
    small = pl.BlockSpec((1, HEAD_DIM), lambda b, h, j, qt, kt: (0, 0))
    grid_spec = pltpu.PrefetchScalarGridSpec(
        num_scalar_prefetch=2,
        grid=(batch, DIFF_HEADS, len(pairs)),
        in_specs=[
            pl.BlockSpec(memory_space=pltpu.SMEM),
            pl.BlockSpec((tq, LANES), lambda b, h, j, qt, kt: (b * nq + qt[j], h)),
            pl.BlockSpec((tq, LANES), lambda b, h, j, qt, kt: (b * nq + kt[j], hb + h)),
            pl.BlockSpec((tq, LANES), lambda b, h, j, qt, kt: (b * nq + kt[j], 2 * hb + h)),
            small, small, small, small,
            pl.BlockSpec((1, LANES), lambda b, h, j, qt, kt: (0, 0)),
        ],
        out_specs=pl.BlockSpec((tq, LANES), lambda b, h, j, qt, kt: (b * nq + qt[j], h)),
        scratch_shapes=[pltpu.VMEM((2 * tq, LANES), BF16), pltpu.VMEM((tq, 2 * tq), F32),
                        pltpu.VMEM((1, 2 * tq), F32), pltpu.VMEM((1, 2 * tq), F32),
                        pltpu.VMEM((LANES, 2 * tq), F32)],
    )
    return pl.pallas_call(
        functools.partial(_diff_attn_kernel, tq=tq, lam_init=lam_init),
        grid_spec=grid_spec,
        out_shape=jax.ShapeDtypeStruct((t, DIFF_WIDTH), BF16),
        compiler_params=_cparams(("parallel", "parallel", "arbitrary")),
        name="diff_attention",
    )(qi_tab, ki_tab, slopes, qkv, qkv, qkv, lq1.reshape(1, -1), lk1.reshape(1, -1), lq2.reshape(1, -1),
      lk2.reshape(1, -1), subln_g.reshape(1, -1))


def _mix_kernel(ra_ref, da_ref, ga_ref, gb_ref, x_ref, wa_ref, wb_ref, wo_ref, o_ref):
    ya = _dot(ra_ref[0], wa_ref[0:SLAB, :])
    for q in range(1, RWKV_SLABS):
        ya = ya + _dot(ra_ref[q], wa_ref[q * SLAB:(q + 1) * SLAB, :])
    yb = _dot(da_ref[...], wb_ref[...])
    mixed = ga_ref[...].astype(F32) * ya + gb_ref[...].astype(F32) * yb
    o_ref[...] = x_ref[...] + _dot(mixed.astype(BF16), wo_ref[...])


def mix_project(ra, da, gates, x, wa, wb, wo, *, tm):
    t, d = x.shape
    const = lambda a: pl.BlockSpec(a.shape, lambda i: (0, 0), pipeline_mode=pl.Buffered(1))
    return pl.pallas_call(
        _mix_kernel,
        grid=(t // tm,),
        in_specs=[
            pl.BlockSpec((RWKV_SLABS, tm, SLAB), lambda i: (0, i, 0)),
            pl.BlockSpec((tm, DIFF_WIDTH), lambda i: (i, 0)),
            pl.BlockSpec((tm, d), lambda i: (i, 0)),
            pl.BlockSpec((tm, d), lambda i: (i, 1)),
            pl.BlockSpec((tm, d), lambda i: (i, 0)),
            const(wa), const(wb), const(wo),
        ],
        out_specs=pl.BlockSpec((tm, d), lambda i: (i, 0)),
        out_shape=jax.ShapeDtypeStruct((t, d), F32),
        compiler_params=_cparams(("parallel",)),
        name="mix_project",
    )(ra, da, gates, gates, x, wa, wb, wo)


def _cross_kernel(x_ref, gc_ref, wq_ref, kv_ref, wo_ref, gf_ref, rw_ref, rb_ref, x2_ref, hf_ref, lg_ref):
    x = x_ref[...]
    h = _rms(x, gc_ref[...], NORM_EPS).astype(BF16)
    q = _dot(h, wq_ref[...]).astype(BF16)
    scale = LANES ** -0.5
    outs = []
    for hd in range(CROSS_HEADS):
        qh = q[:, hd * LANES:(hd + 1) * LANES]
        kh = kv_ref[0, :, hd * LANES:(hd + 1) * LANES]
        vh = kv_ref[0, :, CROSS_WIDTH + hd * LANES:CROSS_WIDTH + (hd + 1) * LANES]
        s = _dot(qh, kh, NT) * scale
        s = s - jnp.max(s, axis=-1, keepdims=True)
        e = jnp.exp(s)
        p = e / jnp.sum(e, axis=-1, keepdims=True)
        outs.append(_dot(p.astype(BF16), vh))
    o = jnp.concatenate(outs, axis=1).astype(BF16)
    x2 = x + _dot(o, wo_ref[...])
    x2_ref[...] = x2
    hf = _rms(x2, gf_ref[...], NORM_EPS)
    tm = hf.shape[0]
    for sidx in range(ROW_TILE):
        hf_ref[pl.ds(sidx, tm, stride=ROW_TILE), :] = hf[:, sidx * LANES:(sidx + 1) * LANES]
    lg_ref[...] = _dot3(hf, rw_ref[...]) + rb_ref[...]


def cross_attention(x, gc, wq, kv, wo, gf, rw, rb, *, seq, tm):
    t, d = x.shape
    per_seq = seq // tm
    const = lambda a: pl.BlockSpec(a.shape, lambda i: (0,) * a.ndim, pipeline_mode=pl.Buffered(1))
    vec = lambda n: pl.BlockSpec((1, n), lambda i: (0, 0))
    return pl.pallas_call(
        _cross_kernel,
        grid=(t // tm,),
        in_specs=[
            pl.BlockSpec((tm, d), lambda i: (i, 0)),
            vec(d), const(wq),
            pl.BlockSpec((1,) + kv.shape[1:], lambda i: (i // per_seq, 0, 0)),
            const(wo), vec(d), const(rw), vec(LANES),
        ],
        out_specs=[pl.BlockSpec((tm, d), lambda i: (i, 0)), pl.BlockSpec((tm * ROW_TILE, LANES), lambda i: (i, 0)),
                   pl.BlockSpec((tm, LANES), lambda i: (i, 0))],
        out_shape=[jax.ShapeDtypeStruct((t, d), F32), jax.ShapeDtypeStruct((t * ROW_TILE, LANES), F32),
                   jax.ShapeDtypeStruct((t, LANES), F32)],
        compiler_params=_cparams(("parallel",)),
        name="cross_attention",
    )(x, gc.reshape(1, d), wq, kv, wo, gf.reshape(1, d), rw, rb)


def _router_kernel(lg_ref, idx_ref, gate_ref):
    x = lg_ref[...]
    lane = lax.broadcasted_iota(jnp.int32, x.shape, 1)
    x = jnp.where(lane < N_EXPERTS, x, -jnp.inf)
    idx_out = jnp.zeros(x.shape, jnp.int32)
    val_out = jnp.zeros(x.shape, F32)
    vals = []
    for j in range(TOP_K):
        m = jnp.max(x, axis=-1, keepdims=True)
        sel = jnp.min(jnp.where(x == m, lane, LANES), axis=-1, keepdims=True)
        idx_out = jnp.where(lane == j, sel, idx_out)
        vals.append(m)
        x = jnp.where(lane == sel, -jnp.inf, x)
    es = [jnp.exp(vj - vals[0]) for vj in vals]
    tot = es[0] + es[1] + es[2] + es[3]
    for j in range(TOP_K):
        val_out = jnp.where(lane == j, es[j] / tot, val_out)
    idx_ref[...] = idx_out[:, 0:TOP_K]
    gate_ref[...] = val_out[:, 0:TOP_K]


def router_topk(logits, *, tm):
    t = logits.shape[0]
    return pl.pallas_call(
        _router_kernel,
        grid=(t // tm,),
        in_specs=[pl.BlockSpec((tm, LANES), lambda i: (i, 0))],
        out_specs=[pl.BlockSpec((tm, TOP_K), lambda i: (i, 0)), pl.BlockSpec((tm, TOP_K), lambda i: (i, 0))],
        out_shape=[jax.ShapeDtypeStruct((t, TOP_K), jnp.int32), jax.ShapeDtypeStruct((t, TOP_K), F32)],
        compiler_params=_cparams(("parallel",)),
        name="router_topk",
    )(logits)


def _moe_kernel(sbe_ref, sbr_ref, src_ref, nsrc_ref, hf_hbm, w1_ref, b1_ref, w2_ref, b2_ref,
                ys_hbm, xbuf, x2d, acc, obuf, gsem, ssem):
    s = pl.program_id(0)
    f = pl.program_id(1)
    nf = pl.num_programs(1)
    nsb = pl.num_programs(0)
    sub_rows = MOE_SUB * ROW_TILE
    nsubs = MOE_SUPER // MOE_SUB
    unroll = 8

    def ceil_sub(n):
        return (n + (MOE_SUB - 1)) // MOE_SUB

    rows = sbr_ref[s]
    nsub = ceil_sub(rows)
    rows_next = jnp.where(s + 1 < nsb, sbr_ref[jnp.minimum(s + 1, nsb - 1)], 0)
    rows_prev = jnp.where(s > 0, sbr_ref[jnp.maximum(s - 1, 0)], 0)

    def slab(ref, r):
        return ref.at[pl.ds(pl.multiple_of(r * ROW_TILE, ROW_TILE), ROW_TILE)]

    def row_in(table, r):
        tok = lax.shift_right_logical(jnp.maximum(table[0, 0, r], 0), 2)
        return pltpu.make_async_copy(slab(hf_hbm, tok), slab(xbuf, r), gsem)

    def row_out(r):
        return pltpu.make_async_copy(slab(obuf, r), slab(ys_hbm, src_ref[0, 0, r]), ssem)

    def start_gather(table, n):
        def group(gi, c):
            for j in range(unroll):
                row_in(table, gi * unroll + j).start()
            return c

        lax.fori_loop(0, n // unroll, group, 0)

    def wait_sub_blocks(buf, other, sem, n):
        for sub in range(nsubs):
            @pl.when(sub < n)
            def _(sub=sub):
                pltpu.make_async_copy(other.at[pl.ds(0, sub_rows)], buf.at[pl.ds(sub * sub_rows, sub_rows)], sem).wait()

    def wait_scatter(n):
        full = n // MOE_SUB
        for sub in range(nsubs):
            @pl.when(sub < full)
            def _(sub=sub):
                pltpu.make_async_copy(obuf.at[pl.ds(sub * sub_rows, sub_rows)], ys_hbm.at[pl.ds(0, sub_rows)], ssem).wait()

        def one(r, c):
            row_out(r).wait()
            return c

        lax.fori_loop(0, n - full * MOE_SUB, one, 0)

    @pl.when((f == 0) & (s == 0) & (rows > 0))
    def _():
        start_gather(src_ref, nsub * MOE_SUB)

    @pl.when((f == 0) & (rows > 0))
    def _():
        wait_sub_blocks(xbuf, hf_hbm, gsem, nsub)

    @pl.when((f == nf - 1) & (rows > 0) & (rows_prev > 0))
    def _():
        wait_scatter(rows_prev)

    for sub in range(nsubs):
        @pl.when(sub < nsub)
        def _(sub=sub):
            sl = slice(sub * MOE_SUB, (sub + 1) * MOE_SUB)
            base = sub * MOE_SUB * ROW_TILE

            @pl.when(f == 0)
            def _():
                for sidx in range(ROW_TILE):
                    piece = xbuf[pl.ds(base + sidx, MOE_SUB, stride=ROW_TILE), :]
                    x2d[sl, sidx * LANES:(sidx + 1) * LANES] = piece.astype(BF16)

            x = x2d[sl, :]
            hb = _dot(x, w1_ref[0]) + b1_ref[0]
            even = (lax.broadcasted_iota(jnp.int32, (MOE_SUB, LANES), 1) % 2) == 0
            acts = []
            for cb in range(MOE_FT // LANES):
                a = hb[:, cb * LANES:(cb + 1) * LANES]
                b = hb[:, MOE_FT + cb * LANES:MOE_FT + (cb + 1) * LANES]
                hg = jnp.where(even, a, pltpu.roll(b, 1, axis=1))
                hl = jnp.where(even, pltpu.roll(a, LANES - 1, axis=1), b)
                xg = jnp.minimum(hg, SWIGLU_LIMIT)
                xl = jnp.clip(hl, -SWIGLU_LIMIT, SWIGLU_LIMIT)
                acts.append((xg * jax.nn.sigmoid(SWIGLU_ALPHA * xg) * (xl + 1.0)).astype(BF16))
            contrib = _dot(jnp.concatenate(acts, axis=1), w2_ref[0])

            @pl.when(f == 0)
            def _():
                acc[sl, :] = contrib

            @pl.when(f > 0)
            def _():
                acc[sl, :] += contrib

            @pl.when(f == nf - 1)
            def _():
                y = acc[sl, :] + b2_ref[0]
                for sidx in range(ROW_TILE):
                    obuf[pl.ds(base + sidx, MOE_SUB, stride=ROW_TILE), :] = y[:, sidx * LANES:(sidx + 1) * LANES]

    @pl.when((f == 0) & (rows_next > 0))
    def _():
        start_gather(nsrc_ref, ceil_sub(rows_next) * MOE_SUB)

    @pl.when((f == nf - 1) & (rows > 0))
    def _():
        def group(gi, c):
            for j in range(unroll):
                row_out(gi * unroll + j).start()
            return c

        def one(r, c):
            row_out(r).start()
            return c

        ngroups = rows // unroll
        lax.fori_loop(0, ngroups, group, 0)
        lax.fori_loop(ngroups * unroll, rows, one, 0)

        @pl.when(rows_next == 0)
        def _():
            wait_scatter(rows)


def moe_experts(sb_e, sb_rows, row_src, hf, w1, b1, w2, b2):
    d = D_MODEL
    t = hf.shape[0] // ROW_TILE
    nsb = sb_e.shape[0]
    nf = D_EXPERT // MOE_FT
    grid_spec = pltpu.PrefetchScalarGridSpec(
        num_scalar_prefetch=2,
        grid=(nsb, nf),
        in_specs=[
            pl.BlockSpec((1, 1, MOE_SUPER), lambda s, f, e, r: (s, 0, 0), memory_space=pltpu.SMEM),
            pl.BlockSpec((1, 1, MOE_SUPER), lambda s, f, e, r: (jnp.minimum(s + 1, nsb - 1), 0, 0),
                         memory_space=pltpu.SMEM),
            pl.BlockSpec(memory_space=pl.ANY),
            pl.BlockSpec((1, d, 2 * MOE_FT), lambda s, f, e, r: (e[s], 0, f)),
            pl.BlockSpec((1, 1, 2 * MOE_FT), lambda s, f, e, r: (e[s], 0, f)),
            pl.BlockSpec((1, MOE_FT, d), lambda s, f, e, r: (e[s], f, 0)),
            pl.BlockSpec((1, 1, d), lambda s, f, e, r: (e[s], 0, 0)),
        ],
        out_specs=pl.BlockSpec(memory_space=pl.ANY),
        scratch_shapes=[pltpu.VMEM((MOE_SUPER * ROW_TILE, LANES), F32), pltpu.VMEM((MOE_SUPER, d), BF16),
                        pltpu.VMEM((MOE_SUPER, d), F32), pltpu.VMEM((MOE_SUPER * ROW_TILE, LANES), F32),
                        pltpu.SemaphoreType.DMA, pltpu.SemaphoreType.DMA],
    )
    return pl.pallas_call(
        _moe_kernel,
        grid_spec=grid_spec,
        out_shape=jax.ShapeDtypeStruct((t * TOP_K * ROW_TILE, LANES), F32),
        compiler_params=_cparams(("arbitrary", "arbitrary")),
        name="moe_experts",
    )(sb_e, sb_rows, row_src, row_src, hf, w1, b1, w2, b2)


def _w2_prep_kernel(w_ref, o_ref, tmp_ref):
    half = MOE_FT // 2
    for c in range(ROW_TILE):
        cols = slice(c * LANES, (c + 1) * LANES)
        tmp_ref[c, pl.ds(0, half, stride=2), :] = w_ref[0, 0:half, cols]
        tmp_ref[c, pl.ds(1, half, stride=2), :] = w_ref[0, half:, cols]
        o_ref[0, :, cols] = tmp_ref[c].astype(BF16)


def expert_w2_prep(w2):
    e, fdim, d = w2.shape
    return pl.pallas_call(
        _w2_prep_kernel,
        grid=(e, fdim // MOE_FT),
        in_specs=[pl.BlockSpec((1, MOE_FT, d), lambda i, j: (i, j, 0))],
        out_specs=pl.BlockSpec((1, MOE_FT, d), lambda i, j: (i, j, 0)),
        out_shape=jax.ShapeDtypeStruct((e, fdim, d), BF16),
        scratch_shapes=[pltpu.VMEM((ROW_TILE, MOE_FT, LANES), F32)],
        compiler_params=_cparams(("parallel", "parallel")),
        name="expert_w2_prep",
    )(w2)


def _combine_kernel(ys_ref, gate_ref, x_ref, g_ref, o_ref, tmp_ref):
    tm = x_ref.shape[0]
    tot = ys_ref[:, 0:ROW_TILE, :] * gate_ref[:, 0:1, :]
    for j in range(1, TOP_K):
        tot = tot + ys_ref[:, j * ROW_TILE:(j + 1) * ROW_TILE, :] * gate_ref[:, j:j + 1, :]
    tmp_ref[...] = tot.reshape(tm * ROW_TILE, LANES)
    pieces = [tmp_ref[pl.ds(sidx, tm, stride=ROW_TILE), :] for sidx in range(ROW_TILE)]
    out = x_ref[...] + jnp.concatenate(pieces, axis=1)
    o_ref[...] = _rms(out, g_ref[...], NORM_EPS)


def combine_final(ys, gate, x, g, *, tm):
    t, d = x.shape
    return pl.pallas_call(
        _combine_kernel,
        grid=(t // tm,),
        in_specs=[pl.BlockSpec((tm, TOP_K * ROW_TILE, LANES), lambda i: (i, 0, 0)),
                  pl.BlockSpec((tm, TOP_K, LANES), lambda i: (i, 0, 0)),
                  pl.BlockSpec((tm, d), lambda i: (i, 0)), pl.BlockSpec((1, d), lambda i: (0, 0))],
        out_specs=pl.BlockSpec((tm, d), lambda i: (i, 0)),
        out_shape=jax.ShapeDtypeStruct((t, d), F32),
        scratch_shapes=[pltpu.VMEM((tm * ROW_TILE, LANES), F32)],
        compiler_params=_cparams(("parallel",)),
        name="combine_final",
    )(ys, gate, x, g.reshape(1, d))


def _routing_tables(top_idx, n_super):
    flat_e = top_idx.reshape(-1)
    n = flat_e.shape[0]
    onehot = (flat_e[:, None] == jnp.arange(N_EXPERTS, dtype=jnp.int32)[None, :]).astype(jnp.int32)
    csum = jnp.cumsum(onehot, axis=0)
    rank = jnp.sum(onehot * csum, axis=1) - 1
    counts = csum[-1]
    nsb = (counts + MOE_SUPER - 1) // MOE_SUPER
    sb_end = jnp.cumsum(nsb)
    sb_start = sb_end - nsb
    dest = sb_start[flat_e] * MOE_SUPER + rank
    row_src = jnp.full((n_super * MOE_SUPER,), -1, jnp.int32).at[dest].set(jnp.arange(n, dtype=jnp.int32))
    s_ids = jnp.arange(n_super, dtype=jnp.int32)
    sb_e = jnp.minimum(jnp.searchsorted(sb_end, s_ids, side="right"), N_EXPERTS - 1).astype(jnp.int32)
    local = s_ids - sb_start[sb_e]
    sb_rows = jnp.clip(counts[sb_e] - local * MOE_SUPER, 0, MOE_SUPER)
    sb_rows = jnp.where(s_ids < sb_end[-1], sb_rows, 0).astype(jnp.int32)
    return sb_e, sb_rows, row_src.reshape(n_super, 1, MOE_SUPER)


def _pad_rows(a, n):
    return jnp.pad(a, ((0, n - a.shape[0]), (0, 0)))


def _layer(x, mem, l, p, batch, seq):
    t = batch * seq
    c = RWKV_WIDTH
    lam_init = 0.8 - 0.6 * math.exp(-0.3 * l)
    w_in = p["w_in"]
    o1 = 3 * c + DECAY_LORA + AAA_LORA + GATE_LORA
    o2 = o1 + 3 * DIFF_WIDTH
    padc = lambda a, n: jnp.pad(a, ((0, 0), (0, n - a.shape[1])))
    w_rwkv = jnp.concatenate([
        w_in[:, :3 * c],
        padc(w_in[:, 3 * c:3 * c + DECAY_LORA], 128),
        padc(w_in[:, 3 * c + DECAY_LORA:3 * c + DECAY_LORA + AAA_LORA], 128),
        padc(w_in[:, 3 * c + DECAY_LORA + AAA_LORA:o1], 256)], axis=1).astype(BF16)
    mu = p["rwkv_mu"]
    pad1 = lambda a, n: jnp.pad(a, (0, n - a.shape[0]))
    mu_p = jnp.concatenate([mu[:3 * c], pad1(mu[3 * c:3 * c + DECAY_LORA], 128),
                            pad1(mu[3 * c + DECAY_LORA:3 * c + DECAY_LORA + AAA_LORA], 128),
                            pad1(mu[3 * c + DECAY_LORA + AAA_LORA:], 256)])
    w_diff = w_in[:, o1:o2].astype(BF16)
    w_gate = w_in[:, o2:].astype(BF16)

    g_mix = p["norm_mix_g"]
    p_rwkv = norm_matmul(x, g_mix, w_rwkv, tm=1024, tn=RWKV_COLS // 2, out_dtype=F32, name="in_proj_rwkv")
    qkv = norm_matmul(x, g_mix, w_diff, tm=1024, tn=1024, out_dtype=BF16, name="in_proj_diff")
    gates = norm_matmul(x, g_mix, w_gate, tm=1024, tn=1024, out_dtype=BF16, act="sigmoid", name="in_proj_gate")

    prep = rwkv_prep(p_rwkv, mu_p, p["rwkv_w0"], p["rwkv_a0"], p["rwkv_k_k"], p["rwkv_k_a"],
                     _pad_rows(p["rwkv_w2"], 128), _pad_rows(p["rwkv_a2"], 128), _pad_rows(p["rwkv_g2"], 256),
                     seq=seq, tm=256)
    ra = rwkv_scan(*prep, p["rwkv_ln_w"], p["rwkv_ln_b"], p["rwkv_r_k"].reshape(-1), batch=batch, seq=seq)

    slopes = (2.0 ** (-8.0 * jnp.arange(1, DIFF_HEADS + 1, dtype=F32) / DIFF_HEADS)).astype(F32)
    da = diff_attention(qkv, slopes, p["diff_lq1"], p["diff_lk1"], p["diff_lq2"], p["diff_lk2"],
                        p["diff_subln_g"], batch=batch, seq=seq, tq=min(1024, seq), lam_init=lam_init)

    x1 = mix_project(ra, da, gates, x, p["rwkv_proj"].astype(BF16), p["diff_proj"].astype(BF16),
                     p["w_out"].astype(BF16), tm=256)

    m_len = mem.shape[0] // batch
    kv = norm_matmul(mem, p["norm_mem_g"], p["cross_wkv"].astype(BF16), tm=min(512, mem.shape[0]),
                     tn=2 * CROSS_WIDTH, out_dtype=BF16, name="cross_kv")
    rw = jnp.pad(p["router_w"], ((0, 0), (0, LANES - N_EXPERTS)))
    rb = jnp.pad(p["router_b"], (0, LANES - N_EXPERTS)).reshape(1, LANES)
    x2, hf, logits = cross_attention(x1, p["norm_cross_g"], p["cross_wq"].astype(BF16),
                                     kv.reshape(batch, m_len, 2 * CROSS_WIDTH), p["cross_wo"].astype(BF16),
                                     p["norm_ffn_g"], rw, rb, seq=seq, tm=256)

    top_idx, gate = router_topk(logits, tm=min(1024, t))
    n_super = (t * TOP_K) // MOE_SUPER + N_EXPERTS
    sb_e, sb_rows, row_src = _routing_tables(top_idx, n_super)
    ys = moe_experts(sb_e, sb_rows, row_src, hf, p["expert_w1"].astype(BF16),
                     p["expert_b1"][:, None, :], expert_w2_prep(p["expert_w2"]), p["expert_b2"][:, None, :])
    gate_b = jnp.broadcast_to(gate[:, :, None], (t, TOP_K, LANES))
    return x2, ys.reshape(t, TOP_K * ROW_TILE, LANES), gate_b


def kernel(x, mem, norm_mix_g, w_in, rwkv_mu, rwkv_w0, rwkv_w2, rwkv_a0, rwkv_a2, rwkv_g2, rwkv_k_k, rwkv_k_a, rwkv_r_k, rwkv_ln_w, rwkv_ln_b, rwkv_proj, diff_lq1, diff_lk1, diff_lq2, diff_lk2, diff_subln_g, diff_proj, w_out, norm_cross_g, norm_mem_g, cross_wq, cross_wkv, cross_wo, norm_ffn_g, router_w, router_b, expert_w1, expert_b1, expert_w2, expert_b2, final_norm_g):
    batch, seq, d = x.shape
    stacked = dict(norm_mix_g=norm_mix_g, w_in=w_in, rwkv_mu=rwkv_mu, rwkv_w0=rwkv_w0, rwkv_w2=rwkv_w2,
                   rwkv_a0=rwkv_a0, rwkv_a2=rwkv_a2, rwkv_g2=rwkv_g2, rwkv_k_k=rwkv_k_k, rwkv_k_a=rwkv_k_a,
                   rwkv_r_k=rwkv_r_k, rwkv_ln_w=rwkv_ln_w, rwkv_ln_b=rwkv_ln_b, rwkv_proj=rwkv_proj,
                   diff_lq1=diff_lq1, diff_lk1=diff_lk1, diff_lq2=diff_lq2, diff_lk2=diff_lk2,
                   diff_subln_g=diff_subln_g, diff_proj=diff_proj, w_out=w_out, norm_cross_g=norm_cross_g,
                   norm_mem_g=norm_mem_g, cross_wq=cross_wq, cross_wkv=cross_wkv, cross_wo=cross_wo,
                   norm_ffn_g=norm_ffn_g, router_w=router_w, router_b=router_b, expert_w1=expert_w1,
                   expert_b1=expert_b1, expert_w2=expert_w2, expert_b2=expert_b2)
    assert w_in.shape[0] == 1, "the closing RMSNorm is fused into the single layer's combine"
    p = {k: v[0] for k, v in stacked.items()}
    x2, ys, gate_b = _layer(x.reshape(batch * seq, d), mem.reshape(-1, d), 0, p, batch, seq)
    out = combine_final(ys, gate_b, x2, final_norm_g, tm=256)
    return out.reshape(batch, seq, d)
```

```python
import functools
import math

import jax
import jax.numpy as jnp
from jax import lax
from jax.experimental import pallas as pl
from jax.experimental.pallas import tpu as pltpu

F32 = jnp.float32
BF16 = jnp.bfloat16
U32 = jnp.uint32

D_MODEL = 2048
NORM_EPS = 1e-5
LOG2E = 1.4426950408889634
LANES = 128
HEAD_DIM = 64
RWKV_WIDTH = 1024
SLAB = 256
RWKV_SLABS = RWKV_WIDTH // SLAB
HEADS_PER_SLAB = SLAB // HEAD_DIM
CHUNK = 64
RWKV_GN_EPS = 64e-5
DECAY_LORA = 64
AAA_LORA = 64
GATE_LORA = 160
LORA_PAD = 512
RWKV_COLS = 3 * RWKV_WIDTH + LORA_PAD
DIFF_WIDTH = 1024
DIFF_HEADS = 8
CROSS_HEADS = 4
CROSS_WIDTH = 512
N_EXPERTS = 32
TOP_K = 4
D_EXPERT = 2048
SWIGLU_ALPHA = 1.702
SWIGLU_LIMIT = 7.0
MOE_SUPER = 1024
MOE_SUB = 256
MOE_FT = 1024
ROW_TILE = D_MODEL // LANES
PACK_ROWS = ROW_TILE // 2
VMEM_LIMIT = 56 * 1024 * 1024

NN = (((1,), (0,)), ((), ()))
NT = (((1,), (1,)), ((), ()))


def _dot(a, b, dims=NN):
    return lax.dot_general(a, b, dims, preferred_element_type=F32)


def _split2(a):
    hi = a.astype(BF16)
    lo = (a - hi.astype(F32)).astype(BF16)
    return hi, lo


def _split3(a):
    hi = a.astype(BF16)
    r1 = a - hi.astype(F32)
    mid = r1.astype(BF16)
    lo = (r1 - mid.astype(F32)).astype(BF16)
    return hi, mid, lo


def _dot3(a, b, dims=NN):
    ah, al = _split2(a)
    bh, bl = _split2(b)
    return _dot(ah, bh, dims) + (_dot(ah, bl, dims) + _dot(al, bh, dims))


def _dot_exact_rhs(a, b_bf16, dims=NN):
    h, m, l = _split3(a)
    return _dot(h, b_bf16, dims) + (_dot(m, b_bf16, dims) + _dot(l, b_bf16, dims))


def _pack_pair(lo, hi):
    lo_b = lax.shift_right_logical(lax.bitcast_convert_type(lo.astype(BF16).astype(F32), U32), jnp.uint32(16))
    hi_b = lax.bitcast_convert_type(hi.astype(BF16).astype(F32), U32) & jnp.uint32(0xFFFF0000)
    return hi_b | lo_b


def _unpack_pair(w):
    lo = lax.bitcast_convert_type(lax.shift_left(w, jnp.uint32(16)), F32)
    hi = lax.bitcast_convert_type(w & jnp.uint32(0xFFFF0000), F32)
    return lo, hi


def _rms(x, g, eps):
    ms = jnp.mean(x * x, axis=-1, keepdims=True)
    return x * lax.rsqrt(ms + eps) * g


def _cparams(sem):
    return pltpu.CompilerParams(dimension_semantics=sem, vmem_limit_bytes=VMEM_LIMIT)


def _norm_matmul_kernel(x_ref, g_ref, w_ref, o_ref, h_ref, *, act):
    @pl.when(pl.program_id(1) == 0)
    def _():
        h_ref[...] = _rms(x_ref[...], g_ref[...], NORM_EPS).astype(BF16)

    y = _dot(h_ref[...], w_ref[...])
    if act == "sigmoid":
        y = jax.nn.sigmoid(y)
    o_ref[...] = y.astype(o_ref.dtype)


def norm_matmul(x, g, w, *, tm, tn, out_dtype, act=None, name):
    m, d = x.shape
    n = w.shape[1]
    return pl.pallas_call(
        functools.partial(_norm_matmul_kernel, act=act),
        grid=(m // tm, n // tn),
        in_specs=[
            pl.BlockSpec((tm, d), lambda i, j: (i, 0)),
            pl.BlockSpec((1, d), lambda i, j: (0, 0)),
            pl.BlockSpec((d, tn), lambda i, j: (0, j)),
        ],
        out_specs=pl.BlockSpec((tm, tn), lambda i, j: (i, j)),
        out_shape=jax.ShapeDtypeStruct((m, n), out_dtype),
        scratch_shapes=[pltpu.VMEM((tm, d), BF16)],
        compiler_params=_cparams(("parallel", "arbitrary")),
        name=name,
    )(x, g.reshape(1, d), w)


def _head_ones():
    r = lax.broadcasted_iota(jnp.int32, (SLAB, SLAB), 0)
    c = lax.broadcasted_iota(jnp.int32, (SLAB, SLAB), 1)
    return ((r // HEAD_DIM) == (c // HEAD_DIM)).astype(BF16)


def _rwkv_prep_kernel(p_ref, prev_ref, mu_ref, w0_ref, a0_ref, kk_ref, ka_ref, w2_ref, a2_ref, g2_ref,
                      r_out, lw_out, k_out, v_out, kkn_out, b_out, g_out, *, tiles_per_seq):
    c = RWKV_WIDTH
    i = pl.program_id(0)
    p = p_ref[...]
    tm = p.shape[0]
    first = (i % tiles_per_seq) == 0
    prev_row = jnp.where(first, 0.0, prev_ref[7:8, :])
    row = lax.broadcasted_iota(jnp.int32, p.shape, 0)
    shifted = jnp.where(row == 0, prev_row, pltpu.roll(p, 1, axis=0))
    ps = p + (shifted - p) * mu_ref[...]
    r = ps[:, 0:c]
    k = ps[:, c:2 * c]
    v = ps[:, 2 * c:3 * c]
    wd = ps[:, 3 * c:3 * c + 128]
    ad = ps[:, 3 * c + 128:3 * c + 256]
    gd = ps[:, 3 * c + 256:3 * c + 512]
    z = -(w0_ref[...] + _dot3(jnp.tanh(wd), w2_ref[...]))
    softplus = jnp.maximum(z, 0.0) + jnp.log1p(jnp.exp(-jnp.abs(z)))
    w = -softplus - 0.5
    lw = -jnp.exp(w)
    a = jax.nn.sigmoid(a0_ref[...] + _dot3(ad, a2_ref[...]))
    g = _dot3(jax.nn.sigmoid(gd), g2_ref[...])
    kkr = k * kk_ref[...]
    k2 = k * (1.0 + (a - 1.0) * ka_ref[...])
    ones = _head_ones()
    for q in range(RWKV_SLABS):
        sl = slice(q * SLAB, (q + 1) * SLAB)
        x = kkr[:, sl]
        ss = _dot_exact_rhs(x * x, ones)
        kkn = x / jnp.maximum(jnp.sqrt(ss), 1e-12)
        r_out[q] = r[:, sl]
        lw_out[q] = lw[:, sl]
        k_out[q] = k2[:, sl]
        v_out[q] = v[:, sl]
        kkn_out[q] = kkn
        b_out[q] = kkn * a[:, sl]
        g_out[q] = g[:, sl]


def rwkv_prep(p, mu, w0, a0, k_k, k_a, w2p, a2p, g2p, *, seq, tm):
    t, cols = p.shape
    c = RWKV_WIDTH
    vec = lambda n: pl.BlockSpec((1, n), lambda i: (0, 0))
    full = lambda a: pl.BlockSpec(a.shape, lambda i: (0, 0))
    out_spec = pl.BlockSpec((RWKV_SLABS, tm, SLAB), lambda i: (0, i, 0))
    out_shape = jax.ShapeDtypeStruct((RWKV_SLABS, t, SLAB), F32)
    return pl.pallas_call(
        functools.partial(_rwkv_prep_kernel, tiles_per_seq=seq // tm),
        grid=(t // tm,),
        in_specs=[
            pl.BlockSpec((tm, cols), lambda i: (i, 0)),
            pl.BlockSpec((8, cols), lambda i: (jnp.maximum(i * (tm // 8) - 1, 0), 0)),
            vec(cols), vec(c), vec(c), vec(c), vec(c), full(w2p), full(a2p), full(g2p),
        ],
        out_specs=[out_spec] * 7,
        out_shape=[out_shape] * 7,
        compiler_params=_cparams(("parallel",)),
        name="rwkv_prep",
    )(p, p, mu.reshape(1, cols), w0.reshape(1, c), a0.reshape(1, c), k_k.reshape(1, c), k_a.reshape(1, c),
      w2p, a2p, g2p)


def _dot1(a, b, dims=NN):
    return _dot(a.astype(BF16), b.astype(BF16), dims)


def _dot3s(a, b, dims=NN):
    ah, al = _split2(a)
    bh, bl = _split2(b)
    m = a.shape[0]
    lhs = jnp.concatenate([ah, al], axis=0)
    if dims is NN:
        n = b.shape[1]
        rhs = jnp.concatenate([bh, bl], axis=1)
    else:
        n = b.shape[0]
        rhs = jnp.concatenate([bh, bl], axis=0)
    p = _dot(lhs, rhs, dims)
    return p[:m, :n] + (p[:m, n:] + p[m:, :n])


def _sum3_exact_rhs(xs, ones):
    parts = []
    for x in xs:
        parts.extend(_split3(x))
    res = _dot(jnp.concatenate(parts, axis=0), ones)
    L = xs[0].shape[0]
    return [res[(3 * i) * L:(3 * i + 1) * L] + (res[(3 * i + 1) * L:(3 * i + 2) * L] + res[(3 * i + 2) * L:(3 * i + 3) * L])
            for i in range(len(xs))]


def _rwkv_scan_kernel(r_ref, lw_ref, k_ref, v_ref, kk_ref, b_ref, g_ref, lnw_ref, lnb_ref, rk_ref,
                      o_ref, st_ref):
    L = CHUNK
    W = SLAB

    @pl.when(pl.program_id(1) == 0)
    def _():
        st_ref[...] = jnp.zeros_like(st_ref)

    t_i = lax.broadcasted_iota(jnp.int32, (L, W), 0)
    lane = lax.broadcasted_iota(jnp.int32, (L, W), 1)
    j_i = lane % HEAD_DIM
    hid = lane // HEAD_DIM
    strict = j_i < t_i
    incl = j_i <= t_i
    eye = (j_i == t_i).astype(F32)
    r2 = lax.broadcasted_iota(jnp.int32, (W, W), 0)
    c2 = lax.broadcasted_iota(jnp.int32, (W, W), 1)
    same_head = (r2 // HEAD_DIM) == (c2 // HEAD_DIM)
    diag = r2 == c2
    ones = same_head.astype(BF16)
    tr = lax.broadcasted_iota(jnp.int32, (L, L), 0)
    tc = lax.broadcasted_iota(jnp.int32, (L, L), 1)
    tri = (tc <= tr).astype(BF16)

    def bd(x):
        return jnp.concatenate([jnp.where(hid == h, x, 0.0) for h in range(HEADS_PER_SLAB)], axis=0)

    U = range(RWKV_SLABS)
    cat0 = lambda xs: jnp.concatenate(xs, axis=0)
    cat1 = lambda xs: jnp.concatenate(xs, axis=1)
    r = [r_ref[u] for u in U]
    lw = [lw_ref[u] for u in U]
    k = [k_ref[u] for u in U]
    v = [v_ref[u] for u in U]
    c3 = [_dot(tri, cat1(_split3(lw[u]))) for u in U]
    cum = [c3[u][:, 0:W] + (c3[u][:, W:2 * W] + c3[u][:, 2 * W:]) for u in U]
    cum_end = [cum[u][L - 1:L, :] for u in U]
    e_cum = [jnp.exp(cum[u]) for u in U]
    e_neg = [jnp.exp(-cum[u]) for u in U]
    e_end = [jnp.exp(cum_end[u] - cum[u]) for u in U]
    at = [-kk_ref[u] * jnp.exp(cum[u] - lw[u]) for u in U]
    rt = [r[u] * e_cum[u] for u in U]
    bt = [b_ref[u] * e_neg[u] for u in U]
    kt = [k[u] * e_neg[u] for u in U]
    bh = [b_ref[u] * e_end[u] for u in U]
    kh = [k[u] * e_end[u] for u in U]
    w_end = [jnp.exp(cum_end[u]) for u in U]

    g_all = [_dot1(cat0([at[u], rt[u]]), cat0([bd(bt[u]), bd(kt[u])]), NT) for u in U]
    a_ab = [jnp.where(strict, g_all[u][0:L, 0:W], 0.0) for u in U]
    a_ak = [jnp.where(strict, g_all[u][0:L, W:], 0.0) for u in U]
    a_rb = [jnp.where(incl, g_all[u][L:, 0:W], 0.0) for u in U]
    a_rk = [jnp.where(incl, g_all[u][L:, W:], 0.0) for u in U]

    tinv = [eye for u in U]
    pw = a_ab
    for it in range(6):
        if it < 5:
            res = [_dot1(pw[u], cat1([bd(tinv[u]), bd(pw[u])])) for u in U]
            tinv = [tinv[u] + res[u][:, 0:W] for u in U]
            pw = [res[u][:, W:] for u in U]
        else:
            tinv = [tinv[u] + _dot1(pw[u], bd(tinv[u])) for u in U]

    bdv = [bd(v[u]) for u in U]
    av = [_dot1(a_ak[u], bdv[u]) for u in U]
    qp = [_dot1(tinv[u], cat1([bd(at[u]), bd(av[u])])) for u in U]
    q1 = [qp[u][:, 0:W] for u in U]
    p1 = [qp[u][:, W:] for u in U]
    qp2 = [_dot1(cat1([a_rb[u], a_rk[u]]),
                  cat0([cat1([bd(q1[u]), bd(p1[u])]), cat1([jnp.zeros_like(bdv[u]), bdv[u]])])) for u in U]
    q2 = [rt[u] + qp2[u][:, 0:W] for u in U]
    p2 = [qp2[u][:, W:] for u in U]
    mp = [_dot1(cat0([bh[u], kh[u]]).T,
                 cat0([cat1([q1[u], p1[u]]), cat1([jnp.zeros_like(v[u]), v[u]])])) for u in U]
    m_bd = [jnp.where(same_head, mp[u][:, 0:W], 0.0) + jnp.where(diag, w_end[u], 0.0) for u in U]
    ys = [_dot3s(cat0([q2[u], m_bd[u]]), st_ref[u]) for u in U]
    y = [ys[u][0:L] + p2[u] for u in U]
    for u in U:
        st_ref[u] = ys[u][L:] + jnp.where(same_head, mp[u][:, W:], 0.0)

    sums = [_sum3_exact_rhs([y[u], r[u] * k[u] * rk_ref[u]], ones) for u in U]
    yc = [y[u] - sums[u][0] * (1.0 / HEAD_DIM) for u in U]
    var = [_sum3_exact_rhs([yc[u] * yc[u]], ones)[0] for u in U]
    for u in U:
        yn = yc[u] * lax.rsqrt(var[u] * (1.0 / HEAD_DIM) + RWKV_GN_EPS) * lnw_ref[u] + lnb_ref[u]
        o_ref[u] = ((yn + sums[u][1] * v[u]) * g_ref[u]).astype(o_ref.dtype)


def rwkv_scan(r, lw, k, v, kk, b, g, ln_w, ln_b, r_k, *, batch, seq):
    nchunk = seq // CHUNK
    blk = pl.BlockSpec((RWKV_SLABS, CHUNK, SLAB), lambda bi, ci: (0, bi * nchunk + ci, 0))
    par = pl.BlockSpec((RWKV_SLABS, 1, SLAB), lambda bi, ci: (0, 0, 0))
    t = batch * seq
    return pl.pallas_call(
        _rwkv_scan_kernel,
        grid=(batch, nchunk),
        in_specs=[blk] * 7 + [par] * 3,
        out_specs=blk,
        out_shape=jax.ShapeDtypeStruct((RWKV_SLABS, t, SLAB), BF16),
        scratch_shapes=[pltpu.VMEM((RWKV_SLABS, SLAB, SLAB), F32)],
        compiler_params=_cparams(("arbitrary", "arbitrary")),
        name="rwkv_scan",
    )(r, lw, k, v, kk, b, g, ln_w.reshape(RWKV_SLABS, 1, SLAB), ln_b.reshape(RWKV_SLABS, 1, SLAB),
      r_k.reshape(RWKV_SLABS, 1, SLAB))


def _diff_attn_kernel(qi_ref, ki_ref, slope_ref, q_ref, k_ref, v_ref, lq1_ref, lk1_ref, lq2_ref, lk2_ref, sg_ref,
                      o_ref, qs_ref, relb_ref, m_ref, l_ref, acc_ref, *, tq, lam_init):
    h = pl.program_id(1)
    qi = qi_ref[pl.program_id(2)]
    ki = ki_ref[pl.program_id(2)]
    c2 = slope_ref[h] * LOG2E

    @pl.when(ki == 0)
    def _():
        m_ref[...] = jnp.full_like(m_ref, -jnp.inf)
        l_ref[...] = jnp.zeros_like(l_ref)
        acc_ref[...] = jnp.zeros_like(acc_ref)
        q = q_ref[...].astype(F32) * (HEAD_DIM ** -0.5 * LOG2E)
        lane = lax.broadcasted_iota(jnp.int32, q.shape, 1)
        m0 = lane < HEAD_DIM
        qs_ref[0:tq, :] = jnp.where(m0, q, 0.0).astype(BF16)
        qs_ref[tq:, :] = jnp.where(m0, 0.0, q).astype(BF16)
        rel = (lax.broadcasted_iota(jnp.int32, (tq, tq), 1) - lax.broadcasted_iota(jnp.int32, (tq, tq), 0))
        relb = rel.astype(F32) * (-c2)
        relb_ref[:, 0:tq] = relb
        relb_ref[:, tq:] = relb

    def step(masked):
        k = k_ref[...]
        vt = v_ref[...].T
        relb = relb_ref[...]
        tile_bias = c2 * ((qi - ki) * tq).astype(F32)
        s = _dot(k, qs_ref[...], NT) + relb
        if masked:
            s = jnp.where(relb > 0.0, -jnp.inf, s)
        m_prev = m_ref[...]
        m_new = jnp.maximum(m_prev, jnp.max(s, axis=0, keepdims=True) - tile_bias)
        alpha = jnp.exp2(m_prev - m_new)
        p = jnp.exp2(s - (m_new + tile_bias))
        l_ref[...] = alpha * l_ref[...] + jnp.sum(p, axis=0, keepdims=True)
        acc_ref[...] = alpha * acc_ref[...] + _dot(vt, p.astype(BF16))
        m_ref[...] = m_new

    @pl.when(ki < qi)
    def _():
        step(False)

    @pl.when(ki == qi)
    def _():
        step(True)
        lam = (jnp.exp(jnp.sum(lq1_ref[...] * lk1_ref[...], axis=-1, keepdims=True))
               - jnp.exp(jnp.sum(lq2_ref[...] * lk2_ref[...], axis=-1, keepdims=True)) + lam_init)
        ot = acc_ref[:, 0:tq] / l_ref[:, 0:tq] - lam * (acc_ref[:, tq:] / l_ref[:, tq:])
        o = _rms(ot.T, sg_ref[...], NORM_EPS) * (1.0 - lam_init)
        o_ref[...] = o.astype(o_ref.dtype)


def diff_attention(qkv, slopes, lq1, lk1, lq2, lk2, subln_g, *, batch, seq, tq, lam_init):
    t = batch * seq
    nq = seq // tq
    hb = DIFF_WIDTH // LANES
    pairs = [(qi, ki) for qi in range(nq) for ki in range(qi + 1)]
    qi_tab = jnp.asarray([pq for pq, _ in pairs], jnp.int32)
    ki_tab = jnp.asarray([pk for _, pk in pairs], jnp.int32)
    small = pl.BlockSpec((1, HEAD_DIM), lambda b, h, j, qt, kt: (0, 0))
    grid_spec = pltpu.PrefetchScalarGridSpec(
        num_scalar_prefetch=2,
        grid=(batch, DIFF_HEADS, len(pairs)),
        in_specs=[
            pl.BlockSpec(memory_space=pltpu.SMEM),
            pl.BlockSpec((tq, LANES), lambda b, h, j, qt, kt: (b * nq + qt[j], h)),
            pl.BlockSpec((tq, LANES), lambda b, h, j, qt, kt: (b * nq + kt[j], hb + h)),
            pl.BlockSpec((tq, LANES), lambda b, h, j, qt, kt: (b * nq + kt[j], 2 * hb + h)),
            small, small, small, small,
            pl.BlockSpec((1, LANES), lambda b, h, j, qt, kt: (0, 0)),
        ],
        out_specs=pl.BlockSpec((tq, LANES), lambda b, h, j, qt, kt: (b * nq + qt[j], h)),
        scratch_shapes=[pltpu.VMEM((2 * tq, LANES), BF16), pltpu.VMEM((tq, 2 * tq), F32),
                        pltpu.VMEM((1, 2 * tq), F32), pltpu.VMEM((1, 2 * tq), F32),
                        pltpu.VMEM((LANES, 2 * tq), F32)],
    )
    return pl.pallas_call(
        functools.partial(_diff_attn_kernel, tq=tq, lam_init=lam_init),
        grid_spec=grid_spec,
        out_shape=jax.ShapeDtypeStruct((t, DIFF_WIDTH), BF16),
        compiler_params=_cparams(("parallel", "parallel", "arbitrary")),
        name="diff_attention",
    )(qi_tab, ki_tab, slopes, qkv, qkv, qkv, lq1.reshape(1, -1), lk1.reshape(1, -1), lq2.reshape(1, -1),
      lk2.reshape(1, -1), subln_g.reshape(1, -1))


def _mix_kernel(ra_ref, da_ref, ga_ref, gb_ref, x_ref, wa_ref, wb_ref, wo_ref, o_ref):
    ya = _dot(ra_ref[0], wa_ref[0:SLAB, :])
    for q in range(1, RWKV_SLABS):
        ya = ya + _dot(ra_ref[q], wa_ref[q * SLAB:(q + 1) * SLAB, :])
    yb = _dot(da_ref[...], wb_ref[...])
    mixed = ga_ref[...].astype(F32) * ya + gb_ref[...].astype(F32) * yb
    o_ref[...] = x_ref[...] + _dot(mixed.astype(BF16), wo_ref[...])


def mix_project(ra, da, gates, x, wa, wb, wo, *, tm):
    t, d = x.shape
    const = lambda a: pl.BlockSpec(a.shape, lambda i: (0, 0), pipeline_mode=pl.Buffered(1))
    return pl.pallas_call(
        _mix_kernel,
        grid=(t // tm,),
        in_specs=[
            pl.BlockSpec((RWKV_SLABS, tm, SLAB), lambda i: (0, i, 0)),
            pl.BlockSpec((tm, DIFF_WIDTH), lambda i: (i, 0)),
            pl.BlockSpec((tm, d), lambda i: (i, 0)),
            pl.BlockSpec((tm, d), lambda i: (i, 1)),
            pl.BlockSpec((tm, d), lambda i: (i, 0)),
            const(wa), const(wb), const(wo),
        ],
        out_specs=pl.BlockSpec((tm, d), lambda i: (i, 0)),
        out_shape=jax.ShapeDtypeStruct((t, d), F32),
        compiler_params=_cparams(("parallel",)),
        name="mix_project",
    )(ra, da, gates, gates, x, wa, wb, wo)


def _cross_kernel(x_ref, gc_ref, wq_ref, kv_ref, wo_ref, gf_ref, rw_ref, rb_ref, x2_ref, hf_ref, lg_ref):
    x = x_ref[...]
    h = _rms(x, gc_ref[...], NORM_EPS).astype(BF16)
    q = _dot(h, wq_ref[...]).astype(BF16)
    scale = LANES ** -0.5
    outs = []
    for hd in range(CROSS_HEADS):
        qh = q[:, hd * LANES:(hd + 1) * LANES]
        kh = kv_ref[0, :, hd * LANES:(hd + 1) * LANES]
        vh = kv_ref[0, :, CROSS_WIDTH + hd * LANES:CROSS_WIDTH + (hd + 1) * LANES]
        s = _dot(qh, kh, NT) * scale
        s = s - jnp.max(s, axis=-1, keepdims=True)
        e = jnp.exp(s)
        p = e / jnp.sum(e, axis=-1, keepdims=True)
        outs.append(_dot(p.astype(BF16), vh))
    o = jnp.concatenate(outs, axis=1).astype(BF16)
    x2 = x + _dot(o, wo_ref[...])
    x2_ref[...] = x2
    hf = _rms(x2, gf_ref[...], NORM_EPS)
    tm = hf.shape[0]
    for c in range(PACK_ROWS):
        hf_ref[pl.ds(c, tm, stride=PACK_ROWS), :] = _pack_pair(hf[:, 2 * c * LANES:(2 * c + 1) * LANES],
                                                               hf[:, (2 * c + 1) * LANES:(2 * c + 2) * LANES])
    lg_ref[...] = _dot3(hf, rw_ref[...]) + rb_ref[...]


def cross_attention(x, gc, wq, kv, wo, gf, rw, rb, *, seq, tm):
    t, d = x.shape
    per_seq = seq // tm
    const = lambda a: pl.BlockSpec(a.shape, lambda i: (0,) * a.ndim, pipeline_mode=pl.Buffered(1))
    vec = lambda n: pl.BlockSpec((1, n), lambda i: (0, 0))
    return pl.pallas_call(
        _cross_kernel,
        grid=(t // tm,),
        in_specs=[
            pl.BlockSpec((tm, d), lambda i: (i, 0)),
            vec(d), const(wq),
            pl.BlockSpec((1,) + kv.shape[1:], lambda i: (i // per_seq, 0, 0)),
            const(wo), vec(d), const(rw), vec(LANES),
        ],
        out_specs=[pl.BlockSpec((tm, d), lambda i: (i, 0)), pl.BlockSpec((tm * PACK_ROWS, LANES), lambda i: (i, 0)),
                   pl.BlockSpec((tm, LANES), lambda i: (i, 0))],
        out_shape=[jax.ShapeDtypeStruct((t, d), F32), jax.ShapeDtypeStruct((t * PACK_ROWS, LANES), U32),
                   jax.ShapeDtypeStruct((t, LANES), F32)],
        compiler_params=_cparams(("parallel",)),
        name="cross_attention",
    )(x, gc.reshape(1, d), wq, kv, wo, gf.reshape(1, d), rw, rb)


def _router_kernel(lg_ref, idx_ref, gate_ref):
    x = lg_ref[...]
    lane = lax.broadcasted_iota(jnp.int32, x.shape, 1)
    x = jnp.where(lane < N_EXPERTS, x, -jnp.inf)
    idx_out = jnp.zeros(x.shape, jnp.int32)
    val_out = jnp.zeros(x.shape, F32)
    vals = []
    for j in range(TOP_K):
        m = jnp.max(x, axis=-1, keepdims=True)
        sel = jnp.min(jnp.where(x == m, lane, LANES), axis=-1, keepdims=True)
        idx_out = jnp.where(lane == j, sel, idx_out)
        vals.append(m)
        x = jnp.where(lane == sel, -jnp.inf, x)
    es = [jnp.exp(vj - vals[0]) for vj in vals]
    tot = es[0] + es[1] + es[2] + es[3]
    for j in range(TOP_K):
        val_out = jnp.where(lane == j, es[j] / tot, val_out)
    idx_ref[...] = idx_out[:, 0:TOP_K]
    gate_ref[...] = val_out[:, 0:TOP_K]


def router_topk(logits, *, tm):
    t = logits.shape[0]
    return pl.pallas_call(
        _router_kernel,
        grid=(t // tm,),
        in_specs=[pl.BlockSpec((tm, LANES), lambda i: (i, 0))],
        out_specs=[pl.BlockSpec((tm, TOP_K), lambda i: (i, 0)), pl.BlockSpec((tm, TOP_K), lambda i: (i, 0))],
        out_shape=[jax.ShapeDtypeStruct((t, TOP_K), jnp.int32), jax.ShapeDtypeStruct((t, TOP_K), F32)],
        compiler_params=_cparams(("parallel",)),
        name="router_topk",
    )(logits)


def _moe_kernel(sbe_ref, sbr_ref, src_ref, nsrc_ref, hf_hbm, w1_ref, b1_ref, w2_ref, b2_ref,
                ys_hbm, xbuf, x2d, acc, obuf, gsem, ssem):
    s = pl.program_id(0)
    f = pl.program_id(1)
    nf = pl.num_programs(1)
    nsb = pl.num_programs(0)
    sub_rows = MOE_SUB * PACK_ROWS
    nsubs = MOE_SUPER // MOE_SUB
    unroll = 8

    def ceil_sub(n):
        return (n + (MOE_SUB - 1)) // MOE_SUB

    rows = sbr_ref[s]
    nsub = ceil_sub(rows)
    rows_next = jnp.where(s + 1 < nsb, sbr_ref[jnp.minimum(s + 1, nsb - 1)], 0)
    rows_prev = jnp.where(s > 0, sbr_ref[jnp.maximum(s - 1, 0)], 0)

    def slab(ref, r):
        return ref.at[pl.ds(pl.multiple_of(r * PACK_ROWS, PACK_ROWS), PACK_ROWS)]

    def row_in(table, r):
        tok = lax.shift_right_logical(jnp.maximum(table[0, 0, r], 0), 2)
        return pltpu.make_async_copy(slab(hf_hbm, tok), slab(xbuf, r), gsem)

    def row_out(r):
        return pltpu.make_async_copy(slab(obuf, r), slab(ys_hbm, src_ref[0, 0, r]), ssem)

    def start_gather(table, n):
        def group(gi, c):
            for j in range(unroll):
                row_in(table, gi * unroll + j).start()
            return c

        lax.fori_loop(0, n // unroll, group, 0)

    def wait_sub_blocks(buf, other, sem, n):
        for sub in range(nsubs):
            @pl.when(sub < n)
            def _(sub=sub):
                pltpu.make_async_copy(other.at[pl.ds(0, sub_rows)], buf.at[pl.ds(sub * sub_rows, sub_rows)], sem).wait()

    def wait_scatter(n):
        full = n // MOE_SUB
        for sub in range(nsubs):
            @pl.when(sub < full)
            def _(sub=sub):
                pltpu.make_async_copy(obuf.at[pl.ds(sub * sub_rows, sub_rows)], ys_hbm.at[pl.ds(0, sub_rows)], ssem).wait()

        def one(r, c):
            row_out(r).wait()
            return c

        lax.fori_loop(0, n - full * MOE_SUB, one, 0)

    @pl.when((f == 0) & (s == 0) & (rows > 0))
    def _():
        start_gather(src_ref, nsub * MOE_SUB)

    @pl.when((f == 0) & (rows > 0))
    def _():
        wait_sub_blocks(xbuf, hf_hbm, gsem, nsub)

    @pl.when((f == nf - 1) & (rows > 0) & (rows_prev > 0))
    def _():
        wait_scatter(rows_prev)

    for sub in range(nsubs):
        @pl.when(sub < nsub)
        def _(sub=sub):
            sl = slice(sub * MOE_SUB, (sub + 1) * MOE_SUB)
            base = sub * sub_rows

            @pl.when(f == 0)
            def _():
                for c in range(PACK_ROWS):
                    lo, hi = _unpack_pair(xbuf[pl.ds(base + c, MOE_SUB, stride=PACK_ROWS), :])
                    x2d[sl, 2 * c * LANES:(2 * c + 1) * LANES] = lo.astype(BF16)
                    x2d[sl, (2 * c + 1) * LANES:(2 * c + 2) * LANES] = hi.astype(BF16)

            x = x2d[sl, :]
            hb = _dot(x, w1_ref[0]) + b1_ref[0]
            even = (lax.broadcasted_iota(jnp.int32, (MOE_SUB, LANES), 1) % 2) == 0
            acts = []
            for cb in range(MOE_FT // LANES):
                a = hb[:, cb * LANES:(cb + 1) * LANES]
                b = hb[:, MOE_FT + cb * LANES:MOE_FT + (cb + 1) * LANES]
                hg = jnp.where(even, a, pltpu.roll(b, 1, axis=1))
                hl = jnp.where(even, pltpu.roll(a, LANES - 1, axis=1), b)
                xg = jnp.minimum(hg, SWIGLU_LIMIT)
                xl = jnp.clip(hl, -SWIGLU_LIMIT, SWIGLU_LIMIT)
                acts.append((xg * jax.nn.sigmoid(SWIGLU_ALPHA * xg) * (xl + 1.0)).astype(BF16))
            contrib = _dot(jnp.concatenate(acts, axis=1), w2_ref[0])

            @pl.when(f == 0)
            def _():
                acc[sl, :] = contrib

            @pl.when(f > 0)
            def _():
                acc[sl, :] += contrib

            @pl.when(f == nf - 1)
            def _():
                y = acc[sl, :] + b2_ref[0]
                for c in range(PACK_ROWS):
                    obuf[pl.ds(base + c, MOE_SUB, stride=PACK_ROWS), :] = _pack_pair(
                        y[:, 2 * c * LANES:(2 * c + 1) * LANES], y[:, (2 * c + 1) * LANES:(2 * c + 2) * LANES])

    @pl.when((f == 0) & (rows_next > 0))
    def _():
        start_gather(nsrc_ref, ceil_sub(rows_next) * MOE_SUB)

    @pl.when((f == nf - 1) & (rows > 0))
    def _():
        def group(gi, c):
            for j in range(unroll):
                row_out(gi * unroll + j).start()
            return c

        def one(r, c):
            row_out(r).start()
            return c

        ngroups = rows // unroll
        lax.fori_loop(0, ngroups, group, 0)
        lax.fori_loop(ngroups * unroll, rows, one, 0)

        @pl.when(rows_next == 0)
        def _():
            wait_scatter(rows)


def moe_experts(sb_e, sb_rows, row_src, hf, w1, b1, w2, b2):
    d = D_MODEL
    t = hf.shape[0] // PACK_ROWS
    nsb = sb_e.shape[0]
    nf = D_EXPERT // MOE_FT
    grid_spec = pltpu.PrefetchScalarGridSpec(
        num_scalar_prefetch=2,
        grid=(nsb, nf),
        in_specs=[
            pl.BlockSpec((1, 1, MOE_SUPER), lambda s, f, e, r: (s, 0, 0), memory_space=pltpu.SMEM),
            pl.BlockSpec((1, 1, MOE_SUPER), lambda s, f, e, r: (jnp.minimum(s + 1, nsb - 1), 0, 0),
                         memory_space=pltpu.SMEM),
            pl.BlockSpec(memory_space=pl.ANY),
            pl.BlockSpec((1, d, 2 * MOE_FT), lambda s, f, e, r: (e[s], 0, f)),
            pl.BlockSpec((1, 1, 2 * MOE_FT), lambda s, f, e, r: (e[s], 0, f)),
            pl.BlockSpec((1, MOE_FT, d), lambda s, f, e, r: (e[s], f, 0)),
            pl.BlockSpec((1, 1, d), lambda s, f, e, r: (e[s], 0, 0)),
        ],
        out_specs=pl.BlockSpec(memory_space=pl.ANY),
        scratch_shapes=[pltpu.VMEM((MOE_SUPER * PACK_ROWS, LANES), U32), pltpu.VMEM((MOE_SUPER, d), BF16),
                        pltpu.VMEM((MOE_SUPER, d), F32), pltpu.VMEM((MOE_SUPER * PACK_ROWS, LANES), U32),
                        pltpu.SemaphoreType.DMA, pltpu.SemaphoreType.DMA],
    )
    return pl.pallas_call(
        _moe_kernel,
        grid_spec=grid_spec,
        out_shape=jax.ShapeDtypeStruct((t * TOP_K * PACK_ROWS, LANES), U32),
        compiler_params=_cparams(("arbitrary", "arbitrary")),
        name="moe_experts",
    )(sb_e, sb_rows, row_src, row_src, hf, w1, b1, w2, b2)


def _w2_prep_kernel(w_ref, o_ref, tmp_ref):
    half = MOE_FT // 2
    for c in range(ROW_TILE):
        cols = slice(c * LANES, (c + 1) * LANES)
        tmp_ref[c, pl.ds(0, half, stride=2), :] = w_ref[0, 0:half, cols]
        tmp_ref[c, pl.ds(1, half, stride=2), :] = w_ref[0, half:, cols]
        o_ref[0, :, cols] = tmp_ref[c].astype(BF16)


def expert_w2_prep(w2):
    e, fdim, d = w2.shape
    return pl.pallas_call(
        _w2_prep_kernel,
        grid=(e, fdim // MOE_FT),
        in_specs=[pl.BlockSpec((1, MOE_FT, d), lambda i, j: (i, j, 0))],
        out_specs=pl.BlockSpec((1, MOE_FT, d), lambda i, j: (i, j, 0)),
        out_shape=jax.ShapeDtypeStruct((e, fdim, d), BF16),
        scratch_shapes=[pltpu.VMEM((ROW_TILE, MOE_FT, LANES), F32)],
        compiler_params=_cparams(("parallel", "parallel")),
        name="expert_w2_prep",
    )(w2)


def _combine_kernel(ys_ref, gate_ref, x_ref, g_ref, o_ref, lo_ref, hi_ref):
    tm = x_ref.shape[0]
    tot_lo = tot_hi = None
    for j in range(TOP_K):
        lo, hi = _unpack_pair(ys_ref[:, j * PACK_ROWS:(j + 1) * PACK_ROWS, :])
        gj = gate_ref[:, j:j + 1, :]
        tot_lo = lo * gj if tot_lo is None else tot_lo + lo * gj
        tot_hi = hi * gj if tot_hi is None else tot_hi + hi * gj
    lo_ref[...] = tot_lo.reshape(tm * PACK_ROWS, LANES)
    hi_ref[...] = tot_hi.reshape(tm * PACK_ROWS, LANES)
    pieces = []
    for c in range(PACK_ROWS):
        pieces.append(lo_ref[pl.ds(c, tm, stride=PACK_ROWS), :])
        pieces.append(hi_ref[pl.ds(c, tm, stride=PACK_ROWS), :])
    out = x_ref[...] + jnp.concatenate(pieces, axis=1)
    o_ref[...] = _rms(out, g_ref[...], NORM_EPS)


def combine_final(ys, gate, x, g, *, tm):
    t, d = x.shape
    return pl.pallas_call(
        _combine_kernel,
        grid=(t // tm,),
        in_specs=[pl.BlockSpec((tm, TOP_K * PACK_ROWS, LANES), lambda i: (i, 0, 0)),
                  pl.BlockSpec((tm, TOP_K, LANES), lambda i: (i, 0, 0)),
                  pl.BlockSpec((tm, d), lambda i: (i, 0)), pl.BlockSpec((1, d), lambda i: (0, 0))],
        out_specs=pl.BlockSpec((tm, d), lambda i: (i, 0)),
        out_shape=jax.ShapeDtypeStruct((t, d), F32),
        scratch_shapes=[pltpu.VMEM((tm * PACK_ROWS, LANES), F32), pltpu.VMEM((tm * PACK_ROWS, LANES), F32)],
        compiler_params=_cparams(("parallel",)),
        name="combine_final",
    )(ys, gate, x, g.reshape(1, d))


def _routing_tables(top_idx, n_super):
    flat_e = top_idx.reshape(-1)
    n = flat_e.shape[0]
    onehot = (flat_e[:, None] == jnp.arange(N_EXPERTS, dtype=jnp.int32)[None, :]).astype(jnp.int32)
    csum = jnp.cumsum(onehot, axis=0)
    rank = jnp.sum(onehot * csum, axis=1) - 1
    counts = csum[-1]
    nsb = (counts + MOE_SUPER - 1) // MOE_SUPER
    sb_end = jnp.cumsum(nsb)
    sb_start = sb_end - nsb
    dest = sb_start[flat_e] * MOE_SUPER + rank
    row_src = jnp.full((n_super * MOE_SUPER,), -1, jnp.int32).at[dest].set(jnp.arange(n, dtype=jnp.int32))
    s_ids = jnp.arange(n_super, dtype=jnp.int32)
    sb_e = jnp.minimum(jnp.searchsorted(sb_end, s_ids, side="right"), N_EXPERTS - 1).astype(jnp.int32)
    local = s_ids - sb_start[sb_e]
    sb_rows = jnp.clip(counts[sb_e] - local * MOE_SUPER, 0, MOE_SUPER)
    sb_rows = jnp.where(s_ids < sb_end[-1], sb_rows, 0).astype(jnp.int32)
    return sb_e, sb_rows, row_src.reshape(n_super, 1, MOE_SUPER)


def _pad_rows(a, n):
    return jnp.pad(a, ((0, n - a.shape[0]), (0, 0)))


def _layer(x, mem, l, p, batch, seq):
    t = batch * seq
    c = RWKV_WIDTH
    lam_init = 0.8 - 0.6 * math.exp(-0.3 * l)
    w_in = p["w_in"]
    o1 = 3 * c + DECAY_LORA + AAA_LORA + GATE_LORA
    o2 = o1 + 3 * DIFF_WIDTH
    padc = lambda a, n: jnp.pad(a, ((0, 0), (0, n - a.shape[1])))
    w_rwkv = jnp.concatenate([
        w_in[:, :3 * c],
        padc(w_in[:, 3 * c:3 * c + DECAY_LORA], 128),
        padc(w_in[:, 3 * c + DECAY_LORA:3 * c + DECAY_LORA + AAA_LORA], 128),
        padc(w_in[:, 3 * c + DECAY_LORA + AAA_LORA:o1], 256)], axis=1).astype(BF16)
    mu = p["rwkv_mu"]
    pad1 = lambda a, n: jnp.pad(a, (0, n - a.shape[0]))
    mu_p = jnp.concatenate([mu[:3 * c], pad1(mu[3 * c:3 * c + DECAY_LORA], 128),
                            pad1(mu[3 * c + DECAY_LORA:3 * c + DECAY_LORA + AAA_LORA], 128),
                            pad1(mu[3 * c + DECAY_LORA + AAA_LORA:], 256)])
    w_diff = w_in[:, o1:o2].astype(BF16)
    w_gate = w_in[:, o2:].astype(BF16)

    g_mix = p["norm_mix_g"]
    p_rwkv = norm_matmul(x, g_mix, w_rwkv, tm=1024, tn=RWKV_COLS // 2, out_dtype=F32, name="in_proj_rwkv")
    qkv = norm_matmul(x, g_mix, w_diff, tm=1024, tn=1024, out_dtype=BF16, name="in_proj_diff")
    gates = norm_matmul(x, g_mix, w_gate, tm=1024, tn=1024, out_dtype=BF16, act="sigmoid", name="in_proj_gate")

    prep = rwkv_prep(p_rwkv, mu_p, p["rwkv_w0"], p["rwkv_a0"], p["rwkv_k_k"], p["rwkv_k_a"],
                     _pad_rows(p["rwkv_w2"], 128), _pad_rows(p["rwkv_a2"], 128), _pad_rows(p["rwkv_g2"], 256),
                     seq=seq, tm=256)
    ra = rwkv_scan(*prep, p["rwkv_ln_w"], p["rwkv_ln_b"], p["rwkv_r_k"].reshape(-1), batch=batch, seq=seq)

    slopes = (2.0 ** (-8.0 * jnp.arange(1, DIFF_HEADS + 1, dtype=F32) / DIFF_HEADS)).astype(F32)
    da = diff_attention(qkv, slopes, p["diff_lq1"], p["diff_lk1"], p["diff_lq2"], p["diff_lk2"],
                        p["diff_subln_g"], batch=batch, seq=seq, tq=min(1024, seq), lam_init=lam_init)

    x1 = mix_project(ra, da, gates, x, p["rwkv_proj"].astype(BF16), p["diff_proj"].astype(BF16),
                     p["w_out"].astype(BF16), tm=256)

    m_len = mem.shape[0] // batch
    kv = norm_matmul(mem, p["norm_mem_g"], p["cross_wkv"].astype(BF16), tm=min(512, mem.shape[0]),
                     tn=2 * CROSS_WIDTH, out_dtype=BF16, name="cross_kv")
    rw = jnp.pad(p["router_w"], ((0, 0), (0, LANES - N_EXPERTS)))
    rb = jnp.pad(p["router_b"], (0, LANES - N_EXPERTS)).reshape(1, LANES)
    x2, hf, logits = cross_attention(x1, p["norm_cross_g"], p["cross_wq"].astype(BF16),
                                     kv.reshape(batch, m_len, 2 * CROSS_WIDTH), p["cross_wo"].astype(BF16),
                                     p["norm_ffn_g"], rw, rb, seq=seq, tm=256)

    top_idx, gate = router_topk(logits, tm=min(1024, t))
    n_super = (t * TOP_K) // MOE_SUPER + N_EXPERTS
    sb_e, sb_rows, row_src = _routing_tables(top_idx, n_super)
    ys = moe_experts(sb_e, sb_rows, row_src, hf, p["expert_w1"].astype(BF16),
                     p["expert_b1"][:, None, :], expert_w2_prep(p["expert_w2"]), p["expert_b2"][:, None, :])
    gate_b = jnp.broadcast_to(gate[:, :, None], (t, TOP_K, LANES))
    return x2, ys.reshape(t, TOP_K * PACK_ROWS, LANES), gate_b


def kernel(x, mem, norm_mix_g, w_in, rwkv_mu, rwkv_w0, rwkv_w2, rwkv_a0, rwkv_a2, rwkv_g2, rwkv_k_k, rwkv_k_a, rwkv_r_k, rwkv_ln_w, rwkv_ln_b, rwkv_proj, diff_lq1, diff_lk1, diff_lq2, diff_lk2, diff_subln_g, diff_proj, w_out, norm_cross_g, norm_mem_g, cross_wq, cross_wkv, cross_wo, norm_ffn_g, router_w, router_b, expert_w1, expert_b1, expert_w2, expert_b2, final_norm_g):
    batch, seq, d = x.shape
    stacked = dict(norm_mix_g=norm_mix_g, w_in=w_in, rwkv_mu=rwkv_mu, rwkv_w0=rwkv_w0, rwkv_w2=rwkv_w2,
                   rwkv_a0=rwkv_a0, rwkv_a2=rwkv_a2, rwkv_g2=rwkv_g2, rwkv_k_k=rwkv_k_k, rwkv_k_a=rwkv_k_a,
                   rwkv_r_k=rwkv_r_k, rwkv_ln_w=rwkv_ln_w, rwkv_ln_b=rwkv_ln_b, rwkv_proj=rwkv_proj,
                   diff_lq1=diff_lq1, diff_lk1=diff_lk1, diff_lq2=diff_lq2, diff_lk2=diff_lk2,
                   diff_subln_g=diff_subln_g, diff_proj=diff_proj, w_out=w_out, norm_cross_g=norm_cross_g,
                   norm_mem_g=norm_mem_g, cross_wq=cross_wq, cross_wkv=cross_wkv, cross_wo=cross_wo,
                   norm_ffn_g=norm_ffn_g, router_w=router_w, router_b=router_b, expert_w1=expert_w1,
                   expert_b1=expert_b1, expert_w2=expert_w2, expert_b2=expert_b2)
    assert w_in.shape[0] == 1, "the closing RMSNorm is fused into the single layer's combine"
    p = {k: v[0] for k, v in stacked.items()}
    x2, ys, gate_b = _layer(x.reshape(batch * seq, d), mem.reshape(-1, d), 0, p, batch, seq)
    out = combine_final(ys, gate_b, x2, final_norm_g, tm=256)
    return out.reshape(batch, seq, d)
```

```python
import functools
import math

import jax
import jax.numpy as jnp
from jax import lax
from jax.experimental import pallas as pl
from jax.experimental.pallas import tpu as pltpu

F32 = jnp.float32
BF16 = jnp.bfloat16
U32 = jnp.uint32

D_MODEL = 2048
NORM_EPS = 1e-5
LOG2E = 1.4426950408889634
LANES = 128
HEAD_DIM = 64
RWKV_WIDTH = 1024
SLAB = 256
RWKV_SLABS = RWKV_WIDTH // SLAB
HEADS_PER_SLAB = SLAB // HEAD_DIM
CHUNK = 64
RWKV_GN_EPS = 64e-5
DECAY_LORA = 64
AAA_LORA = 64
GATE_LORA = 160
LORA_PAD = 512
RWKV_COLS = 3 * RWKV_WIDTH + LORA_PAD
DIFF_WIDTH = 1024
DIFF_HEADS = 8
CROSS_HEADS = 4
CROSS_WIDTH = 512
N_EXPERTS = 32
TOP_K = 4
D_EXPERT = 2048
SWIGLU_ALPHA = 1.702
SWIGLU_LIMIT = 7.0
MOE_SUPER = 1024
MOE_SUB = 256
MOE_FT = 1024
W2_GROUP = 128
ROW_TILE = D_MODEL // LANES
PACK_ROWS = ROW_TILE // 2
VMEM_LIMIT = 56 * 1024 * 1024

NN = (((1,), (0,)), ((), ()))
NT = (((1,), (1,)), ((), ()))


def _dot(a, b, dims=NN):
    return lax.dot_general(a, b, dims, preferred_element_type=F32)


def _split2(a):
    hi = a.astype(BF16)
    lo = (a - hi.astype(F32)).astype(BF16)
    return hi, lo


def _split3(a):
    hi = a.astype(BF16)
    r1 = a - hi.astype(F32)
    mid = r1.astype(BF16)
    lo = (r1 - mid.astype(F32)).astype(BF16)
    return hi, mid, lo


def _dot3(a, b, dims=NN):
    ah, al = _split2(a)
    bh, bl = _split2(b)
    return _dot(ah, bh, dims) + (_dot(ah, bl, dims) + _dot(al, bh, dims))


def _dot_exact_rhs(a, b_bf16, dims=NN):
    h, m, l = _split3(a)
    return _dot(h, b_bf16, dims) + (_dot(m, b_bf16, dims) + _dot(l, b_bf16, dims))


def _pack_pair(lo, hi):
    lo_b = lax.shift_right_logical(lax.bitcast_convert_type(lo.astype(BF16).astype(F32), U32), jnp.uint32(16))
    hi_b = lax.bitcast_convert_type(hi.astype(BF16).astype(F32), U32) & jnp.uint32(0xFFFF0000)
    return hi_b | lo_b


def _unpack_pair(w):
    lo = lax.bitcast_convert_type(lax.shift_left(w, jnp.uint32(16)), F32)
    hi = lax.bitcast_convert_type(w & jnp.uint32(0xFFFF0000), F32)
    return lo, hi


def _rms(x, g, eps):
    ms = jnp.mean(x * x, axis=-1, keepdims=True)
    return x * lax.rsqrt(ms + eps) * g


def _cparams(sem):
    return pltpu.CompilerParams(dimension_semantics=sem, vmem_limit_bytes=VMEM_LIMIT)


def _norm_matmul_kernel(x_ref, g_ref, w_ref, o_ref, h_ref, *, act):
    @pl.when(pl.program_id(1) == 0)
    def _():
        h_ref[...] = _rms(x_ref[...], g_ref[...], NORM_EPS).astype(BF16)

    y = _dot(h_ref[...], w_ref[...])
    if act == "sigmoid":
        y = jax.nn.sigmoid(y)
    o_ref[...] = y.astype(o_ref.dtype)


def norm_matmul(x, g, w, *, tm, tn, out_dtype, act=None, name):
    m, d = x.shape
    n = w.shape[1]
    return pl.pallas_call(
        functools.partial(_norm_matmul_kernel, act=act),
        grid=(m // tm, n // tn),
        in_specs=[
            pl.BlockSpec((tm, d), lambda i, j: (i, 0)),
            pl.BlockSpec((1, d), lambda i, j: (0, 0)),
            pl.BlockSpec((d, tn), lambda i, j: (0, j)),
        ],
        out_specs=pl.BlockSpec((tm, tn), lambda i, j: (i, j)),
        out_shape=jax.ShapeDtypeStruct((m, n), out_dtype),
        scratch_shapes=[pltpu.VMEM((tm, d), BF16)],
        compiler_params=_cparams(("parallel", "arbitrary")),
        name=name,
    )(x, g.reshape(1, d), w)


def _head_ones():
    r = lax.broadcasted_iota(jnp.int32, (SLAB, SLAB), 0)
    c = lax.broadcasted_iota(jnp.int32, (SLAB, SLAB), 1)
    return ((r // HEAD_DIM) == (c // HEAD_DIM)).astype(BF16)


def _rwkv_prep_kernel(p_ref, prev_ref, mu_ref, w0_ref, a0_ref, kk_ref, ka_ref, w2_ref, a2_ref, g2_ref,
                      r_out, lw_out, k_out, v_out, kkn_out, b_out, g_out, *, tiles_per_seq):
    c = RWKV_WIDTH
    i = pl.program_id(0)
    p = p_ref[...]
    tm = p.shape[0]
    first = (i % tiles_per_seq) == 0
    prev_row = jnp.where(first, 0.0, prev_ref[7:8, :])
    row = lax.broadcasted_iota(jnp.int32, p.shape, 0)
    shifted = jnp.where(row == 0, prev_row, pltpu.roll(p, 1, axis=0))
    ps = p + (shifted - p) * mu_ref[...]
    r = ps[:, 0:c]
    k = ps[:, c:2 * c]
    v = ps[:, 2 * c:3 * c]
    wd = ps[:, 3 * c:3 * c + 128]
    ad = ps[:, 3 * c + 128:3 * c + 256]
    gd = ps[:, 3 * c + 256:3 * c + 512]
    z = -(w0_ref[...] + _dot3(jnp.tanh(wd), w2_ref[...]))
    softplus = jnp.maximum(z, 0.0) + jnp.log1p(jnp.exp(-jnp.abs(z)))
    w = -softplus - 0.5
    lw = -jnp.exp(w)
    a = jax.nn.sigmoid(a0_ref[...] + _dot3(ad, a2_ref[...]))
    g = _dot3(jax.nn.sigmoid(gd), g2_ref[...])
    kkr = k * kk_ref[...]
    k2 = k * (1.0 + (a - 1.0) * ka_ref[...])
    ones = _head_ones()
    for q in range(RWKV_SLABS):
        sl = slice(q * SLAB, (q + 1) * SLAB)
        x = kkr[:, sl]
        ss = _dot_exact_rhs(x * x, ones)
        kkn = x / jnp.maximum(jnp.sqrt(ss), 1e-12)
        r_out[q] = r[:, sl]
        lw_out[q] = lw[:, sl]
        k_out[q] = k2[:, sl]
        v_out[q] = v[:, sl]
        kkn_out[q] = kkn
        b_out[q] = kkn * a[:, sl]
        g_out[q] = g[:, sl]


def rwkv_prep(p, mu, w0, a0, k_k, k_a, w2p, a2p, g2p, *, seq, tm):
    t, cols = p.shape
    c = RWKV_WIDTH
    vec = lambda n: pl.BlockSpec((1, n), lambda i: (0, 0))
    full = lambda a: pl.BlockSpec(a.shape, lambda i: (0, 0))
    out_spec = pl.BlockSpec((RWKV_SLABS, tm, SLAB), lambda i: (0, i, 0))
    out_shape = jax.ShapeDtypeStruct((RWKV_SLABS, t, SLAB), F32)
    return pl.pallas_call(
        functools.partial(_rwkv_prep_kernel, tiles_per_seq=seq // tm),
        grid=(t // tm,),
        in_specs=[
            pl.BlockSpec((tm, cols), lambda i: (i, 0)),
            pl.BlockSpec((8, cols), lambda i: (jnp.maximum(i * (tm // 8) - 1, 0), 0)),
            vec(cols), vec(c), vec(c), vec(c), vec(c), full(w2p), full(a2p), full(g2p),
        ],
        out_specs=[out_spec] * 7,
        out_shape=[out_shape] * 7,
        compiler_params=_cparams(("parallel",)),
        name="rwkv_prep",
    )(p, p, mu.reshape(1, cols), w0.reshape(1, c), a0.reshape(1, c), k_k.reshape(1, c), k_a.reshape(1, c),
      w2p, a2p, g2p)


def _dot1(a, b, dims=NN):
    return _dot(a.astype(BF16), b.astype(BF16), dims)


def _dot3s(a, b, dims=NN):
    ah, al = _split2(a)
    bh, bl = _split2(b)
    m = a.shape[0]
    lhs = jnp.concatenate([ah, al], axis=0)
    if dims is NN:
        n = b.shape[1]
        rhs = jnp.concatenate([bh, bl], axis=1)
    else:
        n = b.shape[0]
        rhs = jnp.concatenate([bh, bl], axis=0)
    p = _dot(lhs, rhs, dims)
    return p[:m, :n] + (p[:m, n:] + p[m:, :n])


def _sum3_exact_rhs(xs, ones):
    parts = []
    for x in xs:
        parts.extend(_split3(x))
    res = _dot(jnp.concatenate(parts, axis=0), ones)
    L = xs[0].shape[0]
    return [res[(3 * i) * L:(3 * i + 1) * L] + (res[(3 * i + 1) * L:(3 * i + 2) * L] + res[(3 * i + 2) * L:(3 * i + 3) * L])
            for i in range(len(xs))]


def _expert_weight_chunk(w1_ref, w2a_ref, w2b_ref, w1o_ref, w2o_ref, tmp_ref):
    w1o_ref[0] = w1_ref[0].astype(BF16)
    half = W2_GROUP
    for c in range(ROW_TILE):
        cols = slice(c * LANES, (c + 1) * LANES)
        tmp_ref[c, pl.ds(0, half, stride=2), :] = w2a_ref[0, :, cols]
        tmp_ref[c, pl.ds(1, half, stride=2), :] = w2b_ref[0, :, cols]
        w2o_ref[0, :, cols] = tmp_ref[c].astype(BF16)


def _expert_weight_views(w1, w2):
    e, d, f2 = w1.shape
    fdim = w2.shape[1]
    groups_per_half = (MOE_FT // 2) // W2_GROUP
    n_chunks = e * (fdim // MOE_FT) * groups_per_half
    w1v = w1.reshape(n_chunks, (e * d) // n_chunks, f2)
    w2v = w2.reshape(e * fdim // W2_GROUP, W2_GROUP, d)
    return n_chunks, groups_per_half, w1v, w2v


def _expert_weight_specs(n_chunks, groups_per_half, w1v, w2v, step):
    d = w2v.shape[2]
    a_idx = lambda j: (j // groups_per_half) * (2 * groups_per_half) + j % groups_per_half
    in_specs = [pl.BlockSpec((1,) + w1v.shape[1:], lambda *ids: (step(*ids), 0, 0)),
                pl.BlockSpec((1, W2_GROUP, d), lambda *ids: (a_idx(step(*ids)), 0, 0)),
                pl.BlockSpec((1, W2_GROUP, d), lambda *ids: (a_idx(step(*ids)) + groups_per_half, 0, 0))]
    out_specs = [pl.BlockSpec((1,) + w1v.shape[1:], lambda *ids: (step(*ids), 0, 0)),
                 pl.BlockSpec((1, 2 * W2_GROUP, d), lambda *ids: (step(*ids), 0, 0))]
    out_shape = [jax.ShapeDtypeStruct(w1v.shape, BF16), jax.ShapeDtypeStruct((n_chunks, 2 * W2_GROUP, d), BF16)]
    scratch = pltpu.VMEM((ROW_TILE, 2 * W2_GROUP, LANES), F32)
    return in_specs, out_specs, out_shape, scratch


def expert_weight_prep(w1, w2):
    n_chunks, gph, w1v, w2v = _expert_weight_views(w1, w2)
    in_specs, out_specs, out_shape, scratch = _expert_weight_specs(n_chunks, gph, w1v, w2v, lambda j: j)
    w1b, w2b = pl.pallas_call(
        _expert_weight_chunk, grid=(n_chunks,), in_specs=in_specs, out_specs=out_specs, out_shape=out_shape,
        scratch_shapes=[scratch], compiler_params=_cparams(("parallel",)), name="expert_weight_prep",
    )(w1v, w2v, w2v)
    return w1b.reshape(w1.shape), w2b.reshape(w2.shape)


def _rwkv_scan_kernel(r_ref, lw_ref, k_ref, v_ref, kk_ref, b_ref, g_ref, lnw_ref, lnb_ref, rk_ref, *rest):
    if len(rest) == 2:
        o_ref, st_ref = rest
    else:
        w1_ref, w2a_ref, w2b_ref, o_ref, w1o_ref, w2o_ref, st_ref, tmp_ref = rest
        _expert_weight_chunk(w1_ref, w2a_ref, w2b_ref, w1o_ref, w2o_ref, tmp_ref)
    L = CHUNK
    W = SLAB

    @pl.when(pl.program_id(1) == 0)
    def _():
        st_ref[...] = jnp.zeros_like(st_ref)

    t_i = lax.broadcasted_iota(jnp.int32, (L, W), 0)
    lane = lax.broadcasted_iota(jnp.int32, (L, W), 1)
    j_i = lane % HEAD_DIM
    hid = lane // HEAD_DIM
    strict = j_i < t_i
    incl = j_i <= t_i
    eye = (j_i == t_i).astype(F32)
    r2 = lax.broadcasted_iota(jnp.int32, (W, W), 0)
    c2 = lax.broadcasted_iota(jnp.int32, (W, W), 1)
    same_head = (r2 // HEAD_DIM) == (c2 // HEAD_DIM)
    diag = r2 == c2
    ones = same_head.astype(BF16)
    tr = lax.broadcasted_iota(jnp.int32, (L, L), 0)
    tc = lax.broadcasted_iota(jnp.int32, (L, L), 1)
    tri = (tc <= tr).astype(BF16)

    def bd(x):
        return jnp.concatenate([jnp.where(hid == h, x, 0.0) for h in range(HEADS_PER_SLAB)], axis=0)

    U = range(RWKV_SLABS)
    cat0 = lambda xs: jnp.concatenate(xs, axis=0)
    cat1 = lambda xs: jnp.concatenate(xs, axis=1)
    r = [r_ref[u] for u in U]
    lw = [lw_ref[u] for u in U]
    k = [k_ref[u] for u in U]
    v = [v_ref[u] for u in U]
    c3 = [_dot(tri, cat1(_split3(lw[u]))) for u in U]
    cum = [c3[u][:, 0:W] + (c3[u][:, W:2 * W] + c3[u][:, 2 * W:]) for u in U]
    cum_end = [cum[u][L - 1:L, :] for u in U]
    e_cum = [jnp.exp(cum[u]) for u in U]
    e_neg = [jnp.exp(-cum[u]) for u in U]
    e_end = [jnp.exp(cum_end[u] - cum[u]) for u in U]
    at = [-kk_ref[u] * jnp.exp(cum[u] - lw[u]) for u in U]
    rt = [r[u] * e_cum[u] for u in U]
    bt = [b_ref[u] * e_neg[u] for u in U]
    kt = [k[u] * e_neg[u] for u in U]
    bh = [b_ref[u] * e_end[u] for u in U]
    kh = [k[u] * e_end[u] for u in U]
    w_end = [jnp.exp(cum_end[u]) for u in U]

    g_all = [_dot1(cat0([at[u], rt[u]]), cat0([bd(bt[u]), bd(kt[u])]), NT) for u in U]
    a_ab = [jnp.where(strict, g_all[u][0:L, 0:W], 0.0) for u in U]
    a_ak = [jnp.where(strict, g_all[u][0:L, W:], 0.0) for u in U]
    a_rb = [jnp.where(incl, g_all[u][L:, 0:W], 0.0) for u in U]
    a_rk = [jnp.where(incl, g_all[u][L:, W:], 0.0) for u in U]

    tinv = [eye for u in U]
    pw = a_ab
    for it in range(6):
        if it < 5:
            res = [_dot1(pw[u], cat1([bd(tinv[u]), bd(pw[u])])) for u in U]
            tinv = [tinv[u] + res[u][:, 0:W] for u in U]
            pw = [res[u][:, W:] for u in U]
        else:
            tinv = [tinv[u] + _dot1(pw[u], bd(tinv[u])) for u in U]

    bdv = [bd(v[u]) for u in U]
    av = [_dot1(a_ak[u], bdv[u]) for u in U]
    qp = [_dot1(tinv[u], cat1([bd(at[u]), bd(av[u])])) for u in U]
    q1 = [qp[u][:, 0:W] for u in U]
    p1 = [qp[u][:, W:] for u in U]
    qp2 = [_dot1(cat1([a_rb[u], a_rk[u]]),
                  cat0([cat1([bd(q1[u]), bd(p1[u])]), cat1([jnp.zeros_like(bdv[u]), bdv[u]])])) for u in U]
    q2 = [rt[u] + qp2[u][:, 0:W] for u in U]
    p2 = [qp2[u][:, W:] for u in U]
    mp = [_dot1(cat0([bh[u], kh[u]]).T,
                 cat0([cat1([q1[u], p1[u]]), cat1([jnp.zeros_like(v[u]), v[u]])])) for u in U]
    m_bd = [jnp.where(same_head, mp[u][:, 0:W], 0.0) + jnp.where(diag, w_end[u], 0.0) for u in U]
    ys = [_dot3s(cat0([q2[u], m_bd[u]]), st_ref[u]) for u in U]
    y = [ys[u][0:L] + p2[u] for u in U]
    for u in U:
        st_ref[u] = ys[u][L:] + jnp.where(same_head, mp[u][:, W:], 0.0)

    sums = [_sum3_exact_rhs([y[u], r[u] * k[u] * rk_ref[u]], ones) for u in U]
    yc = [y[u] - sums[u][0] * (1.0 / HEAD_DIM) for u in U]
    var = [_sum3_exact_rhs([yc[u] * yc[u]], ones)[0] for u in U]
    for u in U:
        yn = yc[u] * lax.rsqrt(var[u] * (1.0 / HEAD_DIM) + RWKV_GN_EPS) * lnw_ref[u] + lnb_ref[u]
        o_ref[u] = ((yn + sums[u][1] * v[u]) * g_ref[u]).astype(o_ref.dtype)


def rwkv_scan(r, lw, k, v, kk, b, g, ln_w, ln_b, r_k, w1, w2, *, batch, seq):
    nchunk = seq // CHUNK
    blk = pl.BlockSpec((RWKV_SLABS, CHUNK, SLAB), lambda bi, ci: (0, bi * nchunk + ci, 0))
    par = pl.BlockSpec((RWKV_SLABS, 1, SLAB), lambda bi, ci: (0, 0, 0))
    t = batch * seq
    out_shape = jax.ShapeDtypeStruct((RWKV_SLABS, t, SLAB), BF16)
    state = pltpu.VMEM((RWKV_SLABS, SLAB, SLAB), F32)
    args = (r, lw, k, v, kk, b, g, ln_w.reshape(RWKV_SLABS, 1, SLAB), ln_b.reshape(RWKV_SLABS, 1, SLAB),
            r_k.reshape(RWKV_SLABS, 1, SLAB))
    n_chunks, gph, w1v, w2v = _expert_weight_views(w1, w2)
    if n_chunks != batch * nchunk:
        ra = pl.pallas_call(
            _rwkv_scan_kernel, grid=(batch, nchunk), in_specs=[blk] * 7 + [par] * 3, out_specs=blk,
            out_shape=out_shape, scratch_shapes=[state],
            compiler_params=_cparams(("arbitrary", "arbitrary")), name="rwkv_scan",
        )(*args)
        return (ra,) + expert_weight_prep(w1, w2)
    w_in, w_out, w_shape, w_scratch = _expert_weight_specs(n_chunks, gph, w1v, w2v, lambda bi, ci: bi * nchunk + ci)
    ra, w1b, w2b = pl.pallas_call(
        _rwkv_scan_kernel, grid=(batch, nchunk), in_specs=[blk] * 7 + [par] * 3 + w_in,
        out_specs=[blk] + w_out, out_shape=[out_shape] + w_shape, scratch_shapes=[state, w_scratch],
        compiler_params=_cparams(("arbitrary", "arbitrary")), name="rwkv_scan",
    )(*args, w1v, w2v, w2v)
    return ra, w1b.reshape(w1.shape), w2b.reshape(w2.shape)


def _diff_attn_kernel(qi_ref, ki_ref, slope_ref, q_ref, k_ref, v_ref, lq1_ref, lk1_ref, lq2_ref, lk2_ref, sg_ref,
                      o_ref, qs_ref, relb_ref, m_ref, l_ref, acc_ref, *, tq, lam_init):
    h = pl.program_id(1)
    qi = qi_ref[pl.program_id(2)]
    ki = ki_ref[pl.program_id(2)]
    c2 = slope_ref[h] * LOG2E

    @pl.when(ki == 0)
    def _():
        m_ref[...] = jnp.full_like(m_ref, -jnp.inf)
        l_ref[...] = jnp.zeros_like(l_ref)
        acc_ref[...] = jnp.zeros_like(acc_ref)
        q = q_ref[...].astype(F32) * (HEAD_DIM ** -0.5 * LOG2E)
        lane = lax.broadcasted_iota(jnp.int32, q.shape, 1)
        m0 = lane < HEAD_DIM
        qs_ref[0:tq, :] = jnp.where(m0, q, 0.0).astype(BF16)
        qs_ref[tq:, :] = jnp.where(m0, 0.0, q).astype(BF16)
        rel = (lax.broadcasted_iota(jnp.int32, (tq, tq), 1) - lax.broadcasted_iota(jnp.int32, (tq, tq), 0))
        relb = rel.astype(F32) * (-c2)
        relb_ref[:, 0:tq] = relb
        relb_ref[:, tq:] = relb

    def step(masked):
        k = k_ref[...]
        vt = v_ref[...].T
        relb = relb_ref[...]
        tile_bias = c2 * ((qi - ki) * tq).astype(F32)
        s = _dot(k, qs_ref[...], NT) + relb
        if masked:
            s = jnp.where(relb > 0.0, -jnp.inf, s)
        m_prev = m_ref[...]
        m_new = jnp.maximum(m_prev, jnp.max(s, axis=0, keepdims=True) - tile_bias)
        alpha = jnp.exp2(m_prev - m_new)
        p = jnp.exp2(s - (m_new + tile_bias))
        l_ref[...] = alpha * l_ref[...] + jnp.sum(p, axis=0, keepdims=True)
        acc_ref[...] = alpha * acc_ref[...] + _dot(vt, p.astype(BF16))
        m_ref[...] = m_new

    @pl.when(ki < qi)
    def _():
        step(False)

    @pl.when(ki == qi)
    def _():
        step(True)
        lam = (jnp.exp(jnp.sum(lq1_ref[...] * lk1_ref[...], axis=-1, keepdims=True))
               - jnp.exp(jnp.sum(lq2_ref[...] * lk2_ref[...], axis=-1, keepdims=True)) + lam_init)
        ot = acc_ref[:, 0:tq] / l_ref[:, 0:tq] - lam * (acc_ref[:, tq:] / l_ref[:, tq:])
        o = _rms(ot.T, sg_ref[...], NORM_EPS) * (1.0 - lam_init)
        o_ref[...] = o.astype(o_ref.dtype)


def diff_attention(qkv, slopes, lq1, lk1, lq2, lk2, subln_g, *, batch, seq, tq, lam_init):
    t = batch * seq
    nq = seq // tq
    hb = DIFF_WIDTH // LANES
    pairs = [(qi, ki) for qi in range(nq) for ki in range(qi + 1)]
    qi_tab = jnp.asarray([pq for pq, _ in pairs], jnp.int32)
    ki_tab = jnp.asarray([pk for _, pk in pairs], jnp.int32)
    small = pl.BlockSpec((1, HEAD_DIM), lambda b, h, j, qt, kt: (0, 0))
    grid_spec = pltpu.PrefetchScalarGridSpec(
        num_scalar_prefetch=2,
        grid=(batch, DIFF_HEADS, len(pairs)),
        in_specs=[
            pl.BlockSpec(memory_space=pltpu.SMEM),
            pl.BlockSpec((tq, LANES), lambda b, h, j, qt, kt: (b * nq + qt[j], h)),
            pl.BlockSpec((tq, LANES), lambda b, h, j, qt, kt: (b * nq + kt[j], hb + h)),
            pl.BlockSpec((tq, LANES), lambda b, h, j, qt, kt: (b * nq + kt[j], 2 * hb + h)),
            small, small, small, small,
            pl.BlockSpec((1, LANES), lambda b, h, j, qt, kt: (0, 0)),
        ],
        out_specs=pl.BlockSpec((tq, LANES), lambda b, h, j, qt, kt: (b * nq + qt[j], h)),
        scratch_shapes=[pltpu.VMEM((2 * tq, LANES), BF16), pltpu.VMEM((tq, 2 * tq), F32),
                        pltpu.VMEM((1, 2 * tq), F32), pltpu.VMEM((1, 2 * tq), F32),
                        pltpu.VMEM((LANES, 2 * tq), F32)],
    )
    return pl.pallas_call(
        functools.partial(_diff_attn_kernel, tq=tq, lam_init=lam_init),
        grid_spec=grid_spec,
        out_shape=jax.ShapeDtypeStruct((t, DIFF_WIDTH), BF16),
        compiler_params=_cparams(("parallel", "parallel", "arbitrary")),
        name="diff_attention",
    )(qi_tab, ki_tab, slopes, qkv, qkv, qkv, lq1.reshape(1, -1), lk1.reshape(1, -1), lq2.reshape(1, -1),
      lk2.reshape(1, -1), subln_g.reshape(1, -1))


def _mix_kernel(ra_ref, da_ref, ga_ref, gb_ref, x_ref, wa_ref, wb_ref, wo_ref, o_ref):
    ya = _dot(ra_ref[0], wa_ref[0:SLAB, :])
    for q in range(1, RWKV_SLABS):
        ya = ya + _dot(ra_ref[q], wa_ref[q * SLAB:(q + 1) * SLAB, :])
    yb = _dot(da_ref[...], wb_ref[...])
    mixed = ga_ref[...].astype(F32) * ya + gb_ref[...].astype(F32) * yb
    o_ref[...] = x_ref[...] + _dot(mixed.astype(BF16), wo_ref[...])


def mix_project(ra, da, gates, x, wa, wb, wo, *, tm):
    t, d = x.shape
    const = lambda a: pl.BlockSpec(a.shape, lambda i: (0, 0), pipeline_mode=pl.Buffered(1))
    return pl.pallas_call(
        _mix_kernel,
        grid=(t // tm,),
        in_specs=[
            pl.BlockSpec((RWKV_SLABS, tm, SLAB), lambda i: (0, i, 0)),
            pl.BlockSpec((tm, DIFF_WIDTH), lambda i: (i, 0)),
            pl.BlockSpec((tm, d), lambda i: (i, 0)),
            pl.BlockSpec((tm, d), lambda i: (i, 1)),
            pl.BlockSpec((tm, d), lambda i: (i, 0)),
            const(wa), const(wb), const(wo),
        ],
        out_specs=pl.BlockSpec((tm, d), lambda i: (i, 0)),
        out_shape=jax.ShapeDtypeStruct((t, d), F32),
        compiler_params=_cparams(("parallel",)),
        name="mix_project",
    )(ra, da, gates, gates, x, wa, wb, wo)


def _cross_kernel(x_ref, gc_ref, wq_ref, kv_ref, wo_ref, gf_ref, rw_ref, rb_ref, x2_ref, hf_ref, lg_ref):
    x = x_ref[...]
    h = _rms(x, gc_ref[...], NORM_EPS).astype(BF16)
    q = _dot(h, wq_ref[...]).astype(BF16)
    scale = LANES ** -0.5
    outs = []
    for hd in range(CROSS_HEADS):
        qh = q[:, hd * LANES:(hd + 1) * LANES]
        kh = kv_ref[0, :, hd * LANES:(hd + 1) * LANES]
        vh = kv_ref[0, :, CROSS_WIDTH + hd * LANES:CROSS_WIDTH + (hd + 1) * LANES]
        s = _dot(qh, kh, NT) * scale
        s = s - jnp.max(s, axis=-1, keepdims=True)
        e = jnp.exp(s)
        p = e / jnp.sum(e, axis=-1, keepdims=True)
        outs.append(_dot(p.astype(BF16), vh))
    o = jnp.concatenate(outs, axis=1).astype(BF16)
    x2 = x + _dot(o, wo_ref[...])
    x2_ref[...] = x2
    hf = _rms(x2, gf_ref[...], NORM_EPS)
    tm = hf.shape[0]
    for c in range(PACK_ROWS):
        hf_ref[pl.ds(c, tm, stride=PACK_ROWS), :] = _pack_pair(hf[:, 2 * c * LANES:(2 * c + 1) * LANES],
                                                               hf[:, (2 * c + 1) * LANES:(2 * c + 2) * LANES])
    lg_ref[...] = _dot3(hf, rw_ref[...]) + rb_ref[...]


def cross_attention(x, gc, wq, kv, wo, gf, rw, rb, *, seq, tm):
    t, d = x.shape
    per_seq = seq // tm
    const = lambda a: pl.BlockSpec(a.shape, lambda i: (0,) * a.ndim, pipeline_mode=pl.Buffered(1))
    vec = lambda n: pl.BlockSpec((1, n), lambda i: (0, 0))
    return pl.pallas_call(
        _cross_kernel,
        grid=(t // tm,),
        in_specs=[
            pl.BlockSpec((tm, d), lambda i: (i, 0)),
            vec(d), const(wq),
            pl.BlockSpec((1,) + kv.shape[1:], lambda i: (i // per_seq, 0, 0)),
            const(wo), vec(d), const(rw), vec(LANES),
        ],
        out_specs=[pl.BlockSpec((tm, d), lambda i: (i, 0)), pl.BlockSpec((tm * PACK_ROWS, LANES), lambda i: (i, 0)),
                   pl.BlockSpec((tm, LANES), lambda i: (i, 0))],
        out_shape=[jax.ShapeDtypeStruct((t, d), F32), jax.ShapeDtypeStruct((t * PACK_ROWS, LANES), U32),
                   jax.ShapeDtypeStruct((t, LANES), F32)],
        compiler_params=_cparams(("parallel",)),
        name="cross_attention",
    )(x, gc.reshape(1, d), wq, kv, wo, gf.reshape(1, d), rw, rb)


def _router_kernel(lg_ref, idx_ref, gate_ref):
    x = lg_ref[...]
    lane = lax.broadcasted_iota(jnp.int32, x.shape, 1)
    x = jnp.where(lane < N_EXPERTS, x, -jnp.inf)
    idx_out = jnp.zeros(x.shape, jnp.int32)
    val_out = jnp.zeros(x.shape, F32)
    vals = []
    for j in range(TOP_K):
        m = jnp.max(x, axis=-1, keepdims=True)
        sel = jnp.min(jnp.where(x == m, lane, LANES), axis=-1, keepdims=True)
        idx_out = jnp.where(lane == j, sel, idx_out)
        vals.append(m)
        x = jnp.where(lane == sel, -jnp.inf, x)
    es = [jnp.exp(vj - vals[0]) for vj in vals]
    tot = es[0] + es[1] + es[2] + es[3]
    for j in range(TOP_K):
        val_out = jnp.where(lane == j, es[j] / tot, val_out)
    idx_ref[...] = idx_out[:, 0:TOP_K]
    gate_ref[...] = val_out[:, 0:TOP_K]


def router_topk(logits, *, tm):
    t = logits.shape[0]
    return pl.pallas_call(
        _router_kernel,
        grid=(t // tm,),
        in_specs=[pl.BlockSpec((tm, LANES), lambda i: (i, 0))],
        out_specs=[pl.BlockSpec((tm, TOP_K), lambda i: (i, 0)), pl.BlockSpec((tm, TOP_K), lambda i: (i, 0))],
        out_shape=[jax.ShapeDtypeStruct((t, TOP_K), jnp.int32), jax.ShapeDtypeStruct((t, TOP_K), F32)],
        compiler_params=_cparams(("parallel",)),
        name="router_topk",
    )(logits)


def _moe_kernel(sbe_ref, sbr_ref, src_ref, nsrc_ref, hf_hbm, w1_ref, b1_ref, w2_ref, b2_ref,
                ys_hbm, xbuf, x2d, acc, obuf, gsem, ssem):
    s = pl.program_id(0)
    f = pl.program_id(1)
    nf = pl.num_programs(1)
    nsb = pl.num_programs(0)
    sub_rows = MOE_SUB * PACK_ROWS
    nsubs = MOE_SUPER // MOE_SUB
    unroll = 8

    def ceil_sub(n):
        return (n + (MOE_SUB - 1)) // MOE_SUB

    rows = sbr_ref[s]
    nsub = ceil_sub(rows)
    rows_next = jnp.where(s + 1 < nsb, sbr_ref[jnp.minimum(s + 1, nsb - 1)], 0)
    rows_prev = jnp.where(s > 0, sbr_ref[jnp.maximum(s - 1, 0)], 0)

    def slab(ref, r):
        return ref.at[pl.ds(pl.multiple_of(r * PACK_ROWS, PACK_ROWS), PACK_ROWS)]

    def row_in(table, r):
        tok = lax.shift_right_logical(jnp.maximum(table[0, 0, r], 0), 2)
        return pltpu.make_async_copy(slab(hf_hbm, tok), slab(xbuf, r), gsem)

    def row_out(r):
        return pltpu.make_async_copy(slab(obuf, r), slab(ys_hbm, src_ref[0, 0, r]), ssem)

    def start_gather(table, n):
        def group(gi, c):
            for j in range(unroll):
                row_in(table, gi * unroll + j).start()
            return c

        lax.fori_loop(0, n // unroll, group, 0)

    def wait_sub_blocks(buf, other, sem, n):
        for sub in range(nsubs):
            @pl.when(sub < n)
            def _(sub=sub):
                pltpu.make_async_copy(other.at[pl.ds(0, sub_rows)], buf.at[pl.ds(sub * sub_rows, sub_rows)], sem).wait()

    def wait_scatter(n):
        full = n // MOE_SUB
        for sub in range(nsubs):
            @pl.when(sub < full)
            def _(sub=sub):
                pltpu.make_async_copy(obuf.at[pl.ds(sub * sub_rows, sub_rows)], ys_hbm.at[pl.ds(0, sub_rows)], ssem).wait()

        def one(r, c):
            row_out(r).wait()
            return c

        lax.fori_loop(0, n - full * MOE_SUB, one, 0)

    @pl.when((f == 0) & (s == 0) & (rows > 0))
    def _():
        start_gather(src_ref, nsub * MOE_SUB)

    @pl.when((f == 0) & (rows > 0))
    def _():
        wait_sub_blocks(xbuf, hf_hbm, gsem, nsub)

    @pl.when((f == nf - 1) & (rows > 0) & (rows_prev > 0))
    def _():
        wait_scatter(rows_prev)

    for sub in range(nsubs):
        @pl.when(sub < nsub)
        def _(sub=sub):
            sl = slice(sub * MOE_SUB, (sub + 1) * MOE_SUB)
            base = sub * sub_rows

            @pl.when(f == 0)
            def _():
                for c in range(PACK_ROWS):
                    lo, hi = _unpack_pair(xbuf[pl.ds(base + c, MOE_SUB, stride=PACK_ROWS), :])
                    x2d[sl, 2 * c * LANES:(2 * c + 1) * LANES] = lo.astype(BF16)
                    x2d[sl, (2 * c + 1) * LANES:(2 * c + 2) * LANES] = hi.astype(BF16)

            x = x2d[sl, :]
            hb = _dot(x, w1_ref[0]) + b1_ref[0]
            even = (lax.broadcasted_iota(jnp.int32, (MOE_SUB, LANES), 1) % 2) == 0
            acts = []
            for cb in range(MOE_FT // LANES):
                a = hb[:, cb * LANES:(cb + 1) * LANES]
                b = hb[:, MOE_FT + cb * LANES:MOE_FT + (cb + 1) * LANES]
                hg = jnp.where(even, a, pltpu.roll(b, 1, axis=1))
                hl = jnp.where(even, pltpu.roll(a, LANES - 1, axis=1), b)
                xg = jnp.minimum(hg, SWIGLU_LIMIT)
                xl = jnp.clip(hl, -SWIGLU_LIMIT, SWIGLU_LIMIT)
                acts.append((xg * jax.nn.sigmoid(SWIGLU_ALPHA * xg) * (xl + 1.0)).astype(BF16))
            contrib = _dot(jnp.concatenate(acts, axis=1), w2_ref[0])

            @pl.when(f == 0)
            def _():
                acc[sl, :] = contrib

            @pl.when(f > 0)
            def _():
                acc[sl, :] += contrib

            @pl.when(f == nf - 1)
            def _():
                y = acc[sl, :] + b2_ref[0]
                for c in range(PACK_ROWS):
                    obuf[pl.ds(base + c, MOE_SUB, stride=PACK_ROWS), :] = _pack_pair(
                        y[:, 2 * c * LANES:(2 * c + 1) * LANES], y[:, (2 * c + 1) * LANES:(2 * c + 2) * LANES])

    @pl.when((f == 0) & (rows_next > 0))
    def _():
        start_gather(nsrc_ref, ceil_sub(rows_next) * MOE_SUB)

    @pl.when((f == nf - 1) & (rows > 0))
    def _():
        def group(gi, c):
            for j in range(unroll):
                row_out(gi * unroll + j).start()
            return c

        def one(r, c):
            row_out(r).start()
            return c

        ngroups = rows // unroll
        lax.fori_loop(0, ngroups, group, 0)
        lax.fori_loop(ngroups * unroll, rows, one, 0)

        @pl.when(rows_next == 0)
        def _():
            wait_scatter(rows)


def moe_experts(sb_e, sb_rows, row_src, hf, w1, b1, w2, b2):
    d = D_MODEL
    t = hf.shape[0] // PACK_ROWS
    nsb = sb_e.shape[0]
    nf = D_EXPERT // MOE_FT
    grid_spec = pltpu.PrefetchScalarGridSpec(
        num_scalar_prefetch=2,
        grid=(nsb, nf),
        in_specs=[
            pl.BlockSpec((1, 1, MOE_SUPER), lambda s, f, e, r: (s, 0, 0), memory_space=pltpu.SMEM),
            pl.BlockSpec((1, 1, MOE_SUPER), lambda s, f, e, r: (jnp.minimum(s + 1, nsb - 1), 0, 0),
                         memory_space=pltpu.SMEM),
            pl.BlockSpec(memory_space=pl.ANY),
            pl.BlockSpec((1, d, 2 * MOE_FT), lambda s, f, e, r: (e[s], 0, f)),
            pl.BlockSpec((1, 1, 2 * MOE_FT), lambda s, f, e, r: (e[s], 0, f)),
            pl.BlockSpec((1, MOE_FT, d), lambda s, f, e, r: (e[s], f, 0)),
            pl.BlockSpec((1, 1, d), lambda s, f, e, r: (e[s], 0, 0)),
        ],
        out_specs=pl.BlockSpec(memory_space=pl.ANY),
        scratch_shapes=[pltpu.VMEM((MOE_SUPER * PACK_ROWS, LANES), U32), pltpu.VMEM((MOE_SUPER, d), BF16),
                        pltpu.VMEM((MOE_SUPER, d), F32), pltpu.VMEM((MOE_SUPER * PACK_ROWS, LANES), U32),
                        pltpu.SemaphoreType.DMA, pltpu.SemaphoreType.DMA],
    )
    return pl.pallas_call(
        _moe_kernel,
        grid_spec=grid_spec,
        out_shape=jax.ShapeDtypeStruct((t * TOP_K * PACK_ROWS, LANES), U32),
        compiler_params=_cparams(("arbitrary", "arbitrary")),
        name="moe_experts",
    )(sb_e, sb_rows, row_src, row_src, hf, w1, b1, w2, b2)


def _combine_kernel(ys_ref, gate_ref, x_ref, g_ref, o_ref, lo_ref, hi_ref):
    tm = x_ref.shape[0]
    tot_lo = tot_hi = None
    for j in range(TOP_K):
        lo, hi = _unpack_pair(ys_ref[:, j * PACK_ROWS:(j + 1) * PACK_ROWS, :])
        gj = gate_ref[:, j:j + 1, :]
        tot_lo = lo * gj if tot_lo is None else tot_lo + lo * gj
        tot_hi = hi * gj if tot_hi is None else tot_hi + hi * gj
    lo_ref[...] = tot_lo.reshape(tm * PACK_ROWS, LANES)
    hi_ref[...] = tot_hi.reshape(tm * PACK_ROWS, LANES)
    pieces = []
    for c in range(PACK_ROWS):
        pieces.append(lo_ref[pl.ds(c, tm, stride=PACK_ROWS), :])
        pieces.append(hi_ref[pl.ds(c, tm, stride=PACK_ROWS), :])
    out = x_ref[...] + jnp.concatenate(pieces, axis=1)
    o_ref[...] = _rms(out, g_ref[...], NORM_EPS)


def combine_final(ys, gate, x, g, *, tm):
    t, d = x.shape
    return pl.pallas_call(
        _combine_kernel,
        grid=(t // tm,),
        in_specs=[pl.BlockSpec((tm, TOP_K * PACK_ROWS, LANES), lambda i: (i, 0, 0)),
                  pl.BlockSpec((tm, TOP_K, LANES), lambda i: (i, 0, 0)),
                  pl.BlockSpec((tm, d), lambda i: (i, 0)), pl.BlockSpec((1, d), lambda i: (0, 0))],
        out_specs=pl.BlockSpec((tm, d), lambda i: (i, 0)),
        out_shape=jax.ShapeDtypeStruct((t, d), F32),
        scratch_shapes=[pltpu.VMEM((tm * PACK_ROWS, LANES), F32), pltpu.VMEM((tm * PACK_ROWS, LANES), F32)],
        compiler_params=_cparams(("parallel",)),
        name="combine_final",
    )(ys, gate, x, g.reshape(1, d))


def _routing_tables(top_idx, n_super):
    flat_e = top_idx.reshape(-1)
    n = flat_e.shape[0]
    onehot = (flat_e[:, None] == jnp.arange(N_EXPERTS, dtype=jnp.int32)[None, :]).astype(jnp.int32)
    csum = jnp.cumsum(onehot, axis=0)
    rank = jnp.sum(onehot * csum, axis=1) - 1
    counts = csum[-1]
    nsb = (counts + MOE_SUPER - 1) // MOE_SUPER
    sb_end = jnp.cumsum(nsb)
    sb_start = sb_end - nsb
    dest = sb_start[flat_e] * MOE_SUPER + rank
    row_src = jnp.full((n_super * MOE_SUPER,), -1, jnp.int32).at[dest].set(jnp.arange(n, dtype=jnp.int32))
    s_ids = jnp.arange(n_super, dtype=jnp.int32)
    sb_e = jnp.minimum(jnp.searchsorted(sb_end, s_ids, side="right"), N_EXPERTS - 1).astype(jnp.int32)
    local = s_ids - sb_start[sb_e]
    sb_rows = jnp.clip(counts[sb_e] - local * MOE_SUPER, 0, MOE_SUPER)
    sb_rows = jnp.where(s_ids < sb_end[-1], sb_rows, 0).astype(jnp.int32)
    return sb_e, sb_rows, row_src.reshape(n_super, 1, MOE_SUPER)


def _pad_rows(a, n):
    return jnp.pad(a, ((0, n - a.shape[0]), (0, 0)))


def _layer(x, mem, l, p, batch, seq):
    t = batch * seq
    c = RWKV_WIDTH
    lam_init = 0.8 - 0.6 * math.exp(-0.3 * l)
    w_in = p["w_in"]
    o1 = 3 * c + DECAY_LORA + AAA_LORA + GATE_LORA
    o2 = o1 + 3 * DIFF_WIDTH
    padc = lambda a, n: jnp.pad(a, ((0, 0), (0, n - a.shape[1])))
    w_rwkv = jnp.concatenate([
        w_in[:, :3 * c],
        padc(w_in[:, 3 * c:3 * c + DECAY_LORA], 128),
        padc(w_in[:, 3 * c + DECAY_LORA:3 * c + DECAY_LORA + AAA_LORA], 128),
        padc(w_in[:, 3 * c + DECAY_LORA + AAA_LORA:o1], 256)], axis=1).astype(BF16)
    mu = p["rwkv_mu"]
    pad1 = lambda a, n: jnp.pad(a, (0, n - a.shape[0]))
    mu_p = jnp.concatenate([mu[:3 * c], pad1(mu[3 * c:3 * c + DECAY_LORA], 128),
                            pad1(mu[3 * c + DECAY_LORA:3 * c + DECAY_LORA + AAA_LORA], 128),
                            pad1(mu[3 * c + DECAY_LORA + AAA_LORA:], 256)])
    w_diff = w_in[:, o1:o2].astype(BF16)
    w_gate = w_in[:, o2:].astype(BF16)

    g_mix = p["norm_mix_g"]
    p_rwkv = norm_matmul(x, g_mix, w_rwkv, tm=1024, tn=RWKV_COLS // 2, out_dtype=F32, name="in_proj_rwkv")
    qkv = norm_matmul(x, g_mix, w_diff, tm=1024, tn=1024, out_dtype=BF16, name="in_proj_diff")
    gates = norm_matmul(x, g_mix, w_gate, tm=1024, tn=1024, out_dtype=BF16, act="sigmoid", name="in_proj_gate")

    prep = rwkv_prep(p_rwkv, mu_p, p["rwkv_w0"], p["rwkv_a0"], p["rwkv_k_k"], p["rwkv_k_a"],
                     _pad_rows(p["rwkv_w2"], 128), _pad_rows(p["rwkv_a2"], 128), _pad_rows(p["rwkv_g2"], 256),
                     seq=seq, tm=256)
    ra, w1b, w2b = rwkv_scan(*prep, p["rwkv_ln_w"], p["rwkv_ln_b"], p["rwkv_r_k"].reshape(-1),
                             p["expert_w1"], p["expert_w2"], batch=batch, seq=seq)

    slopes = (2.0 ** (-8.0 * jnp.arange(1, DIFF_HEADS + 1, dtype=F32) / DIFF_HEADS)).astype(F32)
    da = diff_attention(qkv, slopes, p["diff_lq1"], p["diff_lk1"], p["diff_lq2"], p["diff_lk2"],
                        p["diff_subln_g"], batch=batch, seq=seq, tq=min(1024, seq), lam_init=lam_init)

    x1 = mix_project(ra, da, gates, x, p["rwkv_proj"].astype(BF16), p["diff_proj"].astype(BF16),
                     p["w_out"].astype(BF16), tm=256)

    m_len = mem.shape[0] // batch
    kv = norm_matmul(mem, p["norm_mem_g"], p["cross_wkv"].astype(BF16), tm=min(512, mem.shape[0]),
                     tn=2 * CROSS_WIDTH, out_dtype=BF16, name="cross_kv")
    rw = jnp.pad(p["router_w"], ((0, 0), (0, LANES - N_EXPERTS)))
    rb = jnp.pad(p["router_b"], (0, LANES - N_EXPERTS)).reshape(1, LANES)
    x2, hf, logits = cross_attention(x1, p["norm_cross_g"], p["cross_wq"].astype(BF16),
                                     kv.reshape(batch, m_len, 2 * CROSS_WIDTH), p["cross_wo"].astype(BF16),
                                     p["norm_ffn_g"], rw, rb, seq=seq, tm=256)

    top_idx, gate = router_topk(logits, tm=min(1024, t))
    n_super = (t * TOP_K) // MOE_SUPER + N_EXPERTS
    sb_e, sb_rows, row_src = _routing_tables(top_idx, n_super)
    ys = moe_experts(sb_e, sb_rows, row_src, hf, w1b, p["expert_b1"][:, None, :], w2b, p["expert_b2"][:, None, :])
    gate_b = jnp.broadcast_to(gate[:, :, None], (t, TOP_K, LANES))
    return x2, ys.reshape(t, TOP_K * PACK_ROWS, LANES), gate_b


def kernel(x, mem, norm_mix_g, w_in, rwkv_mu, rwkv_w0, rwkv_w2, rwkv_a0, rwkv_a2, rwkv_g2, rwkv_k_k, rwkv_k_a, rwkv_r_k, rwkv_ln_w, rwkv_ln_b, rwkv_proj, diff_lq1, diff_lk1, diff_lq2, diff_lk2, diff_subln_g, diff_proj, w_out, norm_cross_g, norm_mem_g, cross_wq, cross_wkv, cross_wo, norm_ffn_g, router_w, router_b, expert_w1, expert_b1, expert_w2, expert_b2, final_norm_g):
    batch, seq, d = x.shape
    stacked = dict(norm_mix_g=norm_mix_g, w_in=w_in, rwkv_mu=rwkv_mu, rwkv_w0=rwkv_w0, rwkv_w2=rwkv_w2,
                   rwkv_a0=rwkv_a0, rwkv_a2=rwkv_a2, rwkv_g2=rwkv_g2, rwkv_k_k=rwkv_k_k, rwkv_k_a=rwkv_k_a,
                   rwkv_r_k=rwkv_r_k, rwkv_ln_w=rwkv_ln_w, rwkv_ln_b=rwkv_ln_b, rwkv_proj=rwkv_proj,
                   diff_lq1=diff_lq1, diff_lk1=diff_lk1, diff_lq2=diff_lq2, diff_lk2=diff_lk2,
                   diff_subln_g=diff_subln_g, diff_proj=diff_proj, w_out=w_out, norm_cross_g=norm_cross_g,
                   norm_mem_g=norm_mem_g, cross_wq=cross_wq, cross_wkv=cross_wkv, cross_wo=cross_wo,
                   norm_ffn_g=norm_ffn_g, router_w=router_w, router_b=router_b, expert_w1=expert_w1,
                   expert_b1=expert_b1, expert_w2=expert_w2, expert_b2=expert_b2)
    assert w_in.shape[0] == 1, "the closing RMSNorm is fused into the single layer's combine"
    p = {k: v[0] for k, v in stacked.items()}
    x2, ys, gate_b = _layer(x.reshape(batch * seq, d), mem.reshape(-1, d), 0, p, batch, seq)
    out = combine_final(ys, gate_b, x2, final_norm_g, tm=256)
    return out.reshape(batch, seq, d)
```

```python
import functools
import math

import jax
import jax.numpy as jnp
from jax import lax
from jax.experimental import pallas as pl
from jax.experimental.pallas import tpu as pltpu

F32 = jnp.float32
BF16 = jnp.bfloat16
U32 = jnp.uint32

D_MODEL = 2048
NORM_EPS = 1e-5
LOG2E = 1.4426950408889634
LANES = 128
HEAD_DIM = 64
RWKV_WIDTH = 1024
SLAB = 256
RWKV_SLABS = RWKV_WIDTH // SLAB
HEADS_PER_SLAB = SLAB // HEAD_DIM
CHUNK = 64
RWKV_GN_EPS = 64e-5
DECAY_LORA = 64
AAA_LORA = 64
GATE_LORA = 160
LORA_PAD = 512
RWKV_COLS = 3 * RWKV_WIDTH + LORA_PAD
DIFF_WIDTH = 1024
DIFF_HEADS = 8
CROSS_HEADS = 4
CROSS_WIDTH = 512
N_EXPERTS = 32
TOP_K = 4
D_EXPERT = 2048
SWIGLU_ALPHA = 1.702
SWIGLU_LIMIT = 7.0
MOE_SUPER = 1024
MOE_SUB = 256
MOE_FT = 1024
W2_GROUP = 128
ROW_TILE = D_MODEL // LANES
PACK_ROWS = ROW_TILE // 2
VMEM_LIMIT = 56 * 1024 * 1024

NN = (((1,), (0,)), ((), ()))
NT = (((1,), (1,)), ((), ()))


def _dot(a, b, dims=NN):
    return lax.dot_general(a, b, dims, preferred_element_type=F32)


def _split2(a):
    hi = a.astype(BF16)
    lo = (a - hi.astype(F32)).astype(BF16)
    return hi, lo


def _split3(a):
    hi = a.astype(BF16)
    r1 = a - hi.astype(F32)
    mid = r1.astype(BF16)
    lo = (r1 - mid.astype(F32)).astype(BF16)
    return hi, mid, lo


def _dot3(a, b, dims=NN):
    ah, al = _split2(a)
    bh, bl = _split2(b)
    return _dot(ah, bh, dims) + (_dot(ah, bl, dims) + _dot(al, bh, dims))


def _dot_exact_rhs(a, b_bf16, dims=NN):
    h, m, l = _split3(a)
    return _dot(h, b_bf16, dims) + (_dot(m, b_bf16, dims) + _dot(l, b_bf16, dims))


def _pack_pair(lo, hi):
    lo_b = lax.shift_right_logical(lax.bitcast_convert_type(lo.astype(BF16).astype(F32), U32), jnp.uint32(16))
    hi_b = lax.bitcast_convert_type(hi.astype(BF16).astype(F32), U32) & jnp.uint32(0xFFFF0000)
    return hi_b | lo_b


def _unpack_pair(w):
    lo = lax.bitcast_convert_type(lax.shift_left(w, jnp.uint32(16)), F32)
    hi = lax.bitcast_convert_type(w & jnp.uint32(0xFFFF0000), F32)
    return lo, hi


def _rms(x, g, eps):
    ms = jnp.mean(x * x, axis=-1, keepdims=True)
    return x * lax.rsqrt(ms + eps) * g


def _cparams(sem):
    return pltpu.CompilerParams(dimension_semantics=sem, vmem_limit_bytes=VMEM_LIMIT)


def _norm_matmul_kernel(x_ref, g_ref, w_ref, o_ref, h_ref, *, act):
    @pl.when(pl.program_id(1) == 0)
    def _():
        h_ref[...] = _rms(x_ref[...], g_ref[...], NORM_EPS).astype(BF16)

    y = _dot(h_ref[...], w_ref[...])
    if act == "sigmoid":
        y = jax.nn.sigmoid(y)
    o_ref[...] = y.astype(o_ref.dtype)


def norm_matmul(x, g, w, *, tm, tn, out_dtype, act=None, name):
    m, d = x.shape
    n = w.shape[1]
    return pl.pallas_call(
        functools.partial(_norm_matmul_kernel, act=act),
        grid=(m // tm, n // tn),
        in_specs=[
            pl.BlockSpec((tm, d), lambda i, j: (i, 0)),
            pl.BlockSpec((1, d), lambda i, j: (0, 0)),
            pl.BlockSpec((d, tn), lambda i, j: (0, j)),
        ],
        out_specs=pl.BlockSpec((tm, tn), lambda i, j: (i, j)),
        out_shape=jax.ShapeDtypeStruct((m, n), out_dtype),
        scratch_shapes=[pltpu.VMEM((tm, d), BF16)],
        compiler_params=_cparams(("parallel", "arbitrary")),
        name=name,
    )(x, g.reshape(1, d), w)


def _head_ones():
    r = lax.broadcasted_iota(jnp.int32, (SLAB, SLAB), 0)
    c = lax.broadcasted_iota(jnp.int32, (SLAB, SLAB), 1)
    return ((r // HEAD_DIM) == (c // HEAD_DIM)).astype(BF16)


def _rwkv_prep_kernel(p_ref, prev_ref, mu_ref, w0_ref, a0_ref, kk_ref, ka_ref, w2_ref, a2_ref, g2_ref,
                      r_out, lw_out, k_out, v_out, kkn_out, b_out, g_out, *, tiles_per_seq):
    c = RWKV_WIDTH
    i = pl.program_id(0)
    p = p_ref[...]
    tm = p.shape[0]
    first = (i % tiles_per_seq) == 0
    prev_row = jnp.where(first, 0.0, prev_ref[7:8, :])
    row = lax.broadcasted_iota(jnp.int32, p.shape, 0)
    shifted = jnp.where(row == 0, prev_row, pltpu.roll(p, 1, axis=0))
    ps = p + (shifted - p) * mu_ref[...]
    r = ps[:, 0:c]
    k = ps[:, c:2 * c]
    v = ps[:, 2 * c:3 * c]
    wd = ps[:, 3 * c:3 * c + 128]
    ad = ps[:, 3 * c + 128:3 * c + 256]
    gd = ps[:, 3 * c + 256:3 * c + 512]
    z = -(w0_ref[...] + _dot3(jnp.tanh(wd), w2_ref[...]))
    softplus = jnp.maximum(z, 0.0) + jnp.log1p(jnp.exp(-jnp.abs(z)))
    w = -softplus - 0.5
    lw = -jnp.exp(w)
    a = jax.nn.sigmoid(a0_ref[...] + _dot3(ad, a2_ref[...]))
    g = _dot3(jax.nn.sigmoid(gd), g2_ref[...])
    kkr = k * kk_ref[...]
    k2 = k * (1.0 + (a - 1.0) * ka_ref[...])
    ones = _head_ones()
    for q in range(RWKV_SLABS):
        sl = slice(q * SLAB, (q + 1) * SLAB)
        x = kkr[:, sl]
        ss = _dot_exact_rhs(x * x, ones)
        kkn = x / jnp.maximum(jnp.sqrt(ss), 1e-12)
        r_out[q] = r[:, sl]
        lw_out[q] = lw[:, sl]
        k_out[q] = k2[:, sl]
        v_out[q] = v[:, sl]
        kkn_out[q] = kkn
        b_out[q] = kkn * a[:, sl]
        g_out[q] = g[:, sl]


def rwkv_prep(p, mu, w0, a0, k_k, k_a, w2p, a2p, g2p, *, seq, tm):
    t, cols = p.shape
    c = RWKV_WIDTH
    vec = lambda n: pl.BlockSpec((1, n), lambda i: (0, 0))
    full = lambda a: pl.BlockSpec(a.shape, lambda i: (0, 0))
    out_spec = pl.BlockSpec((RWKV_SLABS, tm, SLAB), lambda i: (0, i, 0))
    out_shape = jax.ShapeDtypeStruct((RWKV_SLABS, t, SLAB), F32)
    return pl.pallas_call(
        functools.partial(_rwkv_prep_kernel, tiles_per_seq=seq // tm),
        grid=(t // tm,),
        in_specs=[
            pl.BlockSpec((tm, cols), lambda i: (i, 0)),
            pl.BlockSpec((8, cols), lambda i: (jnp.maximum(i * (tm // 8) - 1, 0), 0)),
            vec(cols), vec(c), vec(c), vec(c), vec(c), full(w2p), full(a2p), full(g2p),
        ],
        out_specs=[out_spec] * 7,
        out_shape=[out_shape] * 7,
        compiler_params=_cparams(("parallel",)),
        name="rwkv_prep",
    )(p, p, mu.reshape(1, cols), w0.reshape(1, c), a0.reshape(1, c), k_k.reshape(1, c), k_a.reshape(1, c),
      w2p, a2p, g2p)


def _dot1(a, b, dims=NN):
    return _dot(a.astype(BF16), b.astype(BF16), dims)


def _dot3s(a, b, dims=NN):
    ah, al = _split2(a)
    bh, bl = _split2(b)
    m = a.shape[0]
    lhs = jnp.concatenate([ah, al], axis=0)
    if dims is NN:
        n = b.shape[1]
        rhs = jnp.concatenate([bh, bl], axis=1)
    else:
        n = b.shape[0]
        rhs = jnp.concatenate([bh, bl], axis=0)
    p = _dot(lhs, rhs, dims)
    return p[:m, :n] + (p[:m, n:] + p[m:, :n])


def _sum3_exact_rhs(xs, ones):
    parts = []
    for x in xs:
        parts.extend(_split3(x))
    res = _dot(jnp.concatenate(parts, axis=0), ones)
    L = xs[0].shape[0]
    return [res[(3 * i) * L:(3 * i + 1) * L] + (res[(3 * i + 1) * L:(3 * i + 2) * L] + res[(3 * i + 2) * L:(3 * i + 3) * L])
            for i in range(len(xs))]


def _expert_weight_chunk(w1_ref, w2a_ref, w2b_ref, w1o_ref, w2o_ref, tmp_ref):
    w1o_ref[0] = w1_ref[0].astype(BF16)
    half = W2_GROUP
    for c in range(ROW_TILE):
        cols = slice(c * LANES, (c + 1) * LANES)
        tmp_ref[c, pl.ds(0, half, stride=2), :] = w2a_ref[0, :, cols]
        tmp_ref[c, pl.ds(1, half, stride=2), :] = w2b_ref[0, :, cols]
        w2o_ref[0, :, cols] = tmp_ref[c].astype(BF16)


def _expert_weight_views(w1, w2):
    e, d, f2 = w1.shape
    fdim = w2.shape[1]
    groups_per_half = (MOE_FT // 2) // W2_GROUP
    n_chunks = e * (fdim // MOE_FT) * groups_per_half
    w1v = w1.reshape(n_chunks, (e * d) // n_chunks, f2)
    w2v = w2.reshape(e * fdim // W2_GROUP, W2_GROUP, d)
    return n_chunks, groups_per_half, w1v, w2v


def _expert_weight_specs(n_chunks, groups_per_half, w1v, w2v, step):
    d = w2v.shape[2]
    a_idx = lambda j: (j // groups_per_half) * (2 * groups_per_half) + j % groups_per_half
    in_specs = [pl.BlockSpec((1,) + w1v.shape[1:], lambda *ids: (step(*ids), 0, 0)),
                pl.BlockSpec((1, W2_GROUP, d), lambda *ids: (a_idx(step(*ids)), 0, 0)),
                pl.BlockSpec((1, W2_GROUP, d), lambda *ids: (a_idx(step(*ids)) + groups_per_half, 0, 0))]
    out_specs = [pl.BlockSpec((1,) + w1v.shape[1:], lambda *ids: (step(*ids), 0, 0)),
                 pl.BlockSpec((1, 2 * W2_GROUP, d), lambda *ids: (step(*ids), 0, 0))]
    out_shape = [jax.ShapeDtypeStruct(w1v.shape, BF16), jax.ShapeDtypeStruct((n_chunks, 2 * W2_GROUP, d), BF16)]
    scratch = pltpu.VMEM((ROW_TILE, 2 * W2_GROUP, LANES), F32)
    return in_specs, out_specs, out_shape, scratch


def expert_weight_prep(w1, w2):
    n_chunks, gph, w1v, w2v = _expert_weight_views(w1, w2)
    in_specs, out_specs, out_shape, scratch = _expert_weight_specs(n_chunks, gph, w1v, w2v, lambda j: j)
    w1b, w2b = pl.pallas_call(
        _expert_weight_chunk, grid=(n_chunks,), in_specs=in_specs, out_specs=out_specs, out_shape=out_shape,
        scratch_shapes=[scratch], compiler_params=_cparams(("parallel",)), name="expert_weight_prep",
    )(w1v, w2v, w2v)
    return w1b.reshape(w1.shape), w2b.reshape(w2.shape)


def _rwkv_scan_kernel(r_ref, lw_ref, k_ref, v_ref, kk_ref, b_ref, g_ref, lnw_ref, lnb_ref, rk_ref, *rest):
    if len(rest) == 2:
        o_ref, st_ref = rest
    else:
        w1_ref, w2a_ref, w2b_ref, o_ref, w1o_ref, w2o_ref, st_ref, tmp_ref = rest
        _expert_weight_chunk(w1_ref, w2a_ref, w2b_ref, w1o_ref, w2o_ref, tmp_ref)
    L = CHUNK
    W = SLAB

    @pl.when(pl.program_id(1) == 0)
    def _():
        st_ref[...] = jnp.zeros_like(st_ref)

    t_i = lax.broadcasted_iota(jnp.int32, (L, W), 0)
    lane = lax.broadcasted_iota(jnp.int32, (L, W), 1)
    j_i = lane % HEAD_DIM
    hid = lane // HEAD_DIM
    strict = j_i < t_i
    incl = j_i <= t_i
    eye = (j_i == t_i).astype(F32)
    r2 = lax.broadcasted_iota(jnp.int32, (W, W), 0)
    c2 = lax.broadcasted_iota(jnp.int32, (W, W), 1)
    same_head = (r2 // HEAD_DIM) == (c2 // HEAD_DIM)
    diag = r2 == c2
    ones = same_head.astype(BF16)
    tr = lax.broadcasted_iota(jnp.int32, (L, L), 0)
    tc = lax.broadcasted_iota(jnp.int32, (L, L), 1)
    tri = (tc <= tr).astype(BF16)

    def bd(x):
        return jnp.concatenate([jnp.where(hid == h, x, 0.0) for h in range(HEADS_PER_SLAB)], axis=0)

    U = range(RWKV_SLABS)
    cat0 = lambda xs: jnp.concatenate(xs, axis=0)
    cat1 = lambda xs: jnp.concatenate(xs, axis=1)
    r = [r_ref[u] for u in U]
    lw = [lw_ref[u] for u in U]
    k = [k_ref[u] for u in U]
    v = [v_ref[u] for u in U]
    c3 = [_dot(tri, cat1(_split3(lw[u]))) for u in U]
    cum = [c3[u][:, 0:W] + (c3[u][:, W:2 * W] + c3[u][:, 2 * W:]) for u in U]
    cum_end = [cum[u][L - 1:L, :] for u in U]
    e_cum = [jnp.exp(cum[u]) for u in U]
    e_neg = [jnp.exp(-cum[u]) for u in U]
    e_end = [jnp.exp(cum_end[u] - cum[u]) for u in U]
    at = [-kk_ref[u] * jnp.exp(cum[u] - lw[u]) for u in U]
    rt = [r[u] * e_cum[u] for u in U]
    bt = [b_ref[u] * e_neg[u] for u in U]
    kt = [k[u] * e_neg[u] for u in U]
    bh = [b_ref[u] * e_end[u] for u in U]
    kh = [k[u] * e_end[u] for u in U]
    w_end = [jnp.exp(cum_end[u]) for u in U]

    g_all = [_dot1(cat0([at[u], rt[u]]), cat0([bd(bt[u]), bd(kt[u])]), NT) for u in U]
    a_ab = [jnp.where(strict, g_all[u][0:L, 0:W], 0.0) for u in U]
    a_ak = [jnp.where(strict, g_all[u][0:L, W:], 0.0) for u in U]
    a_rb = [jnp.where(incl, g_all[u][L:, 0:W], 0.0) for u in U]
    a_rk = [jnp.where(incl, g_all[u][L:, W:], 0.0) for u in U]

    tinv = [eye for u in U]
    pw = a_ab
    for it in range(6):
        if it < 5:
            res = [_dot1(pw[u], cat1([bd(tinv[u]), bd(pw[u])])) for u in U]
            tinv = [tinv[u] + res[u][:, 0:W] for u in U]
            pw = [res[u][:, W:] for u in U]
        else:
            tinv = [tinv[u] + _dot1(pw[u], bd(tinv[u])) for u in U]

    bdv = [bd(v[u]) for u in U]
    av = [_dot1(a_ak[u], bdv[u]) for u in U]
    qp = [_dot1(tinv[u], cat1([bd(at[u]), bd(av[u])])) for u in U]
    q1 = [qp[u][:, 0:W] for u in U]
    p1 = [qp[u][:, W:] for u in U]
    qp2 = [_dot1(cat1([a_rb[u], a_rk[u]]),
                  cat0([cat1([bd(q1[u]), bd(p1[u])]), cat1([jnp.zeros_like(bdv[u]), bdv[u]])])) for u in U]
    q2 = [rt[u] + qp2[u][:, 0:W] for u in U]
    p2 = [qp2[u][:, W:] for u in U]
    mp = [_dot1(cat0([bh[u], kh[u]]).T,
                 cat0([cat1([q1[u], p1[u]]), cat1([jnp.zeros_like(v[u]), v[u]])])) for u in U]
    m_bd = [jnp.where(same_head, mp[u][:, 0:W], 0.0) + jnp.where(diag, w_end[u], 0.0) for u in U]
    ys = [_dot3s(cat0([q2[u], m_bd[u]]), st_ref[u]) for u in U]
    y = [ys[u][0:L] + p2[u] for u in U]
    for u in U:
        st_ref[u] = ys[u][L:] + jnp.where(same_head, mp[u][:, W:], 0.0)

    sums = [_sum3_exact_rhs([y[u], r[u] * k[u] * rk_ref[u]], ones) for u in U]
    yc = [y[u] - sums[u][0] * (1.0 / HEAD_DIM) for u in U]
    var = [_sum3_exact_rhs([yc[u] * yc[u]], ones)[0] for u in U]
    for u in U:
        yn = yc[u] * lax.rsqrt(var[u] * (1.0 / HEAD_DIM) + RWKV_GN_EPS) * lnw_ref[u] + lnb_ref[u]
        o_ref[u] = ((yn + sums[u][1] * v[u]) * g_ref[u]).astype(o_ref.dtype)


def rwkv_scan(r, lw, k, v, kk, b, g, ln_w, ln_b, r_k, w1, w2, *, batch, seq):
    nchunk = seq // CHUNK
    blk = pl.BlockSpec((RWKV_SLABS, CHUNK, SLAB), lambda bi, ci: (0, bi * nchunk + ci, 0))
    par = pl.BlockSpec((RWKV_SLABS, 1, SLAB), lambda bi, ci: (0, 0, 0))
    t = batch * seq
    out_shape = jax.ShapeDtypeStruct((RWKV_SLABS, t, SLAB), BF16)
    state = pltpu.VMEM((RWKV_SLABS, SLAB, SLAB), F32)
    args = (r, lw, k, v, kk, b, g, ln_w.reshape(RWKV_SLABS, 1, SLAB), ln_b.reshape(RWKV_SLABS, 1, SLAB),
            r_k.reshape(RWKV_SLABS, 1, SLAB))
    n_chunks, gph, w1v, w2v = _expert_weight_views(w1, w2)
    if n_chunks != batch * nchunk:
        ra = pl.pallas_call(
            _rwkv_scan_kernel, grid=(batch, nchunk), in_specs=[blk] * 7 + [par] * 3, out_specs=blk,
            out_shape=out_shape, scratch_shapes=[state],
            compiler_params=_cparams(("arbitrary", "arbitrary")), name="rwkv_scan",
        )(*args)
        return (ra,) + expert_weight_prep(w1, w2)
    w_in, w_out, w_shape, w_scratch = _expert_weight_specs(n_chunks, gph, w1v, w2v, lambda bi, ci: bi * nchunk + ci)
    ra, w1b, w2b = pl.pallas_call(
        _rwkv_scan_kernel, grid=(batch, nchunk), in_specs=[blk] * 7 + [par] * 3 + w_in,
        out_specs=[blk] + w_out, out_shape=[out_shape] + w_shape, scratch_shapes=[state, w_scratch],
        compiler_params=_cparams(("arbitrary", "arbitrary")), name="rwkv_scan",
    )(*args, w1v, w2v, w2v)
    return ra, w1b.reshape(w1.shape), w2b.reshape(w2.shape)


def _diff_attn_kernel(qi_ref, ki_ref, slope_ref, q_ref, k_ref, v_ref, lq1_ref, lk1_ref, lq2_ref, lk2_ref, sg_ref,
                      o_ref, qs_ref, relb_ref, m_ref, l_ref, acc_ref, *, tq, lam_init):
    h = pl.program_id(1)
    qi = qi_ref[pl.program_id(2)]
    ki = ki_ref[pl.program_id(2)]
    c2 = slope_ref[h] * LOG2E

    @pl.when(ki == 0)
    def _():
        m_ref[...] = jnp.full_like(m_ref, -jnp.inf)
        l_ref[...] = jnp.zeros_like(l_ref)
        acc_ref[...] = jnp.zeros_like(acc_ref)
        q = q_ref[...].astype(F32) * (HEAD_DIM ** -0.5 * LOG2E)
        lane = lax.broadcasted_iota(jnp.int32, q.shape, 1)
        m0 = lane < HEAD_DIM
        qs_ref[0:tq, :] = jnp.where(m0, q, 0.0).astype(BF16)
        qs_ref[tq:, :] = jnp.where(m0, 0.0, q).astype(BF16)
        rel = (lax.broadcasted_iota(jnp.int32, (tq, tq), 1) - lax.broadcasted_iota(jnp.int32, (tq, tq), 0))
        relb = rel.astype(F32) * (-c2)
        relb_ref[:, 0:tq] = relb
        relb_ref[:, tq:] = relb

    def step(masked):
        k = k_ref[...]
        vt = v_ref[...].T
        relb = relb_ref[...]
        tile_bias = c2 * ((qi - ki) * tq).astype(F32)
        s = _dot(k, qs_ref[...], NT) + relb
        if masked:
            s = jnp.where(relb > 0.0, -jnp.inf, s)
        m_prev = m_ref[...]
        m_new = jnp.maximum(m_prev, jnp.max(s, axis=0, keepdims=True) - tile_bias)
        alpha = jnp.exp2(m_prev - m_new)
        p = jnp.exp2(s - (m_new + tile_bias))
        l_ref[...] = alpha * l_ref[...] + jnp.sum(p, axis=0, keepdims=True)
        acc_ref[...] = alpha * acc_ref[...] + _dot(vt, p.astype(BF16))
        m_ref[...] = m_new

    @pl.when(ki < qi)
    def _():
        step(False)

    @pl.when(ki == qi)
    def _():
        step(True)
        lam = (jnp.exp(jnp.sum(lq1_ref[...] * lk1_ref[...], axis=-1, keepdims=True))
               - jnp.exp(jnp.sum(lq2_ref[...] * lk2_ref[...], axis=-1, keepdims=True)) + lam_init)
        ot = acc_ref[:, 0:tq] / l_ref[:, 0:tq] - lam * (acc_ref[:, tq:] / l_ref[:, tq:])
        o = _rms(ot.T, sg_ref[...], NORM_EPS) * (1.0 - lam_init)
        o_ref[...] = o.astype(o_ref.dtype)


def diff_attention(qkv, slopes, lq1, lk1, lq2, lk2, subln_g, *, batch, seq, tq, lam_init):
    t = batch * seq
    nq = seq // tq
    hb = DIFF_WIDTH // LANES
    pairs = [(qi, ki) for qi in range(nq) for ki in range(qi + 1)]
    qi_tab = jnp.asarray([pq for pq, _ in pairs], jnp.int32)
    ki_tab = jnp.asarray([pk for _, pk in pairs], jnp.int32)
    small = pl.BlockSpec((1, HEAD_DIM), lambda b, h, j, qt, kt: (0, 0))
    grid_spec = pltpu.PrefetchScalarGridSpec(
        num_scalar_prefetch=2,
        grid=(batch, DIFF_HEADS, len(pairs)),
        in_specs=[
            pl.BlockSpec(memory_space=pltpu.SMEM),
            pl.BlockSpec((tq, LANES), lambda b, h, j, qt, kt: (b * nq + qt[j], h)),
            pl.BlockSpec((tq, LANES), lambda b, h, j, qt, kt: (b * nq + kt[j], hb + h)),
            pl.BlockSpec((tq, LANES), lambda b, h, j, qt, kt: (b * nq + kt[j], 2 * hb + h)),
            small, small, small, small,
            pl.BlockSpec((1, LANES), lambda b, h, j, qt, kt: (0, 0)),
        ],
        out_specs=pl.BlockSpec((tq, LANES), lambda b, h, j, qt, kt: (b * nq + qt[j], h)),
        scratch_shapes=[pltpu.VMEM((2 * tq, LANES), BF16), pltpu.VMEM((tq, 2 * tq), F32),
                        pltpu.VMEM((1, 2 * tq), F32), pltpu.VMEM((1, 2 * tq), F32),
                        pltpu.VMEM((LANES, 2 * tq), F32)],
    )
    return pl.pallas_call(
        functools.partial(_diff_attn_kernel, tq=tq, lam_init=lam_init),
        grid_spec=grid_spec,
        out_shape=jax.ShapeDtypeStruct((t, DIFF_WIDTH), BF16),
        compiler_params=_cparams(("parallel", "parallel", "arbitrary")),
        name="diff_attention",
    )(qi_tab, ki_tab, slopes, qkv, qkv, qkv, lq1.reshape(1, -1), lk1.reshape(1, -1), lq2.reshape(1, -1),
      lk2.reshape(1, -1), subln_g.reshape(1, -1))


def _mix_kernel(ra_ref, da_ref, ga_ref, gb_ref, x_ref, wa_ref, wb_ref, wo_ref, o_ref):
    ya = _dot(ra_ref[0], wa_ref[0:SLAB, :])
    for q in range(1, RWKV_SLABS):
        ya = ya + _dot(ra_ref[q], wa_ref[q * SLAB:(q + 1) * SLAB, :])
    yb = _dot(da_ref[...], wb_ref[...])
    mixed = ga_ref[...].astype(F32) * ya + gb_ref[...].astype(F32) * yb
    o_ref[...] = x_ref[...] + _dot(mixed.astype(BF16), wo_ref[...])


def mix_project(ra, da, gates, x, wa, wb, wo, *, tm):
    t, d = x.shape
    const = lambda a: pl.BlockSpec(a.shape, lambda i: (0, 0), pipeline_mode=pl.Buffered(1))
    return pl.pallas_call(
        _mix_kernel,
        grid=(t // tm,),
        in_specs=[
            pl.BlockSpec((RWKV_SLABS, tm, SLAB), lambda i: (0, i, 0)),
            pl.BlockSpec((tm, DIFF_WIDTH), lambda i: (i, 0)),
            pl.BlockSpec((tm, d), lambda i: (i, 0)),
            pl.BlockSpec((tm, d), lambda i: (i, 1)),
            pl.BlockSpec((tm, d), lambda i: (i, 0)),
            const(wa), const(wb), const(wo),
        ],
        out_specs=pl.BlockSpec((tm, d), lambda i: (i, 0)),
        out_shape=jax.ShapeDtypeStruct((t, d), F32),
        compiler_params=_cparams(("parallel",)),
        name="mix_project",
    )(ra, da, gates, gates, x, wa, wb, wo)


def _cross_kernel(x_ref, gc_ref, wq_ref, kv_ref, wo_ref, gf_ref, rw_ref, rb_ref, x2_ref, hf_ref, lg_ref):
    x = x_ref[...]
    h = _rms(x, gc_ref[...], NORM_EPS).astype(BF16)
    q = _dot(h, wq_ref[...]).astype(BF16)
    scale = LANES ** -0.5
    outs = []
    for hd in range(CROSS_HEADS):
        qh = q[:, hd * LANES:(hd + 1) * LANES]
        kh = kv_ref[0, :, hd * LANES:(hd + 1) * LANES]
        vh = kv_ref[0, :, CROSS_WIDTH + hd * LANES:CROSS_WIDTH + (hd + 1) * LANES]
        s = _dot(qh, kh, NT) * scale
        s = s - jnp.max(s, axis=-1, keepdims=True)
        e = jnp.exp(s)
        p = e / jnp.sum(e, axis=-1, keepdims=True)
        outs.append(_dot(p.astype(BF16), vh))
    o = jnp.concatenate(outs, axis=1).astype(BF16)
    x2 = x + _dot(o, wo_ref[...])
    x2_ref[...] = x2
    hf = _rms(x2, gf_ref[...], NORM_EPS)
    tm = hf.shape[0]
    for c in range(PACK_ROWS):
        hf_ref[pl.ds(c, tm, stride=PACK_ROWS), :] = _pack_pair(hf[:, 2 * c * LANES:(2 * c + 1) * LANES],
                                                               hf[:, (2 * c + 1) * LANES:(2 * c + 2) * LANES])
    lg_ref[...] = _dot3(hf, rw_ref[...]) + rb_ref[...]


def cross_attention(x, gc, wq, kv, wo, gf, rw, rb, *, seq, tm):
    t, d = x.shape
    per_seq = seq // tm
    const = lambda a: pl.BlockSpec(a.shape, lambda i: (0,) * a.ndim, pipeline_mode=pl.Buffered(1))
    vec = lambda n: pl.BlockSpec((1, n), lambda i: (0, 0))
    return pl.pallas_call(
        _cross_kernel,
        grid=(t // tm,),
        in_specs=[
            pl.BlockSpec((tm, d), lambda i: (i, 0)),
            vec(d), const(wq),
            pl.BlockSpec((1,) + kv.shape[1:], lambda i: (i // per_seq, 0, 0)),
            const(wo), vec(d), const(rw), vec(LANES),
        ],
        out_specs=[pl.BlockSpec((tm, d), lambda i: (i, 0)), pl.BlockSpec((tm * PACK_ROWS, LANES), lambda i: (i, 0)),
                   pl.BlockSpec((tm, LANES), lambda i: (i, 0))],
        out_shape=[jax.ShapeDtypeStruct((t, d), F32), jax.ShapeDtypeStruct((t * PACK_ROWS, LANES), U32),
                   jax.ShapeDtypeStruct((t, LANES), F32)],
        compiler_params=_cparams(("parallel",)),
        name="cross_attention",
    )(x, gc.reshape(1, d), wq, kv, wo, gf.reshape(1, d), rw, rb)


def _router_kernel(lg_ref, idx_ref, gate_ref):
    x = lg_ref[...]
    lane = lax.broadcasted_iota(jnp.int32, x.shape, 1)
    x = jnp.where(lane < N_EXPERTS, x, -jnp.inf)
    idx_out = jnp.zeros(x.shape, jnp.int32)
    val_out = jnp.zeros(x.shape, F32)
    vals = []
    for j in range(TOP_K):
        m = jnp.max(x, axis=-1, keepdims=True)
        sel = jnp.min(jnp.where(x == m, lane, LANES), axis=-1, keepdims=True)
        idx_out = jnp.where(lane == j, sel, idx_out)
        vals.append(m)
        x = jnp.where(lane == sel, -jnp.inf, x)
    es = [jnp.exp(vj - vals[0]) for vj in vals]
    tot = es[0] + es[1] + es[2] + es[3]
    for j in range(TOP_K):
        val_out = jnp.where(lane == j, es[j] / tot, val_out)
    idx_ref[...] = idx_out[:, 0:TOP_K]
    gate_ref[...] = val_out[:, 0:TOP_K]


def router_topk(logits, *, tm):
    t = logits.shape[0]
    return pl.pallas_call(
        _router_kernel,
        grid=(t // tm,),
        in_specs=[pl.BlockSpec((tm, LANES), lambda i: (i, 0))],
        out_specs=[pl.BlockSpec((tm, TOP_K), lambda i: (i, 0)), pl.BlockSpec((tm, TOP_K), lambda i: (i, 0))],
        out_shape=[jax.ShapeDtypeStruct((t, TOP_K), jnp.int32), jax.ShapeDtypeStruct((t, TOP_K), F32)],
        compiler_params=_cparams(("parallel",)),
        name="router_topk",
    )(logits)


def _moe_kernel(sbe_ref, sbr_ref, src_ref, tok_ref, ntok_ref, hf_hbm, w1_ref, b1_ref, w2_ref, b2_ref,
                ys_hbm, xbuf, x2d, acc, obuf, gsem, ssem):
    s = pl.program_id(0)
    f = pl.program_id(1)
    nf = pl.num_programs(1)
    nsb = pl.num_programs(0)
    sub_rows = MOE_SUB * PACK_ROWS
    nsubs = MOE_SUPER // MOE_SUB
    unroll = 8
    part = MOE_SUB // 2

    def ceil_sub(n):
        return lax.shift_right_logical(n + (MOE_SUB - 1), MOE_SUB.bit_length() - 1)

    rows = sbr_ref[s]
    nsub = ceil_sub(rows)
    rows_next = jnp.where(s + 1 < nsb, sbr_ref[jnp.minimum(s + 1, nsb - 1)], 0)
    rows_prev = jnp.where(s > 0, sbr_ref[jnp.maximum(s - 1, 0)], 0)
    nsub_next = ceil_sub(rows_next)
    nsub_prev = ceil_sub(rows_prev)

    def slab(ref, off):
        return ref.at[pl.ds(pl.multiple_of(off, PACK_ROWS), PACK_ROWS)]

    def row_in(table, r):
        return pltpu.make_async_copy(slab(hf_hbm, table[0, 0, r]), slab(xbuf, r * PACK_ROWS), gsem)

    def row_out(r):
        return pltpu.make_async_copy(slab(obuf, r * PACK_ROWS), slab(ys_hbm, src_ref[0, 0, r] * PACK_ROWS), ssem)

    def start_gather(table, first, n):
        def group(gi, c):
            for j in range(unroll):
                row_in(table, first + gi * unroll + j).start()
            return c

        lax.fori_loop(0, n // unroll, group, 0)

    def wait_sub_blocks(buf, other, sem, n):
        for sub in range(nsubs):
            @pl.when(sub < n)
            def _(sub=sub):
                pltpu.make_async_copy(other.at[pl.ds(0, sub_rows)], buf.at[pl.ds(sub * sub_rows, sub_rows)], sem).wait()

    def wait_scatter(n):
        full = n // MOE_SUB
        for sub in range(nsubs):
            @pl.when(sub < full)
            def _(sub=sub):
                pltpu.make_async_copy(obuf.at[pl.ds(sub * sub_rows, sub_rows)], ys_hbm.at[pl.ds(0, sub_rows)], ssem).wait()

        def one(r, c):
            row_out(r).wait()
            return c

        lax.fori_loop(0, n - full * MOE_SUB, one, 0)

    @pl.when((f == 0) & (s == 0) & (rows > 0))
    def _():
        start_gather(tok_ref, 0, nsub * MOE_SUB)

    @pl.when(f == 0)
    def _():
        wait_sub_blocks(xbuf, hf_hbm, gsem, jnp.maximum(nsub, nsub_prev))

    @pl.when((f == nf - 1) & (rows > 0) & (rows_prev > 0))
    def _():
        wait_scatter(rows_prev)

    for sub in range(nsubs):
        @pl.when(sub < nsub)
        def _(sub=sub):
            sl = slice(sub * MOE_SUB, (sub + 1) * MOE_SUB)
            base = sub * sub_rows

            @pl.when(f == 0)
            def _():
                for c in range(PACK_ROWS):
                    lo, hi = _unpack_pair(xbuf[pl.ds(base + c, MOE_SUB, stride=PACK_ROWS), :])
                    x2d[sl, 2 * c * LANES:(2 * c + 1) * LANES] = lo.astype(BF16)
                    x2d[sl, (2 * c + 1) * LANES:(2 * c + 2) * LANES] = hi.astype(BF16)

            first = sub * MOE_SUB + f * part
            for i in range(part):
                row_in(ntok_ref, first + i).start()

            x = x2d[sl, :]
            hb = _dot(x, w1_ref[0]) + b1_ref[0]
            even = (lax.broadcasted_iota(jnp.int32, (MOE_SUB, LANES), 1) % 2) == 0
            acts = []
            for cb in range(MOE_FT // LANES):
                a = hb[:, cb * LANES:(cb + 1) * LANES]
                b = hb[:, MOE_FT + cb * LANES:MOE_FT + (cb + 1) * LANES]
                hg = jnp.where(even, a, pltpu.roll(b, 1, axis=1))
                hl = jnp.where(even, pltpu.roll(a, LANES - 1, axis=1), b)
                xg = jnp.minimum(hg, SWIGLU_LIMIT)
                xl = jnp.clip(hl, -SWIGLU_LIMIT, SWIGLU_LIMIT)
                acts.append((xg * jax.nn.sigmoid(SWIGLU_ALPHA * xg) * (xl + 1.0)).astype(BF16))
            contrib = _dot(jnp.concatenate(acts, axis=1), w2_ref[0])

            @pl.when(f == 0)
            def _():
                acc[sl, :] = contrib

            @pl.when(f > 0)
            def _():
                acc[sl, :] += contrib

            @pl.when(f == nf - 1)
            def _():
                y = acc[sl, :] + b2_ref[0]
                for c in range(PACK_ROWS):
                    obuf[pl.ds(base + c, MOE_SUB, stride=PACK_ROWS), :] = _pack_pair(
                        y[:, 2 * c * LANES:(2 * c + 1) * LANES], y[:, (2 * c + 1) * LANES:(2 * c + 2) * LANES])

    for sub in range(nsubs):
        @pl.when((sub >= nsub) & (sub < nsub_next))
        def _(sub=sub):
            start_gather(ntok_ref, sub * MOE_SUB + f * part, part)

    @pl.when((f == nf - 1) & (rows > 0))
    def _():
        def group(gi, c):
            for j in range(unroll):
                row_out(gi * unroll + j).start()
            return c

        def one(r, c):
            row_out(r).start()
            return c

        ngroups = rows // unroll
        lax.fori_loop(0, ngroups, group, 0)
        lax.fori_loop(ngroups * unroll, rows, one, 0)

        @pl.when(rows_next == 0)
        def _():
            wait_scatter(rows)


def moe_experts(sb_e, sb_rows, row_src, row_off, hf, w1, b1, w2, b2):
    d = D_MODEL
    t = hf.shape[0] // PACK_ROWS
    nsb = sb_e.shape[0]
    nf = D_EXPERT // MOE_FT
    grid_spec = pltpu.PrefetchScalarGridSpec(
        num_scalar_prefetch=2,
        grid=(nsb, nf),
        in_specs=[
            pl.BlockSpec((1, 1, MOE_SUPER), lambda s, f, e, r: (s, 0, 0), memory_space=pltpu.SMEM),
            pl.BlockSpec((1, 1, MOE_SUPER), lambda s, f, e, r: (s, 0, 0), memory_space=pltpu.SMEM),
            pl.BlockSpec((1, 1, MOE_SUPER), lambda s, f, e, r: (jnp.minimum(s + 1, nsb - 1), 0, 0),
                         memory_space=pltpu.SMEM),
            pl.BlockSpec(memory_space=pl.ANY),
            pl.BlockSpec((1, d, 2 * MOE_FT), lambda s, f, e, r: (e[s], 0, f)),
            pl.BlockSpec((1, 1, 2 * MOE_FT), lambda s, f, e, r: (e[s], 0, f)),
            pl.BlockSpec((1, MOE_FT, d), lambda s, f, e, r: (e[s], f, 0)),
            pl.BlockSpec((1, 1, d), lambda s, f, e, r: (e[s], 0, 0)),
        ],
        out_specs=pl.BlockSpec(memory_space=pl.ANY),
        scratch_shapes=[pltpu.VMEM((MOE_SUPER * PACK_ROWS, LANES), U32), pltpu.VMEM((MOE_SUPER, d), BF16),
                        pltpu.VMEM((MOE_SUPER, d), F32), pltpu.VMEM((MOE_SUPER * PACK_ROWS, LANES), U32),
                        pltpu.SemaphoreType.DMA, pltpu.SemaphoreType.DMA],
    )
    return pl.pallas_call(
        _moe_kernel,
        grid_spec=grid_spec,
        out_shape=jax.ShapeDtypeStruct((t * TOP_K * PACK_ROWS, LANES), U32),
        compiler_params=_cparams(("arbitrary", "arbitrary")),
        name="moe_experts",
    )(sb_e, sb_rows, row_src, row_off, row_off, hf, w1, b1, w2, b2)


def _combine_kernel(ys_ref, gate_ref, x_ref, g_ref, o_ref, lo_ref, hi_ref):
    tm = x_ref.shape[0]
    tot_lo = tot_hi = None
    for j in range(TOP_K):
        lo, hi = _unpack_pair(ys_ref[:, j * PACK_ROWS:(j + 1) * PACK_ROWS, :])
        gj = gate_ref[:, j:j + 1, :]
        tot_lo = lo * gj if tot_lo is None else tot_lo + lo * gj
        tot_hi = hi * gj if tot_hi is None else tot_hi + hi * gj
    lo_ref[...] = tot_lo.reshape(tm * PACK_ROWS, LANES)
    hi_ref[...] = tot_hi.reshape(tm * PACK_ROWS, LANES)
    pieces = []
    for c in range(PACK_ROWS):
        pieces.append(lo_ref[pl.ds(c, tm, stride=PACK_ROWS), :])
        pieces.append(hi_ref[pl.ds(c, tm, stride=PACK_ROWS), :])
    out = x_ref[...] + jnp.concatenate(pieces, axis=1)
    o_ref[...] = _rms(out, g_ref[...], NORM_EPS)


def combine_final(ys, gate, x, g, *, tm):
    t, d = x.shape
    return pl.pallas_call(
        _combine_kernel,
        grid=(t // tm,),
        in_specs=[pl.BlockSpec((tm, TOP_K * PACK_ROWS, LANES), lambda i: (i, 0, 0)),
                  pl.BlockSpec((tm, TOP_K, LANES), lambda i: (i, 0, 0)),
                  pl.BlockSpec((tm, d), lambda i: (i, 0)), pl.BlockSpec((1, d), lambda i: (0, 0))],
        out_specs=pl.BlockSpec((tm, d), lambda i: (i, 0)),
        out_shape=jax.ShapeDtypeStruct((t, d), F32),
        scratch_shapes=[pltpu.VMEM((tm * PACK_ROWS, LANES), F32), pltpu.VMEM((tm * PACK_ROWS, LANES), F32)],
        compiler_params=_cparams(("parallel",)),
        name="combine_final",
    )(ys, gate, x, g.reshape(1, d))


def _routing_tables(top_idx, n_super):
    flat_e = top_idx.reshape(-1)
    n = flat_e.shape[0]
    onehot = (flat_e[:, None] == jnp.arange(N_EXPERTS, dtype=jnp.int32)[None, :]).astype(jnp.int32)
    csum = jnp.cumsum(onehot, axis=0)
    rank = jnp.sum(onehot * csum, axis=1) - 1
    counts = csum[-1]
    nsb = (counts + MOE_SUPER - 1) // MOE_SUPER
    sb_end = jnp.cumsum(nsb)
    sb_start = sb_end - nsb
    dest = sb_start[flat_e] * MOE_SUPER + rank
    row_src = jnp.full((n_super * MOE_SUPER,), -1, jnp.int32).at[dest].set(jnp.arange(n, dtype=jnp.int32))
    s_ids = jnp.arange(n_super, dtype=jnp.int32)
    sb_e = jnp.minimum(jnp.searchsorted(sb_end, s_ids, side="right"), N_EXPERTS - 1).astype(jnp.int32)
    local = s_ids - sb_start[sb_e]
    sb_rows = jnp.clip(counts[sb_e] - local * MOE_SUPER, 0, MOE_SUPER)
    sb_rows = jnp.where(s_ids < sb_end[-1], sb_rows, 0).astype(jnp.int32)
    row_off = jnp.where(row_src >= 0, (row_src // TOP_K) * PACK_ROWS, 0)
    return sb_e, sb_rows, row_src.reshape(n_super, 1, MOE_SUPER), row_off.reshape(n_super, 1, MOE_SUPER)


def _pad_rows(a, n):
    return jnp.pad(a, ((0, n - a.shape[0]), (0, 0)))


def _layer(x, mem, l, p, batch, seq):
    t = batch * seq
    c = RWKV_WIDTH
    lam_init = 0.8 - 0.6 * math.exp(-0.3 * l)
    w_in = p["w_in"]
    o1 = 3 * c + DECAY_LORA + AAA_LORA + GATE_LORA
    o2 = o1 + 3 * DIFF_WIDTH
    padc = lambda a, n: jnp.pad(a, ((0, 0), (0, n - a.shape[1])))
    w_rwkv = jnp.concatenate([
        w_in[:, :3 * c],
        padc(w_in[:, 3 * c:3 * c + DECAY_LORA], 128),
        padc(w_in[:, 3 * c + DECAY_LORA:3 * c + DECAY_LORA + AAA_LORA], 128),
        padc(w_in[:, 3 * c + DECAY_LORA + AAA_LORA:o1], 256)], axis=1).astype(BF16)
    mu = p["rwkv_mu"]
    pad1 = lambda a, n: jnp.pad(a, (0, n - a.shape[0]))
    mu_p = jnp.concatenate([mu[:3 * c], pad1(mu[3 * c:3 * c + DECAY_LORA], 128),
                            pad1(mu[3 * c + DECAY_LORA:3 * c + DECAY_LORA + AAA_LORA], 128),
                            pad1(mu[3 * c + DECAY_LORA + AAA_LORA:], 256)])
    w_diff = w_in[:, o1:o2].astype(BF16)
    w_gate = w_in[:, o2:].astype(BF16)

    g_mix = p["norm_mix_g"]
    p_rwkv = norm_matmul(x, g_mix, w_rwkv, tm=1024, tn=RWKV_COLS // 2, out_dtype=F32, name="in_proj_rwkv")
    qkv = norm_matmul(x, g_mix, w_diff, tm=1024, tn=1024, out_dtype=BF16, name="in_proj_diff")
    gates = norm_matmul(x, g_mix, w_gate, tm=1024, tn=1024, out_dtype=BF16, act="sigmoid", name="in_proj_gate")

    prep = rwkv_prep(p_rwkv, mu_p, p["rwkv_w0"], p["rwkv_a0"], p["rwkv_k_k"], p["rwkv_k_a"],
                     _pad_rows(p["rwkv_w2"], 128), _pad_rows(p["rwkv_a2"], 128), _pad_rows(p["rwkv_g2"], 256),
                     seq=seq, tm=256)
    ra, w1b, w2b = rwkv_scan(*prep, p["rwkv_ln_w"], p["rwkv_ln_b"], p["rwkv_r_k"].reshape(-1),
                             p["expert_w1"], p["expert_w2"], batch=batch, seq=seq)

    slopes = (2.0 ** (-8.0 * jnp.arange(1, DIFF_HEADS + 1, dtype=F32) / DIFF_HEADS)).astype(F32)
    da = diff_attention(qkv, slopes, p["diff_lq1"], p["diff_lk1"], p["diff_lq2"], p["diff_lk2"],
                        p["diff_subln_g"], batch=batch, seq=seq, tq=min(1024, seq), lam_init=lam_init)

    x1 = mix_project(ra, da, gates, x, p["rwkv_proj"].astype(BF16), p["diff_proj"].astype(BF16),
                     p["w_out"].astype(BF16), tm=256)

    m_len = mem.shape[0] // batch
    kv = norm_matmul(mem, p["norm_mem_g"], p["cross_wkv"].astype(BF16), tm=min(512, mem.shape[0]),
                     tn=2 * CROSS_WIDTH, out_dtype=BF16, name="cross_kv")
    rw = jnp.pad(p["router_w"], ((0, 0), (0, LANES - N_EXPERTS)))
    rb = jnp.pad(p["router_b"], (0, LANES - N_EXPERTS)).reshape(1, LANES)
    x2, hf, logits = cross_attention(x1, p["norm_cross_g"], p["cross_wq"].astype(BF16),
                                     kv.reshape(batch, m_len, 2 * CROSS_WIDTH), p["cross_wo"].astype(BF16),
                                     p["norm_ffn_g"], rw, rb, seq=seq, tm=256)

    top_idx, gate = router_topk(logits, tm=min(1024, t))
    n_super = (t * TOP_K) // MOE_SUPER + N_EXPERTS + 1
    sb_e, sb_rows, row_src, row_off = _routing_tables(top_idx, n_super)
    ys = moe_experts(sb_e, sb_rows, row_src, row_off, hf, w1b, p["expert_b1"][:, None, :], w2b, p["expert_b2"][:, None, :])
    gate_b = jnp.broadcast_to(gate[:, :, None], (t, TOP_K, LANES))
    return x2, ys.reshape(t, TOP_K * PACK_ROWS, LANES), gate_b


def kernel(x, mem, norm_mix_g, w_in, rwkv_mu, rwkv_w0, rwkv_w2, rwkv_a0, rwkv_a2, rwkv_g2, rwkv_k_k, rwkv_k_a, rwkv_r_k, rwkv_ln_w, rwkv_ln_b, rwkv_proj, diff_lq1, diff_lk1, diff_lq2, diff_lk2, diff_subln_g, diff_proj, w_out, norm_cross_g, norm_mem_g, cross_wq, cross_wkv, cross_wo, norm_ffn_g, router_w, router_b, expert_w1, expert_b1, expert_w2, expert_b2, final_norm_g):
    batch, seq, d = x.shape
    stacked = dict(norm_mix_g=norm_mix_g, w_in=w_in, rwkv_mu=rwkv_mu, rwkv_w0=rwkv_w0, rwkv_w2=rwkv_w2,
                   rwkv_a0=rwkv_a0, rwkv_a2=rwkv_a2, rwkv_g2=rwkv_g2, rwkv_k_k=rwkv_k_k, rwkv_k_a=rwkv_k_a,
                   rwkv_r_k=rwkv_r_k, rwkv_ln_w=rwkv_ln_w, rwkv_ln_b=rwkv_ln_b, rwkv_proj=rwkv_proj,
                   diff_lq1=diff_lq1, diff_lk1=diff_lk1, diff_lq2=diff_lq2, diff_lk2=diff_lk2,
                   diff_subln_g=diff_subln_g, diff_proj=diff_proj, w_out=w_out, norm_cross_g=norm_cross_g,
                   norm_mem_g=norm_mem_g, cross_wq=cross_wq, cross_wkv=cross_wkv, cross_wo=cross_wo,
                   norm_ffn_g=norm_ffn_g, router_w=router_w, router_b=router_b, expert_w1=expert_w1,
                   expert_b1=expert_b1, expert_w2=expert_w2, expert_b2=expert_b2)
    assert w_in.shape[0] == 1, "the closing RMSNorm is fused into the single layer's combine"
    p = {k: v[0] for k, v in stacked.items()}
    x2, ys, gate_b = _layer(x.reshape(batch * seq, d), mem.reshape(-1, d), 0, p, batch, seq)
    out = combine_final(ys, gate_b, x2, final_norm_g, tm=256)
    return out.reshape(batch, seq, d)
```

```python
import functools
import math

import jax
import jax.numpy as jnp
from jax import lax
from jax.experimental import pallas as pl
from jax.experimental.pallas import tpu as pltpu

F32 = jnp.float32
BF16 = jnp.bfloat16
U32 = jnp.uint32

D_MODEL = 2048
NORM_EPS = 1e-5
LOG2E = 1.4426950408889634
LANES = 128
HEAD_DIM = 64
RWKV_WIDTH = 1024
SLAB = 256
RWKV_SLABS = RWKV_WIDTH // SLAB
HEADS_PER_SLAB = SLAB // HEAD_DIM
CHUNK = 64
RWKV_GN_EPS = 64e-5
DECAY_LORA = 64
AAA_LORA = 64
GATE_LORA = 160
LORA_PAD = 512
RWKV_COLS = 3 * RWKV_WIDTH + LORA_PAD
DIFF_WIDTH = 1024
DIFF_HEADS = 8
CROSS_HEADS = 4
CROSS_WIDTH = 512
N_EXPERTS = 32
TOP_K = 4
D_EXPERT = 2048
SWIGLU_ALPHA = 1.702
SWIGLU_LIMIT = 7.0
MOE_SUPER = 1024
MOE_SUB = 256
MOE_FT = 1024
W2_GROUP = 128
ROW_TILE = D_MODEL // LANES
PACK_ROWS = ROW_TILE // 2
VMEM_LIMIT = 56 * 1024 * 1024

NN = (((1,), (0,)), ((), ()))
NT = (((1,), (1,)), ((), ()))


def _dot(a, b, dims=NN):
    return lax.dot_general(a, b, dims, preferred_element_type=F32)


def _split2(a):
    hi = a.astype(BF16)
    lo = (a - hi.astype(F32)).astype(BF16)
    return hi, lo


def _split3(a):
    hi = a.astype(BF16)
    r1 = a - hi.astype(F32)
    mid = r1.astype(BF16)
    lo = (r1 - mid.astype(F32)).astype(BF16)
    return hi, mid, lo


def _dot3(a, b, dims=NN):
    ah, al = _split2(a)
    bh, bl = _split2(b)
    return _dot(ah, bh, dims) + (_dot(ah, bl, dims) + _dot(al, bh, dims))


def _dot_exact_rhs(a, b_bf16, dims=NN):
    h, m, l = _split3(a)
    return _dot(h, b_bf16, dims) + (_dot(m, b_bf16, dims) + _dot(l, b_bf16, dims))


def _pack_pair(lo, hi):
    lo_b = lax.shift_right_logical(lax.bitcast_convert_type(lo.astype(BF16).astype(F32), U32), jnp.uint32(16))
    hi_b = lax.bitcast_convert_type(hi.astype(BF16).astype(F32), U32) & jnp.uint32(0xFFFF0000)
    return hi_b | lo_b


def _unpack_pair(w):
    lo = lax.bitcast_convert_type(lax.shift_left(w, jnp.uint32(16)), F32)
    hi = lax.bitcast_convert_type(w & jnp.uint32(0xFFFF0000), F32)
    return lo, hi


def _rms(x, g, eps):
    ms = jnp.mean(x * x, axis=-1, keepdims=True)
    return x * lax.rsqrt(ms + eps) * g


def _cparams(sem):
    return pltpu.CompilerParams(dimension_semantics=sem, vmem_limit_bytes=VMEM_LIMIT)


def _norm_matmul_kernel(x_ref, g_ref, w_ref, o_ref, h_ref, *, act):
    @pl.when(pl.program_id(1) == 0)
    def _():
        h_ref[...] = _rms(x_ref[...], g_ref[...], NORM_EPS).astype(BF16)

    y = _dot(h_ref[...], w_ref[...])
    if act == "sigmoid":
        y = jax.nn.sigmoid(y)
    o_ref[...] = y.astype(o_ref.dtype)


def norm_matmul(x, g, w, *, tm, tn, out_dtype, act=None, name):
    m, d = x.shape
    n = w.shape[1]
    return pl.pallas_call(
        functools.partial(_norm_matmul_kernel, act=act),
        grid=(m // tm, n // tn),
        in_specs=[
            pl.BlockSpec((tm, d), lambda i, j: (i, 0)),
            pl.BlockSpec((1, d), lambda i, j: (0, 0)),
            pl.BlockSpec((d, tn), lambda i, j: (0, j)),
        ],
        out_specs=pl.BlockSpec((tm, tn), lambda i, j: (i, j)),
        out_shape=jax.ShapeDtypeStruct((m, n), out_dtype),
        scratch_shapes=[pltpu.VMEM((tm, d), BF16)],
        compiler_params=_cparams(("parallel", "arbitrary")),
        name=name,
    )(x, g.reshape(1, d), w)


def _head_ones():
    r = lax.broadcasted_iota(jnp.int32, (SLAB, SLAB), 0)
    c = lax.broadcasted_iota(jnp.int32, (SLAB, SLAB), 1)
    return ((r // HEAD_DIM) == (c // HEAD_DIM)).astype(BF16)


def _rwkv_prep_kernel(p_ref, prev_ref, mu_ref, w0_ref, a0_ref, kk_ref, ka_ref, w2_ref, a2_ref, g2_ref,
                      r_out, lw_out, k_out, v_out, kkn_out, b_out, g_out, *, tiles_per_seq):
    c = RWKV_WIDTH
    i = pl.program_id(0)
    p = p_ref[...]
    tm = p.shape[0]
    first = (i % tiles_per_seq) == 0
    prev_row = jnp.where(first, 0.0, prev_ref[7:8, :])
    row = lax.broadcasted_iota(jnp.int32, p.shape, 0)
    shifted = jnp.where(row == 0, prev_row, pltpu.roll(p, 1, axis=0))
    ps = p + (shifted - p) * mu_ref[...]
    r = ps[:, 0:c]
    k = ps[:, c:2 * c]
    v = ps[:, 2 * c:3 * c]
    wd = ps[:, 3 * c:3 * c + 128]
    ad = ps[:, 3 * c + 128:3 * c + 256]
    gd = ps[:, 3 * c + 256:3 * c + 512]
    z = -(w0_ref[...] + _dot3(jnp.tanh(wd), w2_ref[...]))
    softplus = jnp.maximum(z, 0.0) + jnp.log1p(jnp.exp(-jnp.abs(z)))
    w = -softplus - 0.5
    lw = -jnp.exp(w)
    a = jax.nn.sigmoid(a0_ref[...] + _dot3(ad, a2_ref[...]))
    g = _dot3(jax.nn.sigmoid(gd), g2_ref[...])
    kkr = k * kk_ref[...]
    k2 = k * (1.0 + (a - 1.0) * ka_ref[...])
    ones = _head_ones()
    for q in range(RWKV_SLABS):
        sl = slice(q * SLAB, (q + 1) * SLAB)
        x = kkr[:, sl]
        ss = _dot_exact_rhs(x * x, ones)
        kkn = x / jnp.maximum(jnp.sqrt(ss), 1e-12)
        r_out[q] = r[:, sl]
        lw_out[q] = lw[:, sl]
        k_out[q] = k2[:, sl]
        v_out[q] = v[:, sl]
        kkn_out[q] = kkn
        b_out[q] = kkn * a[:, sl]
        g_out[q] = g[:, sl]


def rwkv_prep(p, mu, w0, a0, k_k, k_a, w2p, a2p, g2p, *, seq, tm):
    t, cols = p.shape
    c = RWKV_WIDTH
    vec = lambda n: pl.BlockSpec((1, n), lambda i: (0, 0))
    full = lambda a: pl.BlockSpec(a.shape, lambda i: (0, 0))
    out_spec = pl.BlockSpec((RWKV_SLABS, tm, SLAB), lambda i: (0, i, 0))
    out_shape = jax.ShapeDtypeStruct((RWKV_SLABS, t, SLAB), F32)
    return pl.pallas_call(
        functools.partial(_rwkv_prep_kernel, tiles_per_seq=seq // tm),
        grid=(t // tm,),
        in_specs=[
            pl.BlockSpec((tm, cols), lambda i: (i, 0)),
            pl.BlockSpec((8, cols), lambda i: (jnp.maximum(i * (tm // 8) - 1, 0), 0)),
            vec(cols), vec(c), vec(c), vec(c), vec(c), full(w2p), full(a2p), full(g2p),
        ],
        out_specs=[out_spec] * 7,
        out_shape=[out_shape] * 7,
        compiler_params=_cparams(("parallel",)),
        name="rwkv_prep",
    )(p, p, mu.reshape(1, cols), w0.reshape(1, c), a0.reshape(1, c), k_k.reshape(1, c), k_a.reshape(1, c),
      w2p, a2p, g2p)


def _dot1(a, b, dims=NN):
    return _dot(a.astype(BF16), b.astype(BF16), dims)


def _dot3s(a, b, dims=NN):
    ah, al = _split2(a)
    bh, bl = _split2(b)
    m = a.shape[0]
    lhs = jnp.concatenate([ah, al], axis=0)
    if dims is NN:
        n = b.shape[1]
        rhs = jnp.concatenate([bh, bl], axis=1)
    else:
        n = b.shape[0]
        rhs = jnp.concatenate([bh, bl], axis=0)
    p = _dot(lhs, rhs, dims)
    return p[:m, :n] + (p[:m, n:] + p[m:, :n])


def _sum3_exact_rhs(xs, ones):
    parts = []
    for x in xs:
        parts.extend(_split3(x))
    res = _dot(jnp.concatenate(parts, axis=0), ones)
    L = xs[0].shape[0]
    return [res[(3 * i) * L:(3 * i + 1) * L] + (res[(3 * i + 1) * L:(3 * i + 2) * L] + res[(3 * i + 2) * L:(3 * i + 3) * L])
            for i in range(len(xs))]


def _expert_weight_chunk(w1_ref, w2a_ref, w2b_ref, w1o_ref, w2o_ref, tmp_ref):
    w1o_ref[0] = w1_ref[0].astype(BF16)
    half = W2_GROUP
    for c in range(ROW_TILE):
        cols = slice(c * LANES, (c + 1) * LANES)
        tmp_ref[c, pl.ds(0, half, stride=2), :] = w2a_ref[0, :, cols]
        tmp_ref[c, pl.ds(1, half, stride=2), :] = w2b_ref[0, :, cols]
        w2o_ref[0, :, cols] = tmp_ref[c].astype(BF16)


def _expert_weight_views(w1, w2):
    e, d, f2 = w1.shape
    fdim = w2.shape[1]
    groups_per_half = (MOE_FT // 2) // W2_GROUP
    n_chunks = e * (fdim // MOE_FT) * groups_per_half
    w1v = w1.reshape(n_chunks, (e * d) // n_chunks, f2)
    w2v = w2.reshape(e * fdim // W2_GROUP, W2_GROUP, d)
    return n_chunks, groups_per_half, w1v, w2v


def _expert_weight_specs(n_chunks, groups_per_half, w1v, w2v, step):
    d = w2v.shape[2]
    a_idx = lambda j: (j // groups_per_half) * (2 * groups_per_half) + j % groups_per_half
    in_specs = [pl.BlockSpec((1,) + w1v.shape[1:], lambda *ids: (step(*ids), 0, 0)),
                pl.BlockSpec((1, W2_GROUP, d), lambda *ids: (a_idx(step(*ids)), 0, 0)),
                pl.BlockSpec((1, W2_GROUP, d), lambda *ids: (a_idx(step(*ids)) + groups_per_half, 0, 0))]
    out_specs = [pl.BlockSpec((1,) + w1v.shape[1:], lambda *ids: (step(*ids), 0, 0)),
                 pl.BlockSpec((1, 2 * W2_GROUP, d), lambda *ids: (step(*ids), 0, 0))]
    out_shape = [jax.ShapeDtypeStruct(w1v.shape, BF16), jax.ShapeDtypeStruct((n_chunks, 2 * W2_GROUP, d), BF16)]
    scratch = pltpu.VMEM((ROW_TILE, 2 * W2_GROUP, LANES), F32)
    return in_specs, out_specs, out_shape, scratch


def expert_weight_prep(w1, w2):
    n_chunks, gph, w1v, w2v = _expert_weight_views(w1, w2)
    in_specs, out_specs, out_shape, scratch = _expert_weight_specs(n_chunks, gph, w1v, w2v, lambda j: j)
    w1b, w2b = pl.pallas_call(
        _expert_weight_chunk, grid=(n_chunks,), in_specs=in_specs, out_specs=out_specs, out_shape=out_shape,
        scratch_shapes=[scratch], compiler_params=_cparams(("parallel",)), name="expert_weight_prep",
    )(w1v, w2v, w2v)
    return w1b.reshape(w1.shape), w2b.reshape(w2.shape)


def _rwkv_scan_kernel(r_ref, lw_ref, k_ref, v_ref, kk_ref, b_ref, g_ref, lnw_ref, lnb_ref, rk_ref, *rest):
    if len(rest) == 2:
        o_ref, st_ref = rest
    else:
        w1_ref, w2a_ref, w2b_ref, o_ref, w1o_ref, w2o_ref, st_ref, tmp_ref = rest
        _expert_weight_chunk(w1_ref, w2a_ref, w2b_ref, w1o_ref, w2o_ref, tmp_ref)
    L = CHUNK
    W = SLAB

    @pl.when(pl.program_id(1) == 0)
    def _():
        st_ref[...] = jnp.zeros_like(st_ref)

    t_i = lax.broadcasted_iota(jnp.int32, (L, W), 0)
    lane = lax.broadcasted_iota(jnp.int32, (L, W), 1)
    j_i = lane % HEAD_DIM
    hid = lane // HEAD_DIM
    strict = j_i < t_i
    incl = j_i <= t_i
    eye = (j_i == t_i).astype(F32)
    r2 = lax.broadcasted_iota(jnp.int32, (W, W), 0)
    c2 = lax.broadcasted_iota(jnp.int32, (W, W), 1)
    same_head = (r2 // HEAD_DIM) == (c2 // HEAD_DIM)
    diag = r2 == c2
    ones = same_head.astype(BF16)
    tr = lax.broadcasted_iota(jnp.int32, (L, L), 0)
    tc = lax.broadcasted_iota(jnp.int32, (L, L), 1)
    tri = (tc <= tr).astype(BF16)

    def bd(x):
        return jnp.concatenate([jnp.where(hid == h, x, 0.0) for h in range(HEADS_PER_SLAB)], axis=0)

    U = range(RWKV_SLABS)
    cat0 = lambda xs: jnp.concatenate(xs, axis=0)
    cat1 = lambda xs: jnp.concatenate(xs, axis=1)
    r = [r_ref[u] for u in U]
    lw = [lw_ref[u] for u in U]
    k = [k_ref[u] for u in U]
    v = [v_ref[u] for u in U]
    c3 = [_dot(tri, cat1(_split3(lw[u]))) for u in U]
    cum = [c3[u][:, 0:W] + (c3[u][:, W:2 * W] + c3[u][:, 2 * W:]) for u in U]
    cum_end = [cum[u][L - 1:L, :] for u in U]
    e_cum = [jnp.exp(cum[u]) for u in U]
    e_neg = [jnp.exp(-cum[u]) for u in U]
    e_end = [jnp.exp(cum_end[u] - cum[u]) for u in U]
    at = [-kk_ref[u] * jnp.exp(cum[u] - lw[u]) for u in U]
    rt = [r[u] * e_cum[u] for u in U]
    bt = [b_ref[u] * e_neg[u] for u in U]
    kt = [k[u] * e_neg[u] for u in U]
    bh = [b_ref[u] * e_end[u] for u in U]
    kh = [k[u] * e_end[u] for u in U]
    w_end = [jnp.exp(cum_end[u]) for u in U]

    g_all = [_dot1(cat0([at[u], rt[u]]), cat0([bd(bt[u]), bd(kt[u])]), NT) for u in U]
    a_ab = [jnp.where(strict, g_all[u][0:L, 0:W], 0.0) for u in U]
    a_ak = [jnp.where(strict, g_all[u][0:L, W:], 0.0) for u in U]
    a_rb = [jnp.where(incl, g_all[u][L:, 0:W], 0.0) for u in U]
    a_rk = [jnp.where(incl, g_all[u][L:, W:], 0.0) for u in U]

    tinv = [eye for u in U]
    pw = a_ab
    for it in range(6):
        if it < 5:
            res = [_dot1(pw[u], cat1([bd(tinv[u]), bd(pw[u])])) for u in U]
            tinv = [tinv[u] + res[u][:, 0:W] for u in U]
            pw = [res[u][:, W:] for u in U]
        else:
            tinv = [tinv[u] + _dot1(pw[u], bd(tinv[u])) for u in U]

    bdv = [bd(v[u]) for u in U]
    av = [_dot1(a_ak[u], bdv[u]) for u in U]
    qp = [_dot1(tinv[u], cat1([bd(at[u]), bd(av[u])])) for u in U]
    q1 = [qp[u][:, 0:W] for u in U]
    p1 = [qp[u][:, W:] for u in U]
    qp2 = [_dot1(cat1([a_rb[u], a_rk[u]]),
                  cat0([cat1([bd(q1[u]), bd(p1[u])]), cat1([jnp.zeros_like(bdv[u]), bdv[u]])])) for u in U]
    q2 = [rt[u] + qp2[u][:, 0:W] for u in U]
    p2 = [qp2[u][:, W:] for u in U]
    mp = [_dot1(cat0([bh[u], kh[u]]).T,
                 cat0([cat1([q1[u], p1[u]]), cat1([jnp.zeros_like(v[u]), v[u]])])) for u in U]
    m_bd = [jnp.where(same_head, mp[u][:, 0:W], 0.0) + jnp.where(diag, w_end[u], 0.0) for u in U]
    st = [st_ref[u] for u in U]
    st_hl = [_split2(st[u]) for u in U]
    m_hl = [_split2(m_bd[u]) for u in U]
    yq = [_dot(cat0([q2[u].astype(BF16), m_hl[u][0], m_hl[u][1]]), st_hl[u][0]) for u in U]
    y = [yq[u][0:L] + p2[u] for u in U]
    for u in U:
        carry = yq[u][L:L + W] + (yq[u][L + W:] + _dot(m_hl[u][0], st_hl[u][1]))
        st_ref[u] = carry + jnp.where(same_head, mp[u][:, W:], 0.0)

    sums = [_sum3_exact_rhs([y[u], r[u] * k[u] * rk_ref[u]], ones) for u in U]
    yc = [y[u] - sums[u][0] * (1.0 / HEAD_DIM) for u in U]
    var = [_sum3_exact_rhs([yc[u] * yc[u]], ones)[0] for u in U]
    for u in U:
        yn = yc[u] * lax.rsqrt(var[u] * (1.0 / HEAD_DIM) + RWKV_GN_EPS) * lnw_ref[u] + lnb_ref[u]
        o_ref[u] = ((yn + sums[u][1] * v[u]) * g_ref[u]).astype(o_ref.dtype)


def rwkv_scan(r, lw, k, v, kk, b, g, ln_w, ln_b, r_k, w1, w2, *, batch, seq):
    nchunk = seq // CHUNK
    blk = pl.BlockSpec((RWKV_SLABS, CHUNK, SLAB), lambda bi, ci: (0, bi * nchunk + ci, 0))
    par = pl.BlockSpec((RWKV_SLABS, 1, SLAB), lambda bi, ci: (0, 0, 0))
    t = batch * seq
    out_shape = jax.ShapeDtypeStruct((RWKV_SLABS, t, SLAB), BF16)
    state = pltpu.VMEM((RWKV_SLABS, SLAB, SLAB), F32)
    args = (r, lw, k, v, kk, b, g, ln_w.reshape(RWKV_SLABS, 1, SLAB), ln_b.reshape(RWKV_SLABS, 1, SLAB),
            r_k.reshape(RWKV_SLABS, 1, SLAB))
    n_chunks, gph, w1v, w2v = _expert_weight_views(w1, w2)
    if n_chunks != batch * nchunk:
        ra = pl.pallas_call(
            _rwkv_scan_kernel, grid=(batch, nchunk), in_specs=[blk] * 7 + [par] * 3, out_specs=blk,
            out_shape=out_shape, scratch_shapes=[state],
            compiler_params=_cparams(("arbitrary", "arbitrary")), name="rwkv_scan",
        )(*args)
        return (ra,) + expert_weight_prep(w1, w2)
    w_in, w_out, w_shape, w_scratch = _expert_weight_specs(n_chunks, gph, w1v, w2v, lambda bi, ci: bi * nchunk + ci)
    ra, w1b, w2b = pl.pallas_call(
        _rwkv_scan_kernel, grid=(batch, nchunk), in_specs=[blk] * 7 + [par] * 3 + w_in,
        out_specs=[blk] + w_out, out_shape=[out_shape] + w_shape, scratch_shapes=[state, w_scratch],
        compiler_params=_cparams(("arbitrary", "arbitrary")), name="rwkv_scan",
    )(*args, w1v, w2v, w2v)
    return ra, w1b.reshape(w1.shape), w2b.reshape(w2.shape)


def _diff_attn_kernel(qi_ref, ki_ref, slope_ref, q_ref, k_ref, v_ref, lq1_ref, lk1_ref, lq2_ref, lk2_ref, sg_ref,
                      o_ref, qs_ref, relb_ref, m_ref, l_ref, acc_ref, *, tq, lam_init):
    h = pl.program_id(1)
    qi = qi_ref[pl.program_id(2)]
    ki = ki_ref[pl.program_id(2)]
    c2 = slope_ref[h] * LOG2E

    @pl.when(ki == 0)
    def _():
        m_ref[...] = jnp.full_like(m_ref, -jnp.inf)
        l_ref[...] = jnp.zeros_like(l_ref)
        acc_ref[...] = jnp.zeros_like(acc_ref)
        q = q_ref[...].astype(F32) * (HEAD_DIM ** -0.5 * LOG2E)
        lane = lax.broadcasted_iota(jnp.int32, q.shape, 1)
        m0 = lane < HEAD_DIM
        qs_ref[0:tq, :] = jnp.where(m0, q, 0.0).astype(BF16)
        qs_ref[tq:, :] = jnp.where(m0, 0.0, q).astype(BF16)

    @pl.when(pl.program_id(2) == 0)
    def _():
        rel = (lax.broadcasted_iota(jnp.int32, (tq, tq), 1) - lax.broadcasted_iota(jnp.int32, (tq, tq), 0))
        relb = rel.astype(F32) * (-c2)
        relb_ref[:, 0:tq] = relb
        relb_ref[:, tq:] = relb

    def step(masked):
        k = k_ref[...]
        vt = v_ref[...].T
        relb = relb_ref[...]
        tile_bias = c2 * ((qi - ki) * tq).astype(F32)
        s = _dot(k, qs_ref[...], NT) + relb
        if masked:
            s = jnp.where(relb > 0.0, -jnp.inf, s)
        m_prev = m_ref[...]
        m_new = jnp.maximum(m_prev, jnp.max(s, axis=0, keepdims=True) - tile_bias)
        alpha = jnp.exp2(m_prev - m_new)
        p = jnp.exp2(s - (m_new + tile_bias))
        l_ref[...] = alpha * l_ref[...] + jnp.sum(p, axis=0, keepdims=True)
        acc_ref[...] = alpha * acc_ref[...] + _dot(vt, p.astype(BF16))
        m_ref[...] = m_new

    @pl.when(ki < qi)
    def _():
        step(False)

    @pl.when(ki == qi)
    def _():
        step(True)
        lam = (jnp.exp(jnp.sum(lq1_ref[...] * lk1_ref[...], axis=-1, keepdims=True))
               - jnp.exp(jnp.sum(lq2_ref[...] * lk2_ref[...], axis=-1, keepdims=True)) + lam_init)
        ot = acc_ref[:, 0:tq] / l_ref[:, 0:tq] - lam * (acc_ref[:, tq:] / l_ref[:, tq:])
        o = _rms(ot.T, sg_ref[...], NORM_EPS) * (1.0 - lam_init)
        o_ref[...] = o.astype(o_ref.dtype)


def diff_attention(qkv, slopes, lq1, lk1, lq2, lk2, subln_g, *, batch, seq, tq, lam_init):
    t = batch * seq
    nq = seq // tq
    hb = DIFF_WIDTH // LANES
    pairs = [(qi, ki) for qi in range(nq) for ki in range(qi + 1)]
    qi_tab = jnp.asarray([pq for pq, _ in pairs], jnp.int32)
    ki_tab = jnp.asarray([pk for _, pk in pairs], jnp.int32)
    small = pl.BlockSpec((1, HEAD_DIM), lambda b, h, j, qt, kt: (0, 0))
    grid_spec = pltpu.PrefetchScalarGridSpec(
        num_scalar_prefetch=2,
        grid=(batch, DIFF_HEADS, len(pairs)),
        in_specs=[
            pl.BlockSpec(memory_space=pltpu.SMEM),
            pl.BlockSpec((tq, LANES), lambda b, h, j, qt, kt: (b * nq + qt[j], h)),
            pl.BlockSpec((tq, LANES), lambda b, h, j, qt, kt: (b * nq + kt[j], hb + h)),
            pl.BlockSpec((tq, LANES), lambda b, h, j, qt, kt: (b * nq + kt[j], 2 * hb + h)),
            small, small, small, small,
            pl.BlockSpec((1, LANES), lambda b, h, j, qt, kt: (0, 0)),
        ],
        out_specs=pl.BlockSpec((tq, LANES), lambda b, h, j, qt, kt: (b * nq + qt[j], h)),
        scratch_shapes=[pltpu.VMEM((2 * tq, LANES), BF16), pltpu.VMEM((tq, 2 * tq), F32),
                        pltpu.VMEM((1, 2 * tq), F32), pltpu.VMEM((1, 2 * tq), F32),
                        pltpu.VMEM((LANES, 2 * tq), F32)],
    )
    return pl.pallas_call(
        functools.partial(_diff_attn_kernel, tq=tq, lam_init=lam_init),
        grid_spec=grid_spec,
        out_shape=jax.ShapeDtypeStruct((t, DIFF_WIDTH), BF16),
        compiler_params=_cparams(("parallel", "parallel", "arbitrary")),
        name="diff_attention",
    )(qi_tab, ki_tab, slopes, qkv, qkv, qkv, lq1.reshape(1, -1), lk1.reshape(1, -1), lq2.reshape(1, -1),
      lk2.reshape(1, -1), subln_g.reshape(1, -1))


def _mix_kernel(ra_ref, da_ref, ga_ref, gb_ref, x_ref, wa_ref, wb_ref, wo_ref, o_ref):
    ya = _dot(ra_ref[0], wa_ref[0:SLAB, :])
    for q in range(1, RWKV_SLABS):
        ya = ya + _dot(ra_ref[q], wa_ref[q * SLAB:(q + 1) * SLAB, :])
    yb = _dot(da_ref[...], wb_ref[...])
    mixed = ga_ref[...].astype(F32) * ya + gb_ref[...].astype(F32) * yb
    o_ref[...] = x_ref[...] + _dot(mixed.astype(BF16), wo_ref[...])


def mix_project(ra, da, gates, x, wa, wb, wo, *, tm):
    t, d = x.shape
    const = lambda a: pl.BlockSpec(a.shape, lambda i: (0, 0), pipeline_mode=pl.Buffered(1))
    return pl.pallas_call(
        _mix_kernel,
        grid=(t // tm,),
        in_specs=[
            pl.BlockSpec((RWKV_SLABS, tm, SLAB), lambda i: (0, i, 0)),
            pl.BlockSpec((tm, DIFF_WIDTH), lambda i: (i, 0)),
            pl.BlockSpec((tm, d), lambda i: (i, 0)),
            pl.BlockSpec((tm, d), lambda i: (i, 1)),
            pl.BlockSpec((tm, d), lambda i: (i, 0)),
            const(wa), const(wb), const(wo),
        ],
        out_specs=pl.BlockSpec((tm, d), lambda i: (i, 0)),
        out_shape=jax.ShapeDtypeStruct((t, d), F32),
        compiler_params=_cparams(("parallel",)),
        name="mix_project",
    )(ra, da, gates, gates, x, wa, wb, wo)


def _cross_kernel(x_ref, gc_ref, wq_ref, kv_ref, wo_ref, gf_ref, rw_ref, rb_ref, x2_ref, hf_ref, lg_ref):
    x = x_ref[...]
    h = _rms(x, gc_ref[...], NORM_EPS).astype(BF16)
    q = _dot(h, wq_ref[...]).astype(BF16)
    scale = LANES ** -0.5
    outs = []
    for hd in range(CROSS_HEADS):
        qh = q[:, hd * LANES:(hd + 1) * LANES]
        kh = kv_ref[0, :, hd * LANES:(hd + 1) * LANES]
        vh = kv_ref[0, :, CROSS_WIDTH + hd * LANES:CROSS_WIDTH + (hd + 1) * LANES]
        s = _dot(qh, kh, NT) * scale
        s = s - jnp.max(s, axis=-1, keepdims=True)
        e = jnp.exp(s)
        p = e / jnp.sum(e, axis=-1, keepdims=True)
        outs.append(_dot(p.astype(BF16), vh))
    o = jnp.concatenate(outs, axis=1).astype(BF16)
    x2 = x + _dot(o, wo_ref[...])
    x2_ref[...] = x2
    hf = _rms(x2, gf_ref[...], NORM_EPS)
    tm = hf.shape[0]
    for c in range(PACK_ROWS):
        hf_ref[pl.ds(c, tm, stride=PACK_ROWS), :] = _pack_pair(hf[:, 2 * c * LANES:(2 * c + 1) * LANES],
                                                               hf[:, (2 * c + 1) * LANES:(2 * c + 2) * LANES])
    lg_ref[...] = _dot3(hf, rw_ref[...]) + rb_ref[...]


def cross_attention(x, gc, wq, kv, wo, gf, rw, rb, *, seq, tm):
    t, d = x.shape
    per_seq = seq // tm
    const = lambda a: pl.BlockSpec(a.shape, lambda i: (0,) * a.ndim, pipeline_mode=pl.Buffered(1))
    vec = lambda n: pl.BlockSpec((1, n), lambda i: (0, 0))
    return pl.pallas_call(
        _cross_kernel,
        grid=(t // tm,),
        in_specs=[
            pl.BlockSpec((tm, d), lambda i: (i, 0)),
            vec(d), const(wq),
            pl.BlockSpec((1,) + kv.shape[1:], lambda i: (i // per_seq, 0, 0)),
            const(wo), vec(d), const(rw), vec(LANES),
        ],
        out_specs=[pl.BlockSpec((tm, d), lambda i: (i, 0)), pl.BlockSpec((tm * PACK_ROWS, LANES), lambda i: (i, 0)),
                   pl.BlockSpec((tm, LANES), lambda i: (i, 0))],
        out_shape=[jax.ShapeDtypeStruct((t, d), F32), jax.ShapeDtypeStruct((t * PACK_ROWS, LANES), U32),
                   jax.ShapeDtypeStruct((t, LANES), F32)],
        compiler_params=_cparams(("parallel",)),
        name="cross_attention",
    )(x, gc.reshape(1, d), wq, kv, wo, gf.reshape(1, d), rw, rb)


def _router_kernel(lg_ref, idx_ref, gate_ref):
    x = lg_ref[...]
    lane = lax.broadcasted_iota(jnp.int32, x.shape, 1)
    x = jnp.where(lane < N_EXPERTS, x, -jnp.inf)
    idx_out = jnp.zeros(x.shape, jnp.int32)
    val_out = jnp.zeros(x.shape, F32)
    vals = []
    for j in range(TOP_K):
        m = jnp.max(x, axis=-1, keepdims=True)
        sel = jnp.min(jnp.where(x == m, lane, LANES), axis=-1, keepdims=True)
        idx_out = jnp.where(lane == j, sel, idx_out)
        vals.append(m)
        x = jnp.where(lane == sel, -jnp.inf, x)
    es = [jnp.exp(vj - vals[0]) for vj in vals]
    tot = es[0] + es[1] + es[2] + es[3]
    for j in range(TOP_K):
        val_out = jnp.where(lane == j, es[j] / tot, val_out)
    idx_ref[...] = idx_out[:, 0:TOP_K]
    gate_ref[...] = val_out[:, 0:TOP_K]


def router_topk(logits, *, tm):
    t = logits.shape[0]
    return pl.pallas_call(
        _router_kernel,
        grid=(t // tm,),
        in_specs=[pl.BlockSpec((tm, LANES), lambda i: (i, 0))],
        out_specs=[pl.BlockSpec((tm, TOP_K), lambda i: (i, 0)), pl.BlockSpec((tm, TOP_K), lambda i: (i, 0))],
        out_shape=[jax.ShapeDtypeStruct((t, TOP_K), jnp.int32), jax.ShapeDtypeStruct((t, TOP_K), F32)],
        compiler_params=_cparams(("parallel",)),
        name="router_topk",
    )(logits)


def _moe_kernel(sbe_ref, sbr_ref, src_ref, tok_ref, ntok_ref, hf_hbm, w1_ref, b1_ref, w2_ref, b2_ref,
                ys_hbm, xbuf, x2d, acc, obuf, gsem, ssem):
    s = pl.program_id(0)
    f = pl.program_id(1)
    nf = pl.num_programs(1)
    nsb = pl.num_programs(0)
    sub_rows = MOE_SUB * PACK_ROWS
    nsubs = MOE_SUPER // MOE_SUB
    unroll = 8

    def ceil_sub(n):
        return lax.shift_right_logical(n + (MOE_SUB - 1), MOE_SUB.bit_length() - 1)

    rows = sbr_ref[s]
    nsub = ceil_sub(rows)
    rows_next = jnp.where(s + 1 < nsb, sbr_ref[jnp.minimum(s + 1, nsb - 1)], 0)
    rows_prev = jnp.where(s > 0, sbr_ref[jnp.maximum(s - 1, 0)], 0)
    nsub_next = ceil_sub(rows_next)

    def slab(ref, off):
        return ref.at[pl.ds(pl.multiple_of(off, PACK_ROWS), PACK_ROWS)]

    def row_in(table, r):
        return pltpu.make_async_copy(slab(hf_hbm, table[0, 0, r]), slab(xbuf, r * PACK_ROWS), gsem)

    def row_out(r):
        return pltpu.make_async_copy(slab(obuf, r * PACK_ROWS), slab(ys_hbm, src_ref[0, 0, r] * PACK_ROWS), ssem)

    def start_gather(table, first, n):
        def group(gi, c):
            for j in range(unroll):
                row_in(table, first + gi * unroll + j).start()
            return c

        lax.fori_loop(0, n // unroll, group, 0)

    def wait_sub_blocks(buf, other, sem, n):
        for sub in range(nsubs):
            @pl.when(sub < n)
            def _(sub=sub):
                pltpu.make_async_copy(other.at[pl.ds(0, sub_rows)], buf.at[pl.ds(sub * sub_rows, sub_rows)], sem).wait()

    def wait_scatter(n):
        full = n // MOE_SUB
        for sub in range(nsubs):
            @pl.when(sub < full)
            def _(sub=sub):
                pltpu.make_async_copy(obuf.at[pl.ds(sub * sub_rows, sub_rows)], ys_hbm.at[pl.ds(0, sub_rows)], ssem).wait()

        def one(r, c):
            row_out(r).wait()
            return c

        lax.fori_loop(0, n - full * MOE_SUB, one, 0)

    @pl.when((f == 0) & (s == 0) & (rows > 0))
    def _():
        start_gather(tok_ref, 0, nsub * MOE_SUB)

    @pl.when((f == 0) & (rows > 0))
    def _():
        wait_sub_blocks(xbuf, hf_hbm, gsem, nsub)

    @pl.when((f == nf - 1) & (rows > 0) & (rows_prev > 0))
    def _():
        wait_scatter(rows_prev)

    for sub in range(nsubs):
        @pl.when(sub < nsub)
        def _(sub=sub):
            sl = slice(sub * MOE_SUB, (sub + 1) * MOE_SUB)
            base = sub * sub_rows

            @pl.when(f == 0)
            def _():
                for c in range(PACK_ROWS):
                    lo, hi = _unpack_pair(xbuf[pl.ds(base + c, MOE_SUB, stride=PACK_ROWS), :])
                    x2d[sl, 2 * c * LANES:(2 * c + 1) * LANES] = lo.astype(BF16)
                    x2d[sl, (2 * c + 1) * LANES:(2 * c + 2) * LANES] = hi.astype(BF16)

            x = x2d[sl, :]
            hb = _dot(x, w1_ref[0]) + b1_ref[0]
            even = (lax.broadcasted_iota(jnp.int32, (MOE_SUB, LANES), 1) % 2) == 0
            acts = []
            for cb in range(MOE_FT // LANES):
                a = hb[:, cb * LANES:(cb + 1) * LANES]
                b = hb[:, MOE_FT + cb * LANES:MOE_FT + (cb + 1) * LANES]
                hg = jnp.where(even, a, pltpu.roll(b, 1, axis=1))
                hl = jnp.where(even, pltpu.roll(a, LANES - 1, axis=1), b)
                xg = jnp.minimum(hg, SWIGLU_LIMIT)
                xl = jnp.clip(hl, -SWIGLU_LIMIT, SWIGLU_LIMIT)
                acts.append((xg * jax.nn.sigmoid(SWIGLU_ALPHA * xg) * (xl + 1.0)).astype(BF16))
            contrib = _dot(jnp.concatenate(acts, axis=1), w2_ref[0])

            @pl.when(f == 0)
            def _():
                acc[sl, :] = contrib

            @pl.when(f > 0)
            def _():
                acc[sl, :] += contrib

            @pl.when(f == nf - 1)
            def _():
                y = acc[sl, :] + b2_ref[0]
                for c in range(PACK_ROWS):
                    obuf[pl.ds(base + c, MOE_SUB, stride=PACK_ROWS), :] = _pack_pair(
                        y[:, 2 * c * LANES:(2 * c + 1) * LANES], y[:, (2 * c + 1) * LANES:(2 * c + 2) * LANES])

    @pl.when((f == 0) & (rows_next > 0))
    def _():
        start_gather(ntok_ref, 0, nsub_next * MOE_SUB)

    @pl.when((f == nf - 1) & (rows > 0))
    def _():
        def group(gi, c):
            for j in range(unroll):
                row_out(gi * unroll + j).start()
            return c

        def one(r, c):
            row_out(r).start()
            return c

        ngroups = rows // unroll
        lax.fori_loop(0, ngroups, group, 0)
        lax.fori_loop(ngroups * unroll, rows, one, 0)

        @pl.when(rows_next == 0)
        def _():
            wait_scatter(rows)


def moe_experts(sb_e, sb_rows, row_src, row_off, hf, w1, b1, w2, b2):
    d = D_MODEL
    t = hf.shape[0] // PACK_ROWS
    nsb = sb_e.shape[0]
    nf = D_EXPERT // MOE_FT
    grid_spec = pltpu.PrefetchScalarGridSpec(
        num_scalar_prefetch=2,
        grid=(nsb, nf),
        in_specs=[
            pl.BlockSpec((1, 1, MOE_SUPER), lambda s, f, e, r: (s, 0, 0), memory_space=pltpu.SMEM),
            pl.BlockSpec((1, 1, MOE_SUPER), lambda s, f, e, r: (s, 0, 0), memory_space=pltpu.SMEM),
            pl.BlockSpec((1, 1, MOE_SUPER), lambda s, f, e, r: (jnp.minimum(s + 1, nsb - 1), 0, 0),
                         memory_space=pltpu.SMEM),
            pl.BlockSpec(memory_space=pl.ANY),
            pl.BlockSpec((1, d, 2 * MOE_FT), lambda s, f, e, r: (e[s], 0, f)),
            pl.BlockSpec((1, 1, 2 * MOE_FT), lambda s, f, e, r: (e[s], 0, f)),
            pl.BlockSpec((1, MOE_FT, d), lambda s, f, e, r: (e[s], f, 0)),
            pl.BlockSpec((1, 1, d), lambda s, f, e, r: (e[s], 0, 0)),
        ],
        out_specs=pl.BlockSpec(memory_space=pl.ANY),
        scratch_shapes=[pltpu.VMEM((MOE_SUPER * PACK_ROWS, LANES), U32), pltpu.VMEM((MOE_SUPER, d), BF16),
                        pltpu.VMEM((MOE_SUPER, d), F32), pltpu.VMEM((MOE_SUPER * PACK_ROWS, LANES), U32),
                        pltpu.SemaphoreType.DMA, pltpu.SemaphoreType.DMA],
    )
    return pl.pallas_call(
        _moe_kernel,
        grid_spec=grid_spec,
        out_shape=jax.ShapeDtypeStruct((t * TOP_K * PACK_ROWS, LANES), U32),
        compiler_params=_cparams(("arbitrary", "arbitrary")),
        name="moe_experts",
    )(sb_e, sb_rows, row_src, row_off, row_off, hf, w1, b1, w2, b2)


def _combine_kernel(ys_ref, gate_ref, x_ref, g_ref, o_ref, lo_ref, hi_ref):
    tm = x_ref.shape[0]
    tot_lo = tot_hi = None
    for j in range(TOP_K):
        lo, hi = _unpack_pair(ys_ref[:, j * PACK_ROWS:(j + 1) * PACK_ROWS, :])
        gj = gate_ref[:, j:j + 1, :]
        tot_lo = lo * gj if tot_lo is None else tot_lo + lo * gj
        tot_hi = hi * gj if tot_hi is None else tot_hi + hi * gj
    lo_ref[...] = tot_lo.reshape(tm * PACK_ROWS, LANES)
    hi_ref[...] = tot_hi.reshape(tm * PACK_ROWS, LANES)
    pieces = []
    for c in range(PACK_ROWS):
        pieces.append(lo_ref[pl.ds(c, tm, stride=PACK_ROWS), :])
        pieces.append(hi_ref[pl.ds(c, tm, stride=PACK_ROWS), :])
    out = x_ref[...] + jnp.concatenate(pieces, axis=1)
    o_ref[...] = _rms(out, g_ref[...], NORM_EPS)


def combine_final(ys, gate, x, g, *, tm):
    t, d = x.shape
    return pl.pallas_call(
        _combine_kernel,
        grid=(t // tm,),
        in_specs=[pl.BlockSpec((tm, TOP_K * PACK_ROWS, LANES), lambda i: (i, 0, 0)),
                  pl.BlockSpec((tm, TOP_K, LANES), lambda i: (i, 0, 0)),
                  pl.BlockSpec((tm, d), lambda i: (i, 0)), pl.BlockSpec((1, d), lambda i: (0, 0))],
        out_specs=pl.BlockSpec((tm, d), lambda i: (i, 0)),
        out_shape=jax.ShapeDtypeStruct((t, d), F32),
        scratch_shapes=[pltpu.VMEM((tm * PACK_ROWS, LANES), F32), pltpu.VMEM((tm * PACK_ROWS, LANES), F32)],
        compiler_params=_cparams(("parallel",)),
        name="combine_final",
    )(ys, gate, x, g.reshape(1, d))


def _routing_tables(top_idx, n_super):
    flat_e = top_idx.reshape(-1)
    n = flat_e.shape[0]
    onehot = (flat_e[:, None] == jnp.arange(N_EXPERTS, dtype=jnp.int32)[None, :]).astype(jnp.int32)
    csum = jnp.cumsum(onehot, axis=0)
    rank = jnp.sum(onehot * csum, axis=1) - 1
    counts = csum[-1]
    nsb = (counts + MOE_SUPER - 1) // MOE_SUPER
    sb_end = jnp.cumsum(nsb)
    sb_start = sb_end - nsb
    dest = sb_start[flat_e] * MOE_SUPER + rank
    row_src = jnp.full((n_super * MOE_SUPER,), -1, jnp.int32).at[dest].set(jnp.arange(n, dtype=jnp.int32))
    s_ids = jnp.arange(n_super, dtype=jnp.int32)
    sb_e = jnp.minimum(jnp.searchsorted(sb_end, s_ids, side="right"), N_EXPERTS - 1).astype(jnp.int32)
    local = s_ids - sb_start[sb_e]
    sb_rows = jnp.clip(counts[sb_e] - local * MOE_SUPER, 0, MOE_SUPER)
    sb_rows = jnp.where(s_ids < sb_end[-1], sb_rows, 0).astype(jnp.int32)
    row_off = jnp.where(row_src >= 0, (row_src // TOP_K) * PACK_ROWS, 0)
    return sb_e, sb_rows, row_src.reshape(n_super, 1, MOE_SUPER), row_off.reshape(n_super, 1, MOE_SUPER)


def _pad_rows(a, n):
    return jnp.pad(a, ((0, n - a.shape[0]), (0, 0)))


def _layer(x, mem, l, p, batch, seq):
    t = batch * seq
    c = RWKV_WIDTH
    lam_init = 0.8 - 0.6 * math.exp(-0.3 * l)
    w_in = p["w_in"]
    o1 = 3 * c + DECAY_LORA + AAA_LORA + GATE_LORA
    o2 = o1 + 3 * DIFF_WIDTH
    padc = lambda a, n: jnp.pad(a, ((0, 0), (0, n - a.shape[1])))
    w_rwkv = jnp.concatenate([
        w_in[:, :3 * c],
        padc(w_in[:, 3 * c:3 * c + DECAY_LORA], 128),
        padc(w_in[:, 3 * c + DECAY_LORA:3 * c + DECAY_LORA + AAA_LORA], 128),
        padc(w_in[:, 3 * c + DECAY_LORA + AAA_LORA:o1], 256)], axis=1).astype(BF16)
    mu = p["rwkv_mu"]
    pad1 = lambda a, n: jnp.pad(a, (0, n - a.shape[0]))
    mu_p = jnp.concatenate([mu[:3 * c], pad1(mu[3 * c:3 * c + DECAY_LORA], 128),
                            pad1(mu[3 * c + DECAY_LORA:3 * c + DECAY_LORA + AAA_LORA], 128),
                            pad1(mu[3 * c + DECAY_LORA + AAA_LORA:], 256)])
    w_diff = w_in[:, o1:o2].astype(BF16)
    w_gate = w_in[:, o2:].astype(BF16)

    g_mix = p["norm_mix_g"]
    p_rwkv = norm_matmul(x, g_mix, w_rwkv, tm=1024, tn=RWKV_COLS // 2, out_dtype=F32, name="in_proj_rwkv")
    qkv = norm_matmul(x, g_mix, w_diff, tm=1024, tn=1024, out_dtype=BF16, name="in_proj_diff")
    gates = norm_matmul(x, g_mix, w_gate, tm=1024, tn=1024, out_dtype=BF16, act="sigmoid", name="in_proj_gate")

    prep = rwkv_prep(p_rwkv, mu_p, p["rwkv_w0"], p["rwkv_a0"], p["rwkv_k_k"], p["rwkv_k_a"],
                     _pad_rows(p["rwkv_w2"], 128), _pad_rows(p["rwkv_a2"], 128), _pad_rows(p["rwkv_g2"], 256),
                     seq=seq, tm=256)
    ra, w1b, w2b = rwkv_scan(*prep, p["rwkv_ln_w"], p["rwkv_ln_b"], p["rwkv_r_k"].reshape(-1),
                             p["expert_w1"], p["expert_w2"], batch=batch, seq=seq)

    slopes = (2.0 ** (-8.0 * jnp.arange(1, DIFF_HEADS + 1, dtype=F32) / DIFF_HEADS)).astype(F32)
    da = diff_attention(qkv, slopes, p["diff_lq1"], p["diff_lk1"], p["diff_lq2"], p["diff_lk2"],
                        p["diff_subln_g"], batch=batch, seq=seq, tq=min(1024, seq), lam_init=lam_init)

    x1 = mix_project(ra, da, gates, x, p["rwkv_proj"].astype(BF16), p["diff_proj"].astype(BF16),
                     p["w_out"].astype(BF16), tm=256)

    m_len = mem.shape[0] // batch
    kv = norm_matmul(mem, p["norm_mem_g"], p["cross_wkv"].astype(BF16), tm=min(512, mem.shape[0]),
                     tn=2 * CROSS_WIDTH, out_dtype=BF16, name="cross_kv")
    rw = jnp.pad(p["router_w"], ((0, 0), (0, LANES - N_EXPERTS)))
    rb = jnp.pad(p["router_b"], (0, LANES - N_EXPERTS)).reshape(1, LANES)
    x2, hf, logits = cross_attention(x1, p["norm_cross_g"], p["cross_wq"].astype(BF16),
                                     kv.reshape(batch, m_len, 2 * CROSS_WIDTH), p["cross_wo"].astype(BF16),
                                     p["norm_ffn_g"], rw, rb, seq=seq, tm=256)

    top_idx, gate = router_topk(logits, tm=min(1024, t))
    n_super = (t * TOP_K) // MOE_SUPER + N_EXPERTS
    sb_e, sb_rows, row_src, row_off = _routing_tables(top_idx, n_super)
    ys = moe_experts(sb_e, sb_rows, row_src, row_off, hf, w1b, p["expert_b1"][:, None, :], w2b, p["expert_b2"][:, None, :])
    gate_b = jnp.broadcast_to(gate[:, :, None], (t, TOP_K, LANES))
    return x2, ys.reshape(t, TOP_K * PACK_ROWS, LANES), gate_b


def kernel(x, mem, norm_mix_g, w_in, rwkv_mu, rwkv_w0, rwkv_w2, rwkv_a0, rwkv_a2, rwkv_g2, rwkv_k_k, rwkv_k_a, rwkv_r_k, rwkv_ln_w, rwkv_ln_b, rwkv_proj, diff_lq1, diff_lk1, diff_lq2, diff_lk2, diff_subln_g, diff_proj, w_out, norm_cross_g, norm_mem_g, cross_wq, cross_wkv, cross_wo, norm_ffn_g, router_w, router_b, expert_w1, expert_b1, expert_w2, expert_b2, final_norm_g):
    batch, seq, d = x.shape
    stacked = dict(norm_mix_g=norm_mix_g, w_in=w_in, rwkv_mu=rwkv_mu, rwkv_w0=rwkv_w0, rwkv_w2=rwkv_w2,
                   rwkv_a0=rwkv_a0, rwkv_a2=rwkv_a2, rwkv_g2=rwkv_g2, rwkv_k_k=rwkv_k_k, rwkv_k_a=rwkv_k_a,
                   rwkv_r_k=rwkv_r_k, rwkv_ln_w=rwkv_ln_w, rwkv_ln_b=rwkv_ln_b, rwkv_proj=rwkv_proj,
                   diff_lq1=diff_lq1, diff_lk1=diff_lk1, diff_lq2=diff_lq2, diff_lk2=diff_lk2,
                   diff_subln_g=diff_subln_g, diff_proj=diff_proj, w_out=w_out, norm_cross_g=norm_cross_g,
                   norm_mem_g=norm_mem_g, cross_wq=cross_wq, cross_wkv=cross_wkv, cross_wo=cross_wo,
                   norm_ffn_g=norm_ffn_g, router_w=router_w, router_b=router_b, expert_w1=expert_w1,
                   expert_b1=expert_b1, expert_w2=expert_w2, expert_b2=expert_b2)
    assert w_in.shape[0] == 1, "the closing RMSNorm is fused into the single layer's combine"
    p = {k: v[0] for k, v in stacked.items()}
    x2, ys, gate_b = _layer(x.reshape(batch * seq, d), mem.reshape(-1, d), 0, p, batch, seq)
    out = combine_final(ys, gate_b, x2, final_norm_g, tm=256)
    return out.reshape(batch, seq, d)
```

```python
import functools
import math

import jax
import jax.numpy as jnp
from jax import lax
from jax.experimental import pallas as pl
from jax.experimental.pallas import tpu as pltpu

F32 = jnp.float32
BF16 = jnp.bfloat16
U32 = jnp.uint32

D_MODEL = 2048
NORM_EPS = 1e-5
LOG2E = 1.4426950408889634
LANES = 128
HEAD_DIM = 64
RWKV_WIDTH = 1024
SLAB = 256
RWKV_SLABS = RWKV_WIDTH // SLAB
HEADS_PER_SLAB = SLAB // HEAD_DIM
CHUNK = 64
RWKV_GN_EPS = 64e-5
DECAY_LORA = 64
AAA_LORA = 64
GATE_LORA = 160
LORA_PAD = 512
RWKV_COLS = 3 * RWKV_WIDTH + LORA_PAD
DIFF_WIDTH = 1024
DIFF_HEADS = 8
CROSS_HEADS = 4
CROSS_WIDTH = 512
N_EXPERTS = 32
TOP_K = 4
D_EXPERT = 2048
SWIGLU_ALPHA = 1.702
SWIGLU_LIMIT = 7.0
MOE_SUPER = 1024
MOE_SUB = 256
MOE_FT = 1024
W2_GROUP = 128
ROW_TILE = D_MODEL // LANES
PACK_ROWS = ROW_TILE // 2
VMEM_LIMIT = 56 * 1024 * 1024
MOE_VMEM_LIMIT = 62 * 1024 * 1024

NN = (((1,), (0,)), ((), ()))
NT = (((1,), (1,)), ((), ()))


def _dot(a, b, dims=NN):
    return lax.dot_general(a, b, dims, preferred_element_type=F32)


def _split2(a):
    hi = a.astype(BF16)
    lo = (a - hi.astype(F32)).astype(BF16)
    return hi, lo


def _split3(a):
    hi = a.astype(BF16)
    r1 = a - hi.astype(F32)
    mid = r1.astype(BF16)
    lo = (r1 - mid.astype(F32)).astype(BF16)
    return hi, mid, lo


def _dot3(a, b, dims=NN):
    ah, al = _split2(a)
    bh, bl = _split2(b)
    return _dot(ah, bh, dims) + (_dot(ah, bl, dims) + _dot(al, bh, dims))


def _dot_exact_rhs(a, b_bf16, dims=NN):
    h, m, l = _split3(a)
    return _dot(h, b_bf16, dims) + (_dot(m, b_bf16, dims) + _dot(l, b_bf16, dims))


def _pack_pair(lo, hi):
    lo_b = lax.shift_right_logical(lax.bitcast_convert_type(lo.astype(BF16).astype(F32), U32), jnp.uint32(16))
    hi_b = lax.bitcast_convert_type(hi.astype(BF16).astype(F32), U32) & jnp.uint32(0xFFFF0000)
    return hi_b | lo_b


def _unpack_pair(w):
    lo = lax.bitcast_convert_type(lax.shift_left(w, jnp.uint32(16)), F32)
    hi = lax.bitcast_convert_type(w & jnp.uint32(0xFFFF0000), F32)
    return lo, hi


def _rms(x, g, eps):
    ms = jnp.mean(x * x, axis=-1, keepdims=True)
    return x * lax.rsqrt(ms + eps) * g


def _cparams(sem):
    return pltpu.CompilerParams(dimension_semantics=sem, vmem_limit_bytes=VMEM_LIMIT)


def _norm_matmul_kernel(x_ref, g_ref, w_ref, o_ref, h_ref, *, act):
    @pl.when(pl.program_id(1) == 0)
    def _():
        h_ref[...] = _rms(x_ref[...], g_ref[...], NORM_EPS).astype(BF16)

    y = _dot(h_ref[...], w_ref[...])
    if act == "sigmoid":
        y = jax.nn.sigmoid(y)
    o_ref[...] = y.astype(o_ref.dtype)


def norm_matmul(x, g, w, *, tm, tn, out_dtype, act=None, name):
    m, d = x.shape
    n = w.shape[1]
    return pl.pallas_call(
        functools.partial(_norm_matmul_kernel, act=act),
        grid=(m // tm, n // tn),
        in_specs=[
            pl.BlockSpec((tm, d), lambda i, j: (i, 0)),
            pl.BlockSpec((1, d), lambda i, j: (0, 0)),
            pl.BlockSpec((d, tn), lambda i, j: (0, j)),
        ],
        out_specs=pl.BlockSpec((tm, tn), lambda i, j: (i, j)),
        out_shape=jax.ShapeDtypeStruct((m, n), out_dtype),
        scratch_shapes=[pltpu.VMEM((tm, d), BF16)],
        compiler_params=_cparams(("parallel", "arbitrary")),
        name=name,
    )(x, g.reshape(1, d), w)


def _head_ones():
    r = lax.broadcasted_iota(jnp.int32, (SLAB, SLAB), 0)
    c = lax.broadcasted_iota(jnp.int32, (SLAB, SLAB), 1)
    return ((r // HEAD_DIM) == (c // HEAD_DIM)).astype(BF16)


def _rwkv_prep_kernel(p_ref, prev_ref, mu_ref, w0_ref, a0_ref, kk_ref, ka_ref, w2_ref, a2_ref, g2_ref,
                      r_out, lw_out, k_out, v_out, kkn_out, b_out, g_out, *, tiles_per_seq):
    c = RWKV_WIDTH
    i = pl.program_id(0)
    p = p_ref[...]
    tm = p.shape[0]
    first = (i % tiles_per_seq) == 0
    prev_row = jnp.where(first, 0.0, prev_ref[7:8, :])
    row = lax.broadcasted_iota(jnp.int32, p.shape, 0)
    shifted = jnp.where(row == 0, prev_row, pltpu.roll(p, 1, axis=0))
    ps = p + (shifted - p) * mu_ref[...]
    r = ps[:, 0:c]
    k = ps[:, c:2 * c]
    v = ps[:, 2 * c:3 * c]
    wd = ps[:, 3 * c:3 * c + 128]
    ad = ps[:, 3 * c + 128:3 * c + 256]
    gd = ps[:, 3 * c + 256:3 * c + 512]
    z = -(w0_ref[...] + _dot3(jnp.tanh(wd), w2_ref[...]))
    softplus = jnp.maximum(z, 0.0) + jnp.log1p(jnp.exp(-jnp.abs(z)))
    w = -softplus - 0.5
    lw = -jnp.exp(w)
    a = jax.nn.sigmoid(a0_ref[...] + _dot3(ad, a2_ref[...]))
    g = _dot3(jax.nn.sigmoid(gd), g2_ref[...])
    kkr = k * kk_ref[...]
    k2 = k * (1.0 + (a - 1.0) * ka_ref[...])
    ones = _head_ones()
    for q in range(RWKV_SLABS):
        sl = slice(q * SLAB, (q + 1) * SLAB)
        x = kkr[:, sl]
        ss = _dot_exact_rhs(x * x, ones)
        kkn = x / jnp.maximum(jnp.sqrt(ss), 1e-12)
        r_out[q] = r[:, sl]
        lw_out[q] = lw[:, sl]
        k_out[q] = k2[:, sl]
        v_out[q] = v[:, sl]
        kkn_out[q] = kkn
        b_out[q] = kkn * a[:, sl]
        g_out[q] = g[:, sl]


def rwkv_prep(p, mu, w0, a0, k_k, k_a, w2p, a2p, g2p, *, seq, tm):
    t, cols = p.shape
    c = RWKV_WIDTH
    vec = lambda n: pl.BlockSpec((1, n), lambda i: (0, 0))
    full = lambda a: pl.BlockSpec(a.shape, lambda i: (0, 0))
    out_spec = pl.BlockSpec((RWKV_SLABS, tm, SLAB), lambda i: (0, i, 0))
    out_shape = jax.ShapeDtypeStruct((RWKV_SLABS, t, SLAB), F32)
    return pl.pallas_call(
        functools.partial(_rwkv_prep_kernel, tiles_per_seq=seq // tm),
        grid=(t // tm,),
        in_specs=[
            pl.BlockSpec((tm, cols), lambda i: (i, 0)),
            pl.BlockSpec((8, cols), lambda i: (jnp.maximum(i * (tm // 8) - 1, 0), 0)),
            vec(cols), vec(c), vec(c), vec(c), vec(c), full(w2p), full(a2p), full(g2p),
        ],
        out_specs=[out_spec] * 7,
        out_shape=[out_shape] * 7,
        compiler_params=_cparams(("parallel",)),
        name="rwkv_prep",
    )(p, p, mu.reshape(1, cols), w0.reshape(1, c), a0.reshape(1, c), k_k.reshape(1, c), k_a.reshape(1, c),
      w2p, a2p, g2p)


def _dot1(a, b, dims=NN):
    return _dot(a.astype(BF16), b.astype(BF16), dims)


def _dot3s(a, b, dims=NN):
    ah, al = _split2(a)
    bh, bl = _split2(b)
    m = a.shape[0]
    lhs = jnp.concatenate([ah, al], axis=0)
    if dims is NN:
        n = b.shape[1]
        rhs = jnp.concatenate([bh, bl], axis=1)
    else:
        n = b.shape[0]
        rhs = jnp.concatenate([bh, bl], axis=0)
    p = _dot(lhs, rhs, dims)
    return p[:m, :n] + (p[:m, n:] + p[m:, :n])


def _sum3_exact_rhs(xs, ones):
    parts = []
    for x in xs:
        parts.extend(_split3(x))
    res = _dot(jnp.concatenate(parts, axis=0), ones)
    L = xs[0].shape[0]
    return [res[(3 * i) * L:(3 * i + 1) * L] + (res[(3 * i + 1) * L:(3 * i + 2) * L] + res[(3 * i + 2) * L:(3 * i + 3) * L])
            for i in range(len(xs))]


def _expert_weight_chunk(w1_ref, w2a_ref, w2b_ref, w1o_ref, w2o_ref, tmp_ref):
    w1o_ref[0] = w1_ref[0].astype(BF16)
    half = W2_GROUP
    for c in range(ROW_TILE):
        cols = slice(c * LANES, (c + 1) * LANES)
        tmp_ref[c, pl.ds(0, half, stride=2), :] = w2a_ref[0, :, cols]
        tmp_ref[c, pl.ds(1, half, stride=2), :] = w2b_ref[0, :, cols]
        w2o_ref[0, :, cols] = tmp_ref[c].astype(BF16)


def _expert_weight_views(w1, w2):
    e, d, f2 = w1.shape
    fdim = w2.shape[1]
    groups_per_half = (MOE_FT // 2) // W2_GROUP
    n_chunks = e * (fdim // MOE_FT) * groups_per_half
    w1v = w1.reshape(n_chunks, (e * d) // n_chunks, f2)
    w2v = w2.reshape(e * fdim // W2_GROUP, W2_GROUP, d)
    return n_chunks, groups_per_half, w1v, w2v


def _expert_weight_specs(n_chunks, groups_per_half, w1v, w2v, step):
    d = w2v.shape[2]
    a_idx = lambda j: (j // groups_per_half) * (2 * groups_per_half) + j % groups_per_half
    in_specs = [pl.BlockSpec((1,) + w1v.shape[1:], lambda *ids: (step(*ids), 0, 0)),
                pl.BlockSpec((1, W2_GROUP, d), lambda *ids: (a_idx(step(*ids)), 0, 0)),
                pl.BlockSpec((1, W2_GROUP, d), lambda *ids: (a_idx(step(*ids)) + groups_per_half, 0, 0))]
    out_specs = [pl.BlockSpec((1,) + w1v.shape[1:], lambda *ids: (step(*ids), 0, 0)),
                 pl.BlockSpec((1, 2 * W2_GROUP, d), lambda *ids: (step(*ids), 0, 0))]
    out_shape = [jax.ShapeDtypeStruct(w1v.shape, BF16), jax.ShapeDtypeStruct((n_chunks, 2 * W2_GROUP, d), BF16)]
    scratch = pltpu.VMEM((ROW_TILE, 2 * W2_GROUP, LANES), F32)
    return in_specs, out_specs, out_shape, scratch


def expert_weight_prep(w1, w2):
    n_chunks, gph, w1v, w2v = _expert_weight_views(w1, w2)
    in_specs, out_specs, out_shape, scratch = _expert_weight_specs(n_chunks, gph, w1v, w2v, lambda j: j)
    w1b, w2b = pl.pallas_call(
        _expert_weight_chunk, grid=(n_chunks,), in_specs=in_specs, out_specs=out_specs, out_shape=out_shape,
        scratch_shapes=[scratch], compiler_params=_cparams(("parallel",)), name="expert_weight_prep",
    )(w1v, w2v, w2v)
    return w1b.reshape(w1.shape), w2b.reshape(w2.shape)


def _rwkv_scan_kernel(r_ref, lw_ref, k_ref, v_ref, kk_ref, b_ref, g_ref, lnw_ref, lnb_ref, rk_ref, *rest):
    if len(rest) == 2:
        o_ref, st_ref = rest
    else:
        w1_ref, w2a_ref, w2b_ref, o_ref, w1o_ref, w2o_ref, st_ref, tmp_ref = rest
        _expert_weight_chunk(w1_ref, w2a_ref, w2b_ref, w1o_ref, w2o_ref, tmp_ref)
    L = CHUNK
    W = SLAB

    @pl.when(pl.program_id(1) == 0)
    def _():
        st_ref[...] = jnp.zeros_like(st_ref)

    t_i = lax.broadcasted_iota(jnp.int32, (L, W), 0)
    lane = lax.broadcasted_iota(jnp.int32, (L, W), 1)
    j_i = lane % HEAD_DIM
    hid = lane // HEAD_DIM
    strict = j_i < t_i
    incl = j_i <= t_i
    eye = (j_i == t_i).astype(F32)
    r2 = lax.broadcasted_iota(jnp.int32, (W, W), 0)
    c2 = lax.broadcasted_iota(jnp.int32, (W, W), 1)
    same_head = (r2 // HEAD_DIM) == (c2 // HEAD_DIM)
    diag = r2 == c2
    ones = same_head.astype(BF16)
    tr = lax.broadcasted_iota(jnp.int32, (L, L), 0)
    tc = lax.broadcasted_iota(jnp.int32, (L, L), 1)
    tri = (tc <= tr).astype(BF16)

    def bd(x):
        return jnp.concatenate([jnp.where(hid == h, x, 0.0) for h in range(HEADS_PER_SLAB)], axis=0)

    U = range(RWKV_SLABS)
    cat0 = lambda xs: jnp.concatenate(xs, axis=0)
    cat1 = lambda xs: jnp.concatenate(xs, axis=1)
    r = [r_ref[u] for u in U]
    lw = [lw_ref[u] for u in U]
    k = [k_ref[u] for u in U]
    v = [v_ref[u] for u in U]
    c3 = [_dot(tri, cat1(_split3(lw[u]))) for u in U]
    cum = [c3[u][:, 0:W] + (c3[u][:, W:2 * W] + c3[u][:, 2 * W:]) for u in U]
    cum_end = [cum[u][L - 1:L, :] for u in U]
    e_cum = [jnp.exp(cum[u]) for u in U]
    e_neg = [jnp.exp(-cum[u]) for u in U]
    e_end = [jnp.exp(cum_end[u] - cum[u]) for u in U]
    at = [-kk_ref[u] * jnp.exp(cum[u] - lw[u]) for u in U]
    rt = [r[u] * e_cum[u] for u in U]
    bt = [b_ref[u] * e_neg[u] for u in U]
    kt = [k[u] * e_neg[u] for u in U]
    bh = [b_ref[u] * e_end[u] for u in U]
    kh = [k[u] * e_end[u] for u in U]
    w_end = [jnp.exp(cum_end[u]) for u in U]

    g_all = [_dot1(cat0([at[u], rt[u]]), cat0([bd(bt[u]), bd(kt[u])]), NT) for u in U]
    a_ab = [jnp.where(strict, g_all[u][0:L, 0:W], 0.0) for u in U]
    a_ak = [jnp.where(strict, g_all[u][0:L, W:], 0.0) for u in U]
    a_rb = [jnp.where(incl, g_all[u][L:, 0:W], 0.0) for u in U]
    a_rk = [jnp.where(incl, g_all[u][L:, W:], 0.0) for u in U]

    tinv = [eye for u in U]
    pw = a_ab
    for it in range(6):
        if it < 5:
            res = [_dot1(pw[u], cat1([bd(tinv[u]), bd(pw[u])])) for u in U]
            tinv = [tinv[u] + res[u][:, 0:W] for u in U]
            pw = [res[u][:, W:] for u in U]
        else:
            tinv = [tinv[u] + _dot1(pw[u], bd(tinv[u])) for u in U]

    bdv = [bd(v[u]) for u in U]
    av = [_dot1(a_ak[u], bdv[u]) for u in U]
    qp = [_dot1(tinv[u], cat1([bd(at[u]), bd(av[u])])) for u in U]
    q1 = [qp[u][:, 0:W] for u in U]
    p1 = [qp[u][:, W:] for u in U]
    qp2 = [_dot1(cat1([a_rb[u], a_rk[u]]),
                  cat0([cat1([bd(q1[u]), bd(p1[u])]), cat1([jnp.zeros_like(bdv[u]), bdv[u]])])) for u in U]
    q2 = [rt[u] + qp2[u][:, 0:W] for u in U]
    p2 = [qp2[u][:, W:] for u in U]
    mp = [_dot1(cat0([bh[u], kh[u]]).T,
                 cat0([cat1([q1[u], p1[u]]), cat1([jnp.zeros_like(v[u]), v[u]])])) for u in U]
    m_bd = [jnp.where(same_head, mp[u][:, 0:W], 0.0) + jnp.where(diag, w_end[u], 0.0) for u in U]
    st = [st_ref[u] for u in U]
    st_hl = [_split2(st[u]) for u in U]
    m_hl = [_split2(m_bd[u]) for u in U]
    yq = [_dot(cat0([q2[u].astype(BF16), m_hl[u][0], m_hl[u][1]]), st_hl[u][0]) for u in U]
    y = [yq[u][0:L] + p2[u] for u in U]
    for u in U:
        carry = yq[u][L:L + W] + (yq[u][L + W:] + _dot(m_hl[u][0], st_hl[u][1]))
        st_ref[u] = carry + jnp.where(same_head, mp[u][:, W:], 0.0)

    sums = [_sum3_exact_rhs([y[u], r[u] * k[u] * rk_ref[u]], ones) for u in U]
    yc = [y[u] - sums[u][0] * (1.0 / HEAD_DIM) for u in U]
    var = [_sum3_exact_rhs([yc[u] * yc[u]], ones)[0] for u in U]
    for u in U:
        yn = yc[u] * lax.rsqrt(var[u] * (1.0 / HEAD_DIM) + RWKV_GN_EPS) * lnw_ref[u] + lnb_ref[u]
        o_ref[u] = ((yn + sums[u][1] * v[u]) * g_ref[u]).astype(o_ref.dtype)


def rwkv_scan(r, lw, k, v, kk, b, g, ln_w, ln_b, r_k, w1, w2, *, batch, seq):
    nchunk = seq // CHUNK
    blk = pl.BlockSpec((RWKV_SLABS, CHUNK, SLAB), lambda bi, ci: (0, bi * nchunk + ci, 0))
    par = pl.BlockSpec((RWKV_SLABS, 1, SLAB), lambda bi, ci: (0, 0, 0))
    t = batch * seq
    out_shape = jax.ShapeDtypeStruct((RWKV_SLABS, t, SLAB), BF16)
    state = pltpu.VMEM((RWKV_SLABS, SLAB, SLAB), F32)
    args = (r, lw, k, v, kk, b, g, ln_w.reshape(RWKV_SLABS, 1, SLAB), ln_b.reshape(RWKV_SLABS, 1, SLAB),
            r_k.reshape(RWKV_SLABS, 1, SLAB))
    n_chunks, gph, w1v, w2v = _expert_weight_views(w1, w2)
    if n_chunks != batch * nchunk:
        ra = pl.pallas_call(
            _rwkv_scan_kernel, grid=(batch, nchunk), in_specs=[blk] * 7 + [par] * 3, out_specs=blk,
            out_shape=out_shape, scratch_shapes=[state],
            compiler_params=_cparams(("arbitrary", "arbitrary")), name="rwkv_scan",
        )(*args)
        return (ra,) + expert_weight_prep(w1, w2)
    w_in, w_out, w_shape, w_scratch = _expert_weight_specs(n_chunks, gph, w1v, w2v, lambda bi, ci: bi * nchunk + ci)
    ra, w1b, w2b = pl.pallas_call(
        _rwkv_scan_kernel, grid=(batch, nchunk), in_specs=[blk] * 7 + [par] * 3 + w_in,
        out_specs=[blk] + w_out, out_shape=[out_shape] + w_shape, scratch_shapes=[state, w_scratch],
        compiler_params=_cparams(("arbitrary", "arbitrary")), name="rwkv_scan",
    )(*args, w1v, w2v, w2v)
    return ra, w1b.reshape(w1.shape), w2b.reshape(w2.shape)


def _diff_attn_kernel(qi_ref, ki_ref, slope_ref, q_ref, k_ref, v_ref, lq1_ref, lk1_ref, lq2_ref, lk2_ref, sg_ref,
                      o_ref, qs_ref, relb_ref, m_ref, l_ref, acc_ref, *, tq, lam_init):
    h = pl.program_id(1)
    qi = qi_ref[pl.program_id(2)]
    ki = ki_ref[pl.program_id(2)]
    c2 = slope_ref[h] * LOG2E

    @pl.when(ki == 0)
    def _():
        m_ref[...] = jnp.full_like(m_ref, -jnp.inf)
        l_ref[...] = jnp.zeros_like(l_ref)
        acc_ref[...] = jnp.zeros_like(acc_ref)
        q = q_ref[...].astype(F32) * (HEAD_DIM ** -0.5 * LOG2E)
        lane = lax.broadcasted_iota(jnp.int32, q.shape, 1)
        m0 = lane < HEAD_DIM
        qs_ref[0:tq, :] = jnp.where(m0, q, 0.0).astype(BF16)
        qs_ref[tq:, :] = jnp.where(m0, 0.0, q).astype(BF16)

    @pl.when(pl.program_id(2) == 0)
    def _():
        rel = (lax.broadcasted_iota(jnp.int32, (tq, tq), 1) - lax.broadcasted_iota(jnp.int32, (tq, tq), 0))
        relb = rel.astype(F32) * (-c2)
        relb_ref[:, 0:tq] = relb
        relb_ref[:, tq:] = relb

    def step(masked):
        k = k_ref[...]
        vt = v_ref[...].T
        relb = relb_ref[...]
        tile_bias = c2 * ((qi - ki) * tq).astype(F32)
        s = _dot(k, qs_ref[...], NT) + relb
        if masked:
            s = jnp.where(relb > 0.0, -jnp.inf, s)
        m_prev = m_ref[...]
        m_new = jnp.maximum(m_prev, jnp.max(s, axis=0, keepdims=True) - tile_bias)
        alpha = jnp.exp2(m_prev - m_new)
        p = jnp.exp2(s - (m_new + tile_bias))
        l_ref[...] = alpha * l_ref[...] + jnp.sum(p, axis=0, keepdims=True)
        acc_ref[...] = alpha * acc_ref[...] + _dot(vt, p.astype(BF16))
        m_ref[...] = m_new

    @pl.when(ki < qi)
    def _():
        step(False)

    @pl.when(ki == qi)
    def _():
        step(True)
        lam = (jnp.exp(jnp.sum(lq1_ref[...] * lk1_ref[...], axis=-1, keepdims=True))
               - jnp.exp(jnp.sum(lq2_ref[...] * lk2_ref[...], axis=-1, keepdims=True)) + lam_init)
        ot = acc_ref[:, 0:tq] / l_ref[:, 0:tq] - lam * (acc_ref[:, tq:] / l_ref[:, tq:])
        o = _rms(ot.T, sg_ref[...], NORM_EPS) * (1.0 - lam_init)
        o_ref[...] = o.astype(o_ref.dtype)


def diff_attention(qkv, slopes, lq1, lk1, lq2, lk2, subln_g, *, batch, seq, tq, lam_init):
    t = batch * seq
    nq = seq // tq
    hb = DIFF_WIDTH // LANES
    pairs = [(qi, ki) for qi in range(nq) for ki in range(qi + 1)]
    qi_tab = jnp.asarray([pq for pq, _ in pairs], jnp.int32)
    ki_tab = jnp.asarray([pk for _, pk in pairs], jnp.int32)
    small = pl.BlockSpec((1, HEAD_DIM), lambda b, h, j, qt, kt: (0, 0))
    grid_spec = pltpu.PrefetchScalarGridSpec(
        num_scalar_prefetch=2,
        grid=(batch, DIFF_HEADS, len(pairs)),
        in_specs=[
            pl.BlockSpec(memory_space=pltpu.SMEM),
            pl.BlockSpec((tq, LANES), lambda b, h, j, qt, kt: (b * nq + qt[j], h)),
            pl.BlockSpec((tq, LANES), lambda b, h, j, qt, kt: (b * nq + kt[j], hb + h)),
            pl.BlockSpec((tq, LANES), lambda b, h, j, qt, kt: (b * nq + kt[j], 2 * hb + h)),
            small, small, small, small,
            pl.BlockSpec((1, LANES), lambda b, h, j, qt, kt: (0, 0)),
        ],
        out_specs=pl.BlockSpec((tq, LANES), lambda b, h, j, qt, kt: (b * nq + qt[j], h)),
        scratch_shapes=[pltpu.VMEM((2 * tq, LANES), BF16), pltpu.VMEM((tq, 2 * tq), F32),
                        pltpu.VMEM((1, 2 * tq), F32), pltpu.VMEM((1, 2 * tq), F32),
                        pltpu.VMEM((LANES, 2 * tq), F32)],
    )
    return pl.pallas_call(
        functools.partial(_diff_attn_kernel, tq=tq, lam_init=lam_init),
        grid_spec=grid_spec,
        out_shape=jax.ShapeDtypeStruct((t, DIFF_WIDTH), BF16),
        compiler_params=_cparams(("parallel", "parallel", "arbitrary")),
        name="diff_attention",
    )(qi_tab, ki_tab, slopes, qkv, qkv, qkv, lq1.reshape(1, -1), lk1.reshape(1, -1), lq2.reshape(1, -1),
      lk2.reshape(1, -1), subln_g.reshape(1, -1))


def _mix_kernel(ra_ref, da_ref, ga_ref, gb_ref, x_ref, wa_ref, wb_ref, wo_ref, o_ref):
    ya = _dot(ra_ref[0], wa_ref[0:SLAB, :])
    for q in range(1, RWKV_SLABS):
        ya = ya + _dot(ra_ref[q], wa_ref[q * SLAB:(q + 1) * SLAB, :])
    yb = _dot(da_ref[...], wb_ref[...])
    mixed = ga_ref[...].astype(F32) * ya + gb_ref[...].astype(F32) * yb
    o_ref[...] = x_ref[...] + _dot(mixed.astype(BF16), wo_ref[...])


def mix_project(ra, da, gates, x, wa, wb, wo, *, tm):
    t, d = x.shape
    const = lambda a: pl.BlockSpec(a.shape, lambda i: (0, 0), pipeline_mode=pl.Buffered(1))
    return pl.pallas_call(
        _mix_kernel,
        grid=(t // tm,),
        in_specs=[
            pl.BlockSpec((RWKV_SLABS, tm, SLAB), lambda i: (0, i, 0)),
            pl.BlockSpec((tm, DIFF_WIDTH), lambda i: (i, 0)),
            pl.BlockSpec((tm, d), lambda i: (i, 0)),
            pl.BlockSpec((tm, d), lambda i: (i, 1)),
            pl.BlockSpec((tm, d), lambda i: (i, 0)),
            const(wa), const(wb), const(wo),
        ],
        out_specs=pl.BlockSpec((tm, d), lambda i: (i, 0)),
        out_shape=jax.ShapeDtypeStruct((t, d), F32),
        compiler_params=_cparams(("parallel",)),
        name="mix_project",
    )(ra, da, gates, gates, x, wa, wb, wo)


def _cross_kernel(x_ref, gc_ref, wq_ref, kv_ref, wo_ref, gf_ref, rw_ref, rb_ref, x2_ref, hf_ref, lg_ref):
    x = x_ref[...]
    h = _rms(x, gc_ref[...], NORM_EPS).astype(BF16)
    q = _dot(h, wq_ref[...]).astype(BF16)
    scale = LANES ** -0.5
    outs = []
    for hd in range(CROSS_HEADS):
        qh = q[:, hd * LANES:(hd + 1) * LANES]
        kh = kv_ref[0, :, hd * LANES:(hd + 1) * LANES]
        vh = kv_ref[0, :, CROSS_WIDTH + hd * LANES:CROSS_WIDTH + (hd + 1) * LANES]
        s = _dot(qh, kh, NT) * scale
        s = s - jnp.max(s, axis=-1, keepdims=True)
        e = jnp.exp(s)
        p = e / jnp.sum(e, axis=-1, keepdims=True)
        outs.append(_dot(p.astype(BF16), vh))
    o = jnp.concatenate(outs, axis=1).astype(BF16)
    x2 = x + _dot(o, wo_ref[...])
    x2_ref[...] = x2
    hf = _rms(x2, gf_ref[...], NORM_EPS)
    tm = hf.shape[0]
    for c in range(PACK_ROWS):
        hf_ref[pl.ds(c, tm, stride=PACK_ROWS), :] = _pack_pair(hf[:, 2 * c * LANES:(2 * c + 1) * LANES],
                                                               hf[:, (2 * c + 1) * LANES:(2 * c + 2) * LANES])
    lg_ref[...] = _dot3(hf, rw_ref[...]) + rb_ref[...]


def cross_attention(x, gc, wq, kv, wo, gf, rw, rb, *, seq, tm):
    t, d = x.shape
    per_seq = seq // tm
    const = lambda a: pl.BlockSpec(a.shape, lambda i: (0,) * a.ndim, pipeline_mode=pl.Buffered(1))
    vec = lambda n: pl.BlockSpec((1, n), lambda i: (0, 0))
    return pl.pallas_call(
        _cross_kernel,
        grid=(t // tm,),
        in_specs=[
            pl.BlockSpec((tm, d), lambda i: (i, 0)),
            vec(d), const(wq),
            pl.BlockSpec((1,) + kv.shape[1:], lambda i: (i // per_seq, 0, 0)),
            const(wo), vec(d), const(rw), vec(LANES),
        ],
        out_specs=[pl.BlockSpec((tm, d), lambda i: (i, 0)), pl.BlockSpec((tm * PACK_ROWS, LANES), lambda i: (i, 0)),
                   pl.BlockSpec((tm, LANES), lambda i: (i, 0))],
        out_shape=[jax.ShapeDtypeStruct((t, d), F32), jax.ShapeDtypeStruct((t * PACK_ROWS, LANES), U32),
                   jax.ShapeDtypeStruct((t, LANES), F32)],
        compiler_params=_cparams(("parallel",)),
        name="cross_attention",
    )(x, gc.reshape(1, d), wq, kv, wo, gf.reshape(1, d), rw, rb)


def _router_kernel(lg_ref, idx_ref, gate_ref):
    x = lg_ref[...]
    lane = lax.broadcasted_iota(jnp.int32, x.shape, 1)
    x = jnp.where(lane < N_EXPERTS, x, -jnp.inf)
    idx_out = jnp.zeros(x.shape, jnp.int32)
    val_out = jnp.zeros(x.shape, F32)
    vals = []
    for j in range(TOP_K):
        m = jnp.max(x, axis=-1, keepdims=True)
        sel = jnp.min(jnp.where(x == m, lane, LANES), axis=-1, keepdims=True)
        idx_out = jnp.where(lane == j, sel, idx_out)
        vals.append(m)
        x = jnp.where(lane == sel, -jnp.inf, x)
    es = [jnp.exp(vj - vals[0]) for vj in vals]
    tot = es[0] + es[1] + es[2] + es[3]
    for j in range(TOP_K):
        val_out = jnp.where(lane == j, es[j] / tot, val_out)
    idx_ref[...] = idx_out[:, 0:TOP_K]
    gate_ref[...] = val_out[:, 0:TOP_K]


def router_topk(logits, *, tm):
    t = logits.shape[0]
    return pl.pallas_call(
        _router_kernel,
        grid=(t // tm,),
        in_specs=[pl.BlockSpec((tm, LANES), lambda i: (i, 0))],
        out_specs=[pl.BlockSpec((tm, TOP_K), lambda i: (i, 0)), pl.BlockSpec((tm, TOP_K), lambda i: (i, 0))],
        out_shape=[jax.ShapeDtypeStruct((t, TOP_K), jnp.int32), jax.ShapeDtypeStruct((t, TOP_K), F32)],
        compiler_params=_cparams(("parallel",)),
        name="router_topk",
    )(logits)


def _moe_kernel(sbe_ref, sbr_ref, src_ref, tok_ref, ntok_ref, hf_hbm, w1_ref, b1_ref, w2_ref, b2_ref,
                ys_hbm, xbuf, x2d, acc, obuf, gsem, ssem, *, nf):
    s = pl.program_id(0)
    f = pl.program_id(1)
    nsb = pl.num_programs(0)
    sub_rows = MOE_SUB * PACK_ROWS
    nsubs = MOE_SUPER // MOE_SUB
    unroll = 8

    def ceil_sub(n):
        return lax.shift_right_logical(n + (MOE_SUB - 1), MOE_SUB.bit_length() - 1)

    rows = sbr_ref[s]
    nsub = ceil_sub(rows)
    rows_next = jnp.where(s + 1 < nsb, sbr_ref[jnp.minimum(s + 1, nsb - 1)], 0)
    rows_prev = jnp.where(s > 0, sbr_ref[jnp.maximum(s - 1, 0)], 0)
    nsub_next = ceil_sub(rows_next)

    def slab(ref, off):
        return ref.at[pl.ds(pl.multiple_of(off, PACK_ROWS), PACK_ROWS)]

    def row_in(table, r):
        return pltpu.make_async_copy(slab(hf_hbm, table[0, 0, r]), slab(xbuf, r * PACK_ROWS), gsem)

    def row_out(r):
        return pltpu.make_async_copy(slab(obuf, r * PACK_ROWS), slab(ys_hbm, src_ref[0, 0, r] * PACK_ROWS), ssem)

    def start_gather(table, first, n):
        def group(gi, c):
            for j in range(unroll):
                row_in(table, first + gi * unroll + j).start()
            return c

        lax.fori_loop(0, n // unroll, group, 0)

    def wait_sub_blocks(buf, other, sem, n):
        for sub in range(nsubs):
            @pl.when(sub < n)
            def _(sub=sub):
                pltpu.make_async_copy(other.at[pl.ds(0, sub_rows)], buf.at[pl.ds(sub * sub_rows, sub_rows)], sem).wait()

    def wait_scatter(n):
        full = n // MOE_SUB
        for sub in range(nsubs):
            @pl.when(sub < full)
            def _(sub=sub):
                pltpu.make_async_copy(obuf.at[pl.ds(sub * sub_rows, sub_rows)], ys_hbm.at[pl.ds(0, sub_rows)], ssem).wait()

        def one(r, c):
            row_out(r).wait()
            return c

        lax.fori_loop(0, n - full * MOE_SUB, one, 0)

    @pl.when((f == 0) & (s == 0) & (rows > 0))
    def _():
        start_gather(tok_ref, 0, nsub * MOE_SUB)

    @pl.when((f == 0) & (rows > 0))
    def _():
        wait_sub_blocks(xbuf, hf_hbm, gsem, nsub)

    @pl.when((f == nf - 1) & (rows > 0) & (rows_prev > 0))
    def _():
        wait_scatter(rows_prev)

    def rows_of(sub):
        return slice(sub * MOE_SUB, (sub + 1) * MOE_SUB)

    def retile_in(sub):
        for c in range(PACK_ROWS):
            lo, hi = _unpack_pair(xbuf[pl.ds(sub * sub_rows + c, MOE_SUB, stride=PACK_ROWS), :])
            x2d[rows_of(sub), 2 * c * LANES:(2 * c + 1) * LANES] = lo.astype(BF16)
            x2d[rows_of(sub), (2 * c + 1) * LANES:(2 * c + 2) * LANES] = hi.astype(BF16)

    def hidden(sub):
        return _dot(x2d[rows_of(sub), :], w1_ref[0]) + b1_ref[0]

    def activate(hb):
        even = (lax.broadcasted_iota(jnp.int32, (MOE_SUB, LANES), 1) % 2) == 0
        acts = []
        for cb in range(MOE_FT // LANES):
            a = hb[:, cb * LANES:(cb + 1) * LANES]
            b = hb[:, MOE_FT + cb * LANES:MOE_FT + (cb + 1) * LANES]
            hg = jnp.where(even, a, pltpu.roll(b, 1, axis=1))
            hl = jnp.where(even, pltpu.roll(a, LANES - 1, axis=1), b)
            xg = jnp.minimum(hg, SWIGLU_LIMIT)
            xl = jnp.clip(hl, -SWIGLU_LIMIT, SWIGLU_LIMIT)
            acts.append((xg * jax.nn.sigmoid(SWIGLU_ALPHA * xg) * (xl + 1.0)).astype(BF16))
        return jnp.concatenate(acts, axis=1)

    def retile_out(sub, y):
        for c in range(PACK_ROWS):
            obuf[pl.ds(sub * sub_rows + c, MOE_SUB, stride=PACK_ROWS), :] = _pack_pair(
                y[:, 2 * c * LANES:(2 * c + 1) * LANES], y[:, (2 * c + 1) * LANES:(2 * c + 2) * LANES])

    full = nsub == nsubs
    if nf == 2:
        def full_step(first):
            hbs = {}
            outs = {}
            for stage in range(nsubs + 2):
                if stage < nsubs:
                    if first:
                        retile_in(stage)
                    hbs[stage] = hidden(stage)
                if 1 <= stage <= nsubs:
                    sub = stage - 1
                    outs[sub] = _dot(activate(hbs.pop(sub)), w2_ref[0])
                if stage >= 2:
                    sub = stage - 2
                    if first:
                        acc[rows_of(sub), :] = outs.pop(sub)
                    else:
                        retile_out(sub, acc[rows_of(sub), :] + outs.pop(sub) + b2_ref[0])

        @pl.when(full & (f == 0))
        def _():
            full_step(True)

        @pl.when(full & (f == 1))
        def _():
            full_step(False)

    for sub in range(nsubs):
        @pl.when(((sub < nsub) & jnp.logical_not(full)) if nf == 2 else (sub < nsub))
        def _(sub=sub):
            sl = rows_of(sub)

            @pl.when(f == 0)
            def _():
                retile_in(sub)

            contrib = _dot(activate(hidden(sub)), w2_ref[0])

            @pl.when(f == 0)
            def _():
                acc[sl, :] = contrib

            @pl.when(f > 0)
            def _():
                acc[sl, :] += contrib

            @pl.when(f == nf - 1)
            def _():
                retile_out(sub, acc[sl, :] + b2_ref[0])

    @pl.when((f == 0) & (rows_next > 0))
    def _():
        start_gather(ntok_ref, 0, nsub_next * MOE_SUB)

    @pl.when((f == nf - 1) & (rows > 0))
    def _():
        def group(gi, c):
            for j in range(unroll):
                row_out(gi * unroll + j).start()
            return c

        def one(r, c):
            row_out(r).start()
            return c

        ngroups = rows // unroll
        lax.fori_loop(0, ngroups, group, 0)
        lax.fori_loop(ngroups * unroll, rows, one, 0)

        @pl.when(rows_next == 0)
        def _():
            wait_scatter(rows)


def moe_experts(sb_e, sb_rows, row_src, row_off, hf, w1, b1, w2, b2):
    d = D_MODEL
    t = hf.shape[0] // PACK_ROWS
    nsb = sb_e.shape[0]
    nf = D_EXPERT // MOE_FT
    grid_spec = pltpu.PrefetchScalarGridSpec(
        num_scalar_prefetch=2,
        grid=(nsb, nf),
        in_specs=[
            pl.BlockSpec((1, 1, MOE_SUPER), lambda s, f, e, r: (s, 0, 0), memory_space=pltpu.SMEM),
            pl.BlockSpec((1, 1, MOE_SUPER), lambda s, f, e, r: (s, 0, 0), memory_space=pltpu.SMEM),
            pl.BlockSpec((1, 1, MOE_SUPER), lambda s, f, e, r: (jnp.minimum(s + 1, nsb - 1), 0, 0),
                         memory_space=pltpu.SMEM),
            pl.BlockSpec(memory_space=pl.ANY),
            pl.BlockSpec((1, d, 2 * MOE_FT), lambda s, f, e, r: (e[s], 0, f)),
            pl.BlockSpec((1, 1, 2 * MOE_FT), lambda s, f, e, r: (e[s], 0, f)),
            pl.BlockSpec((1, MOE_FT, d), lambda s, f, e, r: (e[s], f, 0)),
            pl.BlockSpec((1, 1, d), lambda s, f, e, r: (e[s], 0, 0)),
        ],
        out_specs=pl.BlockSpec(memory_space=pl.ANY),
        scratch_shapes=[pltpu.VMEM((MOE_SUPER * PACK_ROWS, LANES), U32), pltpu.VMEM((MOE_SUPER, d), BF16),
                        pltpu.VMEM((MOE_SUPER, d), F32), pltpu.VMEM((MOE_SUPER * PACK_ROWS, LANES), U32),
                        pltpu.SemaphoreType.DMA, pltpu.SemaphoreType.DMA],
    )
    return pl.pallas_call(
        functools.partial(_moe_kernel, nf=nf),
        grid_spec=grid_spec,
        out_shape=jax.ShapeDtypeStruct((t * TOP_K * PACK_ROWS, LANES), U32),
        compiler_params=pltpu.CompilerParams(dimension_semantics=("arbitrary", "arbitrary"),
                                             vmem_limit_bytes=MOE_VMEM_LIMIT),
        name="moe_experts",
    )(sb_e, sb_rows, row_src, row_off, row_off, hf, w1, b1, w2, b2)


def _combine_kernel(ys_ref, gate_ref, x_ref, g_ref, o_ref, lo_ref, hi_ref):
    tm = x_ref.shape[0]
    tot_lo = tot_hi = None
    for j in range(TOP_K):
        lo, hi = _unpack_pair(ys_ref[:, j * PACK_ROWS:(j + 1) * PACK_ROWS, :])
        gj = gate_ref[:, j:j + 1, :]
        tot_lo = lo * gj if tot_lo is None else tot_lo + lo * gj
        tot_hi = hi * gj if tot_hi is None else tot_hi + hi * gj
    lo_ref[...] = tot_lo.reshape(tm * PACK_ROWS, LANES)
    hi_ref[...] = tot_hi.reshape(tm * PACK_ROWS, LANES)
    pieces = []
    for c in range(PACK_ROWS):
        pieces.append(lo_ref[pl.ds(c, tm, stride=PACK_ROWS), :])
        pieces.append(hi_ref[pl.ds(c, tm, stride=PACK_ROWS), :])
    out = x_ref[...] + jnp.concatenate(pieces, axis=1)
    o_ref[...] = _rms(out, g_ref[...], NORM_EPS)


def combine_final(ys, gate, x, g, *, tm):
    t, d = x.shape
    return pl.pallas_call(
        _combine_kernel,
        grid=(t // tm,),
        in_specs=[pl.BlockSpec((tm, TOP_K * PACK_ROWS, LANES), lambda i: (i, 0, 0)),
                  pl.BlockSpec((tm, TOP_K, LANES), lambda i: (i, 0, 0)),
                  pl.BlockSpec((tm, d), lambda i: (i, 0)), pl.BlockSpec((1, d), lambda i: (0, 0))],
        out_specs=pl.BlockSpec((tm, d), lambda i: (i, 0)),
        out_shape=jax.ShapeDtypeStruct((t, d), F32),
        scratch_shapes=[pltpu.VMEM((tm * PACK_ROWS, LANES), F32), pltpu.VMEM((tm * PACK_ROWS, LANES), F32)],
        compiler_params=_cparams(("parallel",)),
        name="combine_final",
    )(ys, gate, x, g.reshape(1, d))


def _routing_tables(top_idx, n_super):
    flat_e = top_idx.reshape(-1)
    n = flat_e.shape[0]
    onehot = (flat_e[:, None] == jnp.arange(N_EXPERTS, dtype=jnp.int32)[None, :]).astype(jnp.int32)
    csum = jnp.cumsum(onehot, axis=0)
    rank = jnp.sum(onehot * csum, axis=1) - 1
    counts = csum[-1]
    nsb = (counts + MOE_SUPER - 1) // MOE_SUPER
    sb_end = jnp.cumsum(nsb)
    sb_start = sb_end - nsb
    dest = sb_start[flat_e] * MOE_SUPER + rank
    row_src = jnp.full((n_super * MOE_SUPER,), -1, jnp.int32).at[dest].set(jnp.arange(n, dtype=jnp.int32))
    s_ids = jnp.arange(n_super, dtype=jnp.int32)
    sb_e = jnp.minimum(jnp.searchsorted(sb_end, s_ids, side="right"), N_EXPERTS - 1).astype(jnp.int32)
    local = s_ids - sb_start[sb_e]
    sb_rows = jnp.clip(counts[sb_e] - local * MOE_SUPER, 0, MOE_SUPER)
    sb_rows = jnp.where(s_ids < sb_end[-1], sb_rows, 0).astype(jnp.int32)
    row_off = jnp.where(row_src >= 0, (row_src // TOP_K) * PACK_ROWS, 0)
    return sb_e, sb_rows, row_src.reshape(n_super, 1, MOE_SUPER), row_off.reshape(n_super, 1, MOE_SUPER)


def _pad_rows(a, n):
    return jnp.pad(a, ((0, n - a.shape[0]), (0, 0)))


def _layer(x, mem, l, p, batch, seq):
    t = batch * seq
    c = RWKV_WIDTH
    lam_init = 0.8 - 0.6 * math.exp(-0.3 * l)
    w_in = p["w_in"]
    o1 = 3 * c + DECAY_LORA + AAA_LORA + GATE_LORA
    o2 = o1 + 3 * DIFF_WIDTH
    padc = lambda a, n: jnp.pad(a, ((0, 0), (0, n - a.shape[1])))
    w_rwkv = jnp.concatenate([
        w_in[:, :3 * c],
        padc(w_in[:, 3 * c:3 * c + DECAY_LORA], 128),
        padc(w_in[:, 3 * c + DECAY_LORA:3 * c + DECAY_LORA + AAA_LORA], 128),
        padc(w_in[:, 3 * c + DECAY_LORA + AAA_LORA:o1], 256)], axis=1).astype(BF16)
    mu = p["rwkv_mu"]
    pad1 = lambda a, n: jnp.pad(a, (0, n - a.shape[0]))
    mu_p = jnp.concatenate([mu[:3 * c], pad1(mu[3 * c:3 * c + DECAY_LORA], 128),
                            pad1(mu[3 * c + DECAY_LORA:3 * c + DECAY_LORA + AAA_LORA], 128),
                            pad1(mu[3 * c + DECAY_LORA + AAA_LORA:], 256)])
    w_diff = w_in[:, o1:o2].astype(BF16)
    w_gate = w_in[:, o2:].astype(BF16)

    g_mix = p["norm_mix_g"]
    p_rwkv = norm_matmul(x, g_mix, w_rwkv, tm=1024, tn=RWKV_COLS // 2, out_dtype=F32, name="in_proj_rwkv")
    qkv = norm_matmul(x, g_mix, w_diff, tm=1024, tn=1024, out_dtype=BF16, name="in_proj_diff")
    gates = norm_matmul(x, g_mix, w_gate, tm=1024, tn=1024, out_dtype=BF16, act="sigmoid", name="in_proj_gate")

    prep = rwkv_prep(p_rwkv, mu_p, p["rwkv_w0"], p["rwkv_a0"], p["rwkv_k_k"], p["rwkv_k_a"],
                     _pad_rows(p["rwkv_w2"], 128), _pad_rows(p["rwkv_a2"], 128), _pad_rows(p["rwkv_g2"], 256),
                     seq=seq, tm=256)
    ra, w1b, w2b = rwkv_scan(*prep, p["rwkv_ln_w"], p["rwkv_ln_b"], p["rwkv_r_k"].reshape(-1),
                             p["expert_w1"], p["expert_w2"], batch=batch, seq=seq)

    slopes = (2.0 ** (-8.0 * jnp.arange(1, DIFF_HEADS + 1, dtype=F32) / DIFF_HEADS)).astype(F32)
    da = diff_attention(qkv, slopes, p["diff_lq1"], p["diff_lk1"], p["diff_lq2"], p["diff_lk2"],
                        p["diff_subln_g"], batch=batch, seq=seq, tq=min(1024, seq), lam_init=lam_init)

    x1 = mix_project(ra, da, gates, x, p["rwkv_proj"].astype(BF16), p["diff_proj"].astype(BF16),
                     p["w_out"].astype(BF16), tm=256)

    m_len = mem.shape[0] // batch
    kv = norm_matmul(mem, p["norm_mem_g"], p["cross_wkv"].astype(BF16), tm=min(512, mem.shape[0]),
                     tn=2 * CROSS_WIDTH, out_dtype=BF16, name="cross_kv")
    rw = jnp.pad(p["router_w"], ((0, 0), (0, LANES - N_EXPERTS)))
    rb = jnp.pad(p["router_b"], (0, LANES - N_EXPERTS)).reshape(1, LANES)
    x2, hf, logits = cross_attention(x1, p["norm_cross_g"], p["cross_wq"].astype(BF16),
                                     kv.reshape(batch, m_len, 2 * CROSS_WIDTH), p["cross_wo"].astype(BF16),
                                     p["norm_ffn_g"], rw, rb, seq=seq, tm=256)

    top_idx, gate = router_topk(logits, tm=min(1024, t))
    n_super = (t * TOP_K) // MOE_SUPER + N_EXPERTS
    sb_e, sb_rows, row_src, row_off = _routing_tables(top_idx, n_super)
    ys = moe_experts(sb_e, sb_rows, row_src, row_off, hf, w1b, p["expert_b1"][:, None, :], w2b, p["expert_b2"][:, None, :])
    gate_b = jnp.broadcast_to(gate[:, :, None], (t, TOP_K, LANES))
    return x2, ys.reshape(t, TOP_K * PACK_ROWS, LANES), gate_b


def kernel(x, mem, norm_mix_g, w_in, rwkv_mu, rwkv_w0, rwkv_w2, rwkv_a0, rwkv_a2, rwkv_g2, rwkv_k_k, rwkv_k_a, rwkv_r_k, rwkv_ln_w, rwkv_ln_b, rwkv_proj, diff_lq1, diff_lk1, diff_lq2, diff_lk2, diff_subln_g, diff_proj, w_out, norm_cross_g, norm_mem_g, cross_wq, cross_wkv, cross_wo, norm_ffn_g, router_w, router_b, expert_w1, expert_b1, expert_w2, expert_b2, final_norm_g):
    batch, seq, d = x.shape
    stacked = dict(norm_mix_g=norm_mix_g, w_in=w_in, rwkv_mu=rwkv_mu, rwkv_w0=rwkv_w0, rwkv_w2=rwkv_w2,
                   rwkv_a0=rwkv_a0, rwkv_a2=rwkv_a2, rwkv_g2=rwkv_g2, rwkv_k_k=rwkv_k_k, rwkv_k_a=rwkv_k_a,
                   rwkv_r_k=rwkv_r_k, rwkv_ln_w=rwkv_ln_w, rwkv_ln_b=rwkv_ln_b, rwkv_proj=rwkv_proj,
                   diff_lq1=diff_lq1, diff_lk1=diff_lk1, diff_lq2=diff_lq2, diff_lk2=diff_lk2,
                   diff_subln_g=diff_subln_g, diff_proj=diff_proj, w_out=w_out, norm_cross_g=norm_cross_g,
                   norm_mem_g=norm_mem_g, cross_wq=cross_wq, cross_wkv=cross_wkv, cross_wo=cross_wo,
                   norm_ffn_g=norm_ffn_g, router_w=router_w, router_b=router_b, expert_w1=expert_w1,
                   expert_b1=expert_b1, expert_w2=expert_w2, expert_b2=expert_b2)
    assert w_in.shape[0] == 1, "the closing RMSNorm is fused into the single layer's combine"
    p = {k: v[0] for k, v in stacked.items()}
    x2, ys, gate_b = _layer(x.reshape(batch * seq, d), mem.reshape(-1, d), 0, p, batch, seq)
    out = combine_final(ys, gate_b, x2, final_norm_g, tm=256)
    return out.reshape(batch, seq, d)
```

```python
import functools
import math

import jax
import jax.numpy as jnp
from jax import lax
from jax.experimental import pallas as pl
from jax.experimental.pallas import tpu as pltpu

F32 = jnp.float32
BF16 = jnp.bfloat16
U32 = jnp.uint32

D_MODEL = 2048
NORM_EPS = 1e-5
LOG2E = 1.4426950408889634
LANES = 128
HEAD_DIM = 64
RWKV_WIDTH = 1024
SLAB = 256
RWKV_SLABS = RWKV_WIDTH // SLAB
HEADS_PER_SLAB = SLAB // HEAD_DIM
CHUNK = 64
RWKV_GN_EPS = 64e-5
DECAY_LORA = 64
AAA_LORA = 64
GATE_LORA = 160
LORA_PAD = 512
RWKV_COLS = 3 * RWKV_WIDTH + LORA_PAD
DIFF_WIDTH = 1024
DIFF_HEADS = 8
CROSS_HEADS = 4
CROSS_WIDTH = 512
N_EXPERTS = 32
TOP_K = 4
D_EXPERT = 2048
SWIGLU_ALPHA = 1.702
SWIGLU_LIMIT = 7.0
MOE_SUPER = 1024
MOE_SUB = 256
MOE_FT = 1024
W2_GROUP = 128
ROW_TILE = D_MODEL // LANES
PACK_ROWS = ROW_TILE // 2
VMEM_LIMIT = 56 * 1024 * 1024

NN = (((1,), (0,)), ((), ()))
NT = (((1,), (1,)), ((), ()))


def _dot(a, b, dims=NN):
    return lax.dot_general(a, b, dims, preferred_element_type=F32)


def _split2(a):
    hi = a.astype(BF16)
    lo = (a - hi.astype(F32)).astype(BF16)
    return hi, lo


def _split3(a):
    hi = a.astype(BF16)
    r1 = a - hi.astype(F32)
    mid = r1.astype(BF16)
    lo = (r1 - mid.astype(F32)).astype(BF16)
    return hi, mid, lo


def _dot3(a, b, dims=NN):
    ah, al = _split2(a)
    bh, bl = _split2(b)
    return _dot(ah, bh, dims) + (_dot(ah, bl, dims) + _dot(al, bh, dims))


def _dot_exact_rhs(a, b_bf16, dims=NN):
    h, m, l = _split3(a)
    return _dot(h, b_bf16, dims) + (_dot(m, b_bf16, dims) + _dot(l, b_bf16, dims))


def _pack_pair(lo, hi):
    lo_b = lax.shift_right_logical(lax.bitcast_convert_type(lo.astype(BF16).astype(F32), U32), jnp.uint32(16))
    hi_b = lax.bitcast_convert_type(hi.astype(BF16).astype(F32), U32) & jnp.uint32(0xFFFF0000)
    return hi_b | lo_b


def _unpack_pair(w):
    lo = lax.bitcast_convert_type(lax.shift_left(w, jnp.uint32(16)), F32)
    hi = lax.bitcast_convert_type(w & jnp.uint32(0xFFFF0000), F32)
    return lo, hi


def _rms(x, g, eps):
    ms = jnp.mean(x * x, axis=-1, keepdims=True)
    return x * lax.rsqrt(ms + eps) * g


def _cparams(sem):
    return pltpu.CompilerParams(dimension_semantics=sem, vmem_limit_bytes=VMEM_LIMIT)


def _norm_matmul_kernel(x_ref, g_ref, w_ref, o_ref, h_ref, *, act):
    @pl.when(pl.program_id(1) == 0)
    def _():
        h_ref[...] = _rms(x_ref[...], g_ref[...], NORM_EPS).astype(BF16)

    y = _dot(h_ref[...], w_ref[...])
    if act == "sigmoid":
        y = jax.nn.sigmoid(y)
    o_ref[...] = y.astype(o_ref.dtype)


def norm_matmul(x, g, w, *, tm, tn, out_dtype, act=None, name):
    m, d = x.shape
    n = w.shape[1]
    return pl.pallas_call(
        functools.partial(_norm_matmul_kernel, act=act),
        grid=(m // tm, n // tn),
        in_specs=[
            pl.BlockSpec((tm, d), lambda i, j: (i, 0)),
            pl.BlockSpec((1, d), lambda i, j: (0, 0)),
            pl.BlockSpec((d, tn), lambda i, j: (0, j)),
        ],
        out_specs=pl.BlockSpec((tm, tn), lambda i, j: (i, j)),
        out_shape=jax.ShapeDtypeStruct((m, n), out_dtype),
        scratch_shapes=[pltpu.VMEM((tm, d), BF16)],
        compiler_params=_cparams(("parallel", "arbitrary")),
        name=name,
    )(x, g.reshape(1, d), w)


def _head_ones():
    r = lax.broadcasted_iota(jnp.int32, (SLAB, SLAB), 0)
    c = lax.broadcasted_iota(jnp.int32, (SLAB, SLAB), 1)
    return ((r // HEAD_DIM) == (c // HEAD_DIM)).astype(BF16)


def _rwkv_prep_kernel(p_ref, prev_ref, mu_ref, w0_ref, a0_ref, kk_ref, ka_ref, w2_ref, a2_ref, g2_ref,
                      r_out, lw_out, k_out, v_out, kkn_out, b_out, g_out, *, tiles_per_seq):
    c = RWKV_WIDTH
    i = pl.program_id(0)
    p = p_ref[...]
    tm = p.shape[0]
    first = (i % tiles_per_seq) == 0
    prev_row = jnp.where(first, 0.0, prev_ref[7:8, :])
    row = lax.broadcasted_iota(jnp.int32, p.shape, 0)
    shifted = jnp.where(row == 0, prev_row, pltpu.roll(p, 1, axis=0))
    ps = p + (shifted - p) * mu_ref[...]
    r = ps[:, 0:c]
    k = ps[:, c:2 * c]
    v = ps[:, 2 * c:3 * c]
    wd = ps[:, 3 * c:3 * c + 128]
    ad = ps[:, 3 * c + 128:3 * c + 256]
    gd = ps[:, 3 * c + 256:3 * c + 512]
    z = -(w0_ref[...] + _dot3(jnp.tanh(wd), w2_ref[...]))
    softplus = jnp.maximum(z, 0.0) + jnp.log1p(jnp.exp(-jnp.abs(z)))
    w = -softplus - 0.5
    lw = -jnp.exp(w)
    a = jax.nn.sigmoid(a0_ref[...] + _dot3(ad, a2_ref[...]))
    g = _dot3(jax.nn.sigmoid(gd), g2_ref[...])
    kkr = k * kk_ref[...]
    k2 = k * (1.0 + (a - 1.0) * ka_ref[...])
    ones = _head_ones()
    for q in range(RWKV_SLABS):
        sl = slice(q * SLAB, (q + 1) * SLAB)
        x = kkr[:, sl]
        ss = _dot_exact_rhs(x * x, ones)
        kkn = x / jnp.maximum(jnp.sqrt(ss), 1e-12)
        r_out[q] = r[:, sl].astype(r_out.dtype)
        lw_out[q] = lw[:, sl]
        k_out[q] = k2[:, sl].astype(k_out.dtype)
        v_out[q] = v[:, sl].astype(v_out.dtype)
        kkn_out[q] = kkn.astype(kkn_out.dtype)
        b_out[q] = (kkn * a[:, sl]).astype(b_out.dtype)
        g_out[q] = g[:, sl].astype(g_out.dtype)


def rwkv_prep(p, mu, w0, a0, k_k, k_a, w2p, a2p, g2p, *, seq, tm):
    t, cols = p.shape
    c = RWKV_WIDTH
    vec = lambda n: pl.BlockSpec((1, n), lambda i: (0, 0))
    full = lambda a: pl.BlockSpec(a.shape, lambda i: (0, 0))
    out_spec = pl.BlockSpec((RWKV_SLABS, tm, SLAB), lambda i: (0, i, 0))
    out_shape = jax.ShapeDtypeStruct((RWKV_SLABS, t, SLAB), F32)
    out_bf16 = jax.ShapeDtypeStruct((RWKV_SLABS, t, SLAB), BF16)
    return pl.pallas_call(
        functools.partial(_rwkv_prep_kernel, tiles_per_seq=seq // tm),
        grid=(t // tm,),
        in_specs=[
            pl.BlockSpec((tm, cols), lambda i: (i, 0)),
            pl.BlockSpec((8, cols), lambda i: (jnp.maximum(i * (tm // 8) - 1, 0), 0)),
            vec(cols), vec(c), vec(c), vec(c), vec(c), full(w2p), full(a2p), full(g2p),
        ],
        out_specs=[out_spec] * 7,
        out_shape=[out_bf16, out_shape, out_bf16, out_bf16, out_bf16, out_bf16, out_bf16],
        compiler_params=_cparams(("parallel",)),
        name="rwkv_prep",
    )(p, p, mu.reshape(1, cols), w0.reshape(1, c), a0.reshape(1, c), k_k.reshape(1, c), k_a.reshape(1, c),
      w2p, a2p, g2p)


def _dot1(a, b, dims=NN):
    return _dot(a.astype(BF16), b.astype(BF16), dims)


def _dot3s(a, b, dims=NN):
    ah, al = _split2(a)
    bh, bl = _split2(b)
    m = a.shape[0]
    lhs = jnp.concatenate([ah, al], axis=0)
    if dims is NN:
        n = b.shape[1]
        rhs = jnp.concatenate([bh, bl], axis=1)
    else:
        n = b.shape[0]
        rhs = jnp.concatenate([bh, bl], axis=0)
    p = _dot(lhs, rhs, dims)
    return p[:m, :n] + (p[:m, n:] + p[m:, :n])


def _sum3_exact_rhs(xs, ones):
    parts = []
    for x in xs:
        parts.extend(_split3(x))
    res = _dot(jnp.concatenate(parts, axis=0), ones)
    L = xs[0].shape[0]
    return [res[(3 * i) * L:(3 * i + 1) * L] + (res[(3 * i + 1) * L:(3 * i + 2) * L] + res[(3 * i + 2) * L:(3 * i + 3) * L])
            for i in range(len(xs))]


def _expert_weight_chunk(w1_ref, w2a_ref, w2b_ref, w1o_ref, w2o_ref, tmp_ref):
    w1o_ref[0] = w1_ref[0].astype(BF16)
    half = W2_GROUP
    for c in range(ROW_TILE):
        cols = slice(c * LANES, (c + 1) * LANES)
        tmp_ref[c, pl.ds(0, half, stride=2), :] = w2a_ref[0, :, cols]
        tmp_ref[c, pl.ds(1, half, stride=2), :] = w2b_ref[0, :, cols]
        w2o_ref[0, :, cols] = tmp_ref[c].astype(BF16)


def _expert_weight_views(w1, w2):
    e, d, f2 = w1.shape
    fdim = w2.shape[1]
    groups_per_half = (MOE_FT // 2) // W2_GROUP
    n_chunks = e * (fdim // MOE_FT) * groups_per_half
    w1v = w1.reshape(n_chunks, (e * d) // n_chunks, f2)
    w2v = w2.reshape(e * fdim // W2_GROUP, W2_GROUP, d)
    return n_chunks, groups_per_half, w1v, w2v


def _expert_weight_specs(n_chunks, groups_per_half, w1v, w2v, step):
    d = w2v.shape[2]
    a_idx = lambda j: (j // groups_per_half) * (2 * groups_per_half) + j % groups_per_half
    in_specs = [pl.BlockSpec((1,) + w1v.shape[1:], lambda *ids: (step(*ids), 0, 0)),
                pl.BlockSpec((1, W2_GROUP, d), lambda *ids: (a_idx(step(*ids)), 0, 0)),
                pl.BlockSpec((1, W2_GROUP, d), lambda *ids: (a_idx(step(*ids)) + groups_per_half, 0, 0))]
    out_specs = [pl.BlockSpec((1,) + w1v.shape[1:], lambda *ids: (step(*ids), 0, 0)),
                 pl.BlockSpec((1, 2 * W2_GROUP, d), lambda *ids: (step(*ids), 0, 0))]
    out_shape = [jax.ShapeDtypeStruct(w1v.shape, BF16), jax.ShapeDtypeStruct((n_chunks, 2 * W2_GROUP, d), BF16)]
    scratch = pltpu.VMEM((ROW_TILE, 2 * W2_GROUP, LANES), F32)
    return in_specs, out_specs, out_shape, scratch


def expert_weight_prep(w1, w2):
    n_chunks, gph, w1v, w2v = _expert_weight_views(w1, w2)
    in_specs, out_specs, out_shape, scratch = _expert_weight_specs(n_chunks, gph, w1v, w2v, lambda j: j)
    w1b, w2b = pl.pallas_call(
        _expert_weight_chunk, grid=(n_chunks,), in_specs=in_specs, out_specs=out_specs, out_shape=out_shape,
        scratch_shapes=[scratch], compiler_params=_cparams(("parallel",)), name="expert_weight_prep",
    )(w1v, w2v, w2v)
    return w1b.reshape(w1.shape), w2b.reshape(w2.shape)


def _rwkv_scan_kernel(r_ref, lw_ref, k_ref, v_ref, kk_ref, b_ref, g_ref, lnw_ref, lnb_ref, rk_ref, *rest):
    if len(rest) == 2:
        o_ref, st_ref = rest
    else:
        w1_ref, w2a_ref, w2b_ref, o_ref, w1o_ref, w2o_ref, st_ref, tmp_ref = rest
        _expert_weight_chunk(w1_ref, w2a_ref, w2b_ref, w1o_ref, w2o_ref, tmp_ref)
    L = CHUNK
    W = SLAB

    @pl.when(pl.program_id(1) == 0)
    def _():
        st_ref[...] = jnp.zeros_like(st_ref)

    t_i = lax.broadcasted_iota(jnp.int32, (L, W), 0)
    lane = lax.broadcasted_iota(jnp.int32, (L, W), 1)
    j_i = lane % HEAD_DIM
    hid = lane // HEAD_DIM
    strict = j_i < t_i
    incl = j_i <= t_i
    eye = (j_i == t_i).astype(F32)
    r2 = lax.broadcasted_iota(jnp.int32, (W, W), 0)
    c2 = lax.broadcasted_iota(jnp.int32, (W, W), 1)
    same_head = (r2 // HEAD_DIM) == (c2 // HEAD_DIM)
    diag = r2 == c2
    ones = same_head.astype(BF16)
    tr = lax.broadcasted_iota(jnp.int32, (L, L), 0)
    tc = lax.broadcasted_iota(jnp.int32, (L, L), 1)
    tri = (tc <= tr).astype(BF16)

    def bd(x):
        return jnp.concatenate([jnp.where(hid == h, x, 0.0) for h in range(HEADS_PER_SLAB)], axis=0)

    U = range(RWKV_SLABS)
    cat0 = lambda xs: jnp.concatenate(xs, axis=0)
    cat1 = lambda xs: jnp.concatenate(xs, axis=1)
    r = [r_ref[u].astype(F32) for u in U]
    lw = [lw_ref[u] for u in U]
    k = [k_ref[u].astype(F32) for u in U]
    v = [v_ref[u].astype(F32) for u in U]
    c3 = [_dot(tri, cat1(_split3(lw[u]))) for u in U]
    cum = [c3[u][:, 0:W] + (c3[u][:, W:2 * W] + c3[u][:, 2 * W:]) for u in U]
    cum_end = [cum[u][L - 1:L, :] for u in U]
    e_cum = [jnp.exp(cum[u]) for u in U]
    e_neg = [jnp.exp(-cum[u]) for u in U]
    e_end = [jnp.exp(cum_end[u] - cum[u]) for u in U]
    at = [-kk_ref[u].astype(F32) * jnp.exp(cum[u] - lw[u]) for u in U]
    rt = [r[u] * e_cum[u] for u in U]
    bf = [b_ref[u].astype(F32) for u in U]
    bt = [bf[u] * e_neg[u] for u in U]
    kt = [k[u] * e_neg[u] for u in U]
    bh = [bf[u] * e_end[u] for u in U]
    kh = [k[u] * e_end[u] for u in U]
    w_end = [jnp.exp(cum_end[u]) for u in U]

    g_all = [_dot1(cat0([at[u], rt[u]]), cat0([bd(bt[u]), bd(kt[u])]), NT) for u in U]
    a_ab = [jnp.where(strict, g_all[u][0:L, 0:W], 0.0) for u in U]
    a_ak = [jnp.where(strict, g_all[u][0:L, W:], 0.0) for u in U]
    a_rb = [jnp.where(incl, g_all[u][L:, 0:W], 0.0) for u in U]
    a_rk = [jnp.where(incl, g_all[u][L:, W:], 0.0) for u in U]

    tinv = [eye for u in U]
    pw = a_ab
    for it in range(6):
        if it < 5:
            res = [_dot1(pw[u], cat1([bd(tinv[u]), bd(pw[u])])) for u in U]
            tinv = [tinv[u] + res[u][:, 0:W] for u in U]
            pw = [res[u][:, W:] for u in U]
        else:
            tinv = [tinv[u] + _dot1(pw[u], bd(tinv[u])) for u in U]

    bdv = [bd(v[u]) for u in U]
    av = [_dot1(a_ak[u], bdv[u]) for u in U]
    qp = [_dot1(tinv[u], cat1([bd(at[u]), bd(av[u])])) for u in U]
    q1 = [qp[u][:, 0:W] for u in U]
    p1 = [qp[u][:, W:] for u in U]
    qp2 = [_dot1(cat1([a_rb[u], a_rk[u]]),
                  cat0([cat1([bd(q1[u]), bd(p1[u])]), cat1([jnp.zeros_like(bdv[u]), bdv[u]])])) for u in U]
    q2 = [rt[u] + qp2[u][:, 0:W] for u in U]
    p2 = [qp2[u][:, W:] for u in U]
    mp = [_dot1(cat0([bh[u], kh[u]]).T,
                 cat0([cat1([q1[u], p1[u]]), cat1([jnp.zeros_like(v[u]), v[u]])])) for u in U]
    m_bd = [jnp.where(same_head, mp[u][:, 0:W], 0.0) + jnp.where(diag, w_end[u], 0.0) for u in U]
    st = [st_ref[u] for u in U]
    st_hl = [_split2(st[u]) for u in U]
    m_hl = [_split2(m_bd[u]) for u in U]
    yq = [_dot(cat0([q2[u].astype(BF16), m_hl[u][0], m_hl[u][1]]), st_hl[u][0]) for u in U]
    y = [yq[u][0:L] + p2[u] for u in U]
    for u in U:
        carry = yq[u][L:L + W] + (yq[u][L + W:] + _dot(m_hl[u][0], st_hl[u][1]))
        st_ref[u] = carry + jnp.where(same_head, mp[u][:, W:], 0.0)

    sums = [_sum3_exact_rhs([y[u], r[u] * k[u] * rk_ref[u]], ones) for u in U]
    yc = [y[u] - sums[u][0] * (1.0 / HEAD_DIM) for u in U]
    var = [_sum3_exact_rhs([yc[u] * yc[u]], ones)[0] for u in U]
    for u in U:
        yn = yc[u] * lax.rsqrt(var[u] * (1.0 / HEAD_DIM) + RWKV_GN_EPS) * lnw_ref[u] + lnb_ref[u]
        o_ref[u] = ((yn + sums[u][1] * v[u]) * g_ref[u].astype(F32)).astype(o_ref.dtype)


def rwkv_scan(r, lw, k, v, kk, b, g, ln_w, ln_b, r_k, w1, w2, *, batch, seq):
    nchunk = seq // CHUNK
    blk = pl.BlockSpec((RWKV_SLABS, CHUNK, SLAB), lambda bi, ci: (0, bi * nchunk + ci, 0))
    par = pl.BlockSpec((RWKV_SLABS, 1, SLAB), lambda bi, ci: (0, 0, 0))
    t = batch * seq
    out_shape = jax.ShapeDtypeStruct((RWKV_SLABS, t, SLAB), BF16)
    state = pltpu.VMEM((RWKV_SLABS, SLAB, SLAB), F32)
    args = (r, lw, k, v, kk, b, g, ln_w.reshape(RWKV_SLABS, 1, SLAB), ln_b.reshape(RWKV_SLABS, 1, SLAB),
            r_k.reshape(RWKV_SLABS, 1, SLAB))
    n_chunks, gph, w1v, w2v = _expert_weight_views(w1, w2)
    if n_chunks != batch * nchunk:
        ra = pl.pallas_call(
            _rwkv_scan_kernel, grid=(batch, nchunk), in_specs=[blk] * 7 + [par] * 3, out_specs=blk,
            out_shape=out_shape, scratch_shapes=[state],
            compiler_params=_cparams(("arbitrary", "arbitrary")), name="rwkv_scan",
        )(*args)
        return (ra,) + expert_weight_prep(w1, w2)
    w_in, w_out, w_shape, w_scratch = _expert_weight_specs(n_chunks, gph, w1v, w2v, lambda bi, ci: bi * nchunk + ci)
    ra, w1b, w2b = pl.pallas_call(
        _rwkv_scan_kernel, grid=(batch, nchunk), in_specs=[blk] * 7 + [par] * 3 + w_in,
        out_specs=[blk] + w_out, out_shape=[out_shape] + w_shape, scratch_shapes=[state, w_scratch],
        compiler_params=_cparams(("arbitrary", "arbitrary")), name="rwkv_scan",
    )(*args, w1v, w2v, w2v)
    return ra, w1b.reshape(w1.shape), w2b.reshape(w2.shape)


def _diff_attn_kernel(qi_ref, ki_ref, slope_ref, q_ref, k_ref, v_ref, lq1_ref, lk1_ref, lq2_ref, lk2_ref, sg_ref,
                      o_ref, qs_ref, relb_ref, m_ref, l_ref, acc_ref, *, tq, lam_init):
    h = pl.program_id(1)
    qi = qi_ref[pl.program_id(2)]
    ki = ki_ref[pl.program_id(2)]
    c2 = slope_ref[h] * LOG2E

    @pl.when(ki == 0)
    def _():
        m_ref[...] = jnp.full_like(m_ref, -jnp.inf)
        l_ref[...] = jnp.zeros_like(l_ref)
        acc_ref[...] = jnp.zeros_like(acc_ref)
        q = q_ref[...].astype(F32) * (HEAD_DIM ** -0.5 * LOG2E)
        lane = lax.broadcasted_iota(jnp.int32, q.shape, 1)
        m0 = lane < HEAD_DIM
        qs_ref[0:tq, :] = jnp.where(m0, q, 0.0).astype(BF16)
        qs_ref[tq:, :] = jnp.where(m0, 0.0, q).astype(BF16)

    @pl.when(pl.program_id(2) == 0)
    def _():
        rel = (lax.broadcasted_iota(jnp.int32, (tq, tq), 1) - lax.broadcasted_iota(jnp.int32, (tq, tq), 0))
        relb = rel.astype(F32) * (-c2)
        relb_ref[:, 0:tq] = relb
        relb_ref[:, tq:] = relb

    def step(masked):
        k = k_ref[...]
        vt = v_ref[...].T
        relb = relb_ref[...]
        tile_bias = c2 * ((qi - ki) * tq).astype(F32)
        s = _dot(k, qs_ref[...], NT) + relb
        if masked:
            s = jnp.where(relb > 0.0, -jnp.inf, s)
        m_prev = m_ref[...]
        m_new = jnp.maximum(m_prev, jnp.max(s, axis=0, keepdims=True) - tile_bias)
        alpha = jnp.exp2(m_prev - m_new)
        p = jnp.exp2(s - (m_new + tile_bias))
        l_ref[...] = alpha * l_ref[...] + jnp.sum(p, axis=0, keepdims=True)
        acc_ref[...] = alpha * acc_ref[...] + _dot(vt, p.astype(BF16))
        m_ref[...] = m_new

    @pl.when(ki < qi)
    def _():
        step(False)

    @pl.when(ki == qi)
    def _():
        step(True)
        lam = (jnp.exp(jnp.sum(lq1_ref[...] * lk1_ref[...], axis=-1, keepdims=True))
               - jnp.exp(jnp.sum(lq2_ref[...] * lk2_ref[...], axis=-1, keepdims=True)) + lam_init)
        ot = acc_ref[:, 0:tq] / l_ref[:, 0:tq] - lam * (acc_ref[:, tq:] / l_ref[:, tq:])
        o = _rms(ot.T, sg_ref[...], NORM_EPS) * (1.0 - lam_init)
        o_ref[...] = o.astype(o_ref.dtype)


def diff_attention(qkv, slopes, lq1, lk1, lq2, lk2, subln_g, *, batch, seq, tq, lam_init):
    t = batch * seq
    nq = seq // tq
    hb = DIFF_WIDTH // LANES
    pairs = [(qi, ki) for qi in range(nq) for ki in range(qi + 1)]
    qi_tab = jnp.asarray([pq for pq, _ in pairs], jnp.int32)
    ki_tab = jnp.asarray([pk for _, pk in pairs], jnp.int32)
    small = pl.BlockSpec((1, HEAD_DIM), lambda b, h, j, qt, kt: (0, 0))
    grid_spec = pltpu.PrefetchScalarGridSpec(
        num_scalar_prefetch=2,
        grid=(batch, DIFF_HEADS, len(pairs)),
        in_specs=[
            pl.BlockSpec(memory_space=pltpu.SMEM),
            pl.BlockSpec((tq, LANES), lambda b, h, j, qt, kt: (b * nq + qt[j], h)),
            pl.BlockSpec((tq, LANES), lambda b, h, j, qt, kt: (b * nq + kt[j], hb + h)),
            pl.BlockSpec((tq, LANES), lambda b, h, j, qt, kt: (b * nq + kt[j], 2 * hb + h)),
            small, small, small, small,
            pl.BlockSpec((1, LANES), lambda b, h, j, qt, kt: (0, 0)),
        ],
        out_specs=pl.BlockSpec((tq, LANES), lambda b, h, j, qt, kt: (b * nq + qt[j], h)),
        scratch_shapes=[pltpu.VMEM((2 * tq, LANES), BF16), pltpu.VMEM((tq, 2 * tq), F32),
                        pltpu.VMEM((1, 2 * tq), F32), pltpu.VMEM((1, 2 * tq), F32),
                        pltpu.VMEM((LANES, 2 * tq), F32)],
    )
    return pl.pallas_call(
        functools.partial(_diff_attn_kernel, tq=tq, lam_init=lam_init),
        grid_spec=grid_spec,
        out_shape=jax.ShapeDtypeStruct((t, DIFF_WIDTH), BF16),
        compiler_params=_cparams(("parallel", "parallel", "arbitrary")),
        name="diff_attention",
    )(qi_tab, ki_tab, slopes, qkv, qkv, qkv, lq1.reshape(1, -1), lk1.reshape(1, -1), lq2.reshape(1, -1),
      lk2.reshape(1, -1), subln_g.reshape(1, -1))


def _mix_kernel(ra_ref, da_ref, ga_ref, gb_ref, x_ref, wa_ref, wb_ref, wo_ref, o_ref):
    ya = _dot(ra_ref[0], wa_ref[0:SLAB, :])
    for q in range(1, RWKV_SLABS):
        ya = ya + _dot(ra_ref[q], wa_ref[q * SLAB:(q + 1) * SLAB, :])
    yb = _dot(da_ref[...], wb_ref[...])
    mixed = ga_ref[...].astype(F32) * ya + gb_ref[...].astype(F32) * yb
    o_ref[...] = x_ref[...] + _dot(mixed.astype(BF16), wo_ref[...])


def mix_project(ra, da, gates, x, wa, wb, wo, *, tm):
    t, d = x.shape
    const = lambda a: pl.BlockSpec(a.shape, lambda i: (0, 0), pipeline_mode=pl.Buffered(1))
    return pl.pallas_call(
        _mix_kernel,
        grid=(t // tm,),
        in_specs=[
            pl.BlockSpec((RWKV_SLABS, tm, SLAB), lambda i: (0, i, 0)),
            pl.BlockSpec((tm, DIFF_WIDTH), lambda i: (i, 0)),
            pl.BlockSpec((tm, d), lambda i: (i, 0)),
            pl.BlockSpec((tm, d), lambda i: (i, 1)),
            pl.BlockSpec((tm, d), lambda i: (i, 0)),
            const(wa), const(wb), const(wo),
        ],
        out_specs=pl.BlockSpec((tm, d), lambda i: (i, 0)),
        out_shape=jax.ShapeDtypeStruct((t, d), F32),
        compiler_params=_cparams(("parallel",)),
        name="mix_project",
    )(ra, da, gates, gates, x, wa, wb, wo)


def _cross_kernel(x_ref, gc_ref, wq_ref, kv_ref, wo_ref, gf_ref, rw_ref, rb_ref, x2_ref, hf_ref, lg_ref):
    x = x_ref[...]
    h = _rms(x, gc_ref[...], NORM_EPS).astype(BF16)
    q = _dot(h, wq_ref[...]).astype(BF16)
    scale = LANES ** -0.5
    outs = []
    for hd in range(CROSS_HEADS):
        qh = q[:, hd * LANES:(hd + 1) * LANES]
        kh = kv_ref[0, :, hd * LANES:(hd + 1) * LANES]
        vh = kv_ref[0, :, CROSS_WIDTH + hd * LANES:CROSS_WIDTH + (hd + 1) * LANES]
        s = _dot(qh, kh, NT) * scale
        s = s - jnp.max(s, axis=-1, keepdims=True)
        e = jnp.exp(s)
        p = e / jnp.sum(e, axis=-1, keepdims=True)
        outs.append(_dot(p.astype(BF16), vh))
    o = jnp.concatenate(outs, axis=1).astype(BF16)
    x2 = x + _dot(o, wo_ref[...])
    x2_ref[...] = x2
    hf = _rms(x2, gf_ref[...], NORM_EPS)
    tm = hf.shape[0]
    for c in range(PACK_ROWS):
        hf_ref[pl.ds(c, tm, stride=PACK_ROWS), :] = _pack_pair(hf[:, 2 * c * LANES:(2 * c + 1) * LANES],
                                                               hf[:, (2 * c + 1) * LANES:(2 * c + 2) * LANES])
    lg_ref[...] = _dot3(hf, rw_ref[...]) + rb_ref[...]


def cross_attention(x, gc, wq, kv, wo, gf, rw, rb, *, seq, tm):
    t, d = x.shape
    per_seq = seq // tm
    const = lambda a: pl.BlockSpec(a.shape, lambda i: (0,) * a.ndim, pipeline_mode=pl.Buffered(1))
    vec = lambda n: pl.BlockSpec((1, n), lambda i: (0, 0))
    return pl.pallas_call(
        _cross_kernel,
        grid=(t // tm,),
        in_specs=[
            pl.BlockSpec((tm, d), lambda i: (i, 0)),
            vec(d), const(wq),
            pl.BlockSpec((1,) + kv.shape[1:], lambda i: (i // per_seq, 0, 0)),
            const(wo), vec(d), const(rw), vec(LANES),
        ],
        out_specs=[pl.BlockSpec((tm, d), lambda i: (i, 0)), pl.BlockSpec((tm * PACK_ROWS, LANES), lambda i: (i, 0)),
                   pl.BlockSpec((tm, LANES), lambda i: (i, 0))],
        out_shape=[jax.ShapeDtypeStruct((t, d), F32), jax.ShapeDtypeStruct((t * PACK_ROWS, LANES), U32),
                   jax.ShapeDtypeStruct((t, LANES), F32)],
        compiler_params=_cparams(("parallel",)),
        name="cross_attention",
    )(x, gc.reshape(1, d), wq, kv, wo, gf.reshape(1, d), rw, rb)


def _router_kernel(lg_ref, idx_ref, gate_ref):
    x = lg_ref[...]
    lane = lax.broadcasted_iota(jnp.int32, x.shape, 1)
    x = jnp.where(lane < N_EXPERTS, x, -jnp.inf)
    idx_out = jnp.zeros(x.shape, jnp.int32)
    val_out = jnp.zeros(x.shape, F32)
    vals = []
    for j in range(TOP_K):
        m = jnp.max(x, axis=-1, keepdims=True)
        sel = jnp.min(jnp.where(x == m, lane, LANES), axis=-1, keepdims=True)
        idx_out = jnp.where(lane == j, sel, idx_out)
        vals.append(m)
        x = jnp.where(lane == sel, -jnp.inf, x)
    es = [jnp.exp(vj - vals[0]) for vj in vals]
    tot = es[0] + es[1] + es[2] + es[3]
    for j in range(TOP_K):
        val_out = jnp.where(lane == j, es[j] / tot, val_out)
    idx_ref[...] = idx_out[:, 0:TOP_K]
    gate_ref[...] = val_out[:, 0:TOP_K]


def router_topk(logits, *, tm):
    t = logits.shape[0]
    return pl.pallas_call(
        _router_kernel,
        grid=(t // tm,),
        in_specs=[pl.BlockSpec((tm, LANES), lambda i: (i, 0))],
        out_specs=[pl.BlockSpec((tm, TOP_K), lambda i: (i, 0)), pl.BlockSpec((tm, TOP_K), lambda i: (i, 0))],
        out_shape=[jax.ShapeDtypeStruct((t, TOP_K), jnp.int32), jax.ShapeDtypeStruct((t, TOP_K), F32)],
        compiler_params=_cparams(("parallel",)),
        name="router_topk",
    )(logits)


def _moe_kernel(sbe_ref, sbr_ref, src_ref, tok_ref, ntok_ref, hf_hbm, w1_ref, b1_ref, w2_ref, b2_ref,
                ys_hbm, xbuf, x2d, acc, obuf, gsem, ssem):
    s = pl.program_id(0)
    f = pl.program_id(1)
    nf = pl.num_programs(1)
    nsb = pl.num_programs(0)
    sub_rows = MOE_SUB * PACK_ROWS
    nsubs = MOE_SUPER // MOE_SUB
    unroll = 8

    def ceil_sub(n):
        return lax.shift_right_logical(n + (MOE_SUB - 1), MOE_SUB.bit_length() - 1)

    rows = sbr_ref[s]
    nsub = ceil_sub(rows)
    rows_next = jnp.where(s + 1 < nsb, sbr_ref[jnp.minimum(s + 1, nsb - 1)], 0)
    rows_prev = jnp.where(s > 0, sbr_ref[jnp.maximum(s - 1, 0)], 0)
    nsub_next = ceil_sub(rows_next)

    def slab(ref, off):
        return ref.at[pl.ds(pl.multiple_of(off, PACK_ROWS), PACK_ROWS)]

    def row_in(table, r):
        return pltpu.make_async_copy(slab(hf_hbm, table[0, 0, r]), slab(xbuf, r * PACK_ROWS), gsem)

    def row_out(r):
        return pltpu.make_async_copy(slab(obuf, r * PACK_ROWS), slab(ys_hbm, src_ref[0, 0, r] * PACK_ROWS), ssem)

    def start_gather(table, first, n):
        def group(gi, c):
            for j in range(unroll):
                row_in(table, first + gi * unroll + j).start()
            return c

        lax.fori_loop(0, n // unroll, group, 0)

    def wait_sub_blocks(buf, other, sem, n):
        for sub in range(nsubs):
            @pl.when(sub < n)
            def _(sub=sub):
                pltpu.make_async_copy(other.at[pl.ds(0, sub_rows)], buf.at[pl.ds(sub * sub_rows, sub_rows)], sem).wait()

    def wait_scatter(n):
        full = n // MOE_SUB
        for sub in range(nsubs):
            @pl.when(sub < full)
            def _(sub=sub):
                pltpu.make_async_copy(obuf.at[pl.ds(sub * sub_rows, sub_rows)], ys_hbm.at[pl.ds(0, sub_rows)], ssem).wait()

        def one(r, c):
            row_out(r).wait()
            return c

        lax.fori_loop(0, n - full * MOE_SUB, one, 0)

    @pl.when((f == 0) & (s == 0) & (rows > 0))
    def _():
        start_gather(tok_ref, 0, nsub * MOE_SUB)

    @pl.when((f == 0) & (rows > 0))
    def _():
        wait_sub_blocks(xbuf, hf_hbm, gsem, nsub)

    @pl.when((f == nf - 1) & (rows > 0) & (rows_prev > 0))
    def _():
        wait_scatter(rows_prev)

    for sub in range(nsubs):
        @pl.when(sub < nsub)
        def _(sub=sub):
            sl = slice(sub * MOE_SUB, (sub + 1) * MOE_SUB)
            base = sub * sub_rows

            @pl.when(f == 0)
            def _():
                for c in range(PACK_ROWS):
                    lo, hi = _unpack_pair(xbuf[pl.ds(base + c, MOE_SUB, stride=PACK_ROWS), :])
                    x2d[sl, 2 * c * LANES:(2 * c + 1) * LANES] = lo.astype(BF16)
                    x2d[sl, (2 * c + 1) * LANES:(2 * c + 2) * LANES] = hi.astype(BF16)

            x = x2d[sl, :]
            hb = _dot(x, w1_ref[0]) + b1_ref[0]
            even = (lax.broadcasted_iota(jnp.int32, (MOE_SUB, LANES), 1) % 2) == 0
            acts = []
            for cb in range(MOE_FT // LANES):
                a = hb[:, cb * LANES:(cb + 1) * LANES]
                b = hb[:, MOE_FT + cb * LANES:MOE_FT + (cb + 1) * LANES]
                hg = jnp.where(even, a, pltpu.roll(b, 1, axis=1))
                hl = jnp.where(even, pltpu.roll(a, LANES - 1, axis=1), b)
                xg = jnp.minimum(hg, SWIGLU_LIMIT)
                xl = jnp.clip(hl, -SWIGLU_LIMIT, SWIGLU_LIMIT)
                acts.append((xg * jax.nn.sigmoid(SWIGLU_ALPHA * xg) * (xl + 1.0)).astype(BF16))
            contrib = _dot(jnp.concatenate(acts, axis=1), w2_ref[0])

            @pl.when(f == 0)
            def _():
                acc[sl, :] = contrib

            @pl.when(f > 0)
            def _():
                acc[sl, :] += contrib

            @pl.when(f == nf - 1)
            def _():
                y = acc[sl, :] + b2_ref[0]
                for c in range(PACK_ROWS):
                    obuf[pl.ds(base + c, MOE_SUB, stride=PACK_ROWS), :] = _pack_pair(
                        y[:, 2 * c * LANES:(2 * c + 1) * LANES], y[:, (2 * c + 1) * LANES:(2 * c + 2) * LANES])

    @pl.when((f == 0) & (rows_next > 0))
    def _():
        start_gather(ntok_ref, 0, nsub_next * MOE_SUB)

    @pl.when((f == nf - 1) & (rows > 0))
    def _():
        def group(gi, c):
            for j in range(unroll):
                row_out(gi * unroll + j).start()
            return c

        def one(r, c):
            row_out(r).start()
            return c

        ngroups = rows // unroll
        lax.fori_loop(0, ngroups, group, 0)
        lax.fori_loop(ngroups * unroll, rows, one, 0)

        @pl.when(rows_next == 0)
        def _():
            wait_scatter(rows)


def moe_experts(sb_e, sb_rows, row_src, row_off, hf, w1, b1, w2, b2):
    d = D_MODEL
    t = hf.shape[0] // PACK_ROWS
    nsb = sb_e.shape[0]
    nf = D_EXPERT // MOE_FT
    grid_spec = pltpu.PrefetchScalarGridSpec(
        num_scalar_prefetch=2,
        grid=(nsb, nf),
        in_specs=[
            pl.BlockSpec((1, 1, MOE_SUPER), lambda s, f, e, r: (s, 0, 0), memory_space=pltpu.SMEM),
            pl.BlockSpec((1, 1, MOE_SUPER), lambda s, f, e, r: (s, 0, 0), memory_space=pltpu.SMEM),
            pl.BlockSpec((1, 1, MOE_SUPER), lambda s, f, e, r: (jnp.minimum(s + 1, nsb - 1), 0, 0),
                         memory_space=pltpu.SMEM),
            pl.BlockSpec(memory_space=pl.ANY),
            pl.BlockSpec((1, d, 2 * MOE_FT), lambda s, f, e, r: (e[s], 0, f)),
            pl.BlockSpec((1, 1, 2 * MOE_FT), lambda s, f, e, r: (e[s], 0, f)),
            pl.BlockSpec((1, MOE_FT, d), lambda s, f, e, r: (e[s], f, 0)),
            pl.BlockSpec((1, 1, d), lambda s, f, e, r: (e[s], 0, 0)),
        ],
        out_specs=pl.BlockSpec(memory_space=pl.ANY),
        scratch_shapes=[pltpu.VMEM((MOE_SUPER * PACK_ROWS, LANES), U32), pltpu.VMEM((MOE_SUPER, d), BF16),
                        pltpu.VMEM((MOE_SUPER, d), F32), pltpu.VMEM((MOE_SUPER * PACK_ROWS, LANES), U32),
                        pltpu.SemaphoreType.DMA, pltpu.SemaphoreType.DMA],
    )
    return pl.pallas_call(
        _moe_kernel,
        grid_spec=grid_spec,
        out_shape=jax.ShapeDtypeStruct((t * TOP_K * PACK_ROWS, LANES), U32),
        compiler_params=_cparams(("arbitrary", "arbitrary")),
        name="moe_experts",
    )(sb_e, sb_rows, row_src, row_off, row_off, hf, w1, b1, w2, b2)


def _combine_kernel(ys_ref, gate_ref, x_ref, g_ref, o_ref, lo_ref, hi_ref):
    tm = x_ref.shape[0]
    tot_lo = tot_hi = None
    for j in range(TOP_K):
        lo, hi = _unpack_pair(ys_ref[:, j * PACK_ROWS:(j + 1) * PACK_ROWS, :])
        gj = gate_ref[:, j:j + 1, :]
        tot_lo = lo * gj if tot_lo is None else tot_lo + lo * gj
        tot_hi = hi * gj if tot_hi is None else tot_hi + hi * gj
    lo_ref[...] = tot_lo.reshape(tm * PACK_ROWS, LANES)
    hi_ref[...] = tot_hi.reshape(tm * PACK_ROWS, LANES)
    pieces = []
    for c in range(PACK_ROWS):
        pieces.append(lo_ref[pl.ds(c, tm, stride=PACK_ROWS), :])
        pieces.append(hi_ref[pl.ds(c, tm, stride=PACK_ROWS), :])
    out = x_ref[...] + jnp.concatenate(pieces, axis=1)
    o_ref[...] = _rms(out, g_ref[...], NORM_EPS)


def combine_final(ys, gate, x, g, *, tm):
    t, d = x.shape
    return pl.pallas_call(
        _combine_kernel,
        grid=(t // tm,),
        in_specs=[pl.BlockSpec((tm, TOP_K * PACK_ROWS, LANES), lambda i: (i, 0, 0)),
                  pl.BlockSpec((tm, TOP_K, LANES), lambda i: (i, 0, 0)),
                  pl.BlockSpec((tm, d), lambda i: (i, 0)), pl.BlockSpec((1, d), lambda i: (0, 0))],
        out_specs=pl.BlockSpec((tm, d), lambda i: (i, 0)),
        out_shape=jax.ShapeDtypeStruct((t, d), F32),
        scratch_shapes=[pltpu.VMEM((tm * PACK_ROWS, LANES), F32), pltpu.VMEM((tm * PACK_ROWS, LANES), F32)],
        compiler_params=_cparams(("parallel",)),
        name="combine_final",
    )(ys, gate, x, g.reshape(1, d))


def _routing_tables(top_idx, n_super):
    flat_e = top_idx.reshape(-1)
    n = flat_e.shape[0]
    onehot = (flat_e[:, None] == jnp.arange(N_EXPERTS, dtype=jnp.int32)[None, :]).astype(jnp.int32)
    csum = jnp.cumsum(onehot, axis=0)
    rank = jnp.sum(onehot * csum, axis=1) - 1
    counts = csum[-1]
    nsb = (counts + MOE_SUPER - 1) // MOE_SUPER
    sb_end = jnp.cumsum(nsb)
    sb_start = sb_end - nsb
    dest = sb_start[flat_e] * MOE_SUPER + rank
    row_src = jnp.full((n_super * MOE_SUPER,), -1, jnp.int32).at[dest].set(jnp.arange(n, dtype=jnp.int32))
    s_ids = jnp.arange(n_super, dtype=jnp.int32)
    sb_e = jnp.minimum(jnp.searchsorted(sb_end, s_ids, side="right"), N_EXPERTS - 1).astype(jnp.int32)
    local = s_ids - sb_start[sb_e]
    sb_rows = jnp.clip(counts[sb_e] - local * MOE_SUPER, 0, MOE_SUPER)
    sb_rows = jnp.where(s_ids < sb_end[-1], sb_rows, 0).astype(jnp.int32)
    row_off = jnp.where(row_src >= 0, (row_src // TOP_K) * PACK_ROWS, 0)
    return sb_e, sb_rows, row_src.reshape(n_super, 1, MOE_SUPER), row_off.reshape(n_super, 1, MOE_SUPER)


def _pad_rows(a, n):
    return jnp.pad(a, ((0, n - a.shape[0]), (0, 0)))


def _layer(x, mem, l, p, batch, seq):
    t = batch * seq
    c = RWKV_WIDTH
    lam_init = 0.8 - 0.6 * math.exp(-0.3 * l)
    w_in = p["w_in"]
    o1 = 3 * c + DECAY_LORA + AAA_LORA + GATE_LORA
    o2 = o1 + 3 * DIFF_WIDTH
    padc = lambda a, n: jnp.pad(a, ((0, 0), (0, n - a.shape[1])))
    w_rwkv = jnp.concatenate([
        w_in[:, :3 * c],
        padc(w_in[:, 3 * c:3 * c + DECAY_LORA], 128),
        padc(w_in[:, 3 * c + DECAY_LORA:3 * c + DECAY_LORA + AAA_LORA], 128),
        padc(w_in[:, 3 * c + DECAY_LORA + AAA_LORA:o1], 256)], axis=1).astype(BF16)
    mu = p["rwkv_mu"]
    pad1 = lambda a, n: jnp.pad(a, (0, n - a.shape[0]))
    mu_p = jnp.concatenate([mu[:3 * c], pad1(mu[3 * c:3 * c + DECAY_LORA], 128),
                            pad1(mu[3 * c + DECAY_LORA:3 * c + DECAY_LORA + AAA_LORA], 128),
                            pad1(mu[3 * c + DECAY_LORA + AAA_LORA:], 256)])
    w_diff = w_in[:, o1:o2].astype(BF16)
    w_gate = w_in[:, o2:].astype(BF16)

    g_mix = p["norm_mix_g"]
    p_rwkv = norm_matmul(x, g_mix, w_rwkv, tm=1024, tn=RWKV_COLS // 2, out_dtype=F32, name="in_proj_rwkv")
    qkv = norm_matmul(x, g_mix, w_diff, tm=1024, tn=1024, out_dtype=BF16, name="in_proj_diff")
    gates = norm_matmul(x, g_mix, w_gate, tm=1024, tn=1024, out_dtype=BF16, act="sigmoid", name="in_proj_gate")

    prep = rwkv_prep(p_rwkv, mu_p, p["rwkv_w0"], p["rwkv_a0"], p["rwkv_k_k"], p["rwkv_k_a"],
                     _pad_rows(p["rwkv_w2"], 128), _pad_rows(p["rwkv_a2"], 128), _pad_rows(p["rwkv_g2"], 256),
                     seq=seq, tm=256)
    ra, w1b, w2b = rwkv_scan(*prep, p["rwkv_ln_w"], p["rwkv_ln_b"], p["rwkv_r_k"].reshape(-1),
                             p["expert_w1"], p["expert_w2"], batch=batch, seq=seq)

    slopes = (2.0 ** (-8.0 * jnp.arange(1, DIFF_HEADS + 1, dtype=F32) / DIFF_HEADS)).astype(F32)
    da = diff_attention(qkv, slopes, p["diff_lq1"], p["diff_lk1"], p["diff_lq2"], p["diff_lk2"],
                        p["diff_subln_g"], batch=batch, seq=seq, tq=min(1024, seq), lam_init=lam_init)

    x1 = mix_project(ra, da, gates, x, p["rwkv_proj"].astype(BF16), p["diff_proj"].astype(BF16),
                     p["w_out"].astype(BF16), tm=256)

    m_len = mem.shape[0] // batch
    kv = norm_matmul(mem, p["norm_mem_g"], p["cross_wkv"].astype(BF16), tm=min(512, mem.shape[0]),
                     tn=2 * CROSS_WIDTH, out_dtype=BF16, name="cross_kv")
    rw = jnp.pad(p["router_w"], ((0, 0), (0, LANES - N_EXPERTS)))
    rb = jnp.pad(p["router_b"], (0, LANES - N_EXPERTS)).reshape(1, LANES)
    x2, hf, logits = cross_attention(x1, p["norm_cross_g"], p["cross_wq"].astype(BF16),
                                     kv.reshape(batch, m_len, 2 * CROSS_WIDTH), p["cross_wo"].astype(BF16),
                                     p["norm_ffn_g"], rw, rb, seq=seq, tm=256)

    top_idx, gate = router_topk(logits, tm=min(1024, t))
    n_super = (t * TOP_K) // MOE_SUPER + N_EXPERTS
    sb_e, sb_rows, row_src, row_off = _routing_tables(top_idx, n_super)
    ys = moe_experts(sb_e, sb_rows, row_src, row_off, hf, w1b, p["expert_b1"][:, None, :], w2b, p["expert_b2"][:, None, :])
    gate_b = jnp.broadcast_to(gate[:, :, None], (t, TOP_K, LANES))
    return x2, ys.reshape(t, TOP_K * PACK_ROWS, LANES), gate_b


def kernel(x, mem, norm_mix_g, w_in, rwkv_mu, rwkv_w0, rwkv_w2, rwkv_a0, rwkv_a2, rwkv_g2, rwkv_k_k, rwkv_k_a, rwkv_r_k, rwkv_ln_w, rwkv_ln_b, rwkv_proj, diff_lq1, diff_lk1, diff_lq2, diff_lk2, diff_subln_g, diff_proj, w_out, norm_cross_g, norm_mem_g, cross_wq, cross_wkv, cross_wo, norm_ffn_g, router_w, router_b, expert_w1, expert_b1, expert_w2, expert_b2, final_norm_g):
    batch, seq, d = x.shape
    stacked = dict(norm_mix_g=norm_mix_g, w_in=w_in, rwkv_mu=rwkv_mu, rwkv_w0=rwkv_w0, rwkv_w2=rwkv_w2,
                   rwkv_a0=rwkv_a0, rwkv_a2=rwkv_a2, rwkv_g2=rwkv_g2, rwkv_k_k=rwkv_k_k, rwkv_k_a=rwkv_k_a,
                   rwkv_r_k=rwkv_r_k, rwkv_ln_w=rwkv_ln_w, rwkv_ln_b=rwkv_ln_b, rwkv_proj=rwkv_proj,
                   diff_lq1=diff_lq1, diff_lk1=diff_lk1, diff_lq2=diff_lq2, diff_lk2=diff_lk2,
                   diff_subln_g=diff_subln_g, diff_proj=diff_proj, w_out=w_out, norm_cross_g=norm_cross_g,
                   norm_mem_g=norm_mem_g, cross_wq=cross_wq, cross_wkv=cross_wkv, cross_wo=cross_wo,
                   norm_ffn_g=norm_ffn_g, router_w=router_w, router_b=router_b, expert_w1=expert_w1,
                   expert_b1=expert_b1, expert_w2=expert_w2, expert_b2=expert_b2)
    assert w_in.shape[0] == 1, "the closing RMSNorm is fused into the single layer's combine"
    p = {k: v[0] for k, v in stacked.items()}
    x2, ys, gate_b = _layer(x.reshape(batch * seq, d), mem.reshape(-1, d), 0, p, batch, seq)
    out = combine_final(ys, gate_b, x2, final_norm_g, tm=256)
    return out.reshape(batch, seq, d)
```

```python
import functools
import math

import jax
import jax.numpy as jnp
from jax import lax
from jax.experimental import pallas as pl
from jax.experimental.pallas import tpu as pltpu

F32 = jnp.float32
BF16 = jnp.bfloat16
U32 = jnp.uint32

D_MODEL = 2048
NORM_EPS = 1e-5
LOG2E = 1.4426950408889634
LANES = 128
HEAD_DIM = 64
RWKV_WIDTH = 1024
SLAB = 256
RWKV_SLABS = RWKV_WIDTH // SLAB
HEADS_PER_SLAB = SLAB // HEAD_DIM
CHUNK = 64
RWKV_GN_EPS = 64e-5
DECAY_LORA = 64
AAA_LORA = 64
GATE_LORA = 160
LORA_PAD = 512
RWKV_COLS = 3 * RWKV_WIDTH + LORA_PAD
DIFF_WIDTH = 1024
DIFF_HEADS = 8
CROSS_HEADS = 4
CROSS_WIDTH = 512
N_EXPERTS = 32
TOP_K = 4
D_EXPERT = 2048
SWIGLU_ALPHA = 1.702
SWIGLU_LIMIT = 7.0
MOE_SUPER = 1024
MOE_SUB = 256
MOE_FT = 1024
W2_GROUP = 128
ROW_TILE = D_MODEL // LANES
PACK_ROWS = ROW_TILE // 2
VMEM_LIMIT = 56 * 1024 * 1024

NN = (((1,), (0,)), ((), ()))
NT = (((1,), (1,)), ((), ()))


def _dot(a, b, dims=NN):
    return lax.dot_general(a, b, dims, preferred_element_type=F32)


def _split2(a):
    hi = a.astype(BF16)
    lo = (a - hi.astype(F32)).astype(BF16)
    return hi, lo


def _split3(a):
    hi = a.astype(BF16)
    r1 = a - hi.astype(F32)
    mid = r1.astype(BF16)
    lo = (r1 - mid.astype(F32)).astype(BF16)
    return hi, mid, lo


def _dot3(a, b, dims=NN):
    ah, al = _split2(a)
    bh, bl = _split2(b)
    return _dot(ah, bh, dims) + (_dot(ah, bl, dims) + _dot(al, bh, dims))


def _dot_exact_rhs(a, b_bf16, dims=NN):
    h, m, l = _split3(a)
    return _dot(h, b_bf16, dims) + (_dot(m, b_bf16, dims) + _dot(l, b_bf16, dims))


def _pack_pair(lo, hi):
    lo_b = lax.shift_right_logical(lax.bitcast_convert_type(lo.astype(BF16).astype(F32), U32), jnp.uint32(16))
    hi_b = lax.bitcast_convert_type(hi.astype(BF16).astype(F32), U32) & jnp.uint32(0xFFFF0000)
    return hi_b | lo_b


def _unpack_pair(w):
    lo = lax.bitcast_convert_type(lax.shift_left(w, jnp.uint32(16)), F32)
    hi = lax.bitcast_convert_type(w & jnp.uint32(0xFFFF0000), F32)
    return lo, hi


def _rms(x, g, eps):
    ms = jnp.mean(x * x, axis=-1, keepdims=True)
    return x * lax.rsqrt(ms + eps) * g


def _cparams(sem):
    return pltpu.CompilerParams(dimension_semantics=sem, vmem_limit_bytes=VMEM_LIMIT)


def _norm_matmul_kernel(x_ref, g_ref, w_ref, o_ref, h_ref, *, act):
    @pl.when(pl.program_id(1) == 0)
    def _():
        h_ref[...] = _rms(x_ref[...], g_ref[...], NORM_EPS).astype(BF16)

    y = _dot(h_ref[...], w_ref[...])
    if act == "sigmoid":
        y = jax.nn.sigmoid(y)
    o_ref[...] = y.astype(o_ref.dtype)


def norm_matmul(x, g, w, *, tm, tn, out_dtype, act=None, name):
    m, d = x.shape
    n = w.shape[1]
    return pl.pallas_call(
        functools.partial(_norm_matmul_kernel, act=act),
        grid=(m // tm, n // tn),
        in_specs=[
            pl.BlockSpec((tm, d), lambda i, j: (i, 0)),
            pl.BlockSpec((1, d), lambda i, j: (0, 0)),
            pl.BlockSpec((d, tn), lambda i, j: (0, j)),
        ],
        out_specs=pl.BlockSpec((tm, tn), lambda i, j: (i, j)),
        out_shape=jax.ShapeDtypeStruct((m, n), out_dtype),
        scratch_shapes=[pltpu.VMEM((tm, d), BF16)],
        compiler_params=_cparams(("parallel", "arbitrary")),
        name=name,
    )(x, g.reshape(1, d), w)


def _head_ones():
    r = lax.broadcasted_iota(jnp.int32, (SLAB, SLAB), 0)
    c = lax.broadcasted_iota(jnp.int32, (SLAB, SLAB), 1)
    return ((r // HEAD_DIM) == (c // HEAD_DIM)).astype(BF16)


def _rwkv_prep_kernel(p_ref, prev_ref, mu_ref, w0_ref, a0_ref, kk_ref, ka_ref, w2_ref, a2_ref, g2_ref,
                      r_out, lw_out, k_out, v_out, kkn_out, b_out, g_out, *, tiles_per_seq):
    c = RWKV_WIDTH
    i = pl.program_id(0)
    p = p_ref[...]
    tm = p.shape[0]
    first = (i % tiles_per_seq) == 0
    prev_row = jnp.where(first, 0.0, prev_ref[7:8, :])
    row = lax.broadcasted_iota(jnp.int32, p.shape, 0)
    shifted = jnp.where(row == 0, prev_row, pltpu.roll(p, 1, axis=0))
    ps = p + (shifted - p) * mu_ref[...]
    r = ps[:, 0:c]
    k = ps[:, c:2 * c]
    v = ps[:, 2 * c:3 * c]
    wd = ps[:, 3 * c:3 * c + 128]
    ad = ps[:, 3 * c + 128:3 * c + 256]
    gd = ps[:, 3 * c + 256:3 * c + 512]
    z = -(w0_ref[...] + _dot3(jnp.tanh(wd), w2_ref[...]))
    softplus = jnp.maximum(z, 0.0) + jnp.log1p(jnp.exp(-jnp.abs(z)))
    w = -softplus - 0.5
    lw = -jnp.exp(w)
    a = jax.nn.sigmoid(a0_ref[...] + _dot3(ad, a2_ref[...]))
    g = _dot3(jax.nn.sigmoid(gd), g2_ref[...])
    kkr = k * kk_ref[...]
    k2 = k * (1.0 + (a - 1.0) * ka_ref[...])
    ones = _head_ones()
    for q in range(RWKV_SLABS):
        sl = slice(q * SLAB, (q + 1) * SLAB)
        x = kkr[:, sl]
        ss = _dot_exact_rhs(x * x, ones)
        kkn = x / jnp.maximum(jnp.sqrt(ss), 1e-12)
        r_out[q] = r[:, sl]
        lw_out[q] = lw[:, sl]
        k_out[q] = k2[:, sl]
        v_out[q] = v[:, sl]
        kkn_out[q] = kkn
        b_out[q] = kkn * a[:, sl]
        g_out[q] = g[:, sl]


def rwkv_prep(p, mu, w0, a0, k_k, k_a, w2p, a2p, g2p, *, seq, tm):
    t, cols = p.shape
    c = RWKV_WIDTH
    vec = lambda n: pl.BlockSpec((1, n), lambda i: (0, 0))
    full = lambda a: pl.BlockSpec(a.shape, lambda i: (0, 0))
    out_spec = pl.BlockSpec((RWKV_SLABS, tm, SLAB), lambda i: (0, i, 0))
    out_shape = jax.ShapeDtypeStruct((RWKV_SLABS, t, SLAB), F32)
    return pl.pallas_call(
        functools.partial(_rwkv_prep_kernel, tiles_per_seq=seq // tm),
        grid=(t // tm,),
        in_specs=[
            pl.BlockSpec((tm, cols), lambda i: (i, 0)),
            pl.BlockSpec((8, cols), lambda i: (jnp.maximum(i * (tm // 8) - 1, 0), 0)),
            vec(cols), vec(c), vec(c), vec(c), vec(c), full(w2p), full(a2p), full(g2p),
        ],
        out_specs=[out_spec] * 7,
        out_shape=[out_shape] * 7,
        compiler_params=_cparams(("parallel",)),
        name="rwkv_prep",
    )(p, p, mu.reshape(1, cols), w0.reshape(1, c), a0.reshape(1, c), k_k.reshape(1, c), k_a.reshape(1, c),
      w2p, a2p, g2p)


def _dot1(a, b, dims=NN):
    return _dot(a.astype(BF16), b.astype(BF16), dims)


def _dot3s(a, b, dims=NN):
    ah, al = _split2(a)
    bh, bl = _split2(b)
    m = a.shape[0]
    lhs = jnp.concatenate([ah, al], axis=0)
    if dims is NN:
        n = b.shape[1]
        rhs = jnp.concatenate([bh, bl], axis=1)
    else:
        n = b.shape[0]
        rhs = jnp.concatenate([bh, bl], axis=0)
    p = _dot(lhs, rhs, dims)
    return p[:m, :n] + (p[:m, n:] + p[m:, :n])


def _sum3_exact_rhs(xs, ones):
    parts = []
    for x in xs:
        parts.extend(_split3(x))
    res = _dot(jnp.concatenate(parts, axis=0), ones)
    L = xs[0].shape[0]
    return [res[(3 * i) * L:(3 * i + 1) * L] + (res[(3 * i + 1) * L:(3 * i + 2) * L] + res[(3 * i + 2) * L:(3 * i + 3) * L])
            for i in range(len(xs))]


def _expert_weight_chunk(w1_ref, w2a_ref, w2b_ref, w1o_ref, w2o_ref, tmp_ref):
    w1o_ref[0] = w1_ref[0].astype(BF16)
    half = W2_GROUP
    for c in range(ROW_TILE):
        cols = slice(c * LANES, (c + 1) * LANES)
        tmp_ref[c, pl.ds(0, half, stride=2), :] = w2a_ref[0, :, cols]
        tmp_ref[c, pl.ds(1, half, stride=2), :] = w2b_ref[0, :, cols]
        w2o_ref[0, :, cols] = tmp_ref[c].astype(BF16)


def _expert_weight_views(w1, w2):
    e, d, f2 = w1.shape
    fdim = w2.shape[1]
    groups_per_half = (MOE_FT // 2) // W2_GROUP
    n_chunks = e * (fdim // MOE_FT) * groups_per_half
    w1v = w1.reshape(n_chunks, (e * d) // n_chunks, f2)
    w2v = w2.reshape(e * fdim // W2_GROUP, W2_GROUP, d)
    return n_chunks, groups_per_half, w1v, w2v


def _expert_weight_specs(n_chunks, groups_per_half, w1v, w2v, step):
    d = w2v.shape[2]
    a_idx = lambda j: (j // groups_per_half) * (2 * groups_per_half) + j % groups_per_half
    in_specs = [pl.BlockSpec((1,) + w1v.shape[1:], lambda *ids: (step(*ids), 0, 0)),
                pl.BlockSpec((1, W2_GROUP, d), lambda *ids: (a_idx(step(*ids)), 0, 0)),
                pl.BlockSpec((1, W2_GROUP, d), lambda *ids: (a_idx(step(*ids)) + groups_per_half, 0, 0))]
    out_specs = [pl.BlockSpec((1,) + w1v.shape[1:], lambda *ids: (step(*ids), 0, 0)),
                 pl.BlockSpec((1, 2 * W2_GROUP, d), lambda *ids: (step(*ids), 0, 0))]
    out_shape = [jax.ShapeDtypeStruct(w1v.shape, BF16), jax.ShapeDtypeStruct((n_chunks, 2 * W2_GROUP, d), BF16)]
    scratch = pltpu.VMEM((ROW_TILE, 2 * W2_GROUP, LANES), F32)
    return in_specs, out_specs, out_shape, scratch


def expert_weight_prep(w1, w2):
    n_chunks, gph, w1v, w2v = _expert_weight_views(w1, w2)
    in_specs, out_specs, out_shape, scratch = _expert_weight_specs(n_chunks, gph, w1v, w2v, lambda j: j)
    w1b, w2b = pl.pallas_call(
        _expert_weight_chunk, grid=(n_chunks,), in_specs=in_specs, out_specs=out_specs, out_shape=out_shape,
        scratch_shapes=[scratch], compiler_params=_cparams(("parallel",)), name="expert_weight_prep",
    )(w1v, w2v, w2v)
    return w1b.reshape(w1.shape), w2b.reshape(w2.shape)


def _rwkv_scan_kernel(r_ref, lw_ref, k_ref, v_ref, kk_ref, b_ref, g_ref, lnw_ref, lnb_ref, rk_ref, *rest):
    if len(rest) == 2:
        o_ref, st_ref = rest
    else:
        w1_ref, w2a_ref, w2b_ref, o_ref, w1o_ref, w2o_ref, st_ref, tmp_ref = rest
        _expert_weight_chunk(w1_ref, w2a_ref, w2b_ref, w1o_ref, w2o_ref, tmp_ref)
    L = CHUNK
    W = SLAB

    @pl.when(pl.program_id(1) == 0)
    def _():
        st_ref[...] = jnp.zeros_like(st_ref)

    t_i = lax.broadcasted_iota(jnp.int32, (L, W), 0)
    lane = lax.broadcasted_iota(jnp.int32, (L, W), 1)
    j_i = lane % HEAD_DIM
    hid = lane // HEAD_DIM
    strict = j_i < t_i
    incl = j_i <= t_i
    eye = (j_i == t_i).astype(F32)
    r2 = lax.broadcasted_iota(jnp.int32, (W, W), 0)
    c2 = lax.broadcasted_iota(jnp.int32, (W, W), 1)
    same_head = (r2 // HEAD_DIM) == (c2 // HEAD_DIM)
    diag = r2 == c2
    ones = same_head.astype(BF16)
    tr = lax.broadcasted_iota(jnp.int32, (L, L), 0)
    tc = lax.broadcasted_iota(jnp.int32, (L, L), 1)
    tri = (tc <= tr).astype(BF16)

    def bd(x):
        return jnp.concatenate([jnp.where(hid == h, x, 0.0) for h in range(HEADS_PER_SLAB)], axis=0)

    U = range(RWKV_SLABS)
    cat0 = lambda xs: jnp.concatenate(xs, axis=0)
    cat1 = lambda xs: jnp.concatenate(xs, axis=1)
    r = [r_ref[u] for u in U]
    lw = [lw_ref[u] for u in U]
    k = [k_ref[u] for u in U]
    v = [v_ref[u] for u in U]
    c3 = [_dot(tri, cat1(_split3(lw[u]))) for u in U]
    cum = [c3[u][:, 0:W] + (c3[u][:, W:2 * W] + c3[u][:, 2 * W:]) for u in U]
    cum_end = [cum[u][L - 1:L, :] for u in U]
    e_cum = [jnp.exp(cum[u]) for u in U]
    e_neg = [jnp.exp(-cum[u]) for u in U]
    e_end = [jnp.exp(cum_end[u] - cum[u]) for u in U]
    at = [-kk_ref[u] * jnp.exp(cum[u] - lw[u]) for u in U]
    rt = [r[u] * e_cum[u] for u in U]
    bt = [b_ref[u] * e_neg[u] for u in U]
    kt = [k[u] * e_neg[u] for u in U]
    bh = [b_ref[u] * e_end[u] for u in U]
    kh = [k[u] * e_end[u] for u in U]
    w_end = [jnp.exp(cum_end[u]) for u in U]

    g_all = [_dot1(cat0([at[u], rt[u]]), cat0([bd(bt[u]), bd(kt[u])]), NT) for u in U]
    a_ab = [jnp.where(strict, g_all[u][0:L, 0:W], 0.0) for u in U]
    a_ak = [jnp.where(strict, g_all[u][0:L, W:], 0.0) for u in U]
    a_rb = [jnp.where(incl, g_all[u][L:, 0:W], 0.0) for u in U]
    a_rk = [jnp.where(incl, g_all[u][L:, W:], 0.0) for u in U]

    tinv = [eye for u in U]
    pw = a_ab
    for it in range(6):
        if it < 5:
            res = [_dot1(pw[u], cat1([bd(tinv[u]), bd(pw[u])])) for u in U]
            tinv = [tinv[u] + res[u][:, 0:W] for u in U]
            pw = [res[u][:, W:] for u in U]
        else:
            tinv = [tinv[u] + _dot1(pw[u], bd(tinv[u])) for u in U]

    bdv = [bd(v[u]) for u in U]
    av = [_dot1(a_ak[u], bdv[u]) for u in U]
    qp = [_dot1(tinv[u], cat1([bd(at[u]), bd(av[u])])) for u in U]
    q1 = [qp[u][:, 0:W] for u in U]
    p1 = [qp[u][:, W:] for u in U]
    qp2 = [_dot1(cat1([a_rb[u], a_rk[u]]),
                  cat0([cat1([bd(q1[u]), bd(p1[u])]), cat1([jnp.zeros_like(bdv[u]), bdv[u]])])) for u in U]
    q2 = [rt[u] + qp2[u][:, 0:W] for u in U]
    p2 = [qp2[u][:, W:] for u in U]
    mp = [_dot1(cat0([bh[u], kh[u]]).T,
                 cat0([cat1([q1[u], p1[u]]), cat1([jnp.zeros_like(v[u]), v[u]])])) for u in U]
    m_bd = [jnp.where(same_head, mp[u][:, 0:W], 0.0) + jnp.where(diag, w_end[u], 0.0) for u in U]
    st = [st_ref[u] for u in U]
    st_hl = [_split2(st[u]) for u in U]
    m_hl = [_split2(m_bd[u]) for u in U]
    yq = [_dot(cat0([q2[u].astype(BF16), m_hl[u][0], m_hl[u][1]]), st_hl[u][0]) for u in U]
    y = [yq[u][0:L] + p2[u] for u in U]
    for u in U:
        carry = yq[u][L:L + W] + (yq[u][L + W:] + _dot(m_hl[u][0], st_hl[u][1]))
        st_ref[u] = carry + jnp.where(same_head, mp[u][:, W:], 0.0)

    sums = [_sum3_exact_rhs([y[u], r[u] * k[u] * rk_ref[u]], ones) for u in U]
    yc = [y[u] - sums[u][0] * (1.0 / HEAD_DIM) for u in U]
    var = [_sum3_exact_rhs([yc[u] * yc[u]], ones)[0] for u in U]
    for u in U:
        yn = yc[u] * lax.rsqrt(var[u] * (1.0 / HEAD_DIM) + RWKV_GN_EPS) * lnw_ref[u] + lnb_ref[u]
        o_ref[u] = ((yn + sums[u][1] * v[u]) * g_ref[u]).astype(o_ref.dtype)


def rwkv_scan(r, lw, k, v, kk, b, g, ln_w, ln_b, r_k, w1, w2, *, batch, seq):
    nchunk = seq // CHUNK
    blk = pl.BlockSpec((RWKV_SLABS, CHUNK, SLAB), lambda bi, ci: (0, bi * nchunk + ci, 0))
    par = pl.BlockSpec((RWKV_SLABS, 1, SLAB), lambda bi, ci: (0, 0, 0))
    t = batch * seq
    out_shape = jax.ShapeDtypeStruct((RWKV_SLABS, t, SLAB), BF16)
    state = pltpu.VMEM((RWKV_SLABS, SLAB, SLAB), F32)
    args = (r, lw, k, v, kk, b, g, ln_w.reshape(RWKV_SLABS, 1, SLAB), ln_b.reshape(RWKV_SLABS, 1, SLAB),
            r_k.reshape(RWKV_SLABS, 1, SLAB))
    n_chunks, gph, w1v, w2v = _expert_weight_views(w1, w2)
    if n_chunks != batch * nchunk:
        ra = pl.pallas_call(
            _rwkv_scan_kernel, grid=(batch, nchunk), in_specs=[blk] * 7 + [par] * 3, out_specs=blk,
            out_shape=out_shape, scratch_shapes=[state],
            compiler_params=_cparams(("arbitrary", "arbitrary")), name="rwkv_scan",
        )(*args)
        return (ra,) + expert_weight_prep(w1, w2)
    w_in, w_out, w_shape, w_scratch = _expert_weight_specs(n_chunks, gph, w1v, w2v, lambda bi, ci: bi * nchunk + ci)
    ra, w1b, w2b = pl.pallas_call(
        _rwkv_scan_kernel, grid=(batch, nchunk), in_specs=[blk] * 7 + [par] * 3 + w_in,
        out_specs=[blk] + w_out, out_shape=[out_shape] + w_shape, scratch_shapes=[state, w_scratch],
        compiler_params=_cparams(("arbitrary", "arbitrary")), name="rwkv_scan",
    )(*args, w1v, w2v, w2v)
    return ra, w1b.reshape(w1.shape), w2b.reshape(w2.shape)


def _diff_attn_kernel(qi_ref, ki_ref, slope_ref, q_ref, k_ref, v_ref, lq1_ref, lk1_ref, lq2_ref, lk2_ref, sg_ref,
                      o_ref, qs_ref, relb_ref, m_ref, l_ref, acc_ref, *, tq, lam_init):
    h = pl.program_id(1)
    qi = qi_ref[pl.program_id(2)]
    ki = ki_ref[pl.program_id(2)]
    c2 = slope_ref[h] * LOG2E

    @pl.when(ki == 0)
    def _():
        m_ref[...] = jnp.full_like(m_ref, -jnp.inf)
        l_ref[...] = jnp.zeros_like(l_ref)
        acc_ref[...] = jnp.zeros_like(acc_ref)
        q = q_ref[...].astype(F32) * (HEAD_DIM ** -0.5 * LOG2E)
        lane = lax.broadcasted_iota(jnp.int32, q.shape, 1)
        m0 = lane < HEAD_DIM
        qs_ref[0:tq, :] = jnp.where(m0, q, 0.0).astype(BF16)
        qs_ref[tq:, :] = jnp.where(m0, 0.0, q).astype(BF16)

    @pl.when(pl.program_id(2) == 0)
    def _():
        rel = (lax.broadcasted_iota(jnp.int32, (tq, tq), 1) - lax.broadcasted_iota(jnp.int32, (tq, tq), 0))
        relb = rel.astype(F32) * (-c2)
        relb_ref[:, 0:tq] = relb
        relb_ref[:, tq:] = relb

    def step(masked):
        k = k_ref[...]
        vt = v_ref[...].T
        relb = relb_ref[...]
        tile_bias = c2 * ((qi - ki) * tq).astype(F32)
        s = _dot(k, qs_ref[...], NT) + relb
        if masked:
            s = jnp.where(relb > 0.0, -jnp.inf, s)
        m_prev = m_ref[...]
        m_new = jnp.maximum(m_prev, jnp.max(s, axis=0, keepdims=True) - tile_bias)
        alpha = jnp.exp2(m_prev - m_new)
        p = jnp.exp2(s - (m_new + tile_bias))
        l_ref[...] = alpha * l_ref[...] + jnp.sum(p, axis=0, keepdims=True)
        acc_ref[...] = alpha * acc_ref[...] + _dot(vt, p.astype(BF16))
        m_ref[...] = m_new

    @pl.when(ki < qi)
    def _():
        step(False)

    @pl.when(ki == qi)
    def _():
        step(True)
        lam = (jnp.exp(jnp.sum(lq1_ref[...] * lk1_ref[...], axis=-1, keepdims=True))
               - jnp.exp(jnp.sum(lq2_ref[...] * lk2_ref[...], axis=-1, keepdims=True)) + lam_init)
        ot = acc_ref[:, 0:tq] / l_ref[:, 0:tq] - lam * (acc_ref[:, tq:] / l_ref[:, tq:])
        o = _rms(ot.T, sg_ref[...], NORM_EPS) * (1.0 - lam_init)
        o_ref[...] = o.astype(o_ref.dtype)


def diff_attention(qkv, slopes, lq1, lk1, lq2, lk2, subln_g, *, batch, seq, tq, lam_init):
    t = batch * seq
    nq = seq // tq
    hb = DIFF_WIDTH // LANES
    pairs = [(qi, ki) for qi in range(nq) for ki in range(qi + 1)]
    qi_tab = jnp.asarray([pq for pq, _ in pairs], jnp.int32)
    ki_tab = jnp.asarray([pk for _, pk in pairs], jnp.int32)
    small = pl.BlockSpec((1, HEAD_DIM), lambda b, h, j, qt, kt: (0, 0))
    grid_spec = pltpu.PrefetchScalarGridSpec(
        num_scalar_prefetch=2,
        grid=(batch, DIFF_HEADS, len(pairs)),
        in_specs=[
            pl.BlockSpec(memory_space=pltpu.SMEM),
            pl.BlockSpec((tq, LANES), lambda b, h, j, qt, kt: (b * nq + qt[j], h)),
            pl.BlockSpec((tq, LANES), lambda b, h, j, qt, kt: (b * nq + kt[j], hb + h)),
            pl.BlockSpec((tq, LANES), lambda b, h, j, qt, kt: (b * nq + kt[j], 2 * hb + h)),
            small, small, small, small,
            pl.BlockSpec((1, LANES), lambda b, h, j, qt, kt: (0, 0)),
        ],
        out_specs=pl.BlockSpec((tq, LANES), lambda b, h, j, qt, kt: (b * nq + qt[j], h)),
        scratch_shapes=[pltpu.VMEM((2 * tq, LANES), BF16), pltpu.VMEM((tq, 2 * tq), F32),
                        pltpu.VMEM((1, 2 * tq), F32), pltpu.VMEM((1, 2 * tq), F32),
                        pltpu.VMEM((LANES, 2 * tq), F32)],
    )
    return pl.pallas_call(
        functools.partial(_diff_attn_kernel, tq=tq, lam_init=lam_init),
        grid_spec=grid_spec,
        out_shape=jax.ShapeDtypeStruct((t, DIFF_WIDTH), BF16),
        compiler_params=_cparams(("parallel", "parallel", "arbitrary")),
        name="diff_attention",
    )(qi_tab, ki_tab, slopes, qkv, qkv, qkv, lq1.reshape(1, -1), lk1.reshape(1, -1), lq2.reshape(1, -1),
      lk2.reshape(1, -1), subln_g.reshape(1, -1))


def _mix_kernel(ra_ref, da_ref, ga_ref, gb_ref, x_ref, wa_ref, wb_ref, wo_ref, o_ref):
    ya = _dot(ra_ref[0], wa_ref[0:SLAB, :])
    for q in range(1, RWKV_SLABS):
        ya = ya + _dot(ra_ref[q], wa_ref[q * SLAB:(q + 1) * SLAB, :])
    yb = _dot(da_ref[...], wb_ref[...])
    mixed = ga_ref[...].astype(F32) * ya + gb_ref[...].astype(F32) * yb
    o_ref[...] = x_ref[...] + _dot(mixed.astype(BF16), wo_ref[...])


def mix_project(ra, da, gates, x, wa, wb, wo, *, tm):
    t, d = x.shape
    const = lambda a: pl.BlockSpec(a.shape, lambda i: (0, 0), pipeline_mode=pl.Buffered(1))
    return pl.pallas_call(
        _mix_kernel,
        grid=(t // tm,),
        in_specs=[
            pl.BlockSpec((RWKV_SLABS, tm, SLAB), lambda i: (0, i, 0)),
            pl.BlockSpec((tm, DIFF_WIDTH), lambda i: (i, 0)),
            pl.BlockSpec((tm, d), lambda i: (i, 0)),
            pl.BlockSpec((tm, d), lambda i: (i, 1)),
            pl.BlockSpec((tm, d), lambda i: (i, 0)),
            const(wa), const(wb), const(wo),
        ],
        out_specs=pl.BlockSpec((tm, d), lambda i: (i, 0)),
        out_shape=jax.ShapeDtypeStruct((t, d), F32),
        compiler_params=_cparams(("parallel",)),
        name="mix_project",
    )(ra, da, gates, gates, x, wa, wb, wo)


def _cross_kernel(x_ref, gc_ref, wq_ref, kv_ref, wo_ref, gf_ref, rw_ref, rb_ref, x2_ref, hf_ref, lg_ref):
    x = x_ref[...]
    h = _rms(x, gc_ref[...], NORM_EPS).astype(BF16)
    q = _dot(h, wq_ref[...]).astype(BF16)
    scale = LANES ** -0.5
    outs = []
    for hd in range(CROSS_HEADS):
        qh = q[:, hd * LANES:(hd + 1) * LANES]
        kh = kv_ref[0, :, hd * LANES:(hd + 1) * LANES]
        vh = kv_ref[0, :, CROSS_WIDTH + hd * LANES:CROSS_WIDTH + (hd + 1) * LANES]
        s = _dot(qh, kh, NT) * scale
        s = s - jnp.max(s, axis=-1, keepdims=True)
        e = jnp.exp(s)
        p = e / jnp.sum(e, axis=-1, keepdims=True)
        outs.append(_dot(p.astype(BF16), vh))
    o = jnp.concatenate(outs, axis=1).astype(BF16)
    x2 = x + _dot(o, wo_ref[...])
    x2_ref[...] = x2
    hf = _rms(x2, gf_ref[...], NORM_EPS)
    tm = hf.shape[0]
    for c in range(PACK_ROWS):
        hf_ref[pl.ds(c, tm, stride=PACK_ROWS), :] = _pack_pair(hf[:, 2 * c * LANES:(2 * c + 1) * LANES],
                                                               hf[:, (2 * c + 1) * LANES:(2 * c + 2) * LANES])
    lg_ref[...] = _dot3(hf, rw_ref[...]) + rb_ref[...]


def cross_attention(x, gc, wq, kv, wo, gf, rw, rb, *, seq, tm):
    t, d = x.shape
    per_seq = seq // tm
    const = lambda a: pl.BlockSpec(a.shape, lambda i: (0,) * a.ndim, pipeline_mode=pl.Buffered(1))
    vec = lambda n: pl.BlockSpec((1, n), lambda i: (0, 0))
    return pl.pallas_call(
        _cross_kernel,
        grid=(t // tm,),
        in_specs=[
            pl.BlockSpec((tm, d), lambda i: (i, 0)),
            vec(d), const(wq),
            pl.BlockSpec((1,) + kv.shape[1:], lambda i: (i // per_seq, 0, 0)),
            const(wo), vec(d), const(rw), vec(LANES),
        ],
        out_specs=[pl.BlockSpec((tm, d), lambda i: (i, 0)), pl.BlockSpec((tm * PACK_ROWS, LANES), lambda i: (i, 0)),
                   pl.BlockSpec((tm, LANES), lambda i: (i, 0))],
        out_shape=[jax.ShapeDtypeStruct((t, d), F32), jax.ShapeDtypeStruct((t * PACK_ROWS, LANES), U32),
                   jax.ShapeDtypeStruct((t, LANES), F32)],
        compiler_params=_cparams(("parallel",)),
        name="cross_attention",
    )(x, gc.reshape(1, d), wq, kv, wo, gf.reshape(1, d), rw, rb)


def _router_kernel(lg_ref, idx_ref, gate_ref):
    x = lg_ref[...]
    lane = lax.broadcasted_iota(jnp.int32, x.shape, 1)
    x = jnp.where(lane < N_EXPERTS, x, -jnp.inf)
    idx_out = jnp.zeros(x.shape, jnp.int32)
    val_out = jnp.zeros(x.shape, F32)
    vals = []
    for j in range(TOP_K):
        m = jnp.max(x, axis=-1, keepdims=True)
        sel = jnp.min(jnp.where(x == m, lane, LANES), axis=-1, keepdims=True)
        idx_out = jnp.where(lane == j, sel, idx_out)
        vals.append(m)
        x = jnp.where(lane == sel, -jnp.inf, x)
    es = [jnp.exp(vj - vals[0]) for vj in vals]
    tot = es[0] + es[1] + es[2] + es[3]
    for j in range(TOP_K):
        val_out = jnp.where(lane == j, es[j] / tot, val_out)
    idx_ref[...] = idx_out[:, 0:TOP_K]
    gate_ref[...] = val_out[:, 0:TOP_K]


def router_topk(logits, *, tm):
    t = logits.shape[0]
    return pl.pallas_call(
        _router_kernel,
        grid=(t // tm,),
        in_specs=[pl.BlockSpec((tm, LANES), lambda i: (i, 0))],
        out_specs=[pl.BlockSpec((tm, TOP_K), lambda i: (i, 0)), pl.BlockSpec((tm, TOP_K), lambda i: (i, 0))],
        out_shape=[jax.ShapeDtypeStruct((t, TOP_K), jnp.int32), jax.ShapeDtypeStruct((t, TOP_K), F32)],
        compiler_params=_cparams(("parallel",)),
        name="router_topk",
    )(logits)


def _moe_kernel(sbe_ref, sbr_ref, src_ref, tok_ref, ntok_ref, hf_hbm, w1_ref, b1_ref, w2_ref, b2_ref,
                ys_hbm, xbuf, x2d, acc, obuf, gsem, ssem):
    s = pl.program_id(0)
    f = pl.program_id(1)
    nf = pl.num_programs(1)
    nsb = pl.num_programs(0)
    sub_rows = MOE_SUB * PACK_ROWS
    nsubs = MOE_SUPER // MOE_SUB
    unroll = 8

    def ceil_sub(n):
        return lax.shift_right_logical(n + (MOE_SUB - 1), MOE_SUB.bit_length() - 1)

    rows = sbr_ref[s]
    nsub = ceil_sub(rows)
    rows_next = jnp.where(s + 1 < nsb, sbr_ref[jnp.minimum(s + 1, nsb - 1)], 0)
    rows_prev = jnp.where(s > 0, sbr_ref[jnp.maximum(s - 1, 0)], 0)
    nsub_next = ceil_sub(rows_next)

    def slab(ref, off):
        return ref.at[pl.ds(pl.multiple_of(off, PACK_ROWS), PACK_ROWS)]

    def row_in(table, r):
        return pltpu.make_async_copy(slab(hf_hbm, table[0, 0, r]), slab(xbuf, r * PACK_ROWS), gsem)

    def row_out(r):
        return pltpu.make_async_copy(slab(obuf, r * PACK_ROWS), slab(ys_hbm, src_ref[0, 0, r] * PACK_ROWS), ssem)

    def start_gather(table, first, n):
        def group(gi, c):
            for j in range(unroll):
                row_in(table, first + gi * unroll + j).start()
            return c

        lax.fori_loop(0, n // unroll, group, 0)

    def wait_sub_blocks(buf, other, sem, n):
        for sub in range(nsubs):
            @pl.when(sub < n)
            def _(sub=sub):
                pltpu.make_async_copy(other.at[pl.ds(0, sub_rows)], buf.at[pl.ds(sub * sub_rows, sub_rows)], sem).wait()

    def wait_scatter(n):
        full = n // MOE_SUB
        for sub in range(nsubs):
            @pl.when(sub < full)
            def _(sub=sub):
                pltpu.make_async_copy(obuf.at[pl.ds(sub * sub_rows, sub_rows)], ys_hbm.at[pl.ds(0, sub_rows)], ssem).wait()

        def one(r, c):
            row_out(r).wait()
            return c

        lax.fori_loop(0, n - full * MOE_SUB, one, 0)

    @pl.when((f == 0) & (s == 0) & (rows > 0))
    def _():
        start_gather(tok_ref, 0, nsub * MOE_SUB)

    @pl.when((f == 0) & (rows > 0))
    def _():
        wait_sub_blocks(xbuf, hf_hbm, gsem, nsub)

    @pl.when((f == nf - 1) & (rows > 0) & (rows_prev > 0))
    def _():
        wait_scatter(rows_prev)

    for sub in range(nsubs):
        @pl.when(sub < nsub)
        def _(sub=sub):
            sl = slice(sub * MOE_SUB, (sub + 1) * MOE_SUB)
            base = sub * sub_rows

            @pl.when(f == 0)
            def _():
                for c in range(PACK_ROWS):
                    lo, hi = _unpack_pair(xbuf[pl.ds(base + c, MOE_SUB, stride=PACK_ROWS), :])
                    x2d[sl, 2 * c * LANES:(2 * c + 1) * LANES] = lo.astype(BF16)
                    x2d[sl, (2 * c + 1) * LANES:(2 * c + 2) * LANES] = hi.astype(BF16)

            x = x2d[sl, :]
            hb = _dot(x, w1_ref[0]) + b1_ref[0]
            even = (lax.broadcasted_iota(jnp.int32, (MOE_SUB, LANES), 1) % 2) == 0
            acts = []
            for cb in range(MOE_FT // LANES):
                a = hb[:, cb * LANES:(cb + 1) * LANES]
                b = hb[:, MOE_FT + cb * LANES:MOE_FT + (cb + 1) * LANES]
                hg = jnp.where(even, a, pltpu.roll(b, 1, axis=1))
                hl = jnp.where(even, pltpu.roll(a, LANES - 1, axis=1), b)
                xg = jnp.minimum(hg, SWIGLU_LIMIT)
                xl = jnp.clip(hl, -SWIGLU_LIMIT, SWIGLU_LIMIT)
                acts.append((xg * jax.nn.sigmoid(SWIGLU_ALPHA * xg) * (xl + 1.0)).astype(BF16))
            contrib = _dot(jnp.concatenate(acts, axis=1), w2_ref[0])

            @pl.when(f == 0)
            def _():
                acc[sl, :] = contrib

            @pl.when((f > 0) & (f < nf - 1))
            def _():
                acc[sl, :] += contrib

            @pl.when(f == nf - 1)
            def _():
                y = acc[sl, :] + contrib + b2_ref[0]
                for c in range(PACK_ROWS):
                    obuf[pl.ds(base + c, MOE_SUB, stride=PACK_ROWS), :] = _pack_pair(
                        y[:, 2 * c * LANES:(2 * c + 1) * LANES], y[:, (2 * c + 1) * LANES:(2 * c + 2) * LANES])

    @pl.when((f == 0) & (rows_next > 0))
    def _():
        start_gather(ntok_ref, 0, nsub_next * MOE_SUB)

    @pl.when((f == nf - 1) & (rows > 0))
    def _():
        def group(gi, c):
            for j in range(unroll):
                row_out(gi * unroll + j).start()
            return c

        def one(r, c):
            row_out(r).start()
            return c

        ngroups = rows // unroll
        lax.fori_loop(0, ngroups, group, 0)
        lax.fori_loop(ngroups * unroll, rows, one, 0)

        @pl.when(rows_next == 0)
        def _():
            wait_scatter(rows)


def moe_experts(sb_e, sb_rows, row_src, row_off, hf, w1, b1, w2, b2):
    d = D_MODEL
    t = hf.shape[0] // PACK_ROWS
    nsb = sb_e.shape[0]
    nf = D_EXPERT // MOE_FT
    assert nf >= 2, "the last hidden-column step adds its own contribution to the accumulator of the earlier ones"
    grid_spec = pltpu.PrefetchScalarGridSpec(
        num_scalar_prefetch=2,
        grid=(nsb, nf),
        in_specs=[
            pl.BlockSpec((1, 1, MOE_SUPER), lambda s, f, e, r: (s, 0, 0), memory_space=pltpu.SMEM),
            pl.BlockSpec((1, 1, MOE_SUPER), lambda s, f, e, r: (s, 0, 0), memory_space=pltpu.SMEM),
            pl.BlockSpec((1, 1, MOE_SUPER), lambda s, f, e, r: (jnp.minimum(s + 1, nsb - 1), 0, 0),
                         memory_space=pltpu.SMEM),
            pl.BlockSpec(memory_space=pl.ANY),
            pl.BlockSpec((1, d, 2 * MOE_FT), lambda s, f, e, r: (e[s], 0, f)),
            pl.BlockSpec((1, 1, 2 * MOE_FT), lambda s, f, e, r: (e[s], 0, f)),
            pl.BlockSpec((1, MOE_FT, d), lambda s, f, e, r: (e[s], f, 0)),
            pl.BlockSpec((1, 1, d), lambda s, f, e, r: (e[s], 0, 0)),
        ],
        out_specs=pl.BlockSpec(memory_space=pl.ANY),
        scratch_shapes=[pltpu.VMEM((MOE_SUPER * PACK_ROWS, LANES), U32), pltpu.VMEM((MOE_SUPER, d), BF16),
                        pltpu.VMEM((MOE_SUPER, d), F32), pltpu.VMEM((MOE_SUPER * PACK_ROWS, LANES), U32),
                        pltpu.SemaphoreType.DMA, pltpu.SemaphoreType.DMA],
    )
    return pl.pallas_call(
        _moe_kernel,
        grid_spec=grid_spec,
        out_shape=jax.ShapeDtypeStruct((t * TOP_K * PACK_ROWS, LANES), U32),
        compiler_params=_cparams(("arbitrary", "arbitrary")),
        name="moe_experts",
    )(sb_e, sb_rows, row_src, row_off, row_off, hf, w1, b1, w2, b2)


def _combine_kernel(ys_ref, gate_ref, x_ref, g_ref, o_ref, lo_ref, hi_ref):
    tm = x_ref.shape[0]
    tot_lo = tot_hi = None
    for j in range(TOP_K):
        lo, hi = _unpack_pair(ys_ref[:, j * PACK_ROWS:(j + 1) * PACK_ROWS, :])
        gj = gate_ref[:, j:j + 1, :]
        tot_lo = lo * gj if tot_lo is None else tot_lo + lo * gj
        tot_hi = hi * gj if tot_hi is None else tot_hi + hi * gj
    lo_ref[...] = tot_lo.reshape(tm * PACK_ROWS, LANES)
    hi_ref[...] = tot_hi.reshape(tm * PACK_ROWS, LANES)
    pieces = []
    for c in range(PACK_ROWS):
        pieces.append(lo_ref[pl.ds(c, tm, stride=PACK_ROWS), :])
        pieces.append(hi_ref[pl.ds(c, tm, stride=PACK_ROWS), :])
    out = x_ref[...] + jnp.concatenate(pieces, axis=1)
    o_ref[...] = _rms(out, g_ref[...], NORM_EPS)


def combine_final(ys, gate, x, g, *, tm):
    t, d = x.shape
    return pl.pallas_call(
        _combine_kernel,
        grid=(t // tm,),
        in_specs=[pl.BlockSpec((tm, TOP_K * PACK_ROWS, LANES), lambda i: (i, 0, 0)),
                  pl.BlockSpec((tm, TOP_K, LANES), lambda i: (i, 0, 0)),
                  pl.BlockSpec((tm, d), lambda i: (i, 0)), pl.BlockSpec((1, d), lambda i: (0, 0))],
        out_specs=pl.BlockSpec((tm, d), lambda i: (i, 0)),
        out_shape=jax.ShapeDtypeStruct((t, d), F32),
        scratch_shapes=[pltpu.VMEM((tm * PACK_ROWS, LANES), F32), pltpu.VMEM((tm * PACK_ROWS, LANES), F32)],
        compiler_params=_cparams(("parallel",)),
        name="combine_final",
    )(ys, gate, x, g.reshape(1, d))


def _routing_tables(top_idx, n_super):
    flat_e = top_idx.reshape(-1)
    n = flat_e.shape[0]
    onehot = (flat_e[:, None] == jnp.arange(N_EXPERTS, dtype=jnp.int32)[None, :]).astype(jnp.int32)
    csum = jnp.cumsum(onehot, axis=0)
    rank = jnp.sum(onehot * csum, axis=1) - 1
    counts = csum[-1]
    nsb = (counts + MOE_SUPER - 1) // MOE_SUPER
    sb_end = jnp.cumsum(nsb)
    sb_start = sb_end - nsb
    dest = sb_start[flat_e] * MOE_SUPER + rank
    row_src = jnp.full((n_super * MOE_SUPER,), -1, jnp.int32).at[dest].set(jnp.arange(n, dtype=jnp.int32))
    s_ids = jnp.arange(n_super, dtype=jnp.int32)
    sb_e = jnp.minimum(jnp.searchsorted(sb_end, s_ids, side="right"), N_EXPERTS - 1).astype(jnp.int32)
    local = s_ids - sb_start[sb_e]
    sb_rows = jnp.clip(counts[sb_e] - local * MOE_SUPER, 0, MOE_SUPER)
    sb_rows = jnp.where(s_ids < sb_end[-1], sb_rows, 0).astype(jnp.int32)
    row_off = jnp.where(row_src >= 0, (row_src // TOP_K) * PACK_ROWS, 0)
    return sb_e, sb_rows, row_src.reshape(n_super, 1, MOE_SUPER), row_off.reshape(n_super, 1, MOE_SUPER)


def _pad_rows(a, n):
    return jnp.pad(a, ((0, n - a.shape[0]), (0, 0)))


def _layer(x, mem, l, p, batch, seq):
    t = batch * seq
    c = RWKV_WIDTH
    lam_init = 0.8 - 0.6 * math.exp(-0.3 * l)
    w_in = p["w_in"]
    o1 = 3 * c + DECAY_LORA + AAA_LORA + GATE_LORA
    o2 = o1 + 3 * DIFF_WIDTH
    padc = lambda a, n: jnp.pad(a, ((0, 0), (0, n - a.shape[1])))
    w_rwkv = jnp.concatenate([
        w_in[:, :3 * c],
        padc(w_in[:, 3 * c:3 * c + DECAY_LORA], 128),
        padc(w_in[:, 3 * c + DECAY_LORA:3 * c + DECAY_LORA + AAA_LORA], 128),
        padc(w_in[:, 3 * c + DECAY_LORA + AAA_LORA:o1], 256)], axis=1).astype(BF16)
    mu = p["rwkv_mu"]
    pad1 = lambda a, n: jnp.pad(a, (0, n - a.shape[0]))
    mu_p = jnp.concatenate([mu[:3 * c], pad1(mu[3 * c:3 * c + DECAY_LORA], 128),
                            pad1(mu[3 * c + DECAY_LORA:3 * c + DECAY_LORA + AAA_LORA], 128),
                            pad1(mu[3 * c + DECAY_LORA + AAA_LORA:], 256)])
    w_diff = w_in[:, o1:o2].astype(BF16)
    w_gate = w_in[:, o2:].astype(BF16)

    g_mix = p["norm_mix_g"]
    p_rwkv = norm_matmul(x, g_mix, w_rwkv, tm=1024, tn=RWKV_COLS // 2, out_dtype=F32, name="in_proj_rwkv")
    qkv = norm_matmul(x, g_mix, w_diff, tm=1024, tn=1024, out_dtype=BF16, name="in_proj_diff")
    gates = norm_matmul(x, g_mix, w_gate, tm=1024, tn=1024, out_dtype=BF16, act="sigmoid", name="in_proj_gate")

    prep = rwkv_prep(p_rwkv, mu_p, p["rwkv_w0"], p["rwkv_a0"], p["rwkv_k_k"], p["rwkv_k_a"],
                     _pad_rows(p["rwkv_w2"], 128), _pad_rows(p["rwkv_a2"], 128), _pad_rows(p["rwkv_g2"], 256),
                     seq=seq, tm=256)
    ra, w1b, w2b = rwkv_scan(*prep, p["rwkv_ln_w"], p["rwkv_ln_b"], p["rwkv_r_k"].reshape(-1),
                             p["expert_w1"], p["expert_w2"], batch=batch, seq=seq)

    slopes = (2.0 ** (-8.0 * jnp.arange(1, DIFF_HEADS + 1, dtype=F32) / DIFF_HEADS)).astype(F32)
    da = diff_attention(qkv, slopes, p["diff_lq1"], p["diff_lk1"], p["diff_lq2"], p["diff_lk2"],
                        p["diff_subln_g"], batch=batch, seq=seq, tq=min(1024, seq), lam_init=lam_init)

    x1 = mix_project(ra, da, gates, x, p["rwkv_proj"].astype(BF16), p["diff_proj"].astype(BF16),
                     p["w_out"].astype(BF16), tm=512)

    m_len = mem.shape[0] // batch
    kv = norm_matmul(mem, p["norm_mem_g"], p["cross_wkv"].astype(BF16), tm=min(512, mem.shape[0]),
                     tn=2 * CROSS_WIDTH, out_dtype=BF16, name="cross_kv")
    rw = jnp.pad(p["router_w"], ((0, 0), (0, LANES - N_EXPERTS)))
    rb = jnp.pad(p["router_b"], (0, LANES - N_EXPERTS)).reshape(1, LANES)
    x2, hf, logits = cross_attention(x1, p["norm_cross_g"], p["cross_wq"].astype(BF16),
                                     kv.reshape(batch, m_len, 2 * CROSS_WIDTH), p["cross_wo"].astype(BF16),
                                     p["norm_ffn_g"], rw, rb, seq=seq, tm=512)

    top_idx, gate = router_topk(logits, tm=min(1024, t))
    n_super = (t * TOP_K) // MOE_SUPER + N_EXPERTS
    sb_e, sb_rows, row_src, row_off = _routing_tables(top_idx, n_super)
    ys = moe_experts(sb_e, sb_rows, row_src, row_off, hf, w1b, p["expert_b1"][:, None, :], w2b, p["expert_b2"][:, None, :])
    gate_b = jnp.broadcast_to(gate[:, :, None], (t, TOP_K, LANES))
    return x2, ys.reshape(t, TOP_K * PACK_ROWS, LANES), gate_b


def kernel(x, mem, norm_mix_g, w_in, rwkv_mu, rwkv_w0, rwkv_w2, rwkv_a0, rwkv_a2, rwkv_g2, rwkv_k_k, rwkv_k_a, rwkv_r_k, rwkv_ln_w, rwkv_ln_b, rwkv_proj, diff_lq1, diff_lk1, diff_lq2, diff_lk2, diff_subln_g, diff_proj, w_out, norm_cross_g, norm_mem_g, cross_wq, cross_wkv, cross_wo, norm_ffn_g, router_w, router_b, expert_w1, expert_b1, expert_w2, expert_b2, final_norm_g):
    batch, seq, d = x.shape
    stacked = dict(norm_mix_g=norm_mix_g, w_in=w_in, rwkv_mu=rwkv_mu, rwkv_w0=rwkv_w0, rwkv_w2=rwkv_w2,
                   rwkv_a0=rwkv_a0, rwkv_a2=rwkv_a2, rwkv_g2=rwkv_g2, rwkv_k_k=rwkv_k_k, rwkv_k_a=rwkv_k_a,
                   rwkv_r_k=rwkv_r_k, rwkv_ln_w=rwkv_ln_w, rwkv_ln_b=rwkv_ln_b, rwkv_proj=rwkv_proj,
                   diff_lq1=diff_lq1, diff_lk1=diff_lk1, diff_lq2=diff_lq2, diff_lk2=diff_lk2,
                   diff_subln_g=diff_subln_g, diff_proj=diff_proj, w_out=w_out, norm_cross_g=norm_cross_g,
                   norm_mem_g=norm_mem_g, cross_wq=cross_wq, cross_wkv=cross_wkv, cross_wo=cross_wo,
                   norm_ffn_g=norm_ffn_g, router_w=router_w, router_b=router_b, expert_w1=expert_w1,
                   expert_b1=expert_b1, expert_w2=expert_w2, expert_b2=expert_b2)
    assert w_in.shape[0] == 1, "the closing RMSNorm is fused into the single layer's combine"
    p = {k: v[0] for k, v in stacked.items()}
    x2, ys, gate_b = _layer(x.reshape(batch * seq, d), mem.reshape(-1, d), 0, p, batch, seq)
    out = combine_final(ys, gate_b, x2, final_norm_g, tm=256)
    return out.reshape(batch, seq, d)
```

```python
import functools
import math

import jax
import jax.numpy as jnp
from jax import lax
from jax.experimental import pallas as pl
from jax.experimental.pallas import tpu as pltpu

F32 = jnp.float32
BF16 = jnp.bfloat16
U32 = jnp.uint32

D_MODEL = 2048
NORM_EPS = 1e-5
LOG2E = 1.4426950408889634
LANES = 128
HEAD_DIM = 64
RWKV_WIDTH = 1024
SLAB = 256
RWKV_SLABS = RWKV_WIDTH // SLAB
HEADS_PER_SLAB = SLAB // HEAD_DIM
CHUNK = 64
SCAN_CHUNKS = 2
RWKV_GN_EPS = 64e-5
DECAY_LORA = 64
AAA_LORA = 64
GATE_LORA = 160
LORA_PAD = 512
RWKV_COLS = 3 * RWKV_WIDTH + LORA_PAD
DIFF_WIDTH = 1024
DIFF_HEADS = 8
CROSS_HEADS = 4
CROSS_WIDTH = 512
N_EXPERTS = 32
TOP_K = 4
D_EXPERT = 2048
SWIGLU_ALPHA = 1.702
SWIGLU_LIMIT = 7.0
MOE_SUPER = 1024
MOE_SUB = 256
MOE_FT = 1024
W2_GROUP = 256
ROW_TILE = D_MODEL // LANES
PACK_ROWS = ROW_TILE // 2
VMEM_LIMIT = 56 * 1024 * 1024

NN = (((1,), (0,)), ((), ()))
NT = (((1,), (1,)), ((), ()))


def _dot(a, b, dims=NN):
    return lax.dot_general(a, b, dims, preferred_element_type=F32)


def _split2(a):
    hi = a.astype(BF16)
    lo = (a - hi.astype(F32)).astype(BF16)
    return hi, lo


def _split3(a):
    hi = a.astype(BF16)
    r1 = a - hi.astype(F32)
    mid = r1.astype(BF16)
    lo = (r1 - mid.astype(F32)).astype(BF16)
    return hi, mid, lo


def _dot3(a, b, dims=NN):
    ah, al = _split2(a)
    bh, bl = _split2(b)
    return _dot(ah, bh, dims) + (_dot(ah, bl, dims) + _dot(al, bh, dims))


def _dot_exact_rhs(a, b_bf16, dims=NN):
    h, m, l = _split3(a)
    return _dot(h, b_bf16, dims) + (_dot(m, b_bf16, dims) + _dot(l, b_bf16, dims))


def _pack_pair(lo, hi):
    lo_b = lax.shift_right_logical(lax.bitcast_convert_type(lo.astype(BF16).astype(F32), U32), jnp.uint32(16))
    hi_b = lax.bitcast_convert_type(hi.astype(BF16).astype(F32), U32) & jnp.uint32(0xFFFF0000)
    return hi_b | lo_b


def _unpack_pair(w):
    lo = lax.bitcast_convert_type(lax.shift_left(w, jnp.uint32(16)), F32)
    hi = lax.bitcast_convert_type(w & jnp.uint32(0xFFFF0000), F32)
    return lo, hi


def _rms(x, g, eps):
    ms = jnp.mean(x * x, axis=-1, keepdims=True)
    return x * lax.rsqrt(ms + eps) * g


def _cparams(sem):
    return pltpu.CompilerParams(dimension_semantics=sem, vmem_limit_bytes=VMEM_LIMIT)


def _norm_matmul_kernel(x_ref, g_ref, w_ref, o_ref, h_ref, *, act):
    @pl.when(pl.program_id(1) == 0)
    def _():
        h_ref[...] = _rms(x_ref[...], g_ref[...], NORM_EPS).astype(BF16)

    y = _dot(h_ref[...], w_ref[...])
    if act == "sigmoid":
        y = jax.nn.sigmoid(y)
    o_ref[...] = y.astype(o_ref.dtype)


def norm_matmul(x, g, w, *, tm, tn, out_dtype, act=None, name):
    m, d = x.shape
    n = w.shape[1]
    return pl.pallas_call(
        functools.partial(_norm_matmul_kernel, act=act),
        grid=(m // tm, n // tn),
        in_specs=[
            pl.BlockSpec((tm, d), lambda i, j: (i, 0)),
            pl.BlockSpec((1, d), lambda i, j: (0, 0)),
            pl.BlockSpec((d, tn), lambda i, j: (0, j)),
        ],
        out_specs=pl.BlockSpec((tm, tn), lambda i, j: (i, j)),
        out_shape=jax.ShapeDtypeStruct((m, n), out_dtype),
        scratch_shapes=[pltpu.VMEM((tm, d), BF16)],
        compiler_params=_cparams(("parallel", "arbitrary")),
        name=name,
    )(x, g.reshape(1, d), w)


def _head_ones():
    r = lax.broadcasted_iota(jnp.int32, (SLAB, SLAB), 0)
    c = lax.broadcasted_iota(jnp.int32, (SLAB, SLAB), 1)
    return ((r // HEAD_DIM) == (c // HEAD_DIM)).astype(BF16)


def _rwkv_prep_kernel(p_ref, prev_ref, mu_ref, w0_ref, a0_ref, kk_ref, ka_ref, w2_ref, a2_ref, g2_ref,
                      r_out, lw_out, k_out, v_out, kkn_out, b_out, g_out, *, tiles_per_seq):
    c = RWKV_WIDTH
    i = pl.program_id(0)
    p = p_ref[...]
    tm = p.shape[0]
    first = (i % tiles_per_seq) == 0
    prev_row = jnp.where(first, 0.0, prev_ref[7:8, :])
    row = lax.broadcasted_iota(jnp.int32, p.shape, 0)
    shifted = jnp.where(row == 0, prev_row, pltpu.roll(p, 1, axis=0))
    ps = p + (shifted - p) * mu_ref[...]
    r = ps[:, 0:c]
    k = ps[:, c:2 * c]
    v = ps[:, 2 * c:3 * c]
    wd = ps[:, 3 * c:3 * c + 128]
    ad = ps[:, 3 * c + 128:3 * c + 256]
    gd = ps[:, 3 * c + 256:3 * c + 512]
    z = -(w0_ref[...] + _dot3(jnp.tanh(wd), w2_ref[...]))
    softplus = jnp.maximum(z, 0.0) + jnp.log1p(jnp.exp(-jnp.abs(z)))
    w = -softplus - 0.5
    lw = -jnp.exp(w)
    a = jax.nn.sigmoid(a0_ref[...] + _dot3(ad, a2_ref[...]))
    g = _dot3(jax.nn.sigmoid(gd), g2_ref[...])
    kkr = k * kk_ref[...]
    k2 = k * (1.0 + (a - 1.0) * ka_ref[...])
    ones = _head_ones()
    for q in range(RWKV_SLABS):
        sl = slice(q * SLAB, (q + 1) * SLAB)
        x = kkr[:, sl]
        ss = _dot_exact_rhs(x * x, ones)
        kkn = x / jnp.maximum(jnp.sqrt(ss), 1e-12)
        r_out[q] = r[:, sl]
        lw_out[q] = lw[:, sl]
        k_out[q] = k2[:, sl]
        v_out[q] = v[:, sl]
        kkn_out[q] = kkn
        b_out[q] = kkn * a[:, sl]
        g_out[q] = g[:, sl]


def rwkv_prep(p, mu, w0, a0, k_k, k_a, w2p, a2p, g2p, *, seq, tm):
    t, cols = p.shape
    c = RWKV_WIDTH
    vec = lambda n: pl.BlockSpec((1, n), lambda i: (0, 0))
    full = lambda a: pl.BlockSpec(a.shape, lambda i: (0, 0))
    out_spec = pl.BlockSpec((RWKV_SLABS, tm, SLAB), lambda i: (0, i, 0))
    out_shape = jax.ShapeDtypeStruct((RWKV_SLABS, t, SLAB), F32)
    return pl.pallas_call(
        functools.partial(_rwkv_prep_kernel, tiles_per_seq=seq // tm),
        grid=(t // tm,),
        in_specs=[
            pl.BlockSpec((tm, cols), lambda i: (i, 0)),
            pl.BlockSpec((8, cols), lambda i: (jnp.maximum(i * (tm // 8) - 1, 0), 0)),
            vec(cols), vec(c), vec(c), vec(c), vec(c), full(w2p), full(a2p), full(g2p),
        ],
        out_specs=[out_spec] * 7,
        out_shape=[out_shape] * 7,
        compiler_params=_cparams(("parallel",)),
        name="rwkv_prep",
    )(p, p, mu.reshape(1, cols), w0.reshape(1, c), a0.reshape(1, c), k_k.reshape(1, c), k_a.reshape(1, c),
      w2p, a2p, g2p)


def _dot1(a, b, dims=NN):
    return _dot(a.astype(BF16), b.astype(BF16), dims)


def _dot3s(a, b, dims=NN):
    ah, al = _split2(a)
    bh, bl = _split2(b)
    m = a.shape[0]
    lhs = jnp.concatenate([ah, al], axis=0)
    if dims is NN:
        n = b.shape[1]
        rhs = jnp.concatenate([bh, bl], axis=1)
    else:
        n = b.shape[0]
        rhs = jnp.concatenate([bh, bl], axis=0)
    p = _dot(lhs, rhs, dims)
    return p[:m, :n] + (p[:m, n:] + p[m:, :n])


def _sum3_exact_rhs(xs, ones):
    parts = []
    for x in xs:
        parts.extend(_split3(x))
    res = _dot(jnp.concatenate(parts, axis=0), ones)
    L = xs[0].shape[0]
    return [res[(3 * i) * L:(3 * i + 1) * L] + (res[(3 * i + 1) * L:(3 * i + 2) * L] + res[(3 * i + 2) * L:(3 * i + 3) * L])
            for i in range(len(xs))]


def _expert_weight_chunk(w1_ref, w2a_ref, w2b_ref, w1o_ref, w2o_ref, tmp_ref):
    w1o_ref[0] = w1_ref[0].astype(BF16)
    half = W2_GROUP
    for c in range(ROW_TILE):
        cols = slice(c * LANES, (c + 1) * LANES)
        tmp_ref[c, pl.ds(0, half, stride=2), :] = w2a_ref[0, :, cols]
        tmp_ref[c, pl.ds(1, half, stride=2), :] = w2b_ref[0, :, cols]
        w2o_ref[0, :, cols] = tmp_ref[c].astype(BF16)


def _expert_weight_views(w1, w2):
    e, d, f2 = w1.shape
    fdim = w2.shape[1]
    groups_per_half = (MOE_FT // 2) // W2_GROUP
    n_chunks = e * (fdim // MOE_FT) * groups_per_half
    w1v = w1.reshape(n_chunks, (e * d) // n_chunks, f2)
    w2v = w2.reshape(e * fdim // W2_GROUP, W2_GROUP, d)
    return n_chunks, groups_per_half, w1v, w2v


def _expert_weight_specs(n_chunks, groups_per_half, w1v, w2v, step):
    d = w2v.shape[2]
    a_idx = lambda j: (j // groups_per_half) * (2 * groups_per_half) + j % groups_per_half
    in_specs = [pl.BlockSpec((1,) + w1v.shape[1:], lambda *ids: (step(*ids), 0, 0)),
                pl.BlockSpec((1, W2_GROUP, d), lambda *ids: (a_idx(step(*ids)), 0, 0)),
                pl.BlockSpec((1, W2_GROUP, d), lambda *ids: (a_idx(step(*ids)) + groups_per_half, 0, 0))]
    out_specs = [pl.BlockSpec((1,) + w1v.shape[1:], lambda *ids: (step(*ids), 0, 0)),
                 pl.BlockSpec((1, 2 * W2_GROUP, d), lambda *ids: (step(*ids), 0, 0))]
    out_shape = [jax.ShapeDtypeStruct(w1v.shape, BF16), jax.ShapeDtypeStruct((n_chunks, 2 * W2_GROUP, d), BF16)]
    scratch = pltpu.VMEM((ROW_TILE, 2 * W2_GROUP, LANES), F32)
    return in_specs, out_specs, out_shape, scratch


def expert_weight_prep(w1, w2):
    n_chunks, gph, w1v, w2v = _expert_weight_views(w1, w2)
    in_specs, out_specs, out_shape, scratch = _expert_weight_specs(n_chunks, gph, w1v, w2v, lambda j: j)
    w1b, w2b = pl.pallas_call(
        _expert_weight_chunk, grid=(n_chunks,), in_specs=in_specs, out_specs=out_specs, out_shape=out_shape,
        scratch_shapes=[scratch], compiler_params=_cparams(("parallel",)), name="expert_weight_prep",
    )(w1v, w2v, w2v)
    return w1b.reshape(w1.shape), w2b.reshape(w2.shape)


def _rwkv_scan_kernel(r_ref, lw_ref, k_ref, v_ref, kk_ref, b_ref, g_ref, lnw_ref, lnb_ref, rk_ref, *rest):
    if len(rest) == 2:
        o_ref, st_ref = rest
    else:
        w1_ref, w2a_ref, w2b_ref, o_ref, w1o_ref, w2o_ref, st_ref, tmp_ref = rest
        _expert_weight_chunk(w1_ref, w2a_ref, w2b_ref, w1o_ref, w2o_ref, tmp_ref)
    L = CHUNK
    W = SLAB

    @pl.when(pl.program_id(1) == 0)
    def _():
        st_ref[...] = jnp.zeros_like(st_ref)

    t_i = lax.broadcasted_iota(jnp.int32, (L, W), 0)
    lane = lax.broadcasted_iota(jnp.int32, (L, W), 1)
    j_i = lane % HEAD_DIM
    hid = lane // HEAD_DIM
    strict = j_i < t_i
    incl = j_i <= t_i
    eye = (j_i == t_i).astype(F32)
    r2 = lax.broadcasted_iota(jnp.int32, (W, W), 0)
    c2 = lax.broadcasted_iota(jnp.int32, (W, W), 1)
    same_head = (r2 // HEAD_DIM) == (c2 // HEAD_DIM)
    diag = r2 == c2
    ones = same_head.astype(BF16)
    tr = lax.broadcasted_iota(jnp.int32, (L, L), 0)
    tc = lax.broadcasted_iota(jnp.int32, (L, L), 1)
    tri = (tc <= tr).astype(BF16)

    def bd(x):
        return jnp.concatenate([jnp.where(hid == h, x, 0.0) for h in range(HEADS_PER_SLAB)], axis=0)

    chains = [(u, h) for h in range(SCAN_CHUNKS) for u in range(RWKV_SLABS)]
    U = range(len(chains))
    rows = [slice(h * L, (h + 1) * L) for _, h in chains]
    slab = [u for u, _ in chains]
    cat0 = lambda xs: jnp.concatenate(xs, axis=0)
    cat1 = lambda xs: jnp.concatenate(xs, axis=1)
    r = [r_ref[slab[u], rows[u], :] for u in U]
    lw = [lw_ref[slab[u], rows[u], :] for u in U]
    k = [k_ref[slab[u], rows[u], :] for u in U]
    v = [v_ref[slab[u], rows[u], :] for u in U]
    kkv = [kk_ref[slab[u], rows[u], :] for u in U]
    bv = [b_ref[slab[u], rows[u], :] for u in U]
    c3 = [_dot(tri, cat1(_split3(lw[u]))) for u in U]
    cum = [c3[u][:, 0:W] + (c3[u][:, W:2 * W] + c3[u][:, 2 * W:]) for u in U]
    cum_end = [cum[u][L - 1:L, :] for u in U]
    e_cum = [jnp.exp(cum[u]) for u in U]
    e_neg = [jnp.exp(-cum[u]) for u in U]
    e_end = [jnp.exp(cum_end[u] - cum[u]) for u in U]
    at = [-kkv[u] * jnp.exp(cum[u] - lw[u]) for u in U]
    rt = [r[u] * e_cum[u] for u in U]
    bt = [bv[u] * e_neg[u] for u in U]
    kt = [k[u] * e_neg[u] for u in U]
    bh = [bv[u] * e_end[u] for u in U]
    kh = [k[u] * e_end[u] for u in U]
    w_end = [jnp.exp(cum_end[u]) for u in U]

    g_all = [_dot1(cat0([at[u], rt[u]]), cat0([bd(bt[u]), bd(kt[u])]), NT) for u in U]
    a_ab = [jnp.where(strict, g_all[u][0:L, 0:W], 0.0) for u in U]
    a_ak = [jnp.where(strict, g_all[u][0:L, W:], 0.0) for u in U]
    a_rb = [jnp.where(incl, g_all[u][L:, 0:W], 0.0) for u in U]
    a_rk = [jnp.where(incl, g_all[u][L:, W:], 0.0) for u in U]

    tinv = [eye for u in U]
    pw = a_ab
    for it in range(6):
        if it < 5:
            res = [_dot1(pw[u], cat1([bd(tinv[u]), bd(pw[u])])) for u in U]
            tinv = [tinv[u] + res[u][:, 0:W] for u in U]
            pw = [res[u][:, W:] for u in U]
        else:
            tinv = [tinv[u] + _dot1(pw[u], bd(tinv[u])) for u in U]

    bdv = [bd(v[u]) for u in U]
    av = [_dot1(a_ak[u], bdv[u]) for u in U]
    qp = [_dot1(tinv[u], cat1([bd(at[u]), bd(av[u])])) for u in U]
    q1 = [qp[u][:, 0:W] for u in U]
    p1 = [qp[u][:, W:] for u in U]
    qp2 = [_dot1(cat1([a_rb[u], a_rk[u]]),
                  cat0([cat1([bd(q1[u]), bd(p1[u])]), cat1([jnp.zeros_like(bdv[u]), bdv[u]])])) for u in U]
    q2 = [rt[u] + qp2[u][:, 0:W] for u in U]
    p2 = [qp2[u][:, W:] for u in U]
    mp = [_dot1(cat0([bh[u], kh[u]]).T,
                 cat0([cat1([q1[u], p1[u]]), cat1([jnp.zeros_like(v[u]), v[u]])])) for u in U]
    m_bd = [jnp.where(same_head, mp[u][:, 0:W], 0.0) + jnp.where(diag, w_end[u], 0.0) for u in U]
    m_hl = [_split2(m_bd[u]) for u in U]
    st = [st_ref[s] for s in range(RWKV_SLABS)]
    y = [None for u in U]
    for u in U:
        st_hi, st_lo = _split2(st[slab[u]])
        yq = _dot(cat0([q2[u].astype(BF16), m_hl[u][0], m_hl[u][1]]), st_hi)
        y[u] = yq[0:L] + p2[u]
        carry = yq[L:L + W] + (yq[L + W:] + _dot(m_hl[u][0], st_lo))
        st[slab[u]] = carry + jnp.where(same_head, mp[u][:, W:], 0.0)
    for s in range(RWKV_SLABS):
        st_ref[s] = st[s]

    sums = [_sum3_exact_rhs([y[u], r[u] * k[u] * rk_ref[slab[u]]], ones) for u in U]
    yc = [y[u] - sums[u][0] * (1.0 / HEAD_DIM) for u in U]
    var = [_sum3_exact_rhs([yc[u] * yc[u]], ones)[0] for u in U]
    for u in U:
        yn = yc[u] * lax.rsqrt(var[u] * (1.0 / HEAD_DIM) + RWKV_GN_EPS) * lnw_ref[slab[u]] + lnb_ref[slab[u]]
        o_ref[slab[u], rows[u], :] = ((yn + sums[u][1] * v[u]) * g_ref[slab[u], rows[u], :]).astype(o_ref.dtype)


def rwkv_scan(r, lw, k, v, kk, b, g, ln_w, ln_b, r_k, w1, w2, *, batch, seq):
    nchunk = seq // (SCAN_CHUNKS * CHUNK)
    blk = pl.BlockSpec((RWKV_SLABS, SCAN_CHUNKS * CHUNK, SLAB), lambda bi, ci: (0, bi * nchunk + ci, 0))
    par = pl.BlockSpec((RWKV_SLABS, 1, SLAB), lambda bi, ci: (0, 0, 0))
    t = batch * seq
    out_shape = jax.ShapeDtypeStruct((RWKV_SLABS, t, SLAB), BF16)
    state = pltpu.VMEM((RWKV_SLABS, SLAB, SLAB), F32)
    args = (r, lw, k, v, kk, b, g, ln_w.reshape(RWKV_SLABS, 1, SLAB), ln_b.reshape(RWKV_SLABS, 1, SLAB),
            r_k.reshape(RWKV_SLABS, 1, SLAB))
    n_chunks, gph, w1v, w2v = _expert_weight_views(w1, w2)
    if n_chunks != batch * nchunk:
        ra = pl.pallas_call(
            _rwkv_scan_kernel, grid=(batch, nchunk), in_specs=[blk] * 7 + [par] * 3, out_specs=blk,
            out_shape=out_shape, scratch_shapes=[state],
            compiler_params=_cparams(("arbitrary", "arbitrary")), name="rwkv_scan",
        )(*args)
        return (ra,) + expert_weight_prep(w1, w2)
    w_in, w_out, w_shape, w_scratch = _expert_weight_specs(n_chunks, gph, w1v, w2v, lambda bi, ci: bi * nchunk + ci)
    ra, w1b, w2b = pl.pallas_call(
        _rwkv_scan_kernel, grid=(batch, nchunk), in_specs=[blk] * 7 + [par] * 3 + w_in,
        out_specs=[blk] + w_out, out_shape=[out_shape] + w_shape, scratch_shapes=[state, w_scratch],
        compiler_params=_cparams(("arbitrary", "arbitrary")), name="rwkv_scan",
    )(*args, w1v, w2v, w2v)
    return ra, w1b.reshape(w1.shape), w2b.reshape(w2.shape)


def _diff_attn_kernel(qi_ref, ki_ref, slope_ref, q_ref, k_ref, v_ref, lq1_ref, lk1_ref, lq2_ref, lk2_ref, sg_ref,
                      o_ref, qs_ref, relb_ref, m_ref, l_ref, acc_ref, *, tq, lam_init):
    h = pl.program_id(1)
    qi = qi_ref[pl.program_id(2)]
    ki = ki_ref[pl.program_id(2)]
    c2 = slope_ref[h] * LOG2E

    @pl.when(ki == 0)
    def _():
        m_ref[...] = jnp.full_like(m_ref, -jnp.inf)
        l_ref[...] = jnp.zeros_like(l_ref)
        acc_ref[...] = jnp.zeros_like(acc_ref)
        q = q_ref[...].astype(F32) * (HEAD_DIM ** -0.5 * LOG2E)
        lane = lax.broadcasted_iota(jnp.int32, q.shape, 1)
        m0 = lane < HEAD_DIM
        qs_ref[0:tq, :] = jnp.where(m0, q, 0.0).astype(BF16)
        qs_ref[tq:, :] = jnp.where(m0, 0.0, q).astype(BF16)

    @pl.when(pl.program_id(2) == 0)
    def _():
        rel = (lax.broadcasted_iota(jnp.int32, (tq, tq), 1) - lax.broadcasted_iota(jnp.int32, (tq, tq), 0))
        relb = rel.astype(F32) * (-c2)
        relb_ref[:, 0:tq] = relb
        relb_ref[:, tq:] = relb

    def step(masked):
        k = k_ref[...]
        vt = v_ref[...].T
        relb = relb_ref[...]
        tile_bias = c2 * ((qi - ki) * tq).astype(F32)
        s = _dot(k, qs_ref[...], NT) + relb
        if masked:
            s = jnp.where(relb > 0.0, -jnp.inf, s)
        m_prev = m_ref[...]
        m_new = jnp.maximum(m_prev, jnp.max(s, axis=0, keepdims=True) - tile_bias)
        alpha = jnp.exp2(m_prev - m_new)
        p = jnp.exp2(s - (m_new + tile_bias))
        l_ref[...] = alpha * l_ref[...] + jnp.sum(p, axis=0, keepdims=True)
        acc_ref[...] = alpha * acc_ref[...] + _dot(vt, p.astype(BF16))
        m_ref[...] = m_new

    @pl.when(ki < qi)
    def _():
        step(False)

    @pl.when(ki == qi)
    def _():
        step(True)
        lam = (jnp.exp(jnp.sum(lq1_ref[...] * lk1_ref[...], axis=-1, keepdims=True))
               - jnp.exp(jnp.sum(lq2_ref[...] * lk2_ref[...], axis=-1, keepdims=True)) + lam_init)
        ot = acc_ref[:, 0:tq] / l_ref[:, 0:tq] - lam * (acc_ref[:, tq:] / l_ref[:, tq:])
        o = _rms(ot.T, sg_ref[...], NORM_EPS) * (1.0 - lam_init)
        o_ref[...] = o.astype(o_ref.dtype)


def diff_attention(qkv, slopes, lq1, lk1, lq2, lk2, subln_g, *, batch, seq, tq, lam_init):
    t = batch * seq
    nq = seq // tq
    hb = DIFF_WIDTH // LANES
    pairs = [(qi, ki) for qi in range(nq) for ki in range(qi + 1)]
    qi_tab = jnp.asarray([pq for pq, _ in pairs], jnp.int32)
    ki_tab = jnp.asarray([pk for _, pk in pairs], jnp.int32)
    small = pl.BlockSpec((1, HEAD_DIM), lambda b, h, j, qt, kt: (0, 0))
    grid_spec = pltpu.PrefetchScalarGridSpec(
        num_scalar_prefetch=2,
        grid=(batch, DIFF_HEADS, len(pairs)),
        in_specs=[
            pl.BlockSpec(memory_space=pltpu.SMEM),
            pl.BlockSpec((tq, LANES), lambda b, h, j, qt, kt: (b * nq + qt[j], h)),
            pl.BlockSpec((tq, LANES), lambda b, h, j, qt, kt: (b * nq + kt[j], hb + h)),
            pl.BlockSpec((tq, LANES), lambda b, h, j, qt, kt: (b * nq + kt[j], 2 * hb + h)),
            small, small, small, small,
            pl.BlockSpec((1, LANES), lambda b, h, j, qt, kt: (0, 0)),
        ],
        out_specs=pl.BlockSpec((tq, LANES), lambda b, h, j, qt, kt: (b * nq + qt[j], h)),
        scratch_shapes=[pltpu.VMEM((2 * tq, LANES), BF16), pltpu.VMEM((tq, 2 * tq), F32),
                        pltpu.VMEM((1, 2 * tq), F32), pltpu.VMEM((1, 2 * tq), F32),
                        pltpu.VMEM((LANES, 2 * tq), F32)],
    )
    return pl.pallas_call(
        functools.partial(_diff_attn_kernel, tq=tq, lam_init=lam_init),
        grid_spec=grid_spec,
        out_shape=jax.ShapeDtypeStruct((t, DIFF_WIDTH), BF16),
        compiler_params=_cparams(("parallel", "parallel", "arbitrary")),
        name="diff_attention",
    )(qi_tab, ki_tab, slopes, qkv, qkv, qkv, lq1.reshape(1, -1), lk1.reshape(1, -1), lq2.reshape(1, -1),
      lk2.reshape(1, -1), subln_g.reshape(1, -1))


def _mix_kernel(ra_ref, da_ref, ga_ref, gb_ref, x_ref, wa_ref, wb_ref, wo_ref, o_ref):
    ya = _dot(ra_ref[0], wa_ref[0:SLAB, :])
    for q in range(1, RWKV_SLABS):
        ya = ya + _dot(ra_ref[q], wa_ref[q * SLAB:(q + 1) * SLAB, :])
    yb = _dot(da_ref[...], wb_ref[...])
    mixed = ga_ref[...].astype(F32) * ya + gb_ref[...].astype(F32) * yb
    o_ref[...] = x_ref[...] + _dot(mixed.astype(BF16), wo_ref[...])


def mix_project(ra, da, gates, x, wa, wb, wo, *, tm):
    t, d = x.shape
    const = lambda a: pl.BlockSpec(a.shape, lambda i: (0, 0), pipeline_mode=pl.Buffered(1))
    return pl.pallas_call(
        _mix_kernel,
        grid=(t // tm,),
        in_specs=[
            pl.BlockSpec((RWKV_SLABS, tm, SLAB), lambda i: (0, i, 0)),
            pl.BlockSpec((tm, DIFF_WIDTH), lambda i: (i, 0)),
            pl.BlockSpec((tm, d), lambda i: (i, 0)),
            pl.BlockSpec((tm, d), lambda i: (i, 1)),
            pl.BlockSpec((tm, d), lambda i: (i, 0)),
            const(wa), const(wb), const(wo),
        ],
        out_specs=pl.BlockSpec((tm, d), lambda i: (i, 0)),
        out_shape=jax.ShapeDtypeStruct((t, d), F32),
        compiler_params=_cparams(("parallel",)),
        name="mix_project",
    )(ra, da, gates, gates, x, wa, wb, wo)


def _cross_kernel(x_ref, gc_ref, wq_ref, kv_ref, wo_ref, gf_ref, rw_ref, rb_ref, x2_ref, hf_ref, lg_ref):
    x = x_ref[...]
    h = _rms(x, gc_ref[...], NORM_EPS).astype(BF16)
    q = _dot(h, wq_ref[...]).astype(BF16)
    scale = LANES ** -0.5
    outs = []
    for hd in range(CROSS_HEADS):
        qh = q[:, hd * LANES:(hd + 1) * LANES]
        kh = kv_ref[0, :, hd * LANES:(hd + 1) * LANES]
        vh = kv_ref[0, :, CROSS_WIDTH + hd * LANES:CROSS_WIDTH + (hd + 1) * LANES]
        s = _dot(qh, kh, NT) * scale
        s = s - jnp.max(s, axis=-1, keepdims=True)
        e = jnp.exp(s)
        p = e / jnp.sum(e, axis=-1, keepdims=True)
        outs.append(_dot(p.astype(BF16), vh))
    o = jnp.concatenate(outs, axis=1).astype(BF16)
    x2 = x + _dot(o, wo_ref[...])
    x2_ref[...] = x2
    hf = _rms(x2, gf_ref[...], NORM_EPS)
    tm = hf.shape[0]
    for c in range(PACK_ROWS):
        hf_ref[pl.ds(c, tm, stride=PACK_ROWS), :] = _pack_pair(hf[:, 2 * c * LANES:(2 * c + 1) * LANES],
                                                               hf[:, (2 * c + 1) * LANES:(2 * c + 2) * LANES])
    lg_ref[...] = _dot3(hf, rw_ref[...]) + rb_ref[...]


def cross_attention(x, gc, wq, kv, wo, gf, rw, rb, *, seq, tm):
    t, d = x.shape
    per_seq = seq // tm
    const = lambda a: pl.BlockSpec(a.shape, lambda i: (0,) * a.ndim, pipeline_mode=pl.Buffered(1))
    vec = lambda n: pl.BlockSpec((1, n), lambda i: (0, 0))
    return pl.pallas_call(
        _cross_kernel,
        grid=(t // tm,),
        in_specs=[
            pl.BlockSpec((tm, d), lambda i: (i, 0)),
            vec(d), const(wq),
            pl.BlockSpec((1,) + kv.shape[1:], lambda i: (i // per_seq, 0, 0)),
            const(wo), vec(d), const(rw), vec(LANES),
        ],
        out_specs=[pl.BlockSpec((tm, d), lambda i: (i, 0)), pl.BlockSpec((tm * PACK_ROWS, LANES), lambda i: (i, 0)),
                   pl.BlockSpec((tm, LANES), lambda i: (i, 0))],
        out_shape=[jax.ShapeDtypeStruct((t, d), F32), jax.ShapeDtypeStruct((t * PACK_ROWS, LANES), U32),
                   jax.ShapeDtypeStruct((t, LANES), F32)],
        compiler_params=_cparams(("parallel",)),
        name="cross_attention",
    )(x, gc.reshape(1, d), wq, kv, wo, gf.reshape(1, d), rw, rb)


def _router_kernel(lg_ref, idx_ref, gate_ref):
    x = lg_ref[...]
    lane = lax.broadcasted_iota(jnp.int32, x.shape, 1)
    x = jnp.where(lane < N_EXPERTS, x, -jnp.inf)
    idx_out = jnp.zeros(x.shape, jnp.int32)
    val_out = jnp.zeros(x.shape, F32)
    vals = []
    for j in range(TOP_K):
        m = jnp.max(x, axis=-1, keepdims=True)
        sel = jnp.min(jnp.where(x == m, lane, LANES), axis=-1, keepdims=True)
        idx_out = jnp.where(lane == j, sel, idx_out)
        vals.append(m)
        x = jnp.where(lane == sel, -jnp.inf, x)
    es = [jnp.exp(vj - vals[0]) for vj in vals]
    tot = es[0] + es[1] + es[2] + es[3]
    for j in range(TOP_K):
        val_out = jnp.where(lane == j, es[j] / tot, val_out)
    idx_ref[...] = idx_out[:, 0:TOP_K]
    gate_ref[...] = val_out[:, 0:TOP_K]


def router_topk(logits, *, tm):
    t = logits.shape[0]
    return pl.pallas_call(
        _router_kernel,
        grid=(t // tm,),
        in_specs=[pl.BlockSpec((tm, LANES), lambda i: (i, 0))],
        out_specs=[pl.BlockSpec((tm, TOP_K), lambda i: (i, 0)), pl.BlockSpec((tm, TOP_K), lambda i: (i, 0))],
        out_shape=[jax.ShapeDtypeStruct((t, TOP_K), jnp.int32), jax.ShapeDtypeStruct((t, TOP_K), F32)],
        compiler_params=_cparams(("parallel",)),
        name="router_topk",
    )(logits)


def _moe_kernel(sbe_ref, sbr_ref, src_ref, tok_ref, ntok_ref, hf_hbm, w1_ref, b1_ref, w2_ref, b2_ref,
                ys_hbm, xbuf, x2d, acc, obuf, gsem, ssem):
    s = pl.program_id(0)
    f = pl.program_id(1)
    nf = pl.num_programs(1)
    nsb = pl.num_programs(0)
    sub_rows = MOE_SUB * PACK_ROWS
    nsubs = MOE_SUPER // MOE_SUB
    unroll = 8

    def ceil_sub(n):
        return lax.shift_right_logical(n + (MOE_SUB - 1), MOE_SUB.bit_length() - 1)

    rows = sbr_ref[s]
    nsub = ceil_sub(rows)
    rows_next = jnp.where(s + 1 < nsb, sbr_ref[jnp.minimum(s + 1, nsb - 1)], 0)
    rows_prev = jnp.where(s > 0, sbr_ref[jnp.maximum(s - 1, 0)], 0)
    nsub_next = ceil_sub(rows_next)

    def slab(ref, off):
        return ref.at[pl.ds(pl.multiple_of(off, PACK_ROWS), PACK_ROWS)]

    def row_in(table, r):
        return pltpu.make_async_copy(slab(hf_hbm, table[0, 0, r]), slab(xbuf, r * PACK_ROWS), gsem)

    def row_out(r):
        return pltpu.make_async_copy(slab(obuf, r * PACK_ROWS), slab(ys_hbm, src_ref[0, 0, r] * PACK_ROWS), ssem)

    def start_gather(table, first, n):
        def group(gi, c):
            for j in range(unroll):
                row_in(table, first + gi * unroll + j).start()
            return c

        lax.fori_loop(0, n // unroll, group, 0)

    def wait_sub_blocks(buf, other, sem, n):
        for sub in range(nsubs):
            @pl.when(sub < n)
            def _(sub=sub):
                pltpu.make_async_copy(other.at[pl.ds(0, sub_rows)], buf.at[pl.ds(sub * sub_rows, sub_rows)], sem).wait()

    def wait_scatter(n):
        full = n // MOE_SUB
        for sub in range(nsubs):
            @pl.when(sub < full)
            def _(sub=sub):
                pltpu.make_async_copy(obuf.at[pl.ds(sub * sub_rows, sub_rows)], ys_hbm.at[pl.ds(0, sub_rows)], ssem).wait()

        def one(r, c):
            row_out(r).wait()
            return c

        lax.fori_loop(0, n - full * MOE_SUB, one, 0)

    @pl.when((f == 0) & (s == 0) & (rows > 0))
    def _():
        start_gather(tok_ref, 0, nsub * MOE_SUB)

    @pl.when((f == 0) & (rows > 0))
    def _():
        wait_sub_blocks(xbuf, hf_hbm, gsem, nsub)

    @pl.when((f == nf - 1) & (rows > 0) & (rows_prev > 0))
    def _():
        wait_scatter(rows_prev)

    for sub in range(nsubs):
        @pl.when(sub < nsub)
        def _(sub=sub):
            sl = slice(sub * MOE_SUB, (sub + 1) * MOE_SUB)
            base = sub * sub_rows

            @pl.when(f == 0)
            def _():
                for c in range(PACK_ROWS):
                    lo, hi = _unpack_pair(xbuf[pl.ds(base + c, MOE_SUB, stride=PACK_ROWS), :])
                    x2d[sl, 2 * c * LANES:(2 * c + 1) * LANES] = lo.astype(BF16)
                    x2d[sl, (2 * c + 1) * LANES:(2 * c + 2) * LANES] = hi.astype(BF16)

            x = x2d[sl, :]
            hb = _dot(x, w1_ref[0]) + b1_ref[0]
            even = (lax.broadcasted_iota(jnp.int32, (MOE_SUB, LANES), 1) % 2) == 0
            acts = []
            for cb in range(MOE_FT // LANES):
                a = hb[:, cb * LANES:(cb + 1) * LANES]
                b = hb[:, MOE_FT + cb * LANES:MOE_FT + (cb + 1) * LANES]
                hg = jnp.where(even, a, pltpu.roll(b, 1, axis=1))
                hl = jnp.where(even, pltpu.roll(a, LANES - 1, axis=1), b)
                xg = jnp.minimum(hg, SWIGLU_LIMIT)
                xl = jnp.clip(hl, -SWIGLU_LIMIT, SWIGLU_LIMIT)
                acts.append((xg * jax.nn.sigmoid(SWIGLU_ALPHA * xg) * (xl + 1.0)).astype(BF16))
            contrib = _dot(jnp.concatenate(acts, axis=1), w2_ref[0])

            @pl.when(f == 0)
            def _():
                acc[sl, :] = contrib

            @pl.when((f > 0) & (f < nf - 1))
            def _():
                acc[sl, :] += contrib

            @pl.when(f == nf - 1)
            def _():
                y = acc[sl, :] + contrib + b2_ref[0]
                for c in range(PACK_ROWS):
                    obuf[pl.ds(base + c, MOE_SUB, stride=PACK_ROWS), :] = _pack_pair(
                        y[:, 2 * c * LANES:(2 * c + 1) * LANES], y[:, (2 * c + 1) * LANES:(2 * c + 2) * LANES])

    @pl.when((f == 0) & (rows_next > 0))
    def _():
        start_gather(ntok_ref, 0, nsub_next * MOE_SUB)

    @pl.when((f == nf - 1) & (rows > 0))
    def _():
        def group(gi, c):
            for j in range(unroll):
                row_out(gi * unroll + j).start()
            return c

        def one(r, c):
            row_out(r).start()
            return c

        ngroups = rows // unroll
        lax.fori_loop(0, ngroups, group, 0)
        lax.fori_loop(ngroups * unroll, rows, one, 0)

        @pl.when(rows_next == 0)
        def _():
            wait_scatter(rows)


def moe_experts(sb_e, sb_rows, row_src, row_off, hf, w1, b1, w2, b2):
    d = D_MODEL
    t = hf.shape[0] // PACK_ROWS
    nsb = sb_e.shape[0]
    nf = D_EXPERT // MOE_FT
    assert nf >= 2, "the last hidden-column step adds its own contribution to the accumulator of the earlier ones"
    grid_spec = pltpu.PrefetchScalarGridSpec(
        num_scalar_prefetch=2,
        grid=(nsb, nf),
        in_specs=[
            pl.BlockSpec((1, 1, MOE_SUPER), lambda s, f, e, r: (s, 0, 0), memory_space=pltpu.SMEM),
            pl.BlockSpec((1, 1, MOE_SUPER), lambda s, f, e, r: (s, 0, 0), memory_space=pltpu.SMEM),
            pl.BlockSpec((1, 1, MOE_SUPER), lambda s, f, e, r: (jnp.minimum(s + 1, nsb - 1), 0, 0),
                         memory_space=pltpu.SMEM),
            pl.BlockSpec(memory_space=pl.ANY),
            pl.BlockSpec((1, d, 2 * MOE_FT), lambda s, f, e, r: (e[s], 0, f)),
            pl.BlockSpec((1, 1, 2 * MOE_FT), lambda s, f, e, r: (e[s], 0, f)),
            pl.BlockSpec((1, MOE_FT, d), lambda s, f, e, r: (e[s], f, 0)),
            pl.BlockSpec((1, 1, d), lambda s, f, e, r: (e[s], 0, 0)),
        ],
        out_specs=pl.BlockSpec(memory_space=pl.ANY),
        scratch_shapes=[pltpu.VMEM((MOE_SUPER * PACK_ROWS, LANES), U32), pltpu.VMEM((MOE_SUPER, d), BF16),
                        pltpu.VMEM((MOE_SUPER, d), F32), pltpu.VMEM((MOE_SUPER * PACK_ROWS, LANES), U32),
                        pltpu.SemaphoreType.DMA, pltpu.SemaphoreType.DMA],
    )
    return pl.pallas_call(
        _moe_kernel,
        grid_spec=grid_spec,
        out_shape=jax.ShapeDtypeStruct((t * TOP_K * PACK_ROWS, LANES), U32),
        compiler_params=_cparams(("arbitrary", "arbitrary")),
        name="moe_experts",
    )(sb_e, sb_rows, row_src, row_off, row_off, hf, w1, b1, w2, b2)


def _combine_kernel(ys_ref, gate_ref, x_ref, g_ref, o_ref, lo_ref, hi_ref):
    tm = x_ref.shape[0]
    tot_lo = tot_hi = None
    for j in range(TOP_K):
        lo, hi = _unpack_pair(ys_ref[:, j * PACK_ROWS:(j + 1) * PACK_ROWS, :])
        gj = gate_ref[:, j:j + 1, :]
        tot_lo = lo * gj if tot_lo is None else tot_lo + lo * gj
        tot_hi = hi * gj if tot_hi is None else tot_hi + hi * gj
    lo_ref[...] = tot_lo.reshape(tm * PACK_ROWS, LANES)
    hi_ref[...] = tot_hi.reshape(tm * PACK_ROWS, LANES)
    pieces = []
    for c in range(PACK_ROWS):
        pieces.append(lo_ref[pl.ds(c, tm, stride=PACK_ROWS), :])
        pieces.append(hi_ref[pl.ds(c, tm, stride=PACK_ROWS), :])
    out = x_ref[...] + jnp.concatenate(pieces, axis=1)
    o_ref[...] = _rms(out, g_ref[...], NORM_EPS)


def combine_final(ys, gate, x, g, *, tm):
    t, d = x.shape
    return pl.pallas_call(
        _combine_kernel,
        grid=(t // tm,),
        in_specs=[pl.BlockSpec((tm, TOP_K * PACK_ROWS, LANES), lambda i: (i, 0, 0)),
                  pl.BlockSpec((tm, TOP_K, LANES), lambda i: (i, 0, 0)),
                  pl.BlockSpec((tm, d), lambda i: (i, 0)), pl.BlockSpec((1, d), lambda i: (0, 0))],
        out_specs=pl.BlockSpec((tm, d), lambda i: (i, 0)),
        out_shape=jax.ShapeDtypeStruct((t, d), F32),
        scratch_shapes=[pltpu.VMEM((tm * PACK_ROWS, LANES), F32), pltpu.VMEM((tm * PACK_ROWS, LANES), F32)],
        compiler_params=_cparams(("parallel",)),
        name="combine_final",
    )(ys, gate, x, g.reshape(1, d))


def _routing_tables(top_idx, n_super):
    flat_e = top_idx.reshape(-1)
    n = flat_e.shape[0]
    onehot = (flat_e[:, None] == jnp.arange(N_EXPERTS, dtype=jnp.int32)[None, :]).astype(jnp.int32)
    csum = jnp.cumsum(onehot, axis=0)
    rank = jnp.sum(onehot * csum, axis=1) - 1
    counts = csum[-1]
    nsb = (counts + MOE_SUPER - 1) // MOE_SUPER
    sb_end = jnp.cumsum(nsb)
    sb_start = sb_end - nsb
    dest = sb_start[flat_e] * MOE_SUPER + rank
    row_src = jnp.full((n_super * MOE_SUPER,), -1, jnp.int32).at[dest].set(jnp.arange(n, dtype=jnp.int32))
    s_ids = jnp.arange(n_super, dtype=jnp.int32)
    sb_e = jnp.minimum(jnp.searchsorted(sb_end, s_ids, side="right"), N_EXPERTS - 1).astype(jnp.int32)
    local = s_ids - sb_start[sb_e]
    sb_rows = jnp.clip(counts[sb_e] - local * MOE_SUPER, 0, MOE_SUPER)
    sb_rows = jnp.where(s_ids < sb_end[-1], sb_rows, 0).astype(jnp.int32)
    row_off = jnp.where(row_src >= 0, (row_src // TOP_K) * PACK_ROWS, 0)
    return sb_e, sb_rows, row_src.reshape(n_super, 1, MOE_SUPER), row_off.reshape(n_super, 1, MOE_SUPER)


def _pad_rows(a, n):
    return jnp.pad(a, ((0, n - a.shape[0]), (0, 0)))


def _layer(x, mem, l, p, batch, seq):
    t = batch * seq
    c = RWKV_WIDTH
    lam_init = 0.8 - 0.6 * math.exp(-0.3 * l)
    w_in = p["w_in"]
    o1 = 3 * c + DECAY_LORA + AAA_LORA + GATE_LORA
    o2 = o1 + 3 * DIFF_WIDTH
    padc = lambda a, n: jnp.pad(a, ((0, 0), (0, n - a.shape[1])))
    w_rwkv = jnp.concatenate([
        w_in[:, :3 * c],
        padc(w_in[:, 3 * c:3 * c + DECAY_LORA], 128),
        padc(w_in[:, 3 * c + DECAY_LORA:3 * c + DECAY_LORA + AAA_LORA], 128),
        padc(w_in[:, 3 * c + DECAY_LORA + AAA_LORA:o1], 256)], axis=1).astype(BF16)
    mu = p["rwkv_mu"]
    pad1 = lambda a, n: jnp.pad(a, (0, n - a.shape[0]))
    mu_p = jnp.concatenate([mu[:3 * c], pad1(mu[3 * c:3 * c + DECAY_LORA], 128),
                            pad1(mu[3 * c + DECAY_LORA:3 * c + DECAY_LORA + AAA_LORA], 128),
                            pad1(mu[3 * c + DECAY_LORA + AAA_LORA:], 256)])
    w_diff = w_in[:, o1:o2].astype(BF16)
    w_gate = w_in[:, o2:].astype(BF16)

    g_mix = p["norm_mix_g"]
    p_rwkv = norm_matmul(x, g_mix, w_rwkv, tm=1024, tn=RWKV_COLS // 2, out_dtype=F32, name="in_proj_rwkv")
    qkv = norm_matmul(x, g_mix, w_diff, tm=1024, tn=1024, out_dtype=BF16, name="in_proj_diff")
    gates = norm_matmul(x, g_mix, w_gate, tm=1024, tn=1024, out_dtype=BF16, act="sigmoid", name="in_proj_gate")

    prep = rwkv_prep(p_rwkv, mu_p, p["rwkv_w0"], p["rwkv_a0"], p["rwkv_k_k"], p["rwkv_k_a"],
                     _pad_rows(p["rwkv_w2"], 128), _pad_rows(p["rwkv_a2"], 128), _pad_rows(p["rwkv_g2"], 256),
                     seq=seq, tm=256)
    ra, w1b, w2b = rwkv_scan(*prep, p["rwkv_ln_w"], p["rwkv_ln_b"], p["rwkv_r_k"].reshape(-1),
                             p["expert_w1"], p["expert_w2"], batch=batch, seq=seq)

    slopes = (2.0 ** (-8.0 * jnp.arange(1, DIFF_HEADS + 1, dtype=F32) / DIFF_HEADS)).astype(F32)
    da = diff_attention(qkv, slopes, p["diff_lq1"], p["diff_lk1"], p["diff_lq2"], p["diff_lk2"],
                        p["diff_subln_g"], batch=batch, seq=seq, tq=min(1024, seq), lam_init=lam_init)

    x1 = mix_project(ra, da, gates, x, p["rwkv_proj"].astype(BF16), p["diff_proj"].astype(BF16),
                     p["w_out"].astype(BF16), tm=512)

    m_len = mem.shape[0] // batch
    kv = norm_matmul(mem, p["norm_mem_g"], p["cross_wkv"].astype(BF16), tm=min(512, mem.shape[0]),
                     tn=2 * CROSS_WIDTH, out_dtype=BF16, name="cross_kv")
    rw = jnp.pad(p["router_w"], ((0, 0), (0, LANES - N_EXPERTS)))
    rb = jnp.pad(p["router_b"], (0, LANES - N_EXPERTS)).reshape(1, LANES)
    x2, hf, logits = cross_attention(x1, p["norm_cross_g"], p["cross_wq"].astype(BF16),
                                     kv.reshape(batch, m_len, 2 * CROSS_WIDTH), p["cross_wo"].astype(BF16),
                                     p["norm_ffn_g"], rw, rb, seq=seq, tm=512)

    top_idx, gate = router_topk(logits, tm=min(1024, t))
    n_super = (t * TOP_K) // MOE_SUPER + N_EXPERTS
    sb_e, sb_rows, row_src, row_off = _routing_tables(top_idx, n_super)
    ys = moe_experts(sb_e, sb_rows, row_src, row_off, hf, w1b, p["expert_b1"][:, None, :], w2b, p["expert_b2"][:, None, :])
    gate_b = jnp.broadcast_to(gate[:, :, None], (t, TOP_K, LANES))
    return x2, ys.reshape(t, TOP_K * PACK_ROWS, LANES), gate_b


def kernel(x, mem, norm_mix_g, w_in, rwkv_mu, rwkv_w0, rwkv_w2, rwkv_a0, rwkv_a2, rwkv_g2, rwkv_k_k, rwkv_k_a, rwkv_r_k, rwkv_ln_w, rwkv_ln_b, rwkv_proj, diff_lq1, diff_lk1, diff_lq2, diff_lk2, diff_subln_g, diff_proj, w_out, norm_cross_g, norm_mem_g, cross_wq, cross_wkv, cross_wo, norm_ffn_g, router_w, router_b, expert_w1, expert_b1, expert_w2, expert_b2, final_norm_g):
    batch, seq, d = x.shape
    stacked = dict(norm_mix_g=norm_mix_g, w_in=w_in, rwkv_mu=rwkv_mu, rwkv_w0=rwkv_w0, rwkv_w2=rwkv_w2,
                   rwkv_a0=rwkv_a0, rwkv_a2=rwkv_a2, rwkv_g2=rwkv_g2, rwkv_k_k=rwkv_k_k, rwkv_k_a=rwkv_k_a,
                   rwkv_r_k=rwkv_r_k, rwkv_ln_w=rwkv_ln_w, rwkv_ln_b=rwkv_ln_b, rwkv_proj=rwkv_proj,
                   diff_lq1=diff_lq1, diff_lk1=diff_lk1, diff_lq2=diff_lq2, diff_lk2=diff_lk2,
                   diff_subln_g=diff_subln_g, diff_proj=diff_proj, w_out=w_out, norm_cross_g=norm_cross_g,
                   norm_mem_g=norm_mem_g, cross_wq=cross_wq, cross_wkv=cross_wkv, cross_wo=cross_wo,
                   norm_ffn_g=norm_ffn_g, router_w=router_w, router_b=router_b, expert_w1=expert_w1,
                   expert_b1=expert_b1, expert_w2=expert_w2, expert_b2=expert_b2)
    assert w_in.shape[0] == 1, "the closing RMSNorm is fused into the single layer's combine"
    p = {k: v[0] for k, v in stacked.items()}
    x2, ys, gate_b = _layer(x.reshape(batch * seq, d), mem.reshape(-1, d), 0, p, batch, seq)
    out = combine_final(ys, gate_b, x2, final_norm_g, tm=256)
    return out.reshape(batch, seq, d)
```

```python
import functools
import math

import jax
import jax.numpy as jnp
from jax import lax
from jax.experimental import pallas as pl
from jax.experimental.pallas import tpu as pltpu

F32 = jnp.float32
BF16 = jnp.bfloat16
U32 = jnp.uint32

D_MODEL = 2048
NORM_EPS = 1e-5
LOG2E = 1.4426950408889634
LANES = 128
HEAD_DIM = 64
RWKV_WIDTH = 1024
SLAB = 256
RWKV_SLABS = RWKV_WIDTH // SLAB
HEADS_PER_SLAB = SLAB // HEAD_DIM
CHUNK = 64
SCAN_CHUNKS = 2
RWKV_GN_EPS = 64e-5
DECAY_LORA = 64
AAA_LORA = 64
GATE_LORA = 160
LORA_PAD = 512
RWKV_COLS = 3 * RWKV_WIDTH + LORA_PAD
DIFF_WIDTH = 1024
DIFF_HEADS = 8
CROSS_HEADS = 4
CROSS_WIDTH = 512
N_EXPERTS = 32
TOP_K = 4
D_EXPERT = 2048
SWIGLU_ALPHA = 1.702
SWIGLU_LIMIT = 7.0
MOE_SUPER = 1024
MOE_SUB = 256
MOE_FT = 1024
W2_GROUP = 256
ROW_TILE = D_MODEL // LANES
PACK_ROWS = ROW_TILE // 2
VMEM_LIMIT = 56 * 1024 * 1024

NN = (((1,), (0,)), ((), ()))
NT = (((1,), (1,)), ((), ()))


def _dot(a, b, dims=NN):
    return lax.dot_general(a, b, dims, preferred_element_type=F32)


def _split2(a):
    hi = a.astype(BF16)
    lo = (a - hi.astype(F32)).astype(BF16)
    return hi, lo


def _split3(a):
    hi = a.astype(BF16)
    r1 = a - hi.astype(F32)
    mid = r1.astype(BF16)
    lo = (r1 - mid.astype(F32)).astype(BF16)
    return hi, mid, lo


def _dot3(a, b, dims=NN):
    ah, al = _split2(a)
    bh, bl = _split2(b)
    return _dot(ah, bh, dims) + (_dot(ah, bl, dims) + _dot(al, bh, dims))


def _dot_exact_rhs(a, b_bf16, dims=NN):
    h, m, l = _split3(a)
    return _dot(h, b_bf16, dims) + (_dot(m, b_bf16, dims) + _dot(l, b_bf16, dims))


def _pack_pair(lo, hi):
    lo_b = lax.shift_right_logical(lax.bitcast_convert_type(lo.astype(BF16).astype(F32), U32), jnp.uint32(16))
    hi_b = lax.bitcast_convert_type(hi.astype(BF16).astype(F32), U32) & jnp.uint32(0xFFFF0000)
    return hi_b | lo_b


def _unpack_pair(w):
    lo = lax.bitcast_convert_type(lax.shift_left(w, jnp.uint32(16)), F32)
    hi = lax.bitcast_convert_type(w & jnp.uint32(0xFFFF0000), F32)
    return lo, hi


def _rms(x, g, eps):
    ms = jnp.mean(x * x, axis=-1, keepdims=True)
    return x * lax.rsqrt(ms + eps) * g


def _cparams(sem):
    return pltpu.CompilerParams(dimension_semantics=sem, vmem_limit_bytes=VMEM_LIMIT)


def _norm_matmul_kernel(x_ref, g_ref, w_ref, o_ref, h_ref, *, act):
    @pl.when(pl.program_id(1) == 0)
    def _():
        h_ref[...] = _rms(x_ref[...], g_ref[...], NORM_EPS).astype(BF16)

    y = _dot(h_ref[...], w_ref[...])
    if act == "sigmoid":
        y = jax.nn.sigmoid(y)
    o_ref[...] = y.astype(o_ref.dtype)


def norm_matmul(x, g, w, *, tm, tn, out_dtype, act=None, name):
    m, d = x.shape
    n = w.shape[1]
    return pl.pallas_call(
        functools.partial(_norm_matmul_kernel, act=act),
        grid=(m // tm, n // tn),
        in_specs=[
            pl.BlockSpec((tm, d), lambda i, j: (i, 0)),
            pl.BlockSpec((1, d), lambda i, j: (0, 0)),
            pl.BlockSpec((d, tn), lambda i, j: (0, j)),
        ],
        out_specs=pl.BlockSpec((tm, tn), lambda i, j: (i, j)),
        out_shape=jax.ShapeDtypeStruct((m, n), out_dtype),
        scratch_shapes=[pltpu.VMEM((tm, d), BF16)],
        compiler_params=_cparams(("parallel", "arbitrary")),
        name=name,
    )(x, g.reshape(1, d), w)


def _head_ones():
    r = lax.broadcasted_iota(jnp.int32, (SLAB, SLAB), 0)
    c = lax.broadcasted_iota(jnp.int32, (SLAB, SLAB), 1)
    return ((r // HEAD_DIM) == (c // HEAD_DIM)).astype(BF16)


def _rwkv_prep_kernel(p_ref, prev_ref, mu_ref, w0_ref, a0_ref, kk_ref, ka_ref, w2_ref, a2_ref, g2_ref,
                      r_out, lw_out, k_out, v_out, kkn_out, b_out, g_out, *, tiles_per_seq):
    c = RWKV_WIDTH
    i = pl.program_id(0)
    p = p_ref[...]
    tm = p.shape[0]
    first = (i % tiles_per_seq) == 0
    prev_row = jnp.where(first, 0.0, prev_ref[7:8, :])
    row = lax.broadcasted_iota(jnp.int32, p.shape, 0)
    shifted = jnp.where(row == 0, prev_row, pltpu.roll(p, 1, axis=0))
    ps = p + (shifted - p) * mu_ref[...]
    r = ps[:, 0:c]
    k = ps[:, c:2 * c]
    v = ps[:, 2 * c:3 * c]
    wd = ps[:, 3 * c:3 * c + 128]
    ad = ps[:, 3 * c + 128:3 * c + 256]
    gd = ps[:, 3 * c + 256:3 * c + 512]
    z = -(w0_ref[...] + _dot3(jnp.tanh(wd), w2_ref[...]))
    softplus = jnp.maximum(z, 0.0) + jnp.log1p(jnp.exp(-jnp.abs(z)))
    w = -softplus - 0.5
    lw = -jnp.exp(w)
    a = jax.nn.sigmoid(a0_ref[...] + _dot3(ad, a2_ref[...]))
    g = _dot3(jax.nn.sigmoid(gd), g2_ref[...])
    kkr = k * kk_ref[...]
    k2 = k * (1.0 + (a - 1.0) * ka_ref[...])
    ones = _head_ones()
    for q in range(RWKV_SLABS):
        sl = slice(q * SLAB, (q + 1) * SLAB)
        x = kkr[:, sl]
        ss = _dot_exact_rhs(x * x, ones)
        kkn = x / jnp.maximum(jnp.sqrt(ss), 1e-12)
        r_out[q] = r[:, sl]
        lw_out[q] = lw[:, sl]
        k_out[q] = k2[:, sl]
        v_out[q] = v[:, sl]
        kkn_out[q] = kkn
        b_out[q] = kkn * a[:, sl]
        g_out[q] = g[:, sl]


def rwkv_prep(p, mu, w0, a0, k_k, k_a, w2p, a2p, g2p, *, seq, tm):
    t, cols = p.shape
    c = RWKV_WIDTH
    vec = lambda n: pl.BlockSpec((1, n), lambda i: (0, 0))
    full = lambda a: pl.BlockSpec(a.shape, lambda i: (0, 0))
    out_spec = pl.BlockSpec((RWKV_SLABS, tm, SLAB), lambda i: (0, i, 0))
    out_shape = jax.ShapeDtypeStruct((RWKV_SLABS, t, SLAB), F32)
    return pl.pallas_call(
        functools.partial(_rwkv_prep_kernel, tiles_per_seq=seq // tm),
        grid=(t // tm,),
        in_specs=[
            pl.BlockSpec((tm, cols), lambda i: (i, 0)),
            pl.BlockSpec((8, cols), lambda i: (jnp.maximum(i * (tm // 8) - 1, 0), 0)),
            vec(cols), vec(c), vec(c), vec(c), vec(c), full(w2p), full(a2p), full(g2p),
        ],
        out_specs=[out_spec] * 7,
        out_shape=[out_shape] * 7,
        compiler_params=_cparams(("parallel",)),
        name="rwkv_prep",
    )(p, p, mu.reshape(1, cols), w0.reshape(1, c), a0.reshape(1, c), k_k.reshape(1, c), k_a.reshape(1, c),
      w2p, a2p, g2p)


def _dot1(a, b, dims=NN):
    return _dot(a.astype(BF16), b.astype(BF16), dims)


def _sum3_exact_rhs(xs, ones):
    parts = []
    for x in xs:
        parts.extend(_split3(x))
    res = _dot(jnp.concatenate(parts, axis=0), ones)
    L = xs[0].shape[0]
    return [res[(3 * i) * L:(3 * i + 1) * L] + (res[(3 * i + 1) * L:(3 * i + 2) * L] + res[(3 * i + 2) * L:(3 * i + 3) * L])
            for i in range(len(xs))]


def _expert_weight_chunk(w1_ref, w2a_ref, w2b_ref, w1o_ref, w2o_ref, tmp_ref):
    w1o_ref[0] = w1_ref[0].astype(BF16)
    half = W2_GROUP
    for c in range(ROW_TILE):
        cols = slice(c * LANES, (c + 1) * LANES)
        tmp_ref[c, pl.ds(0, half, stride=2), :] = w2a_ref[0, :, cols]
        tmp_ref[c, pl.ds(1, half, stride=2), :] = w2b_ref[0, :, cols]
        w2o_ref[0, :, cols] = tmp_ref[c].astype(BF16)


def _expert_weight_views(w1, w2):
    e, d, f2 = w1.shape
    fdim = w2.shape[1]
    groups_per_half = (MOE_FT // 2) // W2_GROUP
    n_chunks = e * (fdim // MOE_FT) * groups_per_half
    w1v = w1.reshape(n_chunks, (e * d) // n_chunks, f2)
    w2v = w2.reshape(e * fdim // W2_GROUP, W2_GROUP, d)
    return n_chunks, groups_per_half, w1v, w2v


def _expert_weight_specs(n_chunks, groups_per_half, w1v, w2v, step):
    d = w2v.shape[2]
    a_idx = lambda j: (j // groups_per_half) * (2 * groups_per_half) + j % groups_per_half
    in_specs = [pl.BlockSpec((1,) + w1v.shape[1:], lambda *ids: (step(*ids), 0, 0)),
                pl.BlockSpec((1, W2_GROUP, d), lambda *ids: (a_idx(step(*ids)), 0, 0)),
                pl.BlockSpec((1, W2_GROUP, d), lambda *ids: (a_idx(step(*ids)) + groups_per_half, 0, 0))]
    out_specs = [pl.BlockSpec((1,) + w1v.shape[1:], lambda *ids: (step(*ids), 0, 0)),
                 pl.BlockSpec((1, 2 * W2_GROUP, d), lambda *ids: (step(*ids), 0, 0))]
    out_shape = [jax.ShapeDtypeStruct(w1v.shape, BF16), jax.ShapeDtypeStruct((n_chunks, 2 * W2_GROUP, d), BF16)]
    scratch = pltpu.VMEM((ROW_TILE, 2 * W2_GROUP, LANES), F32)
    return in_specs, out_specs, out_shape, scratch


def expert_weight_prep(w1, w2):
    n_chunks, gph, w1v, w2v = _expert_weight_views(w1, w2)
    in_specs, out_specs, out_shape, scratch = _expert_weight_specs(n_chunks, gph, w1v, w2v, lambda j: j)
    w1b, w2b = pl.pallas_call(
        _expert_weight_chunk, grid=(n_chunks,), in_specs=in_specs, out_specs=out_specs, out_shape=out_shape,
        scratch_shapes=[scratch], compiler_params=_cparams(("parallel",)), name="expert_weight_prep",
    )(w1v, w2v, w2v)
    return w1b.reshape(w1.shape), w2b.reshape(w2.shape)


def _rwkv_scan_kernel(r_ref, lw_ref, k_ref, v_ref, kk_ref, b_ref, g_ref, lnw_ref, lnb_ref, rk_ref, *rest):
    if len(rest) == 2:
        o_ref, st_ref = rest
    else:
        w1_ref, w2a_ref, w2b_ref, o_ref, w1o_ref, w2o_ref, st_ref, tmp_ref = rest
        _expert_weight_chunk(w1_ref, w2a_ref, w2b_ref, w1o_ref, w2o_ref, tmp_ref)
    L = CHUNK
    W = SLAB

    @pl.when(pl.program_id(1) == 0)
    def _():
        st_ref[...] = jnp.zeros_like(st_ref)

    t_i = lax.broadcasted_iota(jnp.int32, (L, W), 0)
    lane = lax.broadcasted_iota(jnp.int32, (L, W), 1)
    j_i = lane % HEAD_DIM
    hid = lane // HEAD_DIM
    strict = j_i < t_i
    incl = j_i <= t_i
    eye = (j_i == t_i).astype(F32)
    r2 = lax.broadcasted_iota(jnp.int32, (W, W), 0)
    c2 = lax.broadcasted_iota(jnp.int32, (W, W), 1)
    same_head = (r2 // HEAD_DIM) == (c2 // HEAD_DIM)
    diag = r2 == c2
    ones = same_head.astype(BF16)
    tr = lax.broadcasted_iota(jnp.int32, (L, L), 0)
    tc = lax.broadcasted_iota(jnp.int32, (L, L), 1)
    tri = (tc <= tr).astype(BF16)

    def bd(x):
        return jnp.concatenate([jnp.where(hid == h, x, 0.0) for h in range(HEADS_PER_SLAB)], axis=0)

    chains = [(u, h) for h in range(SCAN_CHUNKS) for u in range(RWKV_SLABS)]
    U = range(len(chains))
    rows = [slice(h * L, (h + 1) * L) for _, h in chains]
    slab = [u for u, _ in chains]
    cat0 = lambda xs: jnp.concatenate(xs, axis=0)
    cat1 = lambda xs: jnp.concatenate(xs, axis=1)
    r = [r_ref[slab[u], rows[u], :] for u in U]
    lw = [lw_ref[slab[u], rows[u], :] for u in U]
    k = [k_ref[slab[u], rows[u], :] for u in U]
    v = [v_ref[slab[u], rows[u], :] for u in U]
    kkv = [kk_ref[slab[u], rows[u], :] for u in U]
    bv = [b_ref[slab[u], rows[u], :] for u in U]
    c3 = [_dot(tri, cat1(_split3(lw[u]))) for u in U]
    cum = [c3[u][:, 0:W] + (c3[u][:, W:2 * W] + c3[u][:, 2 * W:]) for u in U]
    cum_end = [cum[u][L - 1:L, :] for u in U]
    e_cum = [jnp.exp(cum[u]) for u in U]
    e_neg = [jnp.exp(-cum[u]) for u in U]
    e_end = [jnp.exp(cum_end[u] - cum[u]) for u in U]
    at = [-kkv[u] * jnp.exp(cum[u] - lw[u]) for u in U]
    rt = [r[u] * e_cum[u] for u in U]
    bt = [bv[u] * e_neg[u] for u in U]
    kt = [k[u] * e_neg[u] for u in U]
    bh = [bv[u] * e_end[u] for u in U]
    kh = [k[u] * e_end[u] for u in U]
    w_end = [jnp.exp(cum_end[u]) for u in U]

    g_all = [_dot1(cat0([at[u], rt[u]]), cat0([bd(bt[u]), bd(kt[u])]), NT) for u in U]
    a_ab = [jnp.where(strict, g_all[u][0:L, 0:W], 0.0) for u in U]
    a_ak = [jnp.where(strict, g_all[u][0:L, W:], 0.0) for u in U]
    a_rb = [jnp.where(incl, g_all[u][L:, 0:W], 0.0) for u in U]
    a_rk = [jnp.where(incl, g_all[u][L:, W:], 0.0) for u in U]

    tinv = [eye for u in U]
    pw = a_ab
    for it in range(6):
        if it < 5:
            res = [_dot1(pw[u], cat1([bd(tinv[u]), bd(pw[u])])) for u in U]
            tinv = [tinv[u] + res[u][:, 0:W] for u in U]
            pw = [res[u][:, W:] for u in U]
        else:
            tinv = [tinv[u] + _dot1(pw[u], bd(tinv[u])) for u in U]

    bdv = [bd(v[u]) for u in U]
    av = [_dot1(a_ak[u], bdv[u]) for u in U]
    qp = [_dot1(tinv[u], cat1([bd(at[u]), bd(av[u])])) for u in U]
    q1 = [qp[u][:, 0:W] for u in U]
    p1 = [qp[u][:, W:] for u in U]
    qp2 = [_dot1(cat1([a_rb[u], a_rk[u]]),
                  cat0([cat1([bd(q1[u]), bd(p1[u])]), cat1([jnp.zeros_like(bdv[u]), bdv[u]])])) for u in U]
    q2 = [rt[u] + qp2[u][:, 0:W] for u in U]
    p2 = [qp2[u][:, W:] for u in U]
    mp = [_dot1(cat0([bh[u], kh[u]]).T,
                 cat0([cat1([q1[u], p1[u]]), cat1([jnp.zeros_like(v[u]), v[u]])])) for u in U]
    m_bd = [jnp.where(same_head, mp[u][:, 0:W], 0.0) + jnp.where(diag, w_end[u], 0.0) for u in U]
    m_hl = [_split2(m_bd[u]) for u in U]
    st = [st_ref[s] for s in range(RWKV_SLABS)]
    y = [None for u in U]
    for u in U:
        st_hi, st_lo = _split2(st[slab[u]])
        yq = _dot(cat0([q2[u].astype(BF16), m_hl[u][0], m_hl[u][1]]), st_hi)
        y[u] = yq[0:L] + p2[u]
        carry = yq[L:L + W] + (yq[L + W:] + _dot(m_hl[u][0], st_lo))
        st[slab[u]] = carry + jnp.where(same_head, mp[u][:, W:], 0.0)
    for s in range(RWKV_SLABS):
        st_ref[s] = st[s]

    sums = [_sum3_exact_rhs([y[u], r[u] * k[u] * rk_ref[slab[u]]], ones) for u in U]
    yc = [y[u] - sums[u][0] * (1.0 / HEAD_DIM) for u in U]
    var = [_sum3_exact_rhs([yc[u] * yc[u]], ones)[0] for u in U]
    for u in U:
        yn = yc[u] * lax.rsqrt(var[u] * (1.0 / HEAD_DIM) + RWKV_GN_EPS) * lnw_ref[slab[u]] + lnb_ref[slab[u]]
        o_ref[slab[u], rows[u], :] = ((yn + sums[u][1] * v[u]) * g_ref[slab[u], rows[u], :]).astype(o_ref.dtype)


def rwkv_scan(r, lw, k, v, kk, b, g, ln_w, ln_b, r_k, w1, w2, *, batch, seq):
    nchunk = seq // (SCAN_CHUNKS * CHUNK)
    blk = pl.BlockSpec((RWKV_SLABS, SCAN_CHUNKS * CHUNK, SLAB), lambda bi, ci: (0, bi * nchunk + ci, 0))
    par = pl.BlockSpec((RWKV_SLABS, 1, SLAB), lambda bi, ci: (0, 0, 0))
    t = batch * seq
    out_shape = jax.ShapeDtypeStruct((RWKV_SLABS, t, SLAB), BF16)
    state = pltpu.VMEM((RWKV_SLABS, SLAB, SLAB), F32)
    args = (r, lw, k, v, kk, b, g, ln_w.reshape(RWKV_SLABS, 1, SLAB), ln_b.reshape(RWKV_SLABS, 1, SLAB),
            r_k.reshape(RWKV_SLABS, 1, SLAB))
    n_chunks, gph, w1v, w2v = _expert_weight_views(w1, w2)
    if n_chunks != batch * nchunk:
        ra = pl.pallas_call(
            _rwkv_scan_kernel, grid=(batch, nchunk), in_specs=[blk] * 7 + [par] * 3, out_specs=blk,
            out_shape=out_shape, scratch_shapes=[state],
            compiler_params=_cparams(("arbitrary", "arbitrary")), name="rwkv_scan",
        )(*args)
        return (ra,) + expert_weight_prep(w1, w2)
    w_in, w_out, w_shape, w_scratch = _expert_weight_specs(n_chunks, gph, w1v, w2v, lambda bi, ci: bi * nchunk + ci)
    ra, w1b, w2b = pl.pallas_call(
        _rwkv_scan_kernel, grid=(batch, nchunk), in_specs=[blk] * 7 + [par] * 3 + w_in,
        out_specs=[blk] + w_out, out_shape=[out_shape] + w_shape, scratch_shapes=[state, w_scratch],
        compiler_params=_cparams(("arbitrary", "arbitrary")), name="rwkv_scan",
    )(*args, w1v, w2v, w2v)
    return ra, w1b.reshape(w1.shape), w2b.reshape(w2.shape)


def _diff_attn_kernel(qi_ref, ki_ref, slope_ref, q_ref, k_ref, v_ref, lq1_ref, lk1_ref, lq2_ref, lk2_ref, sg_ref,
                      o_ref, qs_ref, relb_ref, m_ref, l_ref, acc_ref, *, tq, lam_init):
    h = pl.program_id(1)
    qi = qi_ref[pl.program_id(2)]
    ki = ki_ref[pl.program_id(2)]
    c2 = slope_ref[h] * LOG2E

    @pl.when(ki == 0)
    def _():
        m_ref[...] = jnp.full_like(m_ref, -jnp.inf)
        l_ref[...] = jnp.zeros_like(l_ref)
        acc_ref[...] = jnp.zeros_like(acc_ref)
        q = q_ref[...].astype(F32) * (HEAD_DIM ** -0.5 * LOG2E)
        lane = lax.broadcasted_iota(jnp.int32, q.shape, 1)
        m0 = lane < HEAD_DIM
        qs_ref[0:tq, :] = jnp.where(m0, q, 0.0).astype(BF16)
        qs_ref[tq:, :] = jnp.where(m0, 0.0, q).astype(BF16)

    @pl.when(pl.program_id(2) == 0)
    def _():
        rel = (lax.broadcasted_iota(jnp.int32, (tq, tq), 1) - lax.broadcasted_iota(jnp.int32, (tq, tq), 0))
        relb = rel.astype(F32) * (-c2)
        relb_ref[:, 0:tq] = relb
        relb_ref[:, tq:] = relb

    def step(masked):
        k = k_ref[...]
        vt = v_ref[...].T
        relb = relb_ref[...]
        tile_bias = c2 * ((qi - ki) * tq).astype(F32)
        s = _dot(k, qs_ref[...], NT) + relb
        if masked:
            s = jnp.where(relb > 0.0, -jnp.inf, s)
        m_prev = m_ref[...]
        m_new = jnp.maximum(m_prev, jnp.max(s, axis=0, keepdims=True) - tile_bias)
        alpha = jnp.exp2(m_prev - m_new)
        p = jnp.exp2(s - (m_new + tile_bias))
        l_ref[...] = alpha * l_ref[...] + jnp.sum(p, axis=0, keepdims=True)
        acc_ref[...] = alpha * acc_ref[...] + _dot(vt, p.astype(BF16))
        m_ref[...] = m_new

    @pl.when(ki < qi)
    def _():
        step(False)

    @pl.when(ki == qi)
    def _():
        step(True)
        lam = (jnp.exp(jnp.sum(lq1_ref[...] * lk1_ref[...], axis=-1, keepdims=True))
               - jnp.exp(jnp.sum(lq2_ref[...] * lk2_ref[...], axis=-1, keepdims=True)) + lam_init)
        ot = acc_ref[:, 0:tq] / l_ref[:, 0:tq] - lam * (acc_ref[:, tq:] / l_ref[:, tq:])
        o = _rms(ot.T, sg_ref[...], NORM_EPS) * (1.0 - lam_init)
        o_ref[...] = o.astype(o_ref.dtype)


def diff_attention(qkv, slopes, lq1, lk1, lq2, lk2, subln_g, *, batch, seq, tq, lam_init):
    t = batch * seq
    nq = seq // tq
    hb = DIFF_WIDTH // LANES
    pairs = [(qi, ki) for qi in range(nq) for ki in range(qi + 1)]
    qi_tab = jnp.asarray([pq for pq, _ in pairs], jnp.int32)
    ki_tab = jnp.asarray([pk for _, pk in pairs], jnp.int32)
    small = pl.BlockSpec((1, HEAD_DIM), lambda b, h, j, qt, kt: (0, 0))
    grid_spec = pltpu.PrefetchScalarGridSpec(
        num_scalar_prefetch=2,
        grid=(batch, DIFF_HEADS, len(pairs)),
        in_specs=[
            pl.BlockSpec(memory_space=pltpu.SMEM),
            pl.BlockSpec((tq, LANES), lambda b, h, j, qt, kt: (b * nq + qt[j], h)),
            pl.BlockSpec((tq, LANES), lambda b, h, j, qt, kt: (b * nq + kt[j], hb + h)),
            pl.BlockSpec((tq, LANES), lambda b, h, j, qt, kt: (b * nq + kt[j], 2 * hb + h)),
            small, small, small, small,
            pl.BlockSpec((1, LANES), lambda b, h, j, qt, kt: (0, 0)),
        ],
        out_specs=pl.BlockSpec((tq, LANES), lambda b, h, j, qt, kt: (b * nq + qt[j], h)),
        scratch_shapes=[pltpu.VMEM((2 * tq, LANES), BF16), pltpu.VMEM((tq, 2 * tq), F32),
                        pltpu.VMEM((1, 2 * tq), F32), pltpu.VMEM((1, 2 * tq), F32),
                        pltpu.VMEM((LANES, 2 * tq), F32)],
    )
    return pl.pallas_call(
        functools.partial(_diff_attn_kernel, tq=tq, lam_init=lam_init),
        grid_spec=grid_spec,
        out_shape=jax.ShapeDtypeStruct((t, DIFF_WIDTH), BF16),
        compiler_params=_cparams(("parallel", "parallel", "arbitrary")),
        name="diff_attention",
    )(qi_tab, ki_tab, slopes, qkv, qkv, qkv, lq1.reshape(1, -1), lk1.reshape(1, -1), lq2.reshape(1, -1),
      lk2.reshape(1, -1), subln_g.reshape(1, -1))


def _mix_kernel(ra_ref, da_ref, ga_ref, gb_ref, x_ref, wa_ref, wb_ref, wo_ref, o_ref):
    ya = _dot(ra_ref[0], wa_ref[0:SLAB, :])
    for q in range(1, RWKV_SLABS):
        ya = ya + _dot(ra_ref[q], wa_ref[q * SLAB:(q + 1) * SLAB, :])
    yb = _dot(da_ref[...], wb_ref[...])
    mixed = ga_ref[...].astype(F32) * ya + gb_ref[...].astype(F32) * yb
    o_ref[...] = x_ref[...] + _dot(mixed.astype(BF16), wo_ref[...])


def mix_project(ra, da, gates, x, wa, wb, wo, *, tm):
    t, d = x.shape
    const = lambda a: pl.BlockSpec(a.shape, lambda i: (0, 0), pipeline_mode=pl.Buffered(1))
    return pl.pallas_call(
        _mix_kernel,
        grid=(t // tm,),
        in_specs=[
            pl.BlockSpec((RWKV_SLABS, tm, SLAB), lambda i: (0, i, 0)),
            pl.BlockSpec((tm, DIFF_WIDTH), lambda i: (i, 0)),
            pl.BlockSpec((tm, d), lambda i: (i, 0)),
            pl.BlockSpec((tm, d), lambda i: (i, 1)),
            pl.BlockSpec((tm, d), lambda i: (i, 0)),
            const(wa), const(wb), const(wo),
        ],
        out_specs=pl.BlockSpec((tm, d), lambda i: (i, 0)),
        out_shape=jax.ShapeDtypeStruct((t, d), F32),
        compiler_params=_cparams(("parallel",)),
        name="mix_project",
    )(ra, da, gates, gates, x, wa, wb, wo)


def _cross_kernel(x_ref, gc_ref, wq_ref, kv_ref, wo_ref, gf_ref, rw_ref, rb_ref, x2_ref, hf_ref, lg_ref):
    x = x_ref[...]
    h = _rms(x, gc_ref[...], NORM_EPS).astype(BF16)
    q = _dot(h, wq_ref[...]).astype(BF16)
    scale = LANES ** -0.5
    outs = []
    for hd in range(CROSS_HEADS):
        qh = q[:, hd * LANES:(hd + 1) * LANES]
        kh = kv_ref[0, :, hd * LANES:(hd + 1) * LANES]
        vh = kv_ref[0, :, CROSS_WIDTH + hd * LANES:CROSS_WIDTH + (hd + 1) * LANES]
        s = _dot(qh, kh, NT) * scale
        s = s - jnp.max(s, axis=-1, keepdims=True)
        e = jnp.exp(s)
        p = e / jnp.sum(e, axis=-1, keepdims=True)
        outs.append(_dot(p.astype(BF16), vh))
    o = jnp.concatenate(outs, axis=1).astype(BF16)
    x2 = x + _dot(o, wo_ref[...])
    x2_ref[...] = x2
    hf = _rms(x2, gf_ref[...], NORM_EPS)
    tm = hf.shape[0]
    for c in range(PACK_ROWS):
        hf_ref[pl.ds(c, tm, stride=PACK_ROWS), :] = _pack_pair(hf[:, 2 * c * LANES:(2 * c + 1) * LANES],
                                                               hf[:, (2 * c + 1) * LANES:(2 * c + 2) * LANES])
    lg_ref[...] = _dot3(hf, rw_ref[...]) + rb_ref[...]


def cross_attention(x, gc, wq, kv, wo, gf, rw, rb, *, seq, tm):
    t, d = x.shape
    per_seq = seq // tm
    const = lambda a: pl.BlockSpec(a.shape, lambda i: (0,) * a.ndim, pipeline_mode=pl.Buffered(1))
    vec = lambda n: pl.BlockSpec((1, n), lambda i: (0, 0))
    return pl.pallas_call(
        _cross_kernel,
        grid=(t // tm,),
        in_specs=[
            pl.BlockSpec((tm, d), lambda i: (i, 0)),
            vec(d), const(wq),
            pl.BlockSpec((1,) + kv.shape[1:], lambda i: (i // per_seq, 0, 0)),
            const(wo), vec(d), const(rw), vec(LANES),
        ],
        out_specs=[pl.BlockSpec((tm, d), lambda i: (i, 0)), pl.BlockSpec((tm * PACK_ROWS, LANES), lambda i: (i, 0)),
                   pl.BlockSpec((tm, LANES), lambda i: (i, 0))],
        out_shape=[jax.ShapeDtypeStruct((t, d), F32), jax.ShapeDtypeStruct((t * PACK_ROWS, LANES), U32),
                   jax.ShapeDtypeStruct((t, LANES), F32)],
        compiler_params=_cparams(("parallel",)),
        name="cross_attention",
    )(x, gc.reshape(1, d), wq, kv, wo, gf.reshape(1, d), rw, rb)


def _router_kernel(lg_ref, idx_ref, gate_ref):
    x = lg_ref[...]
    lane = lax.broadcasted_iota(jnp.int32, x.shape, 1)
    x = jnp.where(lane < N_EXPERTS, x, -jnp.inf)
    idx_out = jnp.zeros(x.shape, jnp.int32)
    val_out = jnp.zeros(x.shape, F32)
    vals = []
    for j in range(TOP_K):
        m = jnp.max(x, axis=-1, keepdims=True)
        sel = jnp.min(jnp.where(x == m, lane, LANES), axis=-1, keepdims=True)
        idx_out = jnp.where(lane == j, sel, idx_out)
        vals.append(m)
        x = jnp.where(lane == sel, -jnp.inf, x)
    es = [jnp.exp(vj - vals[0]) for vj in vals]
    tot = es[0] + es[1] + es[2] + es[3]
    for j in range(TOP_K):
        val_out = jnp.where(lane == j, es[j] / tot, val_out)
    idx_ref[...] = idx_out[:, 0:TOP_K]
    gate_ref[...] = val_out[:, 0:TOP_K]


def router_topk(logits, *, tm):
    t = logits.shape[0]
    return pl.pallas_call(
        _router_kernel,
        grid=(t // tm,),
        in_specs=[pl.BlockSpec((tm, LANES), lambda i: (i, 0))],
        out_specs=[pl.BlockSpec((tm, TOP_K), lambda i: (i, 0)), pl.BlockSpec((tm, TOP_K), lambda i: (i, 0))],
        out_shape=[jax.ShapeDtypeStruct((t, TOP_K), jnp.int32), jax.ShapeDtypeStruct((t, TOP_K), F32)],
        compiler_params=_cparams(("parallel",)),
        name="router_topk",
    )(logits)


def _moe_kernel(sbe_ref, sbr_ref, src_ref, tok_ref, ntok_ref, hf_hbm, w1_ref, b1_ref, w2_ref, b2_ref,
                ys_hbm, xbuf, x2d, acc, obuf, gsem, ssem):
    s = pl.program_id(0)
    f = pl.program_id(1)
    nf = pl.num_programs(1)
    nsb = pl.num_programs(0)
    sub_rows = MOE_SUB * PACK_ROWS
    nsubs = MOE_SUPER // MOE_SUB
    unroll = 8

    def ceil_sub(n):
        return lax.shift_right_logical(n + (MOE_SUB - 1), MOE_SUB.bit_length() - 1)

    rows = sbr_ref[s]
    nsub = ceil_sub(rows)
    rows_next = jnp.where(s + 1 < nsb, sbr_ref[jnp.minimum(s + 1, nsb - 1)], 0)
    rows_prev = jnp.where(s > 0, sbr_ref[jnp.maximum(s - 1, 0)], 0)
    nsub_next = ceil_sub(rows_next)

    def slab(ref, off):
        return ref.at[pl.ds(pl.multiple_of(off, PACK_ROWS), PACK_ROWS)]

    def row_in(table, r):
        return pltpu.make_async_copy(slab(hf_hbm, table[0, 0, r]), slab(xbuf, r * PACK_ROWS), gsem)

    def row_out(r):
        return pltpu.make_async_copy(slab(obuf, r * PACK_ROWS), slab(ys_hbm, src_ref[0, 0, r] * PACK_ROWS), ssem)

    def start_gather(table, first, n):
        def group(gi, c):
            for j in range(unroll):
                row_in(table, first + gi * unroll + j).start()
            return c

        lax.fori_loop(0, n // unroll, group, 0)

    def wait_sub_blocks(buf, other, sem, n):
        for sub in range(nsubs):
            @pl.when(sub < n)
            def _(sub=sub):
                pltpu.make_async_copy(other.at[pl.ds(0, sub_rows)], buf.at[pl.ds(sub * sub_rows, sub_rows)], sem).wait()

    def wait_scatter(n):
        full = n // MOE_SUB
        for sub in range(nsubs):
            @pl.when(sub < full)
            def _(sub=sub):
                pltpu.make_async_copy(obuf.at[pl.ds(sub * sub_rows, sub_rows)], ys_hbm.at[pl.ds(0, sub_rows)], ssem).wait()

        def one(r, c):
            row_out(r).wait()
            return c

        lax.fori_loop(0, n - full * MOE_SUB, one, 0)

    @pl.when((f == 0) & (s == 0) & (rows > 0))
    def _():
        start_gather(tok_ref, 0, nsub * MOE_SUB)

    @pl.when((f == 0) & (rows > 0))
    def _():
        wait_sub_blocks(xbuf, hf_hbm, gsem, nsub)

    @pl.when((f == nf - 1) & (rows > 0) & (rows_prev > 0))
    def _():
        wait_scatter(rows_prev)

    for sub in range(nsubs):
        @pl.when(sub < nsub)
        def _(sub=sub):
            sl = slice(sub * MOE_SUB, (sub + 1) * MOE_SUB)
            base = sub * sub_rows

            @pl.when(f == 0)
            def _():
                for c in range(PACK_ROWS):
                    lo, hi = _unpack_pair(xbuf[pl.ds(base + c, MOE_SUB, stride=PACK_ROWS), :])
                    x2d[sl, 2 * c * LANES:(2 * c + 1) * LANES] = lo.astype(BF16)
                    x2d[sl, (2 * c + 1) * LANES:(2 * c + 2) * LANES] = hi.astype(BF16)

            x = x2d[sl, :]
            hb = _dot(x, w1_ref[0]) + b1_ref[0]
            even = (lax.broadcasted_iota(jnp.int32, (MOE_SUB, LANES), 1) % 2) == 0
            acts = []
            for cb in range(MOE_FT // LANES):
                a = hb[:, cb * LANES:(cb + 1) * LANES]
                b = hb[:, MOE_FT + cb * LANES:MOE_FT + (cb + 1) * LANES]
                hg = jnp.where(even, a, pltpu.roll(b, 1, axis=1))
                hl = jnp.where(even, pltpu.roll(a, LANES - 1, axis=1), b)
                xg = jnp.minimum(hg, SWIGLU_LIMIT)
                xl = jnp.clip(hl, -SWIGLU_LIMIT, SWIGLU_LIMIT)
                acts.append((xg * jax.nn.sigmoid(SWIGLU_ALPHA * xg) * (xl + 1.0)).astype(BF16))
            contrib = _dot(jnp.concatenate(acts, axis=1), w2_ref[0])

            @pl.when(f == 0)
            def _():
                acc[sl, :] = contrib

            @pl.when((f > 0) & (f < nf - 1))
            def _():
                acc[sl, :] += contrib

            @pl.when(f == nf - 1)
            def _():
                y = acc[sl, :] + contrib + b2_ref[0]
                for c in range(PACK_ROWS):
                    obuf[pl.ds(base + c, MOE_SUB, stride=PACK_ROWS), :] = _pack_pair(
                        y[:, 2 * c * LANES:(2 * c + 1) * LANES], y[:, (2 * c + 1) * LANES:(2 * c + 2) * LANES])

    @pl.when((f == 0) & (rows_next > 0))
    def _():
        start_gather(ntok_ref, 0, nsub_next * MOE_SUB)

    @pl.when((f == nf - 1) & (rows > 0))
    def _():
        def group(gi, c):
            for j in range(unroll):
                row_out(gi * unroll + j).start()
            return c

        def one(r, c):
            row_out(r).start()
            return c

        ngroups = rows // unroll
        lax.fori_loop(0, ngroups, group, 0)
        lax.fori_loop(ngroups * unroll, rows, one, 0)

        @pl.when(rows_next == 0)
        def _():
            wait_scatter(rows)


def moe_experts(sb_e, sb_rows, row_src, row_off, hf, w1, b1, w2, b2):
    d = D_MODEL
    t = hf.shape[0] // PACK_ROWS
    nsb = sb_e.shape[0]
    nf = D_EXPERT // MOE_FT
    assert nf >= 2, "the last hidden-column step adds its own contribution to the accumulator of the earlier ones"
    grid_spec = pltpu.PrefetchScalarGridSpec(
        num_scalar_prefetch=2,
        grid=(nsb, nf),
        in_specs=[
            pl.BlockSpec((1, 1, MOE_SUPER), lambda s, f, e, r: (s, 0, 0), memory_space=pltpu.SMEM),
            pl.BlockSpec((1, 1, MOE_SUPER), lambda s, f, e, r: (s, 0, 0), memory_space=pltpu.SMEM),
            pl.BlockSpec((1, 1, MOE_SUPER), lambda s, f, e, r: (jnp.minimum(s + 1, nsb - 1), 0, 0),
                         memory_space=pltpu.SMEM),
            pl.BlockSpec(memory_space=pl.ANY),
            pl.BlockSpec((1, d, 2 * MOE_FT), lambda s, f, e, r: (e[s], 0, f)),
            pl.BlockSpec((1, 1, 2 * MOE_FT), lambda s, f, e, r: (e[s], 0, f)),
            pl.BlockSpec((1, MOE_FT, d), lambda s, f, e, r: (e[s], f, 0)),
            pl.BlockSpec((1, 1, d), lambda s, f, e, r: (e[s], 0, 0)),
        ],
        out_specs=pl.BlockSpec(memory_space=pl.ANY),
        scratch_shapes=[pltpu.VMEM((MOE_SUPER * PACK_ROWS, LANES), U32), pltpu.VMEM((MOE_SUPER, d), BF16),
                        pltpu.VMEM((MOE_SUPER, d), F32), pltpu.VMEM((MOE_SUPER * PACK_ROWS, LANES), U32),
                        pltpu.SemaphoreType.DMA, pltpu.SemaphoreType.DMA],
    )
    return pl.pallas_call(
        _moe_kernel,
        grid_spec=grid_spec,
        out_shape=jax.ShapeDtypeStruct((t * TOP_K * PACK_ROWS, LANES), U32),
        compiler_params=_cparams(("arbitrary", "arbitrary")),
        name="moe_experts",
    )(sb_e, sb_rows, row_src, row_off, row_off, hf, w1, b1, w2, b2)


def _combine_kernel(ys_ref, gate_ref, x_ref, g_ref, o_ref, lo_ref, hi_ref):
    tm = x_ref.shape[0]
    tot_lo = tot_hi = None
    for j in range(TOP_K):
        lo, hi = _unpack_pair(ys_ref[:, j * PACK_ROWS:(j + 1) * PACK_ROWS, :])
        gj = gate_ref[:, j:j + 1, :]
        tot_lo = lo * gj if tot_lo is None else tot_lo + lo * gj
        tot_hi = hi * gj if tot_hi is None else tot_hi + hi * gj
    lo_ref[...] = tot_lo.reshape(tm * PACK_ROWS, LANES)
    hi_ref[...] = tot_hi.reshape(tm * PACK_ROWS, LANES)
    pieces = []
    for c in range(PACK_ROWS):
        pieces.append(lo_ref[pl.ds(c, tm, stride=PACK_ROWS), :])
        pieces.append(hi_ref[pl.ds(c, tm, stride=PACK_ROWS), :])
    out = x_ref[...] + jnp.concatenate(pieces, axis=1)
    o_ref[...] = _rms(out, g_ref[...], NORM_EPS)


def combine_final(ys, gate, x, g, *, tm):
    t, d = x.shape
    return pl.pallas_call(
        _combine_kernel,
        grid=(t // tm,),
        in_specs=[pl.BlockSpec((tm, TOP_K * PACK_ROWS, LANES), lambda i: (i, 0, 0)),
                  pl.BlockSpec((tm, TOP_K, LANES), lambda i: (i, 0, 0)),
                  pl.BlockSpec((tm, d), lambda i: (i, 0)), pl.BlockSpec((1, d), lambda i: (0, 0))],
        out_specs=pl.BlockSpec((tm, d), lambda i: (i, 0)),
        out_shape=jax.ShapeDtypeStruct((t, d), F32),
        scratch_shapes=[pltpu.VMEM((tm * PACK_ROWS, LANES), F32), pltpu.VMEM((tm * PACK_ROWS, LANES), F32)],
        compiler_params=_cparams(("parallel",)),
        name="combine_final",
    )(ys, gate, x, g.reshape(1, d))


def _routing_tables(top_idx, n_super):
    flat_e = top_idx.reshape(-1)
    n = flat_e.shape[0]
    onehot = (flat_e[:, None] == jnp.arange(N_EXPERTS, dtype=jnp.int32)[None, :]).astype(jnp.int32)
    csum = jnp.cumsum(onehot, axis=0)
    rank = jnp.sum(onehot * csum, axis=1) - 1
    counts = csum[-1]
    nsb = (counts + MOE_SUPER - 1) // MOE_SUPER
    sb_end = jnp.cumsum(nsb)
    sb_start = sb_end - nsb
    dest = sb_start[flat_e] * MOE_SUPER + rank
    row_src = jnp.full((n_super * MOE_SUPER,), -1, jnp.int32).at[dest].set(jnp.arange(n, dtype=jnp.int32))
    s_ids = jnp.arange(n_super, dtype=jnp.int32)
    sb_e = jnp.minimum(jnp.searchsorted(sb_end, s_ids, side="right"), N_EXPERTS - 1).astype(jnp.int32)
    local = s_ids - sb_start[sb_e]
    sb_rows = jnp.clip(counts[sb_e] - local * MOE_SUPER, 0, MOE_SUPER)
    sb_rows = jnp.where(s_ids < sb_end[-1], sb_rows, 0).astype(jnp.int32)
    row_off = jnp.where(row_src >= 0, (row_src // TOP_K) * PACK_ROWS, 0)
    return sb_e, sb_rows, row_src.reshape(n_super, 1, MOE_SUPER), row_off.reshape(n_super, 1, MOE_SUPER)


def _pad_rows(a, n):
    return jnp.pad(a, ((0, n - a.shape[0]), (0, 0)))


def _layer(x, mem, l, p, batch, seq):
    t = batch * seq
    c = RWKV_WIDTH
    lam_init = 0.8 - 0.6 * math.exp(-0.3 * l)
    w_in = p["w_in"]
    o1 = 3 * c + DECAY_LORA + AAA_LORA + GATE_LORA
    o2 = o1 + 3 * DIFF_WIDTH
    padc = lambda a, n: jnp.pad(a, ((0, 0), (0, n - a.shape[1])))
    w_rwkv = jnp.concatenate([
        w_in[:, :3 * c],
        padc(w_in[:, 3 * c:3 * c + DECAY_LORA], 128),
        padc(w_in[:, 3 * c + DECAY_LORA:3 * c + DECAY_LORA + AAA_LORA], 128),
        padc(w_in[:, 3 * c + DECAY_LORA + AAA_LORA:o1], 256)], axis=1).astype(BF16)
    mu = p["rwkv_mu"]
    pad1 = lambda a, n: jnp.pad(a, (0, n - a.shape[0]))
    mu_p = jnp.concatenate([mu[:3 * c], pad1(mu[3 * c:3 * c + DECAY_LORA], 128),
                            pad1(mu[3 * c + DECAY_LORA:3 * c + DECAY_LORA + AAA_LORA], 128),
                            pad1(mu[3 * c + DECAY_LORA + AAA_LORA:], 256)])
    w_diff = w_in[:, o1:o2].astype(BF16)
    w_gate = w_in[:, o2:].astype(BF16)

    g_mix = p["norm_mix_g"]
    p_rwkv = norm_matmul(x, g_mix, w_rwkv, tm=1024, tn=RWKV_COLS // 2, out_dtype=F32, name="in_proj_rwkv")
    qkv = norm_matmul(x, g_mix, w_diff, tm=1024, tn=1536, out_dtype=BF16, name="in_proj_diff")
    gates = norm_matmul(x, g_mix, w_gate, tm=1024, tn=2048, out_dtype=BF16, act="sigmoid", name="in_proj_gate")

    prep = rwkv_prep(p_rwkv, mu_p, p["rwkv_w0"], p["rwkv_a0"], p["rwkv_k_k"], p["rwkv_k_a"],
                     _pad_rows(p["rwkv_w2"], 128), _pad_rows(p["rwkv_a2"], 128), _pad_rows(p["rwkv_g2"], 256),
                     seq=seq, tm=256)
    ra, w1b, w2b = rwkv_scan(*prep, p["rwkv_ln_w"], p["rwkv_ln_b"], p["rwkv_r_k"].reshape(-1),
                             p["expert_w1"], p["expert_w2"], batch=batch, seq=seq)

    slopes = (2.0 ** (-8.0 * jnp.arange(1, DIFF_HEADS + 1, dtype=F32) / DIFF_HEADS)).astype(F32)
    da = diff_attention(qkv, slopes, p["diff_lq1"], p["diff_lk1"], p["diff_lq2"], p["diff_lk2"],
                        p["diff_subln_g"], batch=batch, seq=seq, tq=min(1024, seq), lam_init=lam_init)

    x1 = mix_project(ra, da, gates, x, p["rwkv_proj"].astype(BF16), p["diff_proj"].astype(BF16),
                     p["w_out"].astype(BF16), tm=512)

    m_len = mem.shape[0] // batch
    kv = norm_matmul(mem, p["norm_mem_g"], p["cross_wkv"].astype(BF16), tm=min(512, mem.shape[0]),
                     tn=2 * CROSS_WIDTH, out_dtype=BF16, name="cross_kv")
    rw = jnp.pad(p["router_w"], ((0, 0), (0, LANES - N_EXPERTS)))
    rb = jnp.pad(p["router_b"], (0, LANES - N_EXPERTS)).reshape(1, LANES)
    x2, hf, logits = cross_attention(x1, p["norm_cross_g"], p["cross_wq"].astype(BF16),
                                     kv.reshape(batch, m_len, 2 * CROSS_WIDTH), p["cross_wo"].astype(BF16),
                                     p["norm_ffn_g"], rw, rb, seq=seq, tm=512)

    top_idx, gate = router_topk(logits, tm=min(1024, t))
    n_super = (t * TOP_K) // MOE_SUPER + N_EXPERTS
    sb_e, sb_rows, row_src, row_off = _routing_tables(top_idx, n_super)
    ys = moe_experts(sb_e, sb_rows, row_src, row_off, hf, w1b, p["expert_b1"][:, None, :], w2b, p["expert_b2"][:, None, :])
    gate_b = jnp.broadcast_to(gate[:, :, None], (t, TOP_K, LANES))
    return x2, ys.reshape(t, TOP_K * PACK_ROWS, LANES), gate_b


def kernel(x, mem, norm_mix_g, w_in, rwkv_mu, rwkv_w0, rwkv_w2, rwkv_a0, rwkv_a2, rwkv_g2, rwkv_k_k, rwkv_k_a, rwkv_r_k, rwkv_ln_w, rwkv_ln_b, rwkv_proj, diff_lq1, diff_lk1, diff_lq2, diff_lk2, diff_subln_g, diff_proj, w_out, norm_cross_g, norm_mem_g, cross_wq, cross_wkv, cross_wo, norm_ffn_g, router_w, router_b, expert_w1, expert_b1, expert_w2, expert_b2, final_norm_g):
    batch, seq, d = x.shape
    stacked = dict(norm_mix_g=norm_mix_g, w_in=w_in, rwkv_mu=rwkv_mu, rwkv_w0=rwkv_w0, rwkv_w2=rwkv_w2,
                   rwkv_a0=rwkv_a0, rwkv_a2=rwkv_a2, rwkv_g2=rwkv_g2, rwkv_k_k=rwkv_k_k, rwkv_k_a=rwkv_k_a,
                   rwkv_r_k=rwkv_r_k, rwkv_ln_w=rwkv_ln_w, rwkv_ln_b=rwkv_ln_b, rwkv_proj=rwkv_proj,
                   diff_lq1=diff_lq1, diff_lk1=diff_lk1, diff_lq2=diff_lq2, diff_lk2=diff_lk2,
                   diff_subln_g=diff_subln_g, diff_proj=diff_proj, w_out=w_out, norm_cross_g=norm_cross_g,
                   norm_mem_g=norm_mem_g, cross_wq=cross_wq, cross_wkv=cross_wkv, cross_wo=cross_wo,
                   norm_ffn_g=norm_ffn_g, router_w=router_w, router_b=router_b, expert_w1=expert_w1,
                   expert_b1=expert_b1, expert_w2=expert_w2, expert_b2=expert_b2)
    assert w_in.shape[0] == 1, "the closing RMSNorm is fused into the single layer's combine"
    p = {k: v[0] for k, v in stacked.items()}
    x2, ys, gate_b = _layer(x.reshape(batch * seq, d), mem.reshape(-1, d), 0, p, batch, seq)
    out = combine_final(ys, gate_b, x2, final_norm_g, tm=256)
    return out.reshape(batch, seq, d)
```

```python
import functools
import math

import jax
import jax.numpy as jnp
from jax import lax
from jax.experimental import pallas as pl
from jax.experimental.pallas import tpu as pltpu

F32 = jnp.float32
BF16 = jnp.bfloat16
U32 = jnp.uint32

D_MODEL = 2048
NORM_EPS = 1e-5
LOG2E = 1.4426950408889634
LANES = 128
HEAD_DIM = 64
RWKV_WIDTH = 1024
SLAB = 256
RWKV_SLABS = RWKV_WIDTH // SLAB
HEADS_PER_SLAB = SLAB // HEAD_DIM
CHUNK = 64
SCAN_CHUNKS = 2
RWKV_GN_EPS = 64e-5
DECAY_LORA = 64
AAA_LORA = 64
GATE_LORA = 160
LORA_PAD = 512
RWKV_COLS = 3 * RWKV_WIDTH + LORA_PAD
DIFF_WIDTH = 1024
DIFF_HEADS = 8
CROSS_HEADS = 4
CROSS_WIDTH = 512
N_EXPERTS = 32
TOP_K = 4
D_EXPERT = 2048
SWIGLU_ALPHA = 1.702
SWIGLU_LIMIT = 7.0
MOE_SUPER = 1024
MOE_SUB = 256
MOE_FT = 1024
W2_GROUP = 256
ROW_TILE = D_MODEL // LANES
PACK_ROWS = ROW_TILE // 2
VMEM_LIMIT = 56 * 1024 * 1024

NN = (((1,), (0,)), ((), ()))
NT = (((1,), (1,)), ((), ()))


def _dot(a, b, dims=NN):
    return lax.dot_general(a, b, dims, preferred_element_type=F32)


def _split2(a):
    hi = a.astype(BF16)
    lo = (a - hi.astype(F32)).astype(BF16)
    return hi, lo


def _split3(a):
    hi = a.astype(BF16)
    r1 = a - hi.astype(F32)
    mid = r1.astype(BF16)
    lo = (r1 - mid.astype(F32)).astype(BF16)
    return hi, mid, lo


def _dot3(a, b, dims=NN):
    ah, al = _split2(a)
    bh, bl = _split2(b)
    return _dot(ah, bh, dims) + (_dot(ah, bl, dims) + _dot(al, bh, dims))


def _dot_exact_rhs(a, b_bf16, dims=NN):
    h, m, l = _split3(a)
    return _dot(h, b_bf16, dims) + (_dot(m, b_bf16, dims) + _dot(l, b_bf16, dims))


def _pack_pair(lo, hi):
    lo_b = lax.shift_right_logical(lax.bitcast_convert_type(lo.astype(BF16).astype(F32), U32), jnp.uint32(16))
    hi_b = lax.bitcast_convert_type(hi.astype(BF16).astype(F32), U32) & jnp.uint32(0xFFFF0000)
    return hi_b | lo_b


def _unpack_pair(w):
    lo = lax.bitcast_convert_type(lax.shift_left(w, jnp.uint32(16)), F32)
    hi = lax.bitcast_convert_type(w & jnp.uint32(0xFFFF0000), F32)
    return lo, hi


def _rms(x, g, eps):
    ms = jnp.mean(x * x, axis=-1, keepdims=True)
    return x * lax.rsqrt(ms + eps) * g


def _cparams(sem):
    return pltpu.CompilerParams(dimension_semantics=sem, vmem_limit_bytes=VMEM_LIMIT)


def _norm_matmul_kernel(x_ref, g_ref, w_ref, o_ref, h_ref, *, act):
    @pl.when(pl.program_id(1) == 0)
    def _():
        h_ref[...] = _rms(x_ref[...], g_ref[...], NORM_EPS).astype(BF16)

    y = _dot(h_ref[...], w_ref[...])
    if act == "sigmoid":
        y = jax.nn.sigmoid(y)
    o_ref[...] = y.astype(o_ref.dtype)


def norm_matmul(x, g, w, *, tm, tn, out_dtype, act=None, name):
    m, d = x.shape
    n = w.shape[1]
    return pl.pallas_call(
        functools.partial(_norm_matmul_kernel, act=act),
        grid=(m // tm, n // tn),
        in_specs=[
            pl.BlockSpec((tm, d), lambda i, j: (i, 0)),
            pl.BlockSpec((1, d), lambda i, j: (0, 0)),
            pl.BlockSpec((d, tn), lambda i, j: (0, j)),
        ],
        out_specs=pl.BlockSpec((tm, tn), lambda i, j: (i, j)),
        out_shape=jax.ShapeDtypeStruct((m, n), out_dtype),
        scratch_shapes=[pltpu.VMEM((tm, d), BF16)],
        compiler_params=_cparams(("parallel", "arbitrary")),
        name=name,
    )(x, g.reshape(1, d), w)


def _head_ones():
    r = lax.broadcasted_iota(jnp.int32, (SLAB, SLAB), 0)
    c = lax.broadcasted_iota(jnp.int32, (SLAB, SLAB), 1)
    return ((r // HEAD_DIM) == (c // HEAD_DIM)).astype(BF16)


def _rwkv_prep_kernel(p_ref, prev_ref, mu_ref, w0_ref, a0_ref, kk_ref, ka_ref, w2_ref, a2_ref, g2_ref,
                      r_out, lw_out, k_out, v_out, kkn_out, b_out, g_out, *, tiles_per_seq):
    c = RWKV_WIDTH
    i = pl.program_id(0)
    p = p_ref[...]
    tm = p.shape[0]
    first = (i % tiles_per_seq) == 0
    prev_row = jnp.where(first, 0.0, prev_ref[7:8, :])
    row = lax.broadcasted_iota(jnp.int32, p.shape, 0)
    shifted = jnp.where(row == 0, prev_row, pltpu.roll(p, 1, axis=0))
    ps = p + (shifted - p) * mu_ref[...]
    r = ps[:, 0:c]
    k = ps[:, c:2 * c]
    v = ps[:, 2 * c:3 * c]
    wd = ps[:, 3 * c:3 * c + 128]
    ad = ps[:, 3 * c + 128:3 * c + 256]
    gd = ps[:, 3 * c + 256:3 * c + 512]
    z = -(w0_ref[...] + _dot3(jnp.tanh(wd), w2_ref[...]))
    softplus = jnp.maximum(z, 0.0) + jnp.log1p(jnp.exp(-jnp.abs(z)))
    w = -softplus - 0.5
    lw = -jnp.exp(w)
    a = jax.nn.sigmoid(a0_ref[...] + _dot3(ad, a2_ref[...]))
    g = _dot3(jax.nn.sigmoid(gd), g2_ref[...])
    kkr = k * kk_ref[...]
    k2 = k * (1.0 + (a - 1.0) * ka_ref[...])
    ones = _head_ones()
    for q in range(RWKV_SLABS):
        sl = slice(q * SLAB, (q + 1) * SLAB)
        x = kkr[:, sl]
        ss = _dot_exact_rhs(x * x, ones)
        kkn = x / jnp.maximum(jnp.sqrt(ss), 1e-12)
        r_out[q] = r[:, sl]
        lw_out[q] = lw[:, sl]
        k_out[q] = k2[:, sl]
        v_out[q] = v[:, sl]
        kkn_out[q] = kkn
        b_out[q] = kkn * a[:, sl]
        g_out[q] = g[:, sl]


def rwkv_prep(p, mu, w0, a0, k_k, k_a, w2p, a2p, g2p, *, seq, tm):
    t, cols = p.shape
    c = RWKV_WIDTH
    vec = lambda n: pl.BlockSpec((1, n), lambda i: (0, 0))
    full = lambda a: pl.BlockSpec(a.shape, lambda i: (0, 0))
    out_spec = pl.BlockSpec((RWKV_SLABS, tm, SLAB), lambda i: (0, i, 0))
    out_shape = jax.ShapeDtypeStruct((RWKV_SLABS, t, SLAB), F32)
    return pl.pallas_call(
        functools.partial(_rwkv_prep_kernel, tiles_per_seq=seq // tm),
        grid=(t // tm,),
        in_specs=[
            pl.BlockSpec((tm, cols), lambda i: (i, 0)),
            pl.BlockSpec((8, cols), lambda i: (jnp.maximum(i * (tm // 8) - 1, 0), 0)),
            vec(cols), vec(c), vec(c), vec(c), vec(c), full(w2p), full(a2p), full(g2p),
        ],
        out_specs=[out_spec] * 7,
        out_shape=[out_shape] * 7,
        compiler_params=_cparams(("parallel",)),
        name="rwkv_prep",
    )(p, p, mu.reshape(1, cols), w0.reshape(1, c), a0.reshape(1, c), k_k.reshape(1, c), k_a.reshape(1, c),
      w2p, a2p, g2p)


def _dot1(a, b, dims=NN):
    return _dot(a.astype(BF16), b.astype(BF16), dims)


def _sum3_exact_rhs(xs, ones):
    parts = []
    for x in xs:
        parts.extend(_split3(x))
    res = _dot(jnp.concatenate(parts, axis=0), ones)
    L = xs[0].shape[0]
    return [res[(3 * i) * L:(3 * i + 1) * L] + (res[(3 * i + 1) * L:(3 * i + 2) * L] + res[(3 * i + 2) * L:(3 * i + 3) * L])
            for i in range(len(xs))]


def _expert_weight_chunk(w1_ref, w2a_ref, w2b_ref, w1o_ref, w2o_ref, tmp_ref):
    w1o_ref[0] = w1_ref[0].astype(BF16)
    half = W2_GROUP
    for c in range(ROW_TILE):
        cols = slice(c * LANES, (c + 1) * LANES)
        tmp_ref[c, pl.ds(0, half, stride=2), :] = w2a_ref[0, :, cols]
        tmp_ref[c, pl.ds(1, half, stride=2), :] = w2b_ref[0, :, cols]
        w2o_ref[0, :, cols] = tmp_ref[c].astype(BF16)


def _expert_weight_views(w1, w2):
    e, d, f2 = w1.shape
    fdim = w2.shape[1]
    groups_per_half = (MOE_FT // 2) // W2_GROUP
    n_chunks = e * (fdim // MOE_FT) * groups_per_half
    w1v = w1.reshape(n_chunks, (e * d) // n_chunks, f2)
    w2v = w2.reshape(e * fdim // W2_GROUP, W2_GROUP, d)
    return n_chunks, groups_per_half, w1v, w2v


def _expert_weight_specs(n_chunks, groups_per_half, w1v, w2v, step):
    d = w2v.shape[2]
    a_idx = lambda j: (j // groups_per_half) * (2 * groups_per_half) + j % groups_per_half
    in_specs = [pl.BlockSpec((1,) + w1v.shape[1:], lambda *ids: (step(*ids), 0, 0)),
                pl.BlockSpec((1, W2_GROUP, d), lambda *ids: (a_idx(step(*ids)), 0, 0)),
                pl.BlockSpec((1, W2_GROUP, d), lambda *ids: (a_idx(step(*ids)) + groups_per_half, 0, 0))]
    out_specs = [pl.BlockSpec((1,) + w1v.shape[1:], lambda *ids: (step(*ids), 0, 0)),
                 pl.BlockSpec((1, 2 * W2_GROUP, d), lambda *ids: (step(*ids), 0, 0))]
    out_shape = [jax.ShapeDtypeStruct(w1v.shape, BF16), jax.ShapeDtypeStruct((n_chunks, 2 * W2_GROUP, d), BF16)]
    scratch = pltpu.VMEM((ROW_TILE, 2 * W2_GROUP, LANES), F32)
    return in_specs, out_specs, out_shape, scratch


def expert_weight_prep(w1, w2):
    n_chunks, gph, w1v, w2v = _expert_weight_views(w1, w2)
    in_specs, out_specs, out_shape, scratch = _expert_weight_specs(n_chunks, gph, w1v, w2v, lambda j: j)
    w1b, w2b = pl.pallas_call(
        _expert_weight_chunk, grid=(n_chunks,), in_specs=in_specs, out_specs=out_specs, out_shape=out_shape,
        scratch_shapes=[scratch], compiler_params=_cparams(("parallel",)), name="expert_weight_prep",
    )(w1v, w2v, w2v)
    return w1b.reshape(w1.shape), w2b.reshape(w2.shape)


def _rwkv_scan_kernel(r_ref, lw_ref, k_ref, v_ref, kk_ref, b_ref, g_ref, lnw_ref, lnb_ref, rk_ref, *rest):
    if len(rest) == 2:
        o_ref, st_ref = rest
    else:
        w1_ref, w2a_ref, w2b_ref, o_ref, w1o_ref, w2o_ref, st_ref, tmp_ref = rest
        _expert_weight_chunk(w1_ref, w2a_ref, w2b_ref, w1o_ref, w2o_ref, tmp_ref)
    L = CHUNK
    W = SLAB

    @pl.when(pl.program_id(1) == 0)
    def _():
        st_ref[...] = jnp.zeros_like(st_ref)

    t_i = lax.broadcasted_iota(jnp.int32, (L, W), 0)
    lane = lax.broadcasted_iota(jnp.int32, (L, W), 1)
    j_i = lane % HEAD_DIM
    hid = lane // HEAD_DIM
    strict = j_i < t_i
    incl = j_i <= t_i
    eye = (j_i == t_i).astype(F32)
    r2 = lax.broadcasted_iota(jnp.int32, (W, W), 0)
    c2 = lax.broadcasted_iota(jnp.int32, (W, W), 1)
    same_head = (r2 // HEAD_DIM) == (c2 // HEAD_DIM)
    diag = r2 == c2
    ones = same_head.astype(BF16)
    tr = lax.broadcasted_iota(jnp.int32, (L, L), 0)
    tc = lax.broadcasted_iota(jnp.int32, (L, L), 1)
    tri = (tc <= tr).astype(BF16)

    def bd(x):
        return jnp.concatenate([jnp.where(hid == h, x, 0.0) for h in range(HEADS_PER_SLAB)], axis=0)

    chains = [(u, h) for h in range(SCAN_CHUNKS) for u in range(RWKV_SLABS)]
    U = range(len(chains))
    rows = [slice(h * L, (h + 1) * L) for _, h in chains]
    slab = [u for u, _ in chains]
    cat0 = lambda xs: jnp.concatenate(xs, axis=0)
    cat1 = lambda xs: jnp.concatenate(xs, axis=1)
    r = [r_ref[slab[u], rows[u], :] for u in U]
    lw = [lw_ref[slab[u], rows[u], :] for u in U]
    k = [k_ref[slab[u], rows[u], :] for u in U]
    v = [v_ref[slab[u], rows[u], :] for u in U]
    kkv = [kk_ref[slab[u], rows[u], :] for u in U]
    bv = [b_ref[slab[u], rows[u], :] for u in U]
    c3 = [_dot(tri, cat1(_split3(lw[u]))) for u in U]
    cum = [c3[u][:, 0:W] + (c3[u][:, W:2 * W] + c3[u][:, 2 * W:]) for u in U]
    cum_end = [cum[u][L - 1:L, :] for u in U]
    e_cum = [jnp.exp(cum[u]) for u in U]
    e_neg = [jnp.exp(-cum[u]) for u in U]
    e_end = [jnp.exp(cum_end[u] - cum[u]) for u in U]
    at = [-kkv[u] * jnp.exp(cum[u] - lw[u]) for u in U]
    rt = [r[u] * e_cum[u] for u in U]
    bt = [bv[u] * e_neg[u] for u in U]
    kt = [k[u] * e_neg[u] for u in U]
    bh = [bv[u] * e_end[u] for u in U]
    kh = [k[u] * e_end[u] for u in U]
    w_end = [jnp.exp(cum_end[u]) for u in U]

    g_all = [_dot1(cat0([at[u], rt[u]]), cat0([bd(bt[u]), bd(kt[u])]), NT) for u in U]
    a_ab = [jnp.where(strict, g_all[u][0:L, 0:W], 0.0) for u in U]
    a_ak = [jnp.where(strict, g_all[u][0:L, W:], 0.0) for u in U]
    a_rb = [jnp.where(incl, g_all[u][L:, 0:W], 0.0) for u in U]
    a_rk = [jnp.where(incl, g_all[u][L:, W:], 0.0) for u in U]

    tinv = [eye for u in U]
    pw = a_ab
    for it in range(6):
        if it < 5:
            res = [_dot1(pw[u], cat1([bd(tinv[u]), bd(pw[u])])) for u in U]
            tinv = [tinv[u] + res[u][:, 0:W] for u in U]
            pw = [res[u][:, W:] for u in U]
        else:
            tinv = [tinv[u] + _dot1(pw[u], bd(tinv[u])) for u in U]

    bdv = [bd(v[u]) for u in U]
    av = [_dot1(a_ak[u], bdv[u]) for u in U]
    qp = [_dot1(tinv[u], cat1([bd(at[u]), bd(av[u])])) for u in U]
    q1 = [qp[u][:, 0:W] for u in U]
    p1 = [qp[u][:, W:] for u in U]
    qp2 = [_dot1(cat1([a_rb[u], a_rk[u]]),
                  cat0([cat1([bd(q1[u]), bd(p1[u])]), cat1([jnp.zeros_like(bdv[u]), bdv[u]])])) for u in U]
    q2 = [rt[u] + qp2[u][:, 0:W] for u in U]
    p2 = [qp2[u][:, W:] for u in U]
    mp = [_dot1(cat0([bh[u], kh[u]]).T,
                 cat0([cat1([q1[u], p1[u]]), cat1([jnp.zeros_like(v[u]), v[u]])])) for u in U]
    m_bd = [jnp.where(same_head, mp[u][:, 0:W], 0.0) + jnp.where(diag, w_end[u], 0.0) for u in U]
    m_hl = [_split2(m_bd[u]) for u in U]
    st = [st_ref[s] for s in range(RWKV_SLABS)]
    y = [None for u in U]
    for u in U:
        st_hi, st_lo = _split2(st[slab[u]])
        yq = _dot(cat0([q2[u].astype(BF16), m_hl[u][0], m_hl[u][1]]), st_hi)
        y[u] = yq[0:L] + p2[u]
        carry = yq[L:L + W] + (yq[L + W:] + _dot(m_hl[u][0], st_lo))
        st[slab[u]] = carry + jnp.where(same_head, mp[u][:, W:], 0.0)
    for s in range(RWKV_SLABS):
        st_ref[s] = st[s]

    sums = [_sum3_exact_rhs([y[u], r[u] * k[u] * rk_ref[slab[u]]], ones) for u in U]
    yc = [y[u] - sums[u][0] * (1.0 / HEAD_DIM) for u in U]
    var = [_sum3_exact_rhs([yc[u] * yc[u]], ones)[0] for u in U]
    for u in U:
        yn = yc[u] * lax.rsqrt(var[u] * (1.0 / HEAD_DIM) + RWKV_GN_EPS) * lnw_ref[slab[u]] + lnb_ref[slab[u]]
        o_ref[slab[u], rows[u], :] = ((yn + sums[u][1] * v[u]) * g_ref[slab[u], rows[u], :]).astype(o_ref.dtype)


def rwkv_scan(r, lw, k, v, kk, b, g, ln_w, ln_b, r_k, w1, w2, *, batch, seq):
    nchunk = seq // (SCAN_CHUNKS * CHUNK)
    blk = pl.BlockSpec((RWKV_SLABS, SCAN_CHUNKS * CHUNK, SLAB), lambda bi, ci: (0, bi * nchunk + ci, 0))
    par = pl.BlockSpec((RWKV_SLABS, 1, SLAB), lambda bi, ci: (0, 0, 0))
    t = batch * seq
    out_shape = jax.ShapeDtypeStruct((RWKV_SLABS, t, SLAB), BF16)
    state = pltpu.VMEM((RWKV_SLABS, SLAB, SLAB), F32)
    args = (r, lw, k, v, kk, b, g, ln_w.reshape(RWKV_SLABS, 1, SLAB), ln_b.reshape(RWKV_SLABS, 1, SLAB),
            r_k.reshape(RWKV_SLABS, 1, SLAB))
    n_chunks, gph, w1v, w2v = _expert_weight_views(w1, w2)
    if n_chunks != batch * nchunk:
        ra = pl.pallas_call(
            _rwkv_scan_kernel, grid=(batch, nchunk), in_specs=[blk] * 7 + [par] * 3, out_specs=blk,
            out_shape=out_shape, scratch_shapes=[state],
            compiler_params=_cparams(("arbitrary", "arbitrary")), name="rwkv_scan",
        )(*args)
        return (ra,) + expert_weight_prep(w1, w2)
    w_in, w_out, w_shape, w_scratch = _expert_weight_specs(n_chunks, gph, w1v, w2v, lambda bi, ci: bi * nchunk + ci)
    ra, w1b, w2b = pl.pallas_call(
        _rwkv_scan_kernel, grid=(batch, nchunk), in_specs=[blk] * 7 + [par] * 3 + w_in,
        out_specs=[blk] + w_out, out_shape=[out_shape] + w_shape, scratch_shapes=[state, w_scratch],
        compiler_params=_cparams(("arbitrary", "arbitrary")), name="rwkv_scan",
    )(*args, w1v, w2v, w2v)
    return ra, w1b.reshape(w1.shape), w2b.reshape(w2.shape)


def _diff_attn_kernel(qi_ref, ki_ref, slope_ref, q_ref, k_ref, v_ref, lq1_ref, lk1_ref, lq2_ref, lk2_ref, sg_ref,
                      o_ref, qs_ref, kb_ref, relb_ref, m_ref, l_ref, acc_ref, *, tq, lam_init):
    h = pl.program_id(1)
    qi = qi_ref[pl.program_id(2)]
    ki = ki_ref[pl.program_id(2)]
    c2 = slope_ref[h] * LOG2E

    @pl.when(ki == 0)
    def _():
        m_ref[...] = jnp.full_like(m_ref, -jnp.inf)
        l_ref[...] = jnp.zeros_like(l_ref)
        acc_ref[...] = jnp.zeros_like(acc_ref)
        q = q_ref[...].astype(F32) * (HEAD_DIM ** -0.5 * LOG2E)
        lane = lax.broadcasted_iota(jnp.int32, q.shape, 1)
        m0 = lane < HEAD_DIM
        qs_ref[0:tq, 0:LANES] = jnp.where(m0, q, 0.0).astype(BF16)
        qs_ref[tq:, 0:LANES] = jnp.where(m0, 0.0, q).astype(BF16)

    @pl.when(pl.program_id(2) == 0)
    def _():
        rel = (lax.broadcasted_iota(jnp.int32, (tq, tq), 1) - lax.broadcasted_iota(jnp.int32, (tq, tq), 0))
        relb = rel.astype(F32) * (-c2)
        relb_ref[:, 0:tq] = relb
        relb_ref[:, tq:] = relb
        lane1 = lax.broadcasted_iota(jnp.int32, (1, LANES), 1)
        piece = (lane1 % 6) // 2
        part = lane1 % 2
        first_group = lane1 < 6
        c_hi, c_mid, c_lo = [x.astype(F32) for x in _split3(jnp.full((1, LANES), c2, F32))]
        cp = jnp.where(piece == 0, c_hi, jnp.where(piece == 1, c_mid, c_lo))
        idx = lax.broadcasted_iota(jnp.int32, (tq, LANES), 0)
        idx_part = jnp.where(part == 0, (idx // 32) * 32, idx % 32).astype(F32)
        active = lane1 < 12
        kb_ref[...] = jnp.where(active, jnp.where(first_group, -cp, idx_part), 0.0).astype(BF16)
        qb = jnp.where(active, jnp.where(first_group, idx_part, cp), 0.0).astype(BF16)
        qs_ref[0:tq, LANES:] = qb
        qs_ref[tq:, LANES:] = qb

    def step(masked):
        vt = v_ref[...].T
        tile_bias = c2 * ((qi - ki) * tq).astype(F32)
        if masked:
            relb = relb_ref[...]
            s = _dot(k_ref[...], qs_ref[:, 0:LANES], NT) + relb
            s = jnp.where(relb > 0.0, -jnp.inf, s)
        else:
            k = jnp.concatenate([k_ref[...], kb_ref[...]], axis=1)
            s = _dot(k, qs_ref[...], NT)
        m_prev = m_ref[...]
        m_new = jnp.maximum(m_prev, jnp.max(s, axis=0, keepdims=True) - tile_bias)
        alpha = jnp.exp2(m_prev - m_new)
        p = jnp.exp2(s - (m_new + tile_bias))
        l_ref[...] = alpha * l_ref[...] + jnp.sum(p, axis=0, keepdims=True)
        acc_ref[...] = alpha * acc_ref[...] + _dot(vt, p.astype(BF16))
        m_ref[...] = m_new

    @pl.when(ki < qi)
    def _():
        step(False)

    @pl.when(ki == qi)
    def _():
        step(True)
        lam = (jnp.exp(jnp.sum(lq1_ref[...] * lk1_ref[...], axis=-1, keepdims=True))
               - jnp.exp(jnp.sum(lq2_ref[...] * lk2_ref[...], axis=-1, keepdims=True)) + lam_init)
        ot = acc_ref[:, 0:tq] / l_ref[:, 0:tq] - lam * (acc_ref[:, tq:] / l_ref[:, tq:])
        o = _rms(ot.T, sg_ref[...], NORM_EPS) * (1.0 - lam_init)
        o_ref[...] = o.astype(o_ref.dtype)


def diff_attention(qkv, slopes, lq1, lk1, lq2, lk2, subln_g, *, batch, seq, tq, lam_init):
    t = batch * seq
    nq = seq // tq
    hb = DIFF_WIDTH // LANES
    pairs = [(qi, ki) for qi in range(nq) for ki in range(qi + 1)]
    qi_tab = jnp.asarray([pq for pq, _ in pairs], jnp.int32)
    ki_tab = jnp.asarray([pk for _, pk in pairs], jnp.int32)
    small = pl.BlockSpec((1, HEAD_DIM), lambda b, h, j, qt, kt: (0, 0))
    grid_spec = pltpu.PrefetchScalarGridSpec(
        num_scalar_prefetch=2,
        grid=(batch, DIFF_HEADS, len(pairs)),
        in_specs=[
            pl.BlockSpec(memory_space=pltpu.SMEM),
            pl.BlockSpec((tq, LANES), lambda b, h, j, qt, kt: (b * nq + qt[j], h)),
            pl.BlockSpec((tq, LANES), lambda b, h, j, qt, kt: (b * nq + kt[j], hb + h)),
            pl.BlockSpec((tq, LANES), lambda b, h, j, qt, kt: (b * nq + kt[j], 2 * hb + h)),
            small, small, small, small,
            pl.BlockSpec((1, LANES), lambda b, h, j, qt, kt: (0, 0)),
        ],
        out_specs=pl.BlockSpec((tq, LANES), lambda b, h, j, qt, kt: (b * nq + qt[j], h)),
        scratch_shapes=[pltpu.VMEM((2 * tq, 2 * LANES), BF16), pltpu.VMEM((tq, LANES), BF16),
                        pltpu.VMEM((tq, 2 * tq), F32),
                        pltpu.VMEM((1, 2 * tq), F32), pltpu.VMEM((1, 2 * tq), F32),
                        pltpu.VMEM((LANES, 2 * tq), F32)],
    )
    return pl.pallas_call(
        functools.partial(_diff_attn_kernel, tq=tq, lam_init=lam_init),
        grid_spec=grid_spec,
        out_shape=jax.ShapeDtypeStruct((t, DIFF_WIDTH), BF16),
        compiler_params=_cparams(("parallel", "parallel", "arbitrary")),
        name="diff_attention",
    )(qi_tab, ki_tab, slopes, qkv, qkv, qkv, lq1.reshape(1, -1), lk1.reshape(1, -1), lq2.reshape(1, -1),
      lk2.reshape(1, -1), subln_g.reshape(1, -1))


def _mix_kernel(ra_ref, da_ref, ga_ref, gb_ref, x_ref, wa_ref, wb_ref, wo_ref, o_ref):
    ya = _dot(ra_ref[0], wa_ref[0:SLAB, :])
    for q in range(1, RWKV_SLABS):
        ya = ya + _dot(ra_ref[q], wa_ref[q * SLAB:(q + 1) * SLAB, :])
    yb = _dot(da_ref[...], wb_ref[...])
    mixed = ga_ref[...].astype(F32) * ya + gb_ref[...].astype(F32) * yb
    o_ref[...] = x_ref[...] + _dot(mixed.astype(BF16), wo_ref[...])


def mix_project(ra, da, gates, x, wa, wb, wo, *, tm):
    t, d = x.shape
    const = lambda a: pl.BlockSpec(a.shape, lambda i: (0, 0), pipeline_mode=pl.Buffered(1))
    return pl.pallas_call(
        _mix_kernel,
        grid=(t // tm,),
        in_specs=[
            pl.BlockSpec((RWKV_SLABS, tm, SLAB), lambda i: (0, i, 0)),
            pl.BlockSpec((tm, DIFF_WIDTH), lambda i: (i, 0)),
            pl.BlockSpec((tm, d), lambda i: (i, 0)),
            pl.BlockSpec((tm, d), lambda i: (i, 1)),
            pl.BlockSpec((tm, d), lambda i: (i, 0)),
            const(wa), const(wb), const(wo),
        ],
        out_specs=pl.BlockSpec((tm, d), lambda i: (i, 0)),
        out_shape=jax.ShapeDtypeStruct((t, d), F32),
        compiler_params=_cparams(("parallel",)),
        name="mix_project",
    )(ra, da, gates, gates, x, wa, wb, wo)


def _cross_kernel(x_ref, gc_ref, wq_ref, kv_ref, wo_ref, gf_ref, rw_ref, rb_ref, x2_ref, hf_ref, lg_ref):
    x = x_ref[...]
    h = _rms(x, gc_ref[...], NORM_EPS).astype(BF16)
    q = _dot(h, wq_ref[...]).astype(BF16)
    scale = LANES ** -0.5
    outs = []
    for hd in range(CROSS_HEADS):
        qh = q[:, hd * LANES:(hd + 1) * LANES]
        kh = kv_ref[0, :, hd * LANES:(hd + 1) * LANES]
        vh = kv_ref[0, :, CROSS_WIDTH + hd * LANES:CROSS_WIDTH + (hd + 1) * LANES]
        s = _dot(qh, kh, NT) * scale
        s = s - jnp.max(s, axis=-1, keepdims=True)
        e = jnp.exp(s)
        p = e / jnp.sum(e, axis=-1, keepdims=True)
        outs.append(_dot(p.astype(BF16), vh))
    o = jnp.concatenate(outs, axis=1).astype(BF16)
    x2 = x + _dot(o, wo_ref[...])
    x2_ref[...] = x2
    hf = _rms(x2, gf_ref[...], NORM_EPS)
    tm = hf.shape[0]
    for c in range(PACK_ROWS):
        hf_ref[pl.ds(c, tm, stride=PACK_ROWS), :] = _pack_pair(hf[:, 2 * c * LANES:(2 * c + 1) * LANES],
                                                               hf[:, (2 * c + 1) * LANES:(2 * c + 2) * LANES])
    lg_ref[...] = _dot3(hf, rw_ref[...]) + rb_ref[...]


def cross_attention(x, gc, wq, kv, wo, gf, rw, rb, *, seq, tm):
    t, d = x.shape
    per_seq = seq // tm
    const = lambda a: pl.BlockSpec(a.shape, lambda i: (0,) * a.ndim, pipeline_mode=pl.Buffered(1))
    vec = lambda n: pl.BlockSpec((1, n), lambda i: (0, 0))
    return pl.pallas_call(
        _cross_kernel,
        grid=(t // tm,),
        in_specs=[
            pl.BlockSpec((tm, d), lambda i: (i, 0)),
            vec(d), const(wq),
            pl.BlockSpec((1,) + kv.shape[1:], lambda i: (i // per_seq, 0, 0)),
            const(wo), vec(d), const(rw), vec(LANES),
        ],
        out_specs=[pl.BlockSpec((tm, d), lambda i: (i, 0)), pl.BlockSpec((tm * PACK_ROWS, LANES), lambda i: (i, 0)),
                   pl.BlockSpec((tm, LANES), lambda i: (i, 0))],
        out_shape=[jax.ShapeDtypeStruct((t, d), F32), jax.ShapeDtypeStruct((t * PACK_ROWS, LANES), U32),
                   jax.ShapeDtypeStruct((t, LANES), F32)],
        compiler_params=_cparams(("parallel",)),
        name="cross_attention",
    )(x, gc.reshape(1, d), wq, kv, wo, gf.reshape(1, d), rw, rb)


def _router_kernel(lg_ref, idx_ref, gate_ref):
    x = lg_ref[...]
    lane = lax.broadcasted_iota(jnp.int32, x.shape, 1)
    x = jnp.where(lane < N_EXPERTS, x, -jnp.inf)
    idx_out = jnp.zeros(x.shape, jnp.int32)
    val_out = jnp.zeros(x.shape, F32)
    vals = []
    for j in range(TOP_K):
        m = jnp.max(x, axis=-1, keepdims=True)
        sel = jnp.min(jnp.where(x == m, lane, LANES), axis=-1, keepdims=True)
        idx_out = jnp.where(lane == j, sel, idx_out)
        vals.append(m)
        x = jnp.where(lane == sel, -jnp.inf, x)
    es = [jnp.exp(vj - vals[0]) for vj in vals]
    tot = es[0] + es[1] + es[2] + es[3]
    for j in range(TOP_K):
        val_out = jnp.where(lane == j, es[j] / tot, val_out)
    idx_ref[...] = idx_out[:, 0:TOP_K]
    gate_ref[...] = val_out[:, 0:TOP_K]


def router_topk(logits, *, tm):
    t = logits.shape[0]
    return pl.pallas_call(
        _router_kernel,
        grid=(t // tm,),
        in_specs=[pl.BlockSpec((tm, LANES), lambda i: (i, 0))],
        out_specs=[pl.BlockSpec((tm, TOP_K), lambda i: (i, 0)), pl.BlockSpec((tm, TOP_K), lambda i: (i, 0))],
        out_shape=[jax.ShapeDtypeStruct((t, TOP_K), jnp.int32), jax.ShapeDtypeStruct((t, TOP_K), F32)],
        compiler_params=_cparams(("parallel",)),
        name="router_topk",
    )(logits)


def _moe_kernel(sbe_ref, sbr_ref, src_ref, tok_ref, ntok_ref, hf_hbm, w1_ref, b1_ref, w2_ref, b2_ref,
                ys_hbm, xbuf, x2d, acc, obuf, gsem, ssem):
    s = pl.program_id(0)
    f = pl.program_id(1)
    nf = pl.num_programs(1)
    nsb = pl.num_programs(0)
    sub_rows = MOE_SUB * PACK_ROWS
    nsubs = MOE_SUPER // MOE_SUB
    unroll = 8

    def ceil_sub(n):
        return lax.shift_right_logical(n + (MOE_SUB - 1), MOE_SUB.bit_length() - 1)

    rows = sbr_ref[s]
    nsub = ceil_sub(rows)
    rows_next = jnp.where(s + 1 < nsb, sbr_ref[jnp.minimum(s + 1, nsb - 1)], 0)
    rows_prev = jnp.where(s > 0, sbr_ref[jnp.maximum(s - 1, 0)], 0)
    nsub_next = ceil_sub(rows_next)

    def slab(ref, off):
        return ref.at[pl.ds(pl.multiple_of(off, PACK_ROWS), PACK_ROWS)]

    def row_in(table, r):
        return pltpu.make_async_copy(slab(hf_hbm, table[0, 0, r]), slab(xbuf, r * PACK_ROWS), gsem)

    def row_out(r):
        return pltpu.make_async_copy(slab(obuf, r * PACK_ROWS), slab(ys_hbm, src_ref[0, 0, r] * PACK_ROWS), ssem)

    def start_gather(table, first, n):
        def group(gi, c):
            for j in range(unroll):
                row_in(table, first + gi * unroll + j).start()
            return c

        lax.fori_loop(0, n // unroll, group, 0)

    def wait_sub_blocks(buf, other, sem, n):
        for sub in range(nsubs):
            @pl.when(sub < n)
            def _(sub=sub):
                pltpu.make_async_copy(other.at[pl.ds(0, sub_rows)], buf.at[pl.ds(sub * sub_rows, sub_rows)], sem).wait()

    def wait_scatter(n):
        full = n // MOE_SUB
        for sub in range(nsubs):
            @pl.when(sub < full)
            def _(sub=sub):
                pltpu.make_async_copy(obuf.at[pl.ds(sub * sub_rows, sub_rows)], ys_hbm.at[pl.ds(0, sub_rows)], ssem).wait()

        def one(r, c):
            row_out(r).wait()
            return c

        lax.fori_loop(0, n - full * MOE_SUB, one, 0)

    @pl.when((f == 0) & (s == 0) & (rows > 0))
    def _():
        start_gather(tok_ref, 0, nsub * MOE_SUB)

    @pl.when((f == 0) & (rows > 0))
    def _():
        wait_sub_blocks(xbuf, hf_hbm, gsem, nsub)

    @pl.when((f == nf - 1) & (rows > 0) & (rows_prev > 0))
    def _():
        wait_scatter(rows_prev)

    for sub in range(nsubs):
        @pl.when(sub < nsub)
        def _(sub=sub):
            sl = slice(sub * MOE_SUB, (sub + 1) * MOE_SUB)
            base = sub * sub_rows

            @pl.when(f == 0)
            def _():
                for c in range(PACK_ROWS):
                    lo, hi = _unpack_pair(xbuf[pl.ds(base + c, MOE_SUB, stride=PACK_ROWS), :])
                    x2d[sl, 2 * c * LANES:(2 * c + 1) * LANES] = lo.astype(BF16)
                    x2d[sl, (2 * c + 1) * LANES:(2 * c + 2) * LANES] = hi.astype(BF16)

            x = x2d[sl, :]
            hb = _dot(x, w1_ref[0]) + b1_ref[0]
            even = (lax.broadcasted_iota(jnp.int32, (MOE_SUB, LANES), 1) % 2) == 0
            acts = []
            for cb in range(MOE_FT // LANES):
                a = hb[:, cb * LANES:(cb + 1) * LANES]
                b = hb[:, MOE_FT + cb * LANES:MOE_FT + (cb + 1) * LANES]
                hg = jnp.where(even, a, pltpu.roll(b, 1, axis=1))
                hl = jnp.where(even, pltpu.roll(a, LANES - 1, axis=1), b)
                xg = jnp.minimum(hg, SWIGLU_LIMIT)
                xl = jnp.clip(hl, -SWIGLU_LIMIT, SWIGLU_LIMIT)
                acts.append((xg * jax.nn.sigmoid(SWIGLU_ALPHA * xg) * (xl + 1.0)).astype(BF16))
            contrib = _dot(jnp.concatenate(acts, axis=1), w2_ref[0])

            @pl.when(f == 0)
            def _():
                acc[sl, :] = contrib

            @pl.when((f > 0) & (f < nf - 1))
            def _():
                acc[sl, :] += contrib

            @pl.when(f == nf - 1)
            def _():
                y = acc[sl, :] + contrib + b2_ref[0]
                for c in range(PACK_ROWS):
                    obuf[pl.ds(base + c, MOE_SUB, stride=PACK_ROWS), :] = _pack_pair(
                        y[:, 2 * c * LANES:(2 * c + 1) * LANES], y[:, (2 * c + 1) * LANES:(2 * c + 2) * LANES])

    @pl.when((f == 0) & (rows_next > 0))
    def _():
        start_gather(ntok_ref, 0, nsub_next * MOE_SUB)

    @pl.when((f == nf - 1) & (rows > 0))
    def _():
        def group(gi, c):
            for j in range(unroll):
                row_out(gi * unroll + j).start()
            return c

        def one(r, c):
            row_out(r).start()
            return c

        ngroups = rows // unroll
        lax.fori_loop(0, ngroups, group, 0)
        lax.fori_loop(ngroups * unroll, rows, one, 0)

        @pl.when(rows_next == 0)
        def _():
            wait_scatter(rows)


def moe_experts(sb_e, sb_rows, row_src, row_off, hf, w1, b1, w2, b2):
    d = D_MODEL
    t = hf.shape[0] // PACK_ROWS
    nsb = sb_e.shape[0]
    nf = D_EXPERT // MOE_FT
    assert nf >= 2, "the last hidden-column step adds its own contribution to the accumulator of the earlier ones"
    grid_spec = pltpu.PrefetchScalarGridSpec(
        num_scalar_prefetch=2,
        grid=(nsb, nf),
        in_specs=[
            pl.BlockSpec((1, 1, MOE_SUPER), lambda s, f, e, r: (s, 0, 0), memory_space=pltpu.SMEM),
            pl.BlockSpec((1, 1, MOE_SUPER), lambda s, f, e, r: (s, 0, 0), memory_space=pltpu.SMEM),
            pl.BlockSpec((1, 1, MOE_SUPER), lambda s, f, e, r: (jnp.minimum(s + 1, nsb - 1), 0, 0),
                         memory_space=pltpu.SMEM),
            pl.BlockSpec(memory_space=pl.ANY),
            pl.BlockSpec((1, d, 2 * MOE_FT), lambda s, f, e, r: (e[s], 0, f)),
            pl.BlockSpec((1, 1, 2 * MOE_FT), lambda s, f, e, r: (e[s], 0, f)),
            pl.BlockSpec((1, MOE_FT, d), lambda s, f, e, r: (e[s], f, 0)),
            pl.BlockSpec((1, 1, d), lambda s, f, e, r: (e[s], 0, 0)),
        ],
        out_specs=pl.BlockSpec(memory_space=pl.ANY),
        scratch_shapes=[pltpu.VMEM((MOE_SUPER * PACK_ROWS, LANES), U32), pltpu.VMEM((MOE_SUPER, d), BF16),
                        pltpu.VMEM((MOE_SUPER, d), F32), pltpu.VMEM((MOE_SUPER * PACK_ROWS, LANES), U32),
                        pltpu.SemaphoreType.DMA, pltpu.SemaphoreType.DMA],
    )
    return pl.pallas_call(
        _moe_kernel,
        grid_spec=grid_spec,
        out_shape=jax.ShapeDtypeStruct((t * TOP_K * PACK_ROWS, LANES), U32),
        compiler_params=_cparams(("arbitrary", "arbitrary")),
        name="moe_experts",
    )(sb_e, sb_rows, row_src, row_off, row_off, hf, w1, b1, w2, b2)


def _combine_kernel(ys_ref, gate_ref, x_ref, g_ref, o_ref, lo_ref, hi_ref):
    tm = x_ref.shape[0]
    tot_lo = tot_hi = None
    for j in range(TOP_K):
        lo, hi = _unpack_pair(ys_ref[:, j * PACK_ROWS:(j + 1) * PACK_ROWS, :])
        gj = gate_ref[:, j:j + 1, :]
        tot_lo = lo * gj if tot_lo is None else tot_lo + lo * gj
        tot_hi = hi * gj if tot_hi is None else tot_hi + hi * gj
    lo_ref[...] = tot_lo.reshape(tm * PACK_ROWS, LANES)
    hi_ref[...] = tot_hi.reshape(tm * PACK_ROWS, LANES)
    pieces = []
    for c in range(PACK_ROWS):
        pieces.append(lo_ref[pl.ds(c, tm, stride=PACK_ROWS), :])
        pieces.append(hi_ref[pl.ds(c, tm, stride=PACK_ROWS), :])
    out = x_ref[...] + jnp.concatenate(pieces, axis=1)
    o_ref[...] = _rms(out, g_ref[...], NORM_EPS)


def combine_final(ys, gate, x, g, *, tm):
    t, d = x.shape
    return pl.pallas_call(
        _combine_kernel,
        grid=(t // tm,),
        in_specs=[pl.BlockSpec((tm, TOP_K * PACK_ROWS, LANES), lambda i: (i, 0, 0)),
                  pl.BlockSpec((tm, TOP_K, LANES), lambda i: (i, 0, 0)),
                  pl.BlockSpec((tm, d), lambda i: (i, 0)), pl.BlockSpec((1, d), lambda i: (0, 0))],
        out_specs=pl.BlockSpec((tm, d), lambda i: (i, 0)),
        out_shape=jax.ShapeDtypeStruct((t, d), F32),
        scratch_shapes=[pltpu.VMEM((tm * PACK_ROWS, LANES), F32), pltpu.VMEM((tm * PACK_ROWS, LANES), F32)],
        compiler_params=_cparams(("parallel",)),
        name="combine_final",
    )(ys, gate, x, g.reshape(1, d))


def _routing_tables(top_idx, n_super):
    flat_e = top_idx.reshape(-1)
    n = flat_e.shape[0]
    onehot = (flat_e[:, None] == jnp.arange(N_EXPERTS, dtype=jnp.int32)[None, :]).astype(jnp.int32)
    csum = jnp.cumsum(onehot, axis=0)
    rank = jnp.sum(onehot * csum, axis=1) - 1
    counts = csum[-1]
    nsb = (counts + MOE_SUPER - 1) // MOE_SUPER
    sb_end = jnp.cumsum(nsb)
    sb_start = sb_end - nsb
    dest = sb_start[flat_e] * MOE_SUPER + rank
    row_src = jnp.full((n_super * MOE_SUPER,), -1, jnp.int32).at[dest].set(jnp.arange(n, dtype=jnp.int32))
    s_ids = jnp.arange(n_super, dtype=jnp.int32)
    sb_e = jnp.minimum(jnp.searchsorted(sb_end, s_ids, side="right"), N_EXPERTS - 1).astype(jnp.int32)
    local = s_ids - sb_start[sb_e]
    sb_rows = jnp.clip(counts[sb_e] - local * MOE_SUPER, 0, MOE_SUPER)
    sb_rows = jnp.where(s_ids < sb_end[-1], sb_rows, 0).astype(jnp.int32)
    row_off = jnp.where(row_src >= 0, (row_src // TOP_K) * PACK_ROWS, 0)
    return sb_e, sb_rows, row_src.reshape(n_super, 1, MOE_SUPER), row_off.reshape(n_super, 1, MOE_SUPER)


def _pad_rows(a, n):
    return jnp.pad(a, ((0, n - a.shape[0]), (0, 0)))


def _layer(x, mem, l, p, batch, seq):
    t = batch * seq
    c = RWKV_WIDTH
    lam_init = 0.8 - 0.6 * math.exp(-0.3 * l)
    w_in = p["w_in"]
    o1 = 3 * c + DECAY_LORA + AAA_LORA + GATE_LORA
    o2 = o1 + 3 * DIFF_WIDTH
    padc = lambda a, n: jnp.pad(a, ((0, 0), (0, n - a.shape[1])))
    w_rwkv = jnp.concatenate([
        w_in[:, :3 * c],
        padc(w_in[:, 3 * c:3 * c + DECAY_LORA], 128),
        padc(w_in[:, 3 * c + DECAY_LORA:3 * c + DECAY_LORA + AAA_LORA], 128),
        padc(w_in[:, 3 * c + DECAY_LORA + AAA_LORA:o1], 256)], axis=1).astype(BF16)
    mu = p["rwkv_mu"]
    pad1 = lambda a, n: jnp.pad(a, (0, n - a.shape[0]))
    mu_p = jnp.concatenate([mu[:3 * c], pad1(mu[3 * c:3 * c + DECAY_LORA], 128),
                            pad1(mu[3 * c + DECAY_LORA:3 * c + DECAY_LORA + AAA_LORA], 128),
                            pad1(mu[3 * c + DECAY_LORA + AAA_LORA:], 256)])
    w_diff = w_in[:, o1:o2].astype(BF16)
    w_gate = w_in[:, o2:].astype(BF16)

    g_mix = p["norm_mix_g"]
    p_rwkv = norm_matmul(x, g_mix, w_rwkv, tm=1024, tn=RWKV_COLS // 2, out_dtype=F32, name="in_proj_rwkv")
    qkv = norm_matmul(x, g_mix, w_diff, tm=1024, tn=1536, out_dtype=BF16, name="in_proj_diff")
    gates = norm_matmul(x, g_mix, w_gate, tm=1024, tn=2048, out_dtype=BF16, act="sigmoid", name="in_proj_gate")

    prep = rwkv_prep(p_rwkv, mu_p, p["rwkv_w0"], p["rwkv_a0"], p["rwkv_k_k"], p["rwkv_k_a"],
                     _pad_rows(p["rwkv_w2"], 128), _pad_rows(p["rwkv_a2"], 128), _pad_rows(p["rwkv_g2"], 256),
                     seq=seq, tm=256)
    ra, w1b, w2b = rwkv_scan(*prep, p["rwkv_ln_w"], p["rwkv_ln_b"], p["rwkv_r_k"].reshape(-1),
                             p["expert_w1"], p["expert_w2"], batch=batch, seq=seq)

    slopes = (2.0 ** (-8.0 * jnp.arange(1, DIFF_HEADS + 1, dtype=F32) / DIFF_HEADS)).astype(F32)
    da = diff_attention(qkv, slopes, p["diff_lq1"], p["diff_lk1"], p["diff_lq2"], p["diff_lk2"],
                        p["diff_subln_g"], batch=batch, seq=seq, tq=min(1024, seq), lam_init=lam_init)

    x1 = mix_project(ra, da, gates, x, p["rwkv_proj"].astype(BF16), p["diff_proj"].astype(BF16),
                     p["w_out"].astype(BF16), tm=512)

    m_len = mem.shape[0] // batch
    kv = norm_matmul(mem, p["norm_mem_g"], p["cross_wkv"].astype(BF16), tm=min(512, mem.shape[0]),
                     tn=2 * CROSS_WIDTH, out_dtype=BF16, name="cross_kv")
    rw = jnp.pad(p["router_w"], ((0, 0), (0, LANES - N_EXPERTS)))
    rb = jnp.pad(p["router_b"], (0, LANES - N_EXPERTS)).reshape(1, LANES)
    x2, hf, logits = cross_attention(x1, p["norm_cross_g"], p["cross_wq"].astype(BF16),
                                     kv.reshape(batch, m_len, 2 * CROSS_WIDTH), p["cross_wo"].astype(BF16),
                                     p["norm_ffn_g"], rw, rb, seq=seq, tm=512)

    top_idx, gate = router_topk(logits, tm=min(1024, t))
    n_super = (t * TOP_K) // MOE_SUPER + N_EXPERTS
    sb_e, sb_rows, row_src, row_off = _routing_tables(top_idx, n_super)
    ys = moe_experts(sb_e, sb_rows, row_src, row_off, hf, w1b, p["expert_b1"][:, None, :], w2b, p["expert_b2"][:, None, :])
    gate_b = jnp.broadcast_to(gate[:, :, None], (t, TOP_K, LANES))
    return x2, ys.reshape(t, TOP_K * PACK_ROWS, LANES), gate_b


def kernel(x, mem, norm_mix_g, w_in, rwkv_mu, rwkv_w0, rwkv_w2, rwkv_a0, rwkv_a2, rwkv_g2, rwkv_k_k, rwkv_k_a, rwkv_r_k, rwkv_ln_w, rwkv_ln_b, rwkv_proj, diff_lq1, diff_lk1, diff_lq2, diff_lk2, diff_subln_g, diff_proj, w_out, norm_cross_g, norm_mem_g, cross_wq, cross_wkv, cross_wo, norm_ffn_g, router_w, router_b, expert_w1, expert_b1, expert_w2, expert_b2, final_norm_g):
    batch, seq, d = x.shape
    stacked = dict(norm_mix_g=norm_mix_g, w_in=w_in, rwkv_mu=rwkv_mu, rwkv_w0=rwkv_w0, rwkv_w2=rwkv_w2,
                   rwkv_a0=rwkv_a0, rwkv_a2=rwkv_a2, rwkv_g2=rwkv_g2, rwkv_k_k=rwkv_k_k, rwkv_k_a=rwkv_k_a,
                   rwkv_r_k=rwkv_r_k, rwkv_ln_w=rwkv_ln_w, rwkv_ln_b=rwkv_ln_b, rwkv_proj=rwkv_proj,
                   diff_lq1=diff_lq1, diff_lk1=diff_lk1, diff_lq2=diff_lq2, diff_lk2=diff_lk2,
                   diff_subln_g=diff_subln_g, diff_proj=diff_proj, w_out=w_out, norm_cross_g=norm_cross_g,
                   norm_mem_g=norm_mem_g, cross_wq=cross_wq, cross_wkv=cross_wkv, cross_wo=cross_wo,
                   norm_ffn_g=norm_ffn_g, router_w=router_w, router_b=router_b, expert_w1=expert_w1,
                   expert_b1=expert_b1, expert_w2=expert_w2, expert_b2=expert_b2)
    assert w_in.shape[0] == 1, "the closing RMSNorm is fused into the single layer's combine"
    p = {k: v[0] for k, v in stacked.items()}
    x2, ys, gate_b = _layer(x.reshape(batch * seq, d), mem.reshape(-1, d), 0, p, batch, seq)
    out = combine_final(ys, gate_b, x2, final_norm_g, tm=256)
    return out.reshape(batch, seq, d)
```

```python
import functools
import math

import jax
import jax.numpy as jnp
from jax import lax
from jax.experimental import pallas as pl
from jax.experimental.pallas import tpu as pltpu

F32 = jnp.float32
BF16 = jnp.bfloat16
U32 = jnp.uint32

D_MODEL = 2048
NORM_EPS = 1e-5
LOG2E = 1.4426950408889634
LANES = 128
HEAD_DIM = 64
RWKV_WIDTH = 1024
SLAB = 256
RWKV_SLABS = RWKV_WIDTH // SLAB
HEADS_PER_SLAB = SLAB // HEAD_DIM
CHUNK = 64
SCAN_CHUNKS = 2
RWKV_GN_EPS = 64e-5
DECAY_LORA = 64
AAA_LORA = 64
GATE_LORA = 160
LORA_PAD = 512
RWKV_COLS = 3 * RWKV_WIDTH + LORA_PAD
DIFF_WIDTH = 1024
DIFF_HEADS = 8
CROSS_HEADS = 4
CROSS_WIDTH = 512
N_EXPERTS = 32
TOP_K = 4
D_EXPERT = 2048
SWIGLU_ALPHA = 1.702
SWIGLU_LIMIT = 7.0
MOE_SUPER = 1024
MOE_SUB = 256
MOE_FT = 1024
W2_GROUP = 256
ROW_TILE = D_MODEL // LANES
PACK_ROWS = ROW_TILE // 2
VMEM_LIMIT = 56 * 1024 * 1024

NN = (((1,), (0,)), ((), ()))
NT = (((1,), (1,)), ((), ()))


def _dot(a, b, dims=NN):
    return lax.dot_general(a, b, dims, preferred_element_type=F32)


def _split2(a):
    hi = a.astype(BF16)
    lo = (a - hi.astype(F32)).astype(BF16)
    return hi, lo


def _split3(a):
    hi = a.astype(BF16)
    r1 = a - hi.astype(F32)
    mid = r1.astype(BF16)
    lo = (r1 - mid.astype(F32)).astype(BF16)
    return hi, mid, lo


def _dot3(a, b, dims=NN):
    ah, al = _split2(a)
    bh, bl = _split2(b)
    return _dot(ah, bh, dims) + (_dot(ah, bl, dims) + _dot(al, bh, dims))


def _dot_exact_rhs(a, b_bf16, dims=NN):
    h, m, l = _split3(a)
    return _dot(h, b_bf16, dims) + (_dot(m, b_bf16, dims) + _dot(l, b_bf16, dims))


def _pack_pair(lo, hi):
    lo_b = lax.shift_right_logical(lax.bitcast_convert_type(lo.astype(BF16).astype(F32), U32), jnp.uint32(16))
    hi_b = lax.bitcast_convert_type(hi.astype(BF16).astype(F32), U32) & jnp.uint32(0xFFFF0000)
    return hi_b | lo_b


def _unpack_pair(w):
    lo = lax.bitcast_convert_type(lax.shift_left(w, jnp.uint32(16)), F32)
    hi = lax.bitcast_convert_type(w & jnp.uint32(0xFFFF0000), F32)
    return lo, hi


def _rms(x, g, eps):
    ms = jnp.mean(x * x, axis=-1, keepdims=True)
    return x * lax.rsqrt(ms + eps) * g


def _cparams(sem):
    return pltpu.CompilerParams(dimension_semantics=sem, vmem_limit_bytes=VMEM_LIMIT)


def _norm_matmul_kernel(x_ref, g_ref, w_ref, o_ref, h_ref, *, act):
    @pl.when(pl.program_id(1) == 0)
    def _():
        h_ref[...] = _rms(x_ref[...], g_ref[...], NORM_EPS).astype(BF16)

    y = _dot(h_ref[...], w_ref[...])
    if act == "sigmoid":
        y = jax.nn.sigmoid(y)
    o_ref[...] = y.astype(o_ref.dtype)


def norm_matmul(x, g, w, *, tm, tn, out_dtype, act=None, name):
    m, d = x.shape
    n = w.shape[1]
    return pl.pallas_call(
        functools.partial(_norm_matmul_kernel, act=act),
        grid=(m // tm, n // tn),
        in_specs=[
            pl.BlockSpec((tm, d), lambda i, j: (i, 0)),
            pl.BlockSpec((1, d), lambda i, j: (0, 0)),
            pl.BlockSpec((d, tn), lambda i, j: (0, j)),
        ],
        out_specs=pl.BlockSpec((tm, tn), lambda i, j: (i, j)),
        out_shape=jax.ShapeDtypeStruct((m, n), out_dtype),
        scratch_shapes=[pltpu.VMEM((tm, d), BF16)],
        compiler_params=_cparams(("parallel", "arbitrary")),
        name=name,
    )(x, g.reshape(1, d), w)


def _head_ones():
    r = lax.broadcasted_iota(jnp.int32, (SLAB, SLAB), 0)
    c = lax.broadcasted_iota(jnp.int32, (SLAB, SLAB), 1)
    return ((r // HEAD_DIM) == (c // HEAD_DIM)).astype(BF16)


def _rwkv_prep_kernel(p_ref, prev_ref, mu_ref, w0_ref, a0_ref, kk_ref, ka_ref, w2_ref, a2_ref, g2_ref,
                      r_out, lw_out, k_out, v_out, kkn_out, b_out, g_out, *, tiles_per_seq):
    c = RWKV_WIDTH
    i = pl.program_id(0)
    p = p_ref[...]
    tm = p.shape[0]
    first = (i % tiles_per_seq) == 0
    prev_row = jnp.where(first, 0.0, prev_ref[7:8, :])
    row = lax.broadcasted_iota(jnp.int32, p.shape, 0)
    shifted = jnp.where(row == 0, prev_row, pltpu.roll(p, 1, axis=0))
    ps = p + (shifted - p) * mu_ref[...]
    r = ps[:, 0:c]
    k = ps[:, c:2 * c]
    v = ps[:, 2 * c:3 * c]
    wd = ps[:, 3 * c:3 * c + 128]
    ad = ps[:, 3 * c + 128:3 * c + 256]
    gd = ps[:, 3 * c + 256:3 * c + 512]
    z = -(w0_ref[...] + _dot3(jnp.tanh(wd), w2_ref[...]))
    softplus = jnp.maximum(z, 0.0) + jnp.log1p(jnp.exp(-jnp.abs(z)))
    w = -softplus - 0.5
    lw = -jnp.exp(w)
    a = jax.nn.sigmoid(a0_ref[...] + _dot3(ad, a2_ref[...]))
    g = _dot3(jax.nn.sigmoid(gd), g2_ref[...])
    kkr = k * kk_ref[...]
    k2 = k * (1.0 + (a - 1.0) * ka_ref[...])
    ones = _head_ones()
    for q in range(RWKV_SLABS):
        sl = slice(q * SLAB, (q + 1) * SLAB)
        x = kkr[:, sl]
        ss = _dot_exact_rhs(x * x, ones)
        kkn = x / jnp.maximum(jnp.sqrt(ss), 1e-12)
        r_out[q] = r[:, sl]
        lw_out[q] = lw[:, sl]
        k_out[q] = k2[:, sl]
        v_out[q] = v[:, sl]
        kkn_out[q] = kkn
        b_out[q] = kkn * a[:, sl]
        g_out[q] = g[:, sl]


def rwkv_prep(p, mu, w0, a0, k_k, k_a, w2p, a2p, g2p, *, seq, tm):
    t, cols = p.shape
    c = RWKV_WIDTH
    vec = lambda n: pl.BlockSpec((1, n), lambda i: (0, 0))
    full = lambda a: pl.BlockSpec(a.shape, lambda i: (0, 0))
    out_spec = pl.BlockSpec((RWKV_SLABS, tm, SLAB), lambda i: (0, i, 0))
    out_shape = jax.ShapeDtypeStruct((RWKV_SLABS, t, SLAB), F32)
    return pl.pallas_call(
        functools.partial(_rwkv_prep_kernel, tiles_per_seq=seq // tm),
        grid=(t // tm,),
        in_specs=[
            pl.BlockSpec((tm, cols), lambda i: (i, 0)),
            pl.BlockSpec((8, cols), lambda i: (jnp.maximum(i * (tm // 8) - 1, 0), 0)),
            vec(cols), vec(c), vec(c), vec(c), vec(c), full(w2p), full(a2p), full(g2p),
        ],
        out_specs=[out_spec] * 7,
        out_shape=[out_shape] * 7,
        compiler_params=_cparams(("parallel",)),
        name="rwkv_prep",
    )(p, p, mu.reshape(1, cols), w0.reshape(1, c), a0.reshape(1, c), k_k.reshape(1, c), k_a.reshape(1, c),
      w2p, a2p, g2p)


def _dot1(a, b, dims=NN):
    return _dot(a.astype(BF16), b.astype(BF16), dims)


def _sum3_exact_rhs(xs, ones):
    parts = []
    for x in xs:
        parts.extend(_split3(x))
    res = _dot(jnp.concatenate(parts, axis=0), ones)
    L = xs[0].shape[0]
    return [res[(3 * i) * L:(3 * i + 1) * L] + (res[(3 * i + 1) * L:(3 * i + 2) * L] + res[(3 * i + 2) * L:(3 * i + 3) * L])
            for i in range(len(xs))]


def _expert_weight_chunk(w1_ref, w2a_ref, w2b_ref, w1o_ref, w2o_ref, tmp_ref):
    w1o_ref[0] = w1_ref[0].astype(BF16)
    half = W2_GROUP
    for c in range(ROW_TILE):
        cols = slice(c * LANES, (c + 1) * LANES)
        tmp_ref[c, pl.ds(0, half, stride=2), :] = w2a_ref[0, :, cols]
        tmp_ref[c, pl.ds(1, half, stride=2), :] = w2b_ref[0, :, cols]
        w2o_ref[0, :, cols] = tmp_ref[c].astype(BF16)


def _expert_weight_views(w1, w2):
    e, d, f2 = w1.shape
    fdim = w2.shape[1]
    groups_per_half = (MOE_FT // 2) // W2_GROUP
    n_chunks = e * (fdim // MOE_FT) * groups_per_half
    w1v = w1.reshape(n_chunks, (e * d) // n_chunks, f2)
    w2v = w2.reshape(e * fdim // W2_GROUP, W2_GROUP, d)
    return n_chunks, groups_per_half, w1v, w2v


def _expert_weight_specs(n_chunks, groups_per_half, w1v, w2v, step):
    d = w2v.shape[2]
    a_idx = lambda j: (j // groups_per_half) * (2 * groups_per_half) + j % groups_per_half
    in_specs = [pl.BlockSpec((1,) + w1v.shape[1:], lambda *ids: (step(*ids), 0, 0)),
                pl.BlockSpec((1, W2_GROUP, d), lambda *ids: (a_idx(step(*ids)), 0, 0)),
                pl.BlockSpec((1, W2_GROUP, d), lambda *ids: (a_idx(step(*ids)) + groups_per_half, 0, 0))]
    out_specs = [pl.BlockSpec((1,) + w1v.shape[1:], lambda *ids: (step(*ids), 0, 0)),
                 pl.BlockSpec((1, 2 * W2_GROUP, d), lambda *ids: (step(*ids), 0, 0))]
    out_shape = [jax.ShapeDtypeStruct(w1v.shape, BF16), jax.ShapeDtypeStruct((n_chunks, 2 * W2_GROUP, d), BF16)]
    scratch = pltpu.VMEM((ROW_TILE, 2 * W2_GROUP, LANES), F32)
    return in_specs, out_specs, out_shape, scratch


def expert_weight_prep(w1, w2):
    n_chunks, gph, w1v, w2v = _expert_weight_views(w1, w2)
    in_specs, out_specs, out_shape, scratch = _expert_weight_specs(n_chunks, gph, w1v, w2v, lambda j: j)
    w1b, w2b = pl.pallas_call(
        _expert_weight_chunk, grid=(n_chunks,), in_specs=in_specs, out_specs=out_specs, out_shape=out_shape,
        scratch_shapes=[scratch], compiler_params=_cparams(("parallel",)), name="expert_weight_prep",
    )(w1v, w2v, w2v)
    return w1b.reshape(w1.shape), w2b.reshape(w2.shape)


def _rwkv_scan_kernel(r_ref, lw_ref, k_ref, v_ref, kk_ref, b_ref, g_ref, lnw_ref, lnb_ref, rk_ref, *rest):
    if len(rest) == 2:
        o_ref, st_ref = rest
    else:
        w1_ref, w2a_ref, w2b_ref, o_ref, w1o_ref, w2o_ref, st_ref, tmp_ref = rest
        _expert_weight_chunk(w1_ref, w2a_ref, w2b_ref, w1o_ref, w2o_ref, tmp_ref)
    L = CHUNK
    W = SLAB

    @pl.when(pl.program_id(1) == 0)
    def _():
        st_ref[...] = jnp.zeros_like(st_ref)

    t_i = lax.broadcasted_iota(jnp.int32, (L, W), 0)
    lane = lax.broadcasted_iota(jnp.int32, (L, W), 1)
    j_i = lane % HEAD_DIM
    hid = lane // HEAD_DIM
    strict = j_i < t_i
    incl = j_i <= t_i
    eye = (j_i == t_i).astype(F32)
    r2 = lax.broadcasted_iota(jnp.int32, (W, W), 0)
    c2 = lax.broadcasted_iota(jnp.int32, (W, W), 1)
    same_head = (r2 // HEAD_DIM) == (c2 // HEAD_DIM)
    diag = r2 == c2
    ones = same_head.astype(BF16)
    tr = lax.broadcasted_iota(jnp.int32, (L, L), 0)
    tc = lax.broadcasted_iota(jnp.int32, (L, L), 1)
    tri = (tc <= tr).astype(BF16)

    def bd(x):
        return jnp.concatenate([jnp.where(hid == h, x, 0.0) for h in range(HEADS_PER_SLAB)], axis=0)

    chains = [(u, h) for h in range(SCAN_CHUNKS) for u in range(RWKV_SLABS)]
    U = range(len(chains))
    rows = [slice(h * L, (h + 1) * L) for _, h in chains]
    slab = [u for u, _ in chains]
    cat0 = lambda xs: jnp.concatenate(xs, axis=0)
    cat1 = lambda xs: jnp.concatenate(xs, axis=1)
    r = [r_ref[slab[u], rows[u], :] for u in U]
    lw = [lw_ref[slab[u], rows[u], :] for u in U]
    k = [k_ref[slab[u], rows[u], :] for u in U]
    v = [v_ref[slab[u], rows[u], :] for u in U]
    kkv = [kk_ref[slab[u], rows[u], :] for u in U]
    bv = [b_ref[slab[u], rows[u], :] for u in U]
    c3 = [_dot(tri, cat1(_split3(lw[u]))) for u in U]
    cum = [c3[u][:, 0:W] + (c3[u][:, W:2 * W] + c3[u][:, 2 * W:]) for u in U]
    cum_end = [cum[u][L - 1:L, :] for u in U]
    e_cum = [jnp.exp(cum[u]) for u in U]
    e_neg = [jnp.exp(-cum[u]) for u in U]
    e_end = [jnp.exp(cum_end[u] - cum[u]) for u in U]
    at = [-kkv[u] * jnp.exp(cum[u] - lw[u]) for u in U]
    rt = [r[u] * e_cum[u] for u in U]
    bt = [bv[u] * e_neg[u] for u in U]
    kt = [k[u] * e_neg[u] for u in U]
    bh = [bv[u] * e_end[u] for u in U]
    kh = [k[u] * e_end[u] for u in U]
    w_end = [jnp.exp(cum_end[u]) for u in U]

    g_all = [_dot1(cat0([at[u], rt[u]]), cat0([bd(bt[u]), bd(kt[u])]), NT) for u in U]
    a_ab = [jnp.where(strict, g_all[u][0:L, 0:W], 0.0) for u in U]
    a_ak = [jnp.where(strict, g_all[u][0:L, W:], 0.0) for u in U]
    a_rb = [jnp.where(incl, g_all[u][L:, 0:W], 0.0) for u in U]
    a_rk = [jnp.where(incl, g_all[u][L:, W:], 0.0) for u in U]

    tinv = [eye for u in U]
    pw = a_ab
    for it in range(6):
        if it < 5:
            res = [_dot1(pw[u], cat1([bd(tinv[u]), bd(pw[u])])) for u in U]
            tinv = [tinv[u] + res[u][:, 0:W] for u in U]
            pw = [res[u][:, W:] for u in U]
        else:
            tinv = [tinv[u] + _dot1(pw[u], bd(tinv[u])) for u in U]

    bdv = [bd(v[u]) for u in U]
    av = [_dot1(a_ak[u], bdv[u]) for u in U]
    qp = [_dot1(tinv[u], cat1([bd(at[u]), bd(av[u])])) for u in U]
    q1 = [qp[u][:, 0:W] for u in U]
    p1 = [qp[u][:, W:] for u in U]
    qp2 = [_dot1(cat1([a_rb[u], a_rk[u]]),
                  cat0([cat1([bd(q1[u]), bd(p1[u])]), cat1([jnp.zeros_like(bdv[u]), bdv[u]])])) for u in U]
    q2 = [rt[u] + qp2[u][:, 0:W] for u in U]
    p2 = [qp2[u][:, W:] for u in U]
    mp = [_dot1(cat0([bh[u], kh[u]]).T,
                 cat0([cat1([q1[u], p1[u]]), cat1([jnp.zeros_like(v[u]), v[u]])])) for u in U]
    m_bd = [jnp.where(same_head, mp[u][:, 0:W], 0.0) + jnp.where(diag, w_end[u], 0.0) for u in U]
    m_hl = [_split2(m_bd[u]) for u in U]
    st = [st_ref[s] for s in range(RWKV_SLABS)]
    y = [None for u in U]
    for u in U:
        st_hi, st_lo = _split2(st[slab[u]])
        yq = _dot(cat0([q2[u].astype(BF16), m_hl[u][0], m_hl[u][1]]), st_hi)
        y[u] = yq[0:L] + p2[u]
        carry = yq[L:L + W] + (yq[L + W:] + _dot(m_hl[u][0], st_lo))
        st[slab[u]] = carry + jnp.where(same_head, mp[u][:, W:], 0.0)
    for s in range(RWKV_SLABS):
        st_ref[s] = st[s]

    sums = [_sum3_exact_rhs([y[u], r[u] * k[u] * rk_ref[slab[u]]], ones) for u in U]
    yc = [y[u] - sums[u][0] * (1.0 / HEAD_DIM) for u in U]
    var = [_sum3_exact_rhs([yc[u] * yc[u]], ones)[0] for u in U]
    for u in U:
        yn = yc[u] * lax.rsqrt(var[u] * (1.0 / HEAD_DIM) + RWKV_GN_EPS) * lnw_ref[slab[u]] + lnb_ref[slab[u]]
        o_ref[slab[u], rows[u], :] = ((yn + sums[u][1] * v[u]) * g_ref[slab[u], rows[u], :]).astype(o_ref.dtype)


def rwkv_scan(r, lw, k, v, kk, b, g, ln_w, ln_b, r_k, w1, w2, *, batch, seq):
    nchunk = seq // (SCAN_CHUNKS * CHUNK)
    blk = pl.BlockSpec((RWKV_SLABS, SCAN_CHUNKS * CHUNK, SLAB), lambda bi, ci: (0, bi * nchunk + ci, 0))
    par = pl.BlockSpec((RWKV_SLABS, 1, SLAB), lambda bi, ci: (0, 0, 0))
    t = batch * seq
    out_shape = jax.ShapeDtypeStruct((RWKV_SLABS, t, SLAB), BF16)
    state = pltpu.VMEM((RWKV_SLABS, SLAB, SLAB), F32)
    args = (r, lw, k, v, kk, b, g, ln_w.reshape(RWKV_SLABS, 1, SLAB), ln_b.reshape(RWKV_SLABS, 1, SLAB),
            r_k.reshape(RWKV_SLABS, 1, SLAB))
    n_chunks, gph, w1v, w2v = _expert_weight_views(w1, w2)
    if n_chunks != batch * nchunk:
        ra = pl.pallas_call(
            _rwkv_scan_kernel, grid=(batch, nchunk), in_specs=[blk] * 7 + [par] * 3, out_specs=blk,
            out_shape=out_shape, scratch_shapes=[state],
            compiler_params=_cparams(("arbitrary", "arbitrary")), name="rwkv_scan",
        )(*args)
        return (ra,) + expert_weight_prep(w1, w2)
    w_in, w_out, w_shape, w_scratch = _expert_weight_specs(n_chunks, gph, w1v, w2v, lambda bi, ci: bi * nchunk + ci)
    ra, w1b, w2b = pl.pallas_call(
        _rwkv_scan_kernel, grid=(batch, nchunk), in_specs=[blk] * 7 + [par] * 3 + w_in,
        out_specs=[blk] + w_out, out_shape=[out_shape] + w_shape, scratch_shapes=[state, w_scratch],
        compiler_params=_cparams(("arbitrary", "arbitrary")), name="rwkv_scan",
    )(*args, w1v, w2v, w2v)
    return ra, w1b.reshape(w1.shape), w2b.reshape(w2.shape)


def _diff_attn_kernel(qi_ref, ki_ref, slope_ref, q_ref, k_ref, v_ref, lq1_ref, lk1_ref, lq2_ref, lk2_ref, sg_ref,
                      o_ref, qs_ref, kb_ref, relb_ref, m_ref, l_ref, acc_ref, *, tq, lam_init):
    h = pl.program_id(1)
    qi = qi_ref[pl.program_id(2)]
    ki = ki_ref[pl.program_id(2)]
    c2 = slope_ref[h] * LOG2E

    @pl.when(ki == 0)
    def _():
        m_ref[...] = jnp.full_like(m_ref, -jnp.inf)
        l_ref[...] = jnp.zeros_like(l_ref)
        acc_ref[...] = jnp.zeros_like(acc_ref)
        q = q_ref[...].astype(F32) * (HEAD_DIM ** -0.5 * LOG2E)
        lane = lax.broadcasted_iota(jnp.int32, q.shape, 1)
        m0 = lane < HEAD_DIM
        qs_ref[0:tq, 0:LANES] = jnp.where(m0, q, 0.0).astype(BF16)
        qs_ref[tq:, 0:LANES] = jnp.where(m0, 0.0, q).astype(BF16)

    @pl.when(pl.program_id(2) == 0)
    def _():
        rel = (lax.broadcasted_iota(jnp.int32, (tq, tq), 1) - lax.broadcasted_iota(jnp.int32, (tq, tq), 0))
        relb = rel.astype(F32) * (-c2)
        relb_ref[:, 0:tq] = relb
        relb_ref[:, tq:] = relb
        lane1 = lax.broadcasted_iota(jnp.int32, (1, LANES), 1)
        piece = (lane1 % 6) // 2
        part = lane1 % 2
        first_group = lane1 < 6
        c_hi, c_mid, c_lo = [x.astype(F32) for x in _split3(jnp.full((1, LANES), c2, F32))]
        cp = jnp.where(piece == 0, c_hi, jnp.where(piece == 1, c_mid, c_lo))
        idx = lax.broadcasted_iota(jnp.int32, (tq, LANES), 0)
        idx_part = jnp.where(part == 0, (idx // 32) * 32, idx % 32).astype(F32)
        active = lane1 < 12
        kb_ref[...] = jnp.where(active, jnp.where(first_group, -cp, idx_part), 0.0).astype(BF16)
        qb = jnp.where(active, jnp.where(first_group, idx_part, cp), 0.0).astype(BF16)
        qs_ref[0:tq, LANES:] = qb
        qs_ref[tq:, LANES:] = qb

    def step(masked):
        vt = v_ref[...].T
        tile_bias = c2 * ((qi - ki) * tq).astype(F32)
        if masked:
            relb = relb_ref[...]
            s = _dot(k_ref[...], qs_ref[:, 0:LANES], NT) + relb
            s = jnp.where(relb > 0.0, -jnp.inf, s)
        else:
            k = jnp.concatenate([k_ref[...], kb_ref[...]], axis=1)
            s = _dot(k, qs_ref[...], NT)
        m_prev = m_ref[...]
        m_new = jnp.maximum(m_prev, jnp.max(s, axis=0, keepdims=True) - tile_bias)
        alpha = jnp.exp2(m_prev - m_new)
        p = jnp.exp2(s - (m_new + tile_bias))
        l_ref[...] = alpha * l_ref[...] + jnp.sum(p, axis=0, keepdims=True)
        acc_ref[...] = alpha * acc_ref[...] + _dot(vt, p.astype(BF16))
        m_ref[...] = m_new

    @pl.when(ki < qi)
    def _():
        step(False)

    @pl.when(ki == qi)
    def _():
        step(True)
        lam = (jnp.exp(jnp.sum(lq1_ref[...] * lk1_ref[...], axis=-1, keepdims=True))
               - jnp.exp(jnp.sum(lq2_ref[...] * lk2_ref[...], axis=-1, keepdims=True)) + lam_init)
        ot = acc_ref[:, 0:tq] / l_ref[:, 0:tq] - lam * (acc_ref[:, tq:] / l_ref[:, tq:])
        o = _rms(ot.T, sg_ref[...], NORM_EPS) * (1.0 - lam_init)
        o_ref[...] = o.astype(o_ref.dtype)


def diff_attention(qkv, slopes, lq1, lk1, lq2, lk2, subln_g, *, batch, seq, tq, lam_init):
    t = batch * seq
    nq = seq // tq
    hb = DIFF_WIDTH // LANES
    pairs = [(qi, ki) for qi in range(nq) for ki in range(qi + 1)]
    qi_tab = jnp.asarray([pq for pq, _ in pairs], jnp.int32)
    ki_tab = jnp.asarray([pk for _, pk in pairs], jnp.int32)
    small = pl.BlockSpec((1, HEAD_DIM), lambda b, h, j, qt, kt: (0, 0))
    grid_spec = pltpu.PrefetchScalarGridSpec(
        num_scalar_prefetch=2,
        grid=(batch, DIFF_HEADS, len(pairs)),
        in_specs=[
            pl.BlockSpec(memory_space=pltpu.SMEM),
            pl.BlockSpec((tq, LANES), lambda b, h, j, qt, kt: (b * nq + qt[j], h)),
            pl.BlockSpec((tq, LANES), lambda b, h, j, qt, kt: (b * nq + kt[j], hb + h)),
            pl.BlockSpec((tq, LANES), lambda b, h, j, qt, kt: (b * nq + kt[j], 2 * hb + h)),
            small, small, small, small,
            pl.BlockSpec((1, LANES), lambda b, h, j, qt, kt: (0, 0)),
        ],
        out_specs=pl.BlockSpec((tq, LANES), lambda b, h, j, qt, kt: (b * nq + qt[j], h)),
        scratch_shapes=[pltpu.VMEM((2 * tq, 2 * LANES), BF16), pltpu.VMEM((tq, LANES), BF16),
                        pltpu.VMEM((tq, 2 * tq), F32),
                        pltpu.VMEM((1, 2 * tq), F32), pltpu.VMEM((1, 2 * tq), F32),
                        pltpu.VMEM((LANES, 2 * tq), F32)],
    )
    return pl.pallas_call(
        functools.partial(_diff_attn_kernel, tq=tq, lam_init=lam_init),
        grid_spec=grid_spec,
        out_shape=jax.ShapeDtypeStruct((t, DIFF_WIDTH), BF16),
        compiler_params=_cparams(("parallel", "parallel", "arbitrary")),
        name="diff_attention",
    )(qi_tab, ki_tab, slopes, qkv, qkv, qkv, lq1.reshape(1, -1), lk1.reshape(1, -1), lq2.reshape(1, -1),
      lk2.reshape(1, -1), subln_g.reshape(1, -1))


def _mix_kernel(ra_ref, da_ref, ga_ref, gb_ref, x_ref, wa_ref, wb_ref, wo_ref, o_ref):
    ya = _dot(ra_ref[0], wa_ref[0:SLAB, :])
    for q in range(1, RWKV_SLABS):
        ya = ya + _dot(ra_ref[q], wa_ref[q * SLAB:(q + 1) * SLAB, :])
    yb = _dot(da_ref[...], wb_ref[...])
    mixed = ga_ref[...].astype(F32) * ya + gb_ref[...].astype(F32) * yb
    o_ref[...] = x_ref[...] + _dot(mixed.astype(BF16), wo_ref[...])


def mix_project(ra, da, gates, x, wa, wb, wo, *, tm):
    t, d = x.shape
    const = lambda a: pl.BlockSpec(a.shape, lambda i: (0, 0), pipeline_mode=pl.Buffered(1))
    return pl.pallas_call(
        _mix_kernel,
        grid=(t // tm,),
        in_specs=[
            pl.BlockSpec((RWKV_SLABS, tm, SLAB), lambda i: (0, i, 0)),
            pl.BlockSpec((tm, DIFF_WIDTH), lambda i: (i, 0)),
            pl.BlockSpec((tm, d), lambda i: (i, 0)),
            pl.BlockSpec((tm, d), lambda i: (i, 1)),
            pl.BlockSpec((tm, d), lambda i: (i, 0)),
            const(wa), const(wb), const(wo),
        ],
        out_specs=pl.BlockSpec((tm, d), lambda i: (i, 0)),
        out_shape=jax.ShapeDtypeStruct((t, d), F32),
        compiler_params=_cparams(("parallel",)),
        name="mix_project",
    )(ra, da, gates, gates, x, wa, wb, wo)


def _cross_kernel(x_ref, gc_ref, wq_ref, kv_ref, wo_ref, gf_ref, rw_ref, rb_ref, x2_ref, hf_ref, lg_ref):
    x = x_ref[...]
    h = _rms(x, gc_ref[...], NORM_EPS).astype(BF16)
    q = _dot(h, wq_ref[...]).astype(BF16)
    scale = LANES ** -0.5
    outs = []
    for hd in range(CROSS_HEADS):
        qh = q[:, hd * LANES:(hd + 1) * LANES]
        kh = kv_ref[0, :, hd * LANES:(hd + 1) * LANES]
        vh = kv_ref[0, :, CROSS_WIDTH + hd * LANES:CROSS_WIDTH + (hd + 1) * LANES]
        s = _dot(qh, kh, NT) * scale
        s = s - jnp.max(s, axis=-1, keepdims=True)
        e = jnp.exp(s)
        p = e / jnp.sum(e, axis=-1, keepdims=True)
        outs.append(_dot(p.astype(BF16), vh))
    o = jnp.concatenate(outs, axis=1).astype(BF16)
    x2 = x + _dot(o, wo_ref[...])
    x2_ref[...] = x2
    hf = _rms(x2, gf_ref[...], NORM_EPS)
    tm = hf.shape[0]
    for c in range(PACK_ROWS):
        hf_ref[pl.ds(c, tm, stride=PACK_ROWS), :] = _pack_pair(hf[:, 2 * c * LANES:(2 * c + 1) * LANES],
                                                               hf[:, (2 * c + 1) * LANES:(2 * c + 2) * LANES])
    lg_ref[...] = _dot3(hf, rw_ref[...]) + rb_ref[...]


def cross_attention(x, gc, wq, kv, wo, gf, rw, rb, *, seq, tm):
    t, d = x.shape
    per_seq = seq // tm
    const = lambda a: pl.BlockSpec(a.shape, lambda i: (0,) * a.ndim, pipeline_mode=pl.Buffered(1))
    vec = lambda n: pl.BlockSpec((1, n), lambda i: (0, 0))
    return pl.pallas_call(
        _cross_kernel,
        grid=(t // tm,),
        in_specs=[
            pl.BlockSpec((tm, d), lambda i: (i, 0)),
            vec(d), const(wq),
            pl.BlockSpec((1,) + kv.shape[1:], lambda i: (i // per_seq, 0, 0)),
            const(wo), vec(d), const(rw), vec(LANES),
        ],
        out_specs=[pl.BlockSpec((tm, d), lambda i: (i, 0)), pl.BlockSpec((tm * PACK_ROWS, LANES), lambda i: (i, 0)),
                   pl.BlockSpec((tm, LANES), lambda i: (i, 0))],
        out_shape=[jax.ShapeDtypeStruct((t, d), F32), jax.ShapeDtypeStruct((t * PACK_ROWS, LANES), U32),
                   jax.ShapeDtypeStruct((t, LANES), F32)],
        compiler_params=_cparams(("parallel",)),
        name="cross_attention",
    )(x, gc.reshape(1, d), wq, kv, wo, gf.reshape(1, d), rw, rb)


def _router_kernel(lg_ref, idx_ref, gate_ref):
    x = lg_ref[...]
    lane = lax.broadcasted_iota(jnp.int32, x.shape, 1)
    x = jnp.where(lane < N_EXPERTS, x, -jnp.inf)
    idx_out = jnp.zeros(x.shape, jnp.int32)
    val_out = jnp.zeros(x.shape, F32)
    vals = []
    for j in range(TOP_K):
        m = jnp.max(x, axis=-1, keepdims=True)
        sel = jnp.min(jnp.where(x == m, lane, LANES), axis=-1, keepdims=True)
        idx_out = jnp.where(lane == j, sel, idx_out)
        vals.append(m)
        x = jnp.where(lane == sel, -jnp.inf, x)
    es = [jnp.exp(vj - vals[0]) for vj in vals]
    tot = es[0] + es[1] + es[2] + es[3]
    for j in range(TOP_K):
        val_out = jnp.where(lane == j, es[j] / tot, val_out)
    idx_ref[...] = idx_out[:, 0:TOP_K]
    gate_ref[...] = val_out[:, 0:TOP_K]


def router_topk(logits, *, tm):
    t = logits.shape[0]
    return pl.pallas_call(
        _router_kernel,
        grid=(t // tm,),
        in_specs=[pl.BlockSpec((tm, LANES), lambda i: (i, 0))],
        out_specs=[pl.BlockSpec((tm, TOP_K), lambda i: (i, 0)), pl.BlockSpec((tm, TOP_K), lambda i: (i, 0))],
        out_shape=[jax.ShapeDtypeStruct((t, TOP_K), jnp.int32), jax.ShapeDtypeStruct((t, TOP_K), F32)],
        compiler_params=_cparams(("parallel",)),
        name="router_topk",
    )(logits)


def _moe_kernel(sbe_ref, sbr_ref, src_ref, tok_ref, ntok_ref, hf_hbm, w1_ref, b1_ref, w2_ref, b2_ref,
                ys_hbm, xbuf, x2d, acc, obuf, gsem, ssem):
    s = pl.program_id(0)
    f = pl.program_id(1)
    nf = pl.num_programs(1)
    nsb = pl.num_programs(0)
    sub_rows = MOE_SUB * PACK_ROWS
    nsubs = MOE_SUPER // MOE_SUB
    unroll = 8

    def ceil_sub(n):
        return lax.shift_right_logical(n + (MOE_SUB - 1), MOE_SUB.bit_length() - 1)

    rows = sbr_ref[s]
    nsub = ceil_sub(rows)
    rows_next = jnp.where(s + 1 < nsb, sbr_ref[jnp.minimum(s + 1, nsb - 1)], 0)
    rows_prev = jnp.where(s > 0, sbr_ref[jnp.maximum(s - 1, 0)], 0)
    nsub_next = ceil_sub(rows_next)

    def slab(ref, off):
        return ref.at[pl.ds(pl.multiple_of(off, PACK_ROWS), PACK_ROWS)]

    def row_in(table, r):
        return pltpu.make_async_copy(slab(hf_hbm, table[0, 0, r]), slab(xbuf, r * PACK_ROWS), gsem)

    def row_out(r):
        return pltpu.make_async_copy(slab(obuf, r * PACK_ROWS), slab(ys_hbm, src_ref[0, 0, r] * PACK_ROWS), ssem)

    def start_gather(table, first, n):
        def group(gi, c):
            for j in range(unroll):
                row_in(table, first + gi * unroll + j).start(priority=j % 2)
            return c

        lax.fori_loop(0, n // unroll, group, 0)

    def wait_sub_blocks(buf, other, sem, n):
        for sub in range(nsubs):
            @pl.when(sub < n)
            def _(sub=sub):
                pltpu.make_async_copy(other.at[pl.ds(0, sub_rows)], buf.at[pl.ds(sub * sub_rows, sub_rows)], sem).wait()

    def wait_scatter(n):
        full = n // MOE_SUB
        for sub in range(nsubs):
            @pl.when(sub < full)
            def _(sub=sub):
                pltpu.make_async_copy(obuf.at[pl.ds(sub * sub_rows, sub_rows)], ys_hbm.at[pl.ds(0, sub_rows)], ssem).wait()

        def one(r, c):
            row_out(r).wait()
            return c

        lax.fori_loop(0, n - full * MOE_SUB, one, 0)

    @pl.when((f == 0) & (s == 0) & (rows > 0))
    def _():
        start_gather(tok_ref, 0, nsub * MOE_SUB)

    @pl.when((f == 0) & (rows > 0))
    def _():
        wait_sub_blocks(xbuf, hf_hbm, gsem, nsub)

    @pl.when((f == nf - 1) & (rows > 0) & (rows_prev > 0))
    def _():
        wait_scatter(rows_prev)

    for sub in range(nsubs):
        @pl.when(sub < nsub)
        def _(sub=sub):
            sl = slice(sub * MOE_SUB, (sub + 1) * MOE_SUB)
            base = sub * sub_rows

            @pl.when(f == 0)
            def _():
                for c in range(PACK_ROWS):
                    lo, hi = _unpack_pair(xbuf[pl.ds(base + c, MOE_SUB, stride=PACK_ROWS), :])
                    x2d[sl, 2 * c * LANES:(2 * c + 1) * LANES] = lo.astype(BF16)
                    x2d[sl, (2 * c + 1) * LANES:(2 * c + 2) * LANES] = hi.astype(BF16)

            x = x2d[sl, :]
            hb = _dot(x, w1_ref[0]) + b1_ref[0]
            even = (lax.broadcasted_iota(jnp.int32, (MOE_SUB, LANES), 1) % 2) == 0
            acts = []
            for cb in range(MOE_FT // LANES):
                a = hb[:, cb * LANES:(cb + 1) * LANES]
                b = hb[:, MOE_FT + cb * LANES:MOE_FT + (cb + 1) * LANES]
                hg = jnp.where(even, a, pltpu.roll(b, 1, axis=1))
                hl = jnp.where(even, pltpu.roll(a, LANES - 1, axis=1), b)
                xg = jnp.minimum(hg, SWIGLU_LIMIT)
                xl = jnp.clip(hl, -SWIGLU_LIMIT, SWIGLU_LIMIT)
                acts.append((xg * jax.nn.sigmoid(SWIGLU_ALPHA * xg) * (xl + 1.0)).astype(BF16))
            contrib = _dot(jnp.concatenate(acts, axis=1), w2_ref[0])

            @pl.when(f == 0)
            def _():
                acc[sl, :] = contrib

            @pl.when((f > 0) & (f < nf - 1))
            def _():
                acc[sl, :] += contrib

            @pl.when(f == nf - 1)
            def _():
                y = acc[sl, :] + contrib + b2_ref[0]
                for c in range(PACK_ROWS):
                    obuf[pl.ds(base + c, MOE_SUB, stride=PACK_ROWS), :] = _pack_pair(
                        y[:, 2 * c * LANES:(2 * c + 1) * LANES], y[:, (2 * c + 1) * LANES:(2 * c + 2) * LANES])

    @pl.when((f == 0) & (rows_next > 0))
    def _():
        start_gather(ntok_ref, 0, nsub_next * MOE_SUB)

    @pl.when((f == nf - 1) & (rows > 0))
    def _():
        def group(gi, c):
            for j in range(unroll):
                row_out(gi * unroll + j).start(priority=j % 2)
            return c

        def one(r, c):
            row_out(r).start()
            return c

        ngroups = rows // unroll
        lax.fori_loop(0, ngroups, group, 0)
        lax.fori_loop(ngroups * unroll, rows, one, 0)

        @pl.when(rows_next == 0)
        def _():
            wait_scatter(rows)


def moe_experts(sb_e, sb_rows, row_src, row_off, hf, w1, b1, w2, b2):
    d = D_MODEL
    t = hf.shape[0] // PACK_ROWS
    nsb = sb_e.shape[0]
    nf = D_EXPERT // MOE_FT
    assert nf >= 2, "the last hidden-column step adds its own contribution to the accumulator of the earlier ones"
    grid_spec = pltpu.PrefetchScalarGridSpec(
        num_scalar_prefetch=2,
        grid=(nsb, nf),
        in_specs=[
            pl.BlockSpec((1, 1, MOE_SUPER), lambda s, f, e, r: (s, 0, 0), memory_space=pltpu.SMEM),
            pl.BlockSpec((1, 1, MOE_SUPER), lambda s, f, e, r: (s, 0, 0), memory_space=pltpu.SMEM),
            pl.BlockSpec((1, 1, MOE_SUPER), lambda s, f, e, r: (jnp.minimum(s + 1, nsb - 1), 0, 0),
                         memory_space=pltpu.SMEM),
            pl.BlockSpec(memory_space=pl.ANY),
            pl.BlockSpec((1, d, 2 * MOE_FT), lambda s, f, e, r: (e[s], 0, f)),
            pl.BlockSpec((1, 1, 2 * MOE_FT), lambda s, f, e, r: (e[s], 0, f)),
            pl.BlockSpec((1, MOE_FT, d), lambda s, f, e, r: (e[s], f, 0)),
            pl.BlockSpec((1, 1, d), lambda s, f, e, r: (e[s], 0, 0)),
        ],
        out_specs=pl.BlockSpec(memory_space=pl.ANY),
        scratch_shapes=[pltpu.VMEM((MOE_SUPER * PACK_ROWS, LANES), U32), pltpu.VMEM((MOE_SUPER, d), BF16),
                        pltpu.VMEM((MOE_SUPER, d), F32), pltpu.VMEM((MOE_SUPER * PACK_ROWS, LANES), U32),
                        pltpu.SemaphoreType.DMA, pltpu.SemaphoreType.DMA],
    )
    return pl.pallas_call(
        _moe_kernel,
        grid_spec=grid_spec,
        out_shape=jax.ShapeDtypeStruct((t * TOP_K * PACK_ROWS, LANES), U32),
        compiler_params=_cparams(("arbitrary", "arbitrary")),
        name="moe_experts",
    )(sb_e, sb_rows, row_src, row_off, row_off, hf, w1, b1, w2, b2)


def _combine_kernel(ys_ref, gate_ref, x_ref, g_ref, o_ref, lo_ref, hi_ref):
    tm = x_ref.shape[0]
    tot_lo = tot_hi = None
    for j in range(TOP_K):
        lo, hi = _unpack_pair(ys_ref[:, j * PACK_ROWS:(j + 1) * PACK_ROWS, :])
        gj = gate_ref[:, j:j + 1, :]
        tot_lo = lo * gj if tot_lo is None else tot_lo + lo * gj
        tot_hi = hi * gj if tot_hi is None else tot_hi + hi * gj
    lo_ref[...] = tot_lo.reshape(tm * PACK_ROWS, LANES)
    hi_ref[...] = tot_hi.reshape(tm * PACK_ROWS, LANES)
    pieces = []
    for c in range(PACK_ROWS):
        pieces.append(lo_ref[pl.ds(c, tm, stride=PACK_ROWS), :])
        pieces.append(hi_ref[pl.ds(c, tm, stride=PACK_ROWS), :])
    out = x_ref[...] + jnp.concatenate(pieces, axis=1)
    o_ref[...] = _rms(out, g_ref[...], NORM_EPS)


def combine_final(ys, gate, x, g, *, tm):
    t, d = x.shape
    return pl.pallas_call(
        _combine_kernel,
        grid=(t // tm,),
        in_specs=[pl.BlockSpec((tm, TOP_K * PACK_ROWS, LANES), lambda i: (i, 0, 0)),
                  pl.BlockSpec((tm, TOP_K, LANES), lambda i: (i, 0, 0)),
                  pl.BlockSpec((tm, d), lambda i: (i, 0)), pl.BlockSpec((1, d), lambda i: (0, 0))],
        out_specs=pl.BlockSpec((tm, d), lambda i: (i, 0)),
        out_shape=jax.ShapeDtypeStruct((t, d), F32),
        scratch_shapes=[pltpu.VMEM((tm * PACK_ROWS, LANES), F32), pltpu.VMEM((tm * PACK_ROWS, LANES), F32)],
        compiler_params=_cparams(("parallel",)),
        name="combine_final",
    )(ys, gate, x, g.reshape(1, d))


def _routing_tables(top_idx, n_super):
    flat_e = top_idx.reshape(-1)
    n = flat_e.shape[0]
    onehot = (flat_e[:, None] == jnp.arange(N_EXPERTS, dtype=jnp.int32)[None, :]).astype(jnp.int32)
    csum = jnp.cumsum(onehot, axis=0)
    rank = jnp.sum(onehot * csum, axis=1) - 1
    counts = csum[-1]
    nsb = (counts + MOE_SUPER - 1) // MOE_SUPER
    sb_end = jnp.cumsum(nsb)
    sb_start = sb_end - nsb
    dest = sb_start[flat_e] * MOE_SUPER + rank
    row_src = jnp.full((n_super * MOE_SUPER,), -1, jnp.int32).at[dest].set(jnp.arange(n, dtype=jnp.int32))
    s_ids = jnp.arange(n_super, dtype=jnp.int32)
    sb_e = jnp.minimum(jnp.searchsorted(sb_end, s_ids, side="right"), N_EXPERTS - 1).astype(jnp.int32)
    local = s_ids - sb_start[sb_e]
    sb_rows = jnp.clip(counts[sb_e] - local * MOE_SUPER, 0, MOE_SUPER)
    sb_rows = jnp.where(s_ids < sb_end[-1], sb_rows, 0).astype(jnp.int32)
    row_off = jnp.where(row_src >= 0, (row_src // TOP_K) * PACK_ROWS, 0)
    return sb_e, sb_rows, row_src.reshape(n_super, 1, MOE_SUPER), row_off.reshape(n_super, 1, MOE_SUPER)


def _pad_rows(a, n):
    return jnp.pad(a, ((0, n - a.shape[0]), (0, 0)))


def _layer(x, mem, l, p, batch, seq):
    t = batch * seq
    c = RWKV_WIDTH
    lam_init = 0.8 - 0.6 * math.exp(-0.3 * l)
    w_in = p["w_in"]
    o1 = 3 * c + DECAY_LORA + AAA_LORA + GATE_LORA
    o2 = o1 + 3 * DIFF_WIDTH
    padc = lambda a, n: jnp.pad(a, ((0, 0), (0, n - a.shape[1])))
    w_rwkv = jnp.concatenate([
        w_in[:, :3 * c],
        padc(w_in[:, 3 * c:3 * c + DECAY_LORA], 128),
        padc(w_in[:, 3 * c + DECAY_LORA:3 * c + DECAY_LORA + AAA_LORA], 128),
        padc(w_in[:, 3 * c + DECAY_LORA + AAA_LORA:o1], 256)], axis=1).astype(BF16)
    mu = p["rwkv_mu"]
    pad1 = lambda a, n: jnp.pad(a, (0, n - a.shape[0]))
    mu_p = jnp.concatenate([mu[:3 * c], pad1(mu[3 * c:3 * c + DECAY_LORA], 128),
                            pad1(mu[3 * c + DECAY_LORA:3 * c + DECAY_LORA + AAA_LORA], 128),
                            pad1(mu[3 * c + DECAY_LORA + AAA_LORA:], 256)])
    w_diff = w_in[:, o1:o2].astype(BF16)
    w_gate = w_in[:, o2:].astype(BF16)

    g_mix = p["norm_mix_g"]
    p_rwkv = norm_matmul(x, g_mix, w_rwkv, tm=1024, tn=RWKV_COLS // 2, out_dtype=F32, name="in_proj_rwkv")
    qkv = norm_matmul(x, g_mix, w_diff, tm=1024, tn=1536, out_dtype=BF16, name="in_proj_diff")
    gates = norm_matmul(x, g_mix, w_gate, tm=1024, tn=2048, out_dtype=BF16, act="sigmoid", name="in_proj_gate")

    prep = rwkv_prep(p_rwkv, mu_p, p["rwkv_w0"], p["rwkv_a0"], p["rwkv_k_k"], p["rwkv_k_a"],
                     _pad_rows(p["rwkv_w2"], 128), _pad_rows(p["rwkv_a2"], 128), _pad_rows(p["rwkv_g2"], 256),
                     seq=seq, tm=256)
    ra, w1b, w2b = rwkv_scan(*prep, p["rwkv_ln_w"], p["rwkv_ln_b"], p["rwkv_r_k"].reshape(-1),
                             p["expert_w1"], p["expert_w2"], batch=batch, seq=seq)

    slopes = (2.0 ** (-8.0 * jnp.arange(1, DIFF_HEADS + 1, dtype=F32) / DIFF_HEADS)).astype(F32)
    da = diff_attention(qkv, slopes, p["diff_lq1"], p["diff_lk1"], p["diff_lq2"], p["diff_lk2"],
                        p["diff_subln_g"], batch=batch, seq=seq, tq=min(1024, seq), lam_init=lam_init)

    x1 = mix_project(ra, da, gates, x, p["rwkv_proj"].astype(BF16), p["diff_proj"].astype(BF16),
                     p["w_out"].astype(BF16), tm=512)

    m_len = mem.shape[0] // batch
    kv = norm_matmul(mem, p["norm_mem_g"], p["cross_wkv"].astype(BF16), tm=min(512, mem.shape[0]),
                     tn=2 * CROSS_WIDTH, out_dtype=BF16, name="cross_kv")
    rw = jnp.pad(p["router_w"], ((0, 0), (0, LANES - N_EXPERTS)))
    rb = jnp.pad(p["router_b"], (0, LANES - N_EXPERTS)).reshape(1, LANES)
    x2, hf, logits = cross_attention(x1, p["norm_cross_g"], p["cross_wq"].astype(BF16),
                                     kv.reshape(batch, m_len, 2 * CROSS_WIDTH), p["cross_wo"].astype(BF16),
                                     p["norm_ffn_g"], rw, rb, seq=seq, tm=512)

    top_idx, gate = router_topk(logits, tm=min(1024, t))
    n_super = (t * TOP_K) // MOE_SUPER + N_EXPERTS
    sb_e, sb_rows, row_src, row_off = _routing_tables(top_idx, n_super)
    ys = moe_experts(sb_e, sb_rows, row_src, row_off, hf, w1b, p["expert_b1"][:, None, :], w2b, p["expert_b2"][:, None, :])
    gate_b = jnp.broadcast_to(gate[:, :, None], (t, TOP_K, LANES))
    return x2, ys.reshape(t, TOP_K * PACK_ROWS, LANES), gate_b


def kernel(x, mem, norm_mix_g, w_in, rwkv_mu, rwkv_w0, rwkv_w2, rwkv_a0, rwkv_a2, rwkv_g2, rwkv_k_k, rwkv_k_a, rwkv_r_k, rwkv_ln_w, rwkv_ln_b, rwkv_proj, diff_lq1, diff_lk1, diff_lq2, diff_lk2, diff_subln_g, diff_proj, w_out, norm_cross_g, norm_mem_g, cross_wq, cross_wkv, cross_wo, norm_ffn_g, router_w, router_b, expert_w1, expert_b1, expert_w2, expert_b2, final_norm_g):
    batch, seq, d = x.shape
    stacked = dict(norm_mix_g=norm_mix_g, w_in=w_in, rwkv_mu=rwkv_mu, rwkv_w0=rwkv_w0, rwkv_w2=rwkv_w2,
                   rwkv_a0=rwkv_a0, rwkv_a2=rwkv_a2, rwkv_g2=rwkv_g2, rwkv_k_k=rwkv_k_k, rwkv_k_a=rwkv_k_a,
                   rwkv_r_k=rwkv_r_k, rwkv_ln_w=rwkv_ln_w, rwkv_ln_b=rwkv_ln_b, rwkv_proj=rwkv_proj,
                   diff_lq1=diff_lq1, diff_lk1=diff_lk1, diff_lq2=diff_lq2, diff_lk2=diff_lk2,
                   diff_subln_g=diff_subln_g, diff_proj=diff_proj, w_out=w_out, norm_cross_g=norm_cross_g,
                   norm_mem_g=norm_mem_g, cross_wq=cross_wq, cross_wkv=cross_wkv, cross_wo=cross_wo,
                   norm_ffn_g=norm_ffn_g, router_w=router_w, router_b=router_b, expert_w1=expert_w1,
                   expert_b1=expert_b1, expert_w2=expert_w2, expert_b2=expert_b2)
    assert w_in.shape[0] == 1, "the closing RMSNorm is fused into the single layer's combine"
    p = {k: v[0] for k, v in stacked.items()}
    x2, ys, gate_b = _layer(x.reshape(batch * seq, d), mem.reshape(-1, d), 0, p, batch, seq)
    out = combine_final(ys, gate_b, x2, final_norm_g, tm=256)
    return out.reshape(batch, seq, d)
```
